```python
import jax, jax.numpy as jnp
from jax import lax
import numpy as np

D_MODEL = 2048
BATCH = 2
SEQ = 8192
DEPTH = 1

GRID_W = 64
HEAD_DIM = 128
N_Q_HEADS = 16
N_KV_HEADS = 4
Q_BLOCK = 128
ROPE_THETA = 10000.0
ROPE_AXIS_DIM = HEAD_DIM // 2
HGRN_HEADS = D_MODEL // 128
HGRN_EXPAND = 128
HGRN_HEAD_V = D_MODEL // HGRN_HEADS
HGRN_CHUNK = 64
D_FF = 4 * D_MODEL
PLE_DIM = 256
EPS = 1e-6

ATTN_Q = N_Q_HEADS * HEAD_DIM
ATTN_KV = N_KV_HEADS * HEAD_DIM
HGRN_K = HGRN_HEADS * HGRN_EXPAND
HGRN_V = HGRN_HEADS * HGRN_HEAD_V
IN_SPLITS = (ATTN_Q, ATTN_KV, ATTN_KV, HGRN_K, HGRN_K, HGRN_K, HGRN_V, HGRN_V, D_MODEL, D_MODEL)
D_IN = ATTN_Q + 2 * ATTN_KV + 3 * HGRN_K + 2 * HGRN_V + 2 * D_MODEL

kernel_name = "hybrid_gqa_axialrope_hgrn2_bidir_gated_merge"


def rmsnorm(x, gain):
    xf = x.astype(jnp.float32)
    y = xf * lax.rsqrt(jnp.mean(xf * xf, axis=-1, keepdims=True) + EPS)
    return (y * gain.astype(jnp.float32)).astype(x.dtype)


def split_columns(proj):
    outs = []
    start = 0
    for width in IN_SPLITS:
        outs.append(proj[..., start:start + width])
        start += width
    return outs


def axial_rope_tables(seq_len):
    rows = seq_len // GRID_W
    row = jnp.repeat(jnp.arange(rows, dtype=jnp.float32), GRID_W)
    col = jnp.tile(jnp.arange(GRID_W, dtype=jnp.float32), rows)
    inv_freq = ROPE_THETA ** (-jnp.arange(0, ROPE_AXIS_DIM, 2, dtype=jnp.float32) / ROPE_AXIS_DIM)
    ang_row = row[:, None] * inv_freq[None, :]
    ang_col = col[:, None] * inv_freq[None, :]
    return (jnp.cos(ang_row), jnp.sin(ang_row), jnp.cos(ang_col), jnp.sin(ang_col))


def rotate_half_pairs(x, cos, sin):
    x1, x2 = jnp.split(x, 2, axis=-1)
    c = cos[None, :, None, :]
    s = sin[None, :, None, :]
    return jnp.concatenate([x1 * c - x2 * s, x1 * s + x2 * c], axis=-1)


def apply_axial_rope(x, tables):
    cos_r, sin_r, cos_c, sin_c = tables
    x_row, x_col = jnp.split(x, 2, axis=-1)
    out = jnp.concatenate([rotate_half_pairs(x_row, cos_r, sin_r),
                           rotate_half_pairs(x_col, cos_c, sin_c)], axis=-1)
    return out.astype(x.dtype)


def bidirectional_gqa(q, k, v):
    B, S = q.shape[0], q.shape[1]
    groups = N_Q_HEADS // N_KV_HEADS
    n_blocks = S // Q_BLOCK
    qb = q.reshape(B, n_blocks, Q_BLOCK, N_KV_HEADS, groups, HEAD_DIM).transpose(1, 0, 3, 4, 2, 5)
    scale = HEAD_DIM ** -0.5

    def one_block(q_blk):
        s = jnp.einsum('bhgqd,bkhd->bhgqk', q_blk, k).astype(jnp.float32) * scale
        w = jax.nn.softmax(s, axis=-1).astype(v.dtype)
        return jnp.einsum('bhgqk,bkhd->bqhgd', w, v)

    o = lax.map(one_block, qb)
    return jnp.moveaxis(o, 0, 1).reshape(B, S, N_Q_HEADS * HEAD_DIM)


def hgrn2_chunkwise(q, k, v, log_f):
    B, H, S, DK = q.shape
    DV = v.shape[-1]
    C = HGRN_CHUNK
    n_chunks = S // C

    def to_chunks(t):
        return jnp.moveaxis(t.reshape(B, H, n_chunks, C, t.shape[-1]), 2, 0)

    lower_tri = jnp.tril(jnp.ones((C, C), dtype=bool))[:, :, None]

    def step(state, inp):
        qc, kc, vc, ac = inp
        A = jnp.cumsum(ac, axis=-2)
        A_end = A[..., -1:, :]
        o_inter = jnp.einsum('bhtk,bhkv->bhtv', qc * jnp.exp(A), state)
        diff = A[..., :, None, :] - A[..., None, :, :]
        decay = jnp.exp(jnp.where(lower_tri, diff, -jnp.inf))
        scores = jnp.einsum('bhtk,bhsk,bhtsk->bhts', qc, kc, decay)
        o_intra = jnp.einsum('bhts,bhsv->bhtv', scores, vc)
        new_state = (jnp.exp(A_end[..., 0, :])[..., None] * state
                     + jnp.einsum('bhsk,bhsv->bhkv', kc * jnp.exp(A_end - A), vc))
        return new_state, o_inter + o_intra

    s0 = jnp.zeros((B, H, DK, DV), dtype=jnp.float32)
    _, o = lax.scan(step, s0, (to_chunks(q), to_chunks(k), to_chunks(v), to_chunks(log_f)))
    return jnp.moveaxis(o, 0, 2).reshape(B, H, S, DV)


def hgrn2_bidirectional(q_r, f_fwd, f_bwd, i_r, g_r, lb, g_norm):
    B, S = q_r.shape[0], q_r.shape[1]

    def heads(t, d):
        return t.astype(jnp.float32).reshape(B, S, HGRN_HEADS, d).transpose(0, 2, 1, 3)

    q = heads(jax.nn.silu(q_r), HGRN_EXPAND)
    v = heads(i_r, HGRN_HEAD_V)
    lbf = lb.astype(jnp.float32)

    def gate(f_raw, lower):
        f = lower + (1.0 - lower) * jax.nn.sigmoid(f_raw.astype(jnp.float32))
        return heads(1.0 - f, HGRN_EXPAND), heads(jnp.log(f), HGRN_EXPAND)

    k_fw, logf_fw = gate(f_fwd, lbf[0])
    k_bw, logf_bw = gate(f_bwd, lbf[1])
    o_fw = hgrn2_chunkwise(q, k_fw, v, logf_fw)
    rev = lambda t: jnp.flip(t, axis=2)
    o_bw = rev(hgrn2_chunkwise(rev(q), rev(k_bw), rev(v), rev(logf_bw)))
    o = (o_fw + o_bw).transpose(0, 2, 1, 3)
    o = rmsnorm(o, g_norm)
    gate_out = jax.nn.silu(g_r.astype(jnp.float32)).reshape(B, S, HGRN_HEADS, HGRN_HEAD_V)
    return (o * gate_out).reshape(B, S, HGRN_V).astype(q_r.dtype)


def setup_inputs(seed: int = 0) -> dict:
    key = jax.random.key(seed)
    ks = jax.random.split(key, 20)
    f32 = jnp.float32
    nrm = lambda k, shape, scale: jax.random.normal(k, shape, f32) * scale
    gain = lambda k, shape: 1.0 + 0.05 * jax.random.normal(k, shape, f32)
    return {
        "x": jax.random.normal(ks[0], (BATCH, SEQ, D_MODEL), f32),
        "p": jax.random.normal(ks[1], (DEPTH, BATCH, SEQ, PLE_DIM), f32),
        "g_mix": gain(ks[2], (DEPTH, D_MODEL)),
        "w_in": nrm(ks[3], (DEPTH, D_MODEL, D_IN), D_MODEL ** -0.5),
        "g_q": gain(ks[4], (DEPTH, HEAD_DIM)),
        "g_k": gain(ks[5], (DEPTH, HEAD_DIM)),
        "w_o_attn": nrm(ks[6], (DEPTH, ATTN_Q, D_MODEL), ATTN_Q ** -0.5),
        "hgrn_lb": nrm(ks[7], (DEPTH + 1, 2, HGRN_K), 0.5),
        "g_hgrn": gain(ks[8], (DEPTH, HGRN_HEAD_V)),
        "w_o_hgrn": nrm(ks[9], (DEPTH, HGRN_V, D_MODEL), HGRN_V ** -0.5),
        "w_out": nrm(ks[10], (DEPTH, D_MODEL, D_MODEL), D_MODEL ** -0.5),
        "g_mlp": gain(ks[11], (DEPTH, D_MODEL)),
        "w_up": nrm(ks[12], (DEPTH, D_MODEL, D_FF), D_MODEL ** -0.5),
        "w_down": nrm(ks[13], (DEPTH, D_FF, D_MODEL), D_FF ** -0.5),
        "g_ple": gain(ks[14], (DEPTH, D_MODEL)),
        "w_ple_gate": nrm(ks[15], (DEPTH, D_MODEL, D_MODEL), D_MODEL ** -0.5),
        "w_ple": nrm(ks[16], (DEPTH, PLE_DIM, D_MODEL), PLE_DIM ** -0.5),
        "g_final": gain(ks[17], (D_MODEL,)),
    }


def reference(x, p, g_mix, w_in, g_q, g_k, w_o_attn, hgrn_lb, g_hgrn, w_o_hgrn, w_out,
              g_mlp, w_up, w_down, g_ple, w_ple_gate, w_ple, g_final):
    B, S = x.shape[0], x.shape[1]
    rope_tables = axial_rope_tables(S)
    lb_all = jnp.cumsum(jax.nn.softmax(hgrn_lb.astype(jnp.float32), axis=0), axis=0)

    for i in range(DEPTH):
        h = rmsnorm(x, g_mix[i])
        proj = jnp.einsum('bsd,de->bse', h, w_in[i])
        q_a, k_a, v_a, q_r, f_fwd, f_bwd, i_r, g_r, gate_a, gate_r = split_columns(proj)

        q_a = apply_axial_rope(rmsnorm(q_a.reshape(B, S, N_Q_HEADS, HEAD_DIM), g_q[i]), rope_tables)
        k_a = apply_axial_rope(rmsnorm(k_a.reshape(B, S, N_KV_HEADS, HEAD_DIM), g_k[i]), rope_tables)
        v_a = v_a.reshape(B, S, N_KV_HEADS, HEAD_DIM)
        y_attn = jnp.einsum('bse,ed->bsd', bidirectional_gqa(q_a, k_a, v_a), w_o_attn[i])

        o_r = hgrn2_bidirectional(q_r, f_fwd, f_bwd, i_r, g_r, lb_all[i], g_hgrn[i])
        y_hgrn = jnp.einsum('bse,ed->bsd', o_r, w_o_hgrn[i])

        mixed = jax.nn.sigmoid(gate_a) * y_attn + jax.nn.sigmoid(gate_r) * y_hgrn
        x = x + jnp.einsum('bsd,de->bse', mixed, w_out[i])

        h = rmsnorm(x, g_mlp[i])
        u = jnp.square(jax.nn.relu(jnp.einsum('bsd,df->bsf', h, w_up[i])))
        x = x + jnp.einsum('bsf,fd->bsd', u, w_down[i])

        ple_gate = jax.nn.sigmoid(jnp.einsum('bsd,de->bse', rmsnorm(x, g_ple[i]), w_ple_gate[i]))
        x = x + ple_gate * jnp.einsum('bsc,cd->bsd', p[i].astype(x.dtype), w_ple[i])

    return rmsnorm(x, g_final)
```

```python
import functools

import jax
import jax.numpy as jnp
from jax import lax
from jax.experimental import pallas as pl
from jax.experimental.pallas import tpu as pltpu

F32 = jnp.float32
BF16 = jnp.bfloat16

EPS = 1e-6
HEAD_DIM = 128
N_Q_HEADS = 16
N_KV_HEADS = 4
GQA_GROUPS = N_Q_HEADS // N_KV_HEADS
GRID_W = 64
ROPE_THETA = 10000.0
HGRN_HEADS = 16
LANES = 128
SUBLANES = 8
VMEM_LIMIT = 56 * 1024 * 1024


def _cparams(sem):
    return pltpu.CompilerParams(dimension_semantics=sem, vmem_limit_bytes=VMEM_LIMIT)


def _sigmoid(x):
    return 1.0 / (1.0 + jnp.exp(-x))


def _rms(x, gain):
    ms = jnp.mean(x * x, axis=-1, keepdims=True)
    return x * lax.rsqrt(ms + EPS) * gain


def _rmsnorm_kernel(x_ref, g_ref, o_ref):
    o_ref[...] = _rms(x_ref[...], g_ref[...]).astype(o_ref.dtype)


def rmsnorm_bf16(x, gain, tm=512):
    m, d = x.shape
    tm = min(tm, m)
    return pl.pallas_call(
        _rmsnorm_kernel,
        grid=(m // tm,),
        in_specs=[pl.BlockSpec((tm, d), lambda i: (i, 0)), pl.BlockSpec((1, d), lambda i: (0, 0))],
        out_specs=pl.BlockSpec((tm, d), lambda i: (i, 0)),
        out_shape=jax.ShapeDtypeStruct((m, d), BF16),
        compiler_params=_cparams(("parallel",)),
        name="rmsnorm",
    )(x, gain.reshape(1, d))


def _proj_plain_kernel(h_ref, w_ref, o_ref, *, act):
    acc = jnp.dot(h_ref[...], w_ref[...], preferred_element_type=F32)
    if act == "silu":
        acc = acc * _sigmoid(acc)
    elif act == "sigmoid":
        acc = _sigmoid(acc)
    o_ref[...] = acc.astype(o_ref.dtype)


def _proj_logf_kernel(h_ref, w_ref, lb_ref, o_ref):
    acc = jnp.dot(h_ref[...], w_ref[...], preferred_element_type=F32)
    lb = lb_ref[...]
    o_ref[...] = jnp.log(lb + (1.0 - lb) * _sigmoid(acc))


def _proj_qk_kernel(h_ref, w_ref, g_ref, cos_ref, sina_ref, sinb_ref, o_ref, *, scale):
    acc = jnp.dot(h_ref[...], w_ref[...], preferred_element_type=F32)
    g = g_ref[...]
    cos, sina, sinb = cos_ref[...], sina_ref[...], sinb_ref[...]
    for hd in range(acc.shape[1] // HEAD_DIM):
        sl = slice(hd * HEAD_DIM, (hd + 1) * HEAD_DIM)
        y = _rms(acc[:, sl], g)
        y = (y * cos + pltpu.roll(y, HEAD_DIM - 32, 1) * sina + pltpu.roll(y, 32, 1) * sinb)
        o_ref[:, sl] = (y * scale).astype(o_ref.dtype)


def _proj_call(kernel, h, w, col0, ncols, out_dtype, extra=(), extra_specs=(), tm=1024, tn=512):
    m, d = h.shape
    tm = min(tm, m)
    assert col0 % tn == 0 and ncols % tn == 0 and m % tm == 0
    jb = col0 // tn
    return pl.pallas_call(
        kernel,
        grid=(m // tm, ncols // tn),
        in_specs=[pl.BlockSpec((tm, d), lambda i, j: (i, 0)),
                  pl.BlockSpec((d, tn), lambda i, j: (0, jb + j))] + list(extra_specs),
        out_specs=pl.BlockSpec((tm, tn), lambda i, j: (i, j)),
        out_shape=jax.ShapeDtypeStruct((m, ncols), out_dtype),
        compiler_params=_cparams(("parallel", "arbitrary")),
        name="in_proj",
    )(h, w, *extra)


def proj_plain(h, w, col0, ncols, act, out_dtype=BF16):
    return _proj_call(functools.partial(_proj_plain_kernel, act=act), h, w, col0, ncols, out_dtype)


def proj_logf(h, w, col0, ncols, lb, tn=512):
    return _proj_call(_proj_logf_kernel, h, w, col0, ncols, F32, extra=(lb.reshape(1, ncols),),
                      extra_specs=(pl.BlockSpec((1, tn), lambda i, j: (0, j)),), tn=tn)


def proj_qk(h, w, col0, ncols, gain, rope, seq, scale, tm=1024):
    tm = min(tm, seq)
    nsb = seq // tm
    tab = pl.BlockSpec((tm, HEAD_DIM), lambda i, j: (i % nsb, 0))
    return _proj_call(functools.partial(_proj_qk_kernel, scale=scale), h, w, col0, ncols, BF16,
                      extra=(gain.reshape(1, HEAD_DIM),) + tuple(rope),
                      extra_specs=(pl.BlockSpec((1, HEAD_DIM), lambda i, j: (0, 0)), tab, tab, tab), tm=tm)


def rope_tables(seq):
    half = HEAD_DIM // 2
    t = jnp.arange(seq, dtype=jnp.int32)
    row = (t // GRID_W).astype(F32)
    col = (t % GRID_W).astype(F32)
    inv_freq = ROPE_THETA ** (-jnp.arange(0, half, 2, dtype=F32) / half)
    ang_r = row[:, None] * inv_freq[None, :]
    ang_c = col[:, None] * inv_freq[None, :]
    z = jnp.zeros_like(ang_r)
    cos = jnp.concatenate([jnp.cos(ang_r)] * 2 + [jnp.cos(ang_c)] * 2, axis=-1)
    sina = jnp.concatenate([-jnp.sin(ang_r), z, -jnp.sin(ang_c), z], axis=-1)
    sinb = jnp.concatenate([z, jnp.sin(ang_r), z, jnp.sin(ang_c)], axis=-1)
    return cos, sina, sinb


def _flash_kernel(q_ref, k_ref, v_ref, o_ref, m_scr, l_scr, acc_scr):
    ki = pl.program_id(3)

    @pl.when(ki == 0)
    def _():
        m_scr[...] = jnp.full(m_scr.shape, -jnp.inf, F32)
        l_scr[...] = jnp.zeros(l_scr.shape, F32)
        acc_scr[...] = jnp.zeros(acc_scr.shape, F32)

    k = k_ref[0]
    v = v_ref[0]
    for g in range(GQA_GROUPS):
        q = q_ref[0, :, g * HEAD_DIM:(g + 1) * HEAD_DIM]
        s = lax.dot_general(q, k, (((1,), (1,)), ((), ())), preferred_element_type=F32)
        m_prev = m_scr[g]
        m_new = jnp.maximum(m_prev, jnp.max(s, axis=1, keepdims=True))
        alpha = jnp.exp(m_prev - m_new)
        p = jnp.exp(s - m_new)
        l_scr[g] = alpha * l_scr[g] + jnp.sum(p, axis=1, keepdims=True)
        acc_scr[g] = alpha * acc_scr[g] + jnp.dot(p.astype(BF16), v, preferred_element_type=F32)
        m_scr[g] = m_new

    @pl.when(ki == pl.num_programs(3) - 1)
    def _():
        for g in range(GQA_GROUPS):
            o_ref[0, :, g * HEAD_DIM:(g + 1) * HEAD_DIM] = (acc_scr[g] / l_scr[g]).astype(o_ref.dtype)


def flash_gqa(q, k, v, tq=512, tk=512):
    b, s, _ = q.shape
    tq, tk = min(tq, s), min(tk, s)
    gw = GQA_GROUPS * HEAD_DIM
    return pl.pallas_call(
        _flash_kernel,
        grid=(b, N_KV_HEADS, s // tq, s // tk),
        in_specs=[pl.BlockSpec((1, tq, gw), lambda bi, h, qi, ki: (bi, qi, h)),
                  pl.BlockSpec((1, tk, HEAD_DIM), lambda bi, h, qi, ki: (bi, ki, h)),
                  pl.BlockSpec((1, tk, HEAD_DIM), lambda bi, h, qi, ki: (bi, ki, h))],
        out_specs=pl.BlockSpec((1, tq, gw), lambda bi, h, qi, ki: (bi, qi, h)),
        out_shape=jax.ShapeDtypeStruct(q.shape, BF16),
        scratch_shapes=[pltpu.VMEM((GQA_GROUPS, tq, 1), F32), pltpu.VMEM((GQA_GROUPS, tq, 1), F32),
                        pltpu.VMEM((GQA_GROUPS, tq, HEAD_DIM), F32)],
        compiler_params=_cparams(("parallel", "parallel", "parallel", "arbitrary")),
        name="flash_gqa",
    )(q, k, v)


HGRN_CHUNK = 128
DIAG = SUBLANES


def _split3(x):
    hi = x.astype(BF16)
    r1 = x - hi.astype(F32)
    mid = r1.astype(BF16)
    lo = (r1 - mid.astype(F32)).astype(BF16)
    return hi, mid, lo


def _group_row(x, group, row):
    c, n = x.shape
    xr = x.reshape(c // group, group, n)
    return jnp.broadcast_to(xr[:, row:row + 1, :], xr.shape).reshape(c, n)


def _hgrn_chunk(q_ref, v_ref, lf_ref, o_ref, st_scr, r0, consts, *, rev):
    tri, ones_red, rowi, level_masks = consts
    c = HGRN_CHUNK
    q = q_ref[0, pl.ds(r0, c), :].astype(F32)
    vb = v_ref[0, pl.ds(r0, c), :]
    v = vb.astype(F32)
    lf = lf_ref[0, pl.ds(r0, c), :]
    k = 1.0 - jnp.exp(lf)

    hi, mid, lo = _split3(lf)
    a = (jnp.dot(tri, hi, preferred_element_type=F32) + jnp.dot(tri, mid, preferred_element_type=F32)
         + jnp.dot(tri, lo, preferred_element_type=F32))
    edge = 0 if rev else c - 1
    a_end = a[edge:edge + 1, :]

    st = st_scr[...]
    o = lax.dot_general((q * jnp.exp(a)).astype(BF16), st.astype(BF16), (((1,), (1,)), ((), ())),
                        preferred_element_type=F32)
    kdec = (k * jnp.exp(a_end - a)).astype(BF16)
    st_scr[...] = st * jnp.exp(a_end) + lax.dot_general(vb, kdec, (((0,), (0,)), ((), ())),
                                                        preferred_element_type=F32)

    p = jnp.zeros((c, c), F32)
    m = DIAG
    for mask in level_masks:
        ref_row = m if rev else m - 1
        e = jnp.exp(-jnp.abs(a - _group_row(a, 2 * m, ref_row)))
        sc = lax.dot_general((q * e).astype(BF16), (k * e).astype(BF16), (((1,), (1,)), ((), ())),
                             preferred_element_type=F32)
        p = p + jnp.where(mask, sc, 0.0)
        m *= 2
    o = o + jnp.dot(p.astype(BF16), vb, preferred_element_type=F32)

    for j in range(DIAG):
        d = a - _group_row(a, DIAG, j)
        valid = (rowi <= j) if rev else (rowi >= j)
        x = jnp.where(valid, q * jnp.exp(jnp.minimum(d, 0.0)) * _group_row(k, DIAG, j), 0.0)
        r = jnp.dot(x.astype(BF16), ones_red, preferred_element_type=F32)
        o = o + r * _group_row(v, DIAG, j)

    o_ref[0, pl.ds(r0, c), :] = o


def _hgrn_kernel(qf_ref, vf_ref, lff_ref, qb_ref, vb_ref, lfb_ref, of_ref, ob_ref, sf_scr, sb_scr):
    @pl.when(pl.program_id(2) == 0)
    def _():
        sf_scr[...] = jnp.zeros(sf_scr.shape, F32)
        sb_scr[...] = jnp.zeros(sb_scr.shape, F32)

    c = HGRN_CHUNK
    ti = lax.broadcasted_iota(jnp.int32, (c, c), 0)
    si = lax.broadcasted_iota(jnp.int32, (c, c), 1)
    rowi = lax.broadcasted_iota(jnp.int32, (c, LANES), 0) % DIAG
    ones_red = jnp.ones((LANES, LANES), BF16)

    def consts(rev):
        tri = ((si >= ti) if rev else (si <= ti)).astype(BF16)
        masks = []
        m = DIAG
        while m < c:
            same = (ti // (2 * m)) == (si // (2 * m))
            t_late = ((ti // m) % 2) == (0 if rev else 1)
            s_early = ((si // m) % 2) == (1 if rev else 0)
            masks.append(same & t_late & s_early)
            m *= 2
        return tri, ones_red, rowi, masks

    cf, cb = consts(False), consts(True)
    n = qf_ref.shape[1] // c

    def body(ci, carry):
        _hgrn_chunk(qf_ref, vf_ref, lff_ref, of_ref, sf_scr, pl.multiple_of(ci * c, c), cf, rev=False)
        _hgrn_chunk(qb_ref, vb_ref, lfb_ref, ob_ref, sb_scr, pl.multiple_of((n - 1 - ci) * c, c), cb, rev=True)
        return carry

    lax.fori_loop(0, n, body, 0)


def hgrn2_bidir(q, v, logf, ts=1024):
    b, s, hk = q.shape
    ts = min(ts, s)
    nt = s // ts
    nh = hk // HEAD_DIM
    blk = (1, ts, HEAD_DIM)
    fwd = lambda bi, h, i: (bi, i, h)
    bwd = lambda bi, h, i: (bi, nt - 1 - i, h)
    out = jax.ShapeDtypeStruct((b, s, hk), F32)
    return pl.pallas_call(
        _hgrn_kernel,
        grid=(b, nh, nt),
        in_specs=[pl.BlockSpec(blk, fwd), pl.BlockSpec(blk, fwd), pl.BlockSpec(blk, fwd),
                  pl.BlockSpec(blk, bwd), pl.BlockSpec(blk, bwd),
                  pl.BlockSpec(blk, lambda bi, h, i: (bi, nt - 1 - i, nh + h))],
        out_specs=[pl.BlockSpec(blk, fwd), pl.BlockSpec(blk, bwd)],
        out_shape=[out, out],
        scratch_shapes=[pltpu.VMEM((HEAD_DIM, HEAD_DIM), F32), pltpu.VMEM((HEAD_DIM, HEAD_DIM), F32)],
        compiler_params=_cparams(("parallel", "parallel", "arbitrary")),
        name="hgrn2",
    )(q, v, logf, q, v, logf)


def _merge_kernel(att_ref, of_ref, ob_ref, gr_ref, gn_ref, woa_ref, woh_ref, sa_ref, sh_ref, o_ref, or_scr):
    @pl.when(pl.program_id(1) == 0)
    def _():
        gn = gn_ref[...]
        for hd in range(or_scr.shape[1] // HEAD_DIM):
            sl = slice(hd * HEAD_DIM, (hd + 1) * HEAD_DIM)
            o = _rms(of_ref[:, sl] + ob_ref[:, sl], gn)
            or_scr[:, sl] = (o * gr_ref[:, sl].astype(F32)).astype(or_scr.dtype)

    ya = jnp.dot(att_ref[...], woa_ref[...], preferred_element_type=F32)
    yh = jnp.dot(or_scr[...], woh_ref[...], preferred_element_type=F32)
    o_ref[...] = (sa_ref[...].astype(F32) * ya + sh_ref[...].astype(F32) * yh).astype(o_ref.dtype)


def gated_merge(att, o_fw, o_bw, g_silu, g_norm, w_oa, w_oh, gates, tm=512, tn=512):
    m, d = att.shape
    tm = min(tm, m)
    nj = d // tn
    row = pl.BlockSpec((tm, d), lambda i, j: (i, 0))
    wsp = pl.BlockSpec((d, tn), lambda i, j: (0, j))
    return pl.pallas_call(
        _merge_kernel,
        grid=(m // tm, nj),
        in_specs=[row, row, row, row, pl.BlockSpec((1, HEAD_DIM), lambda i, j: (0, 0)), wsp, wsp,
                  pl.BlockSpec((tm, tn), lambda i, j: (i, j)),
                  pl.BlockSpec((tm, tn), lambda i, j: (i, nj + j))],
        out_specs=pl.BlockSpec((tm, tn), lambda i, j: (i, j)),
        out_shape=jax.ShapeDtypeStruct((m, d), BF16),
        scratch_shapes=[pltpu.VMEM((tm, d), BF16)],
        compiler_params=_cparams(("parallel", "arbitrary")),
        name="gated_merge",
    )(att, o_fw, o_bw, g_silu, g_norm.reshape(1, HEAD_DIM), w_oa, w_oh, gates, gates)


def _resid_proj_kernel(x_ref, a_ref, w_ref, o_ref):
    o_ref[...] = x_ref[...] + jnp.dot(a_ref[...], w_ref[...], preferred_element_type=F32)


def resid_proj(x, a, w, tm=1024, tn=512):
    m, d = x.shape
    tm = min(tm, m)
    return pl.pallas_call(
        _resid_proj_kernel,
        grid=(m // tm, d // tn),
        in_specs=[pl.BlockSpec((tm, tn), lambda i, j: (i, j)), pl.BlockSpec((tm, a.shape[1]), lambda i, j: (i, 0)),
                  pl.BlockSpec((a.shape[1], tn), lambda i, j: (0, j))],
        out_specs=pl.BlockSpec((tm, tn), lambda i, j: (i, j)),
        out_shape=jax.ShapeDtypeStruct((m, d), F32),
        compiler_params=_cparams(("parallel", "arbitrary")),
        name="out_proj",
    )(x, a, w)


def _mlp_kernel(x_ref, g_ref, wu_ref, wd_ref, o_ref, h_scr, acc_scr):
    f = pl.program_id(1)

    @pl.when(f == 0)
    def _():
        x = x_ref[...]
        h_scr[...] = _rms(x, g_ref[...]).astype(h_scr.dtype)
        acc_scr[...] = x

    u = jnp.maximum(jnp.dot(h_scr[...], wu_ref[...], preferred_element_type=F32), 0.0)
    acc_scr[...] += jnp.dot((u * u).astype(BF16), wd_ref[...], preferred_element_type=F32)

    @pl.when(f == pl.num_programs(1) - 1)
    def _():
        o_ref[...] = acc_scr[...]


def mlp_block(x, gain, w_up, w_down, tm=512, tf=512):
    m, d = x.shape
    ff = w_up.shape[1]
    tm = min(tm, m)
    return pl.pallas_call(
        _mlp_kernel,
        grid=(m // tm, ff // tf),
        in_specs=[pl.BlockSpec((tm, d), lambda i, f: (i, 0)), pl.BlockSpec((1, d), lambda i, f: (0, 0)),
                  pl.BlockSpec((d, tf), lambda i, f: (0, f)), pl.BlockSpec((tf, d), lambda i, f: (f, 0))],
        out_specs=pl.BlockSpec((tm, d), lambda i, f: (i, 0)),
        out_shape=jax.ShapeDtypeStruct((m, d), F32),
        scratch_shapes=[pltpu.VMEM((tm, d), BF16), pltpu.VMEM((tm, d), F32)],
        compiler_params=_cparams(("parallel", "arbitrary")),
        name="mlp",
    )(x, gain.reshape(1, d), w_up, w_down)


def _ple_kernel(x_ref, g_ref, wg_ref, p_ref, wp_ref, gf_ref, o_ref):
    x = x_ref[...]
    h = _rms(x, g_ref[...]).astype(BF16)
    gate = _sigmoid(jnp.dot(h, wg_ref[...], preferred_element_type=F32))
    emb = jnp.dot(p_ref[...].astype(BF16), wp_ref[...], preferred_element_type=F32)
    o_ref[...] = _rms(x + gate * emb, gf_ref[...])


def ple_final(x, gain, w_gate, p, w_p, g_final, tm=512):
    m, d = x.shape
    c = p.shape[1]
    tm = min(tm, m)
    const = lambda i: (0, 0)
    return pl.pallas_call(
        _ple_kernel,
        grid=(m // tm,),
        in_specs=[pl.BlockSpec((tm, d), lambda i: (i, 0)), pl.BlockSpec((1, d), const),
                  pl.BlockSpec((d, d), const), pl.BlockSpec((tm, c), lambda i: (i, 0)),
                  pl.BlockSpec((c, d), const), pl.BlockSpec((1, d), const)],
        out_specs=pl.BlockSpec((tm, d), lambda i: (i, 0)),
        out_shape=jax.ShapeDtypeStruct((m, d), F32),
        compiler_params=_cparams(("parallel",)),
        name="ple_final",
    )(x, gain.reshape(1, d), w_gate, p, w_p, g_final.reshape(1, d))


def kernel(x, p, g_mix, w_in, g_q, g_k, w_o_attn, hgrn_lb, g_hgrn, w_o_hgrn, w_out, g_mlp, w_up, w_down,
           g_ple, w_ple_gate, w_ple, g_final):
    b, s, d = x.shape
    m = b * s
    depth = w_in.shape[0]
    attn_q = N_Q_HEADS * HEAD_DIM
    attn_kv = N_KV_HEADS * HEAD_DIM
    hk = HGRN_HEADS * HEAD_DIM
    rope = rope_tables(s)
    lb_all = jnp.cumsum(jax.nn.softmax(hgrn_lb.astype(F32), axis=0), axis=0)

    xf = x.reshape(m, d)
    for i in range(depth):
        w = w_in[i].astype(BF16)
        h = rmsnorm_bf16(xf, g_mix[i])
        c0 = 0
        q_a = proj_qk(h, w, c0, attn_q, g_q[i], rope, s, HEAD_DIM ** -0.5); c0 += attn_q
        k_a = proj_qk(h, w, c0, attn_kv, g_k[i], rope, s, 1.0); c0 += attn_kv
        v_a = proj_plain(h, w, c0, attn_kv, None); c0 += attn_kv
        q_r = proj_plain(h, w, c0, hk, "silu"); c0 += hk
        logf = proj_logf(h, w, c0, 2 * hk, lb_all[i]); c0 += 2 * hk
        i_r = proj_plain(h, w, c0, hk, None); c0 += hk
        g_r = proj_plain(h, w, c0, hk, "silu"); c0 += hk
        gates = proj_plain(h, w, c0, 2 * d, "sigmoid"); c0 += 2 * d

        att = flash_gqa(q_a.reshape(b, s, attn_q), k_a.reshape(b, s, attn_kv), v_a.reshape(b, s, attn_kv))
        o_fw, o_bw = hgrn2_bidir(q_r.reshape(b, s, hk), i_r.reshape(b, s, hk), logf.reshape(b, s, 2 * hk))
        mixed = gated_merge(att.reshape(m, attn_q), o_fw.reshape(m, hk), o_bw.reshape(m, hk), g_r, g_hgrn[i],
                            w_o_attn[i].astype(BF16), w_o_hgrn[i].astype(BF16), gates)
        xf = resid_proj(xf, mixed, w_out[i].astype(BF16))
        xf = mlp_block(xf, g_mlp[i], w_up[i].astype(BF16), w_down[i].astype(BF16))
        assert depth == 1
        xf = ple_final(xf, g_ple[i], w_ple_gate[i].astype(BF16), p[i].reshape(m, -1), w_ple[i].astype(BF16),
                       g_final)
    return xf.reshape(b, s, d)
```

```python
import functools

import jax
import jax.numpy as jnp
from jax import lax
from jax.experimental import pallas as pl
from jax.experimental.pallas import tpu as pltpu

F32 = jnp.float32
BF16 = jnp.bfloat16

EPS = 1e-6
LOG2_E = 1.4426950408889634
HEAD_DIM = 128
N_Q_HEADS = 16
N_KV_HEADS = 4
GQA_GROUPS = N_Q_HEADS // N_KV_HEADS
GRID_W = 64
ROPE_THETA = 10000.0
HGRN_HEADS = 16
LANES = 128
SUBLANES = 8
VMEM_LIMIT = 56 * 1024 * 1024


def _cparams(sem):
    return pltpu.CompilerParams(dimension_semantics=sem, vmem_limit_bytes=VMEM_LIMIT)


def _sigmoid(x):
    return 1.0 / (1.0 + jnp.exp(-x))


def _rms(x, gain):
    ms = jnp.mean(x * x, axis=-1, keepdims=True)
    return x * lax.rsqrt(ms + EPS) * gain


def _rmsnorm_kernel(x_ref, g_ref, o_ref):
    o_ref[...] = _rms(x_ref[...], g_ref[...]).astype(o_ref.dtype)


def rmsnorm_bf16(x, gain, tm=512):
    m, d = x.shape
    tm = min(tm, m)
    return pl.pallas_call(
        _rmsnorm_kernel,
        grid=(m // tm,),
        in_specs=[pl.BlockSpec((tm, d), lambda i: (i, 0)), pl.BlockSpec((1, d), lambda i: (0, 0))],
        out_specs=pl.BlockSpec((tm, d), lambda i: (i, 0)),
        out_shape=jax.ShapeDtypeStruct((m, d), BF16),
        compiler_params=_cparams(("parallel",)),
        name="rmsnorm",
    )(x, gain.reshape(1, d))


def _proj_plain_kernel(h_ref, w_ref, o_ref, *, act):
    acc = jnp.dot(h_ref[...], w_ref[...], preferred_element_type=F32)
    if act == "silu":
        acc = acc * _sigmoid(acc)
    elif act == "sigmoid":
        acc = _sigmoid(acc)
    o_ref[...] = acc.astype(o_ref.dtype)


def _proj_logf_kernel(h_ref, w_ref, lb_ref, o_ref):
    acc = jnp.dot(h_ref[...], w_ref[...], preferred_element_type=F32)
    lb = lb_ref[...]
    o_ref[...] = jnp.log(lb + (1.0 - lb) * _sigmoid(acc))


def _proj_qk_kernel(h_ref, w_ref, g_ref, cos_ref, sina_ref, sinb_ref, o_ref, *, scale):
    acc = jnp.dot(h_ref[...], w_ref[...], preferred_element_type=F32)
    g = g_ref[...]
    cos, sina, sinb = cos_ref[...], sina_ref[...], sinb_ref[...]
    for hd in range(acc.shape[1] // HEAD_DIM):
        sl = slice(hd * HEAD_DIM, (hd + 1) * HEAD_DIM)
        y = _rms(acc[:, sl], g)
        y = (y * cos + pltpu.roll(y, HEAD_DIM - 32, 1) * sina + pltpu.roll(y, 32, 1) * sinb)
        o_ref[:, sl] = (y * scale).astype(o_ref.dtype)


def _proj_call(kernel, h, w, col0, ncols, out_dtype, extra=(), extra_specs=(), tm=1024, tn=512):
    m, d = h.shape
    tm = min(tm, m)
    assert col0 % tn == 0 and ncols % tn == 0 and m % tm == 0
    jb = col0 // tn
    return pl.pallas_call(
        kernel,
        grid=(m // tm, ncols // tn),
        in_specs=[pl.BlockSpec((tm, d), lambda i, j: (i, 0)),
                  pl.BlockSpec((d, tn), lambda i, j: (0, jb + j))] + list(extra_specs),
        out_specs=pl.BlockSpec((tm, tn), lambda i, j: (i, j)),
        out_shape=jax.ShapeDtypeStruct((m, ncols), out_dtype),
        compiler_params=_cparams(("parallel", "arbitrary")),
        name="in_proj",
    )(h, w, *extra)


def proj_plain(h, w, col0, ncols, act, out_dtype=BF16):
    return _proj_call(functools.partial(_proj_plain_kernel, act=act), h, w, col0, ncols, out_dtype)


def proj_logf(h, w, col0, ncols, lb, tn=512):
    return _proj_call(_proj_logf_kernel, h, w, col0, ncols, F32, extra=(lb.reshape(1, ncols),),
                      extra_specs=(pl.BlockSpec((1, tn), lambda i, j: (0, j)),), tn=tn)


def proj_qk(h, w, col0, ncols, gain, rope, seq, scale, tm=1024):
    tm = min(tm, seq)
    nsb = seq // tm
    tab = pl.BlockSpec((tm, HEAD_DIM), lambda i, j: (i % nsb, 0))
    return _proj_call(functools.partial(_proj_qk_kernel, scale=scale), h, w, col0, ncols, BF16,
                      extra=(gain.reshape(1, HEAD_DIM),) + tuple(rope),
                      extra_specs=(pl.BlockSpec((1, HEAD_DIM), lambda i, j: (0, 0)), tab, tab, tab), tm=tm)


def rope_tables(seq):
    half = HEAD_DIM // 2
    t = jnp.arange(seq, dtype=jnp.int32)
    row = (t // GRID_W).astype(F32)
    col = (t % GRID_W).astype(F32)
    inv_freq = ROPE_THETA ** (-jnp.arange(0, half, 2, dtype=F32) / half)
    ang_r = row[:, None] * inv_freq[None, :]
    ang_c = col[:, None] * inv_freq[None, :]
    z = jnp.zeros_like(ang_r)
    cos = jnp.concatenate([jnp.cos(ang_r)] * 2 + [jnp.cos(ang_c)] * 2, axis=-1)
    sina = jnp.concatenate([-jnp.sin(ang_r), z, -jnp.sin(ang_c), z], axis=-1)
    sinb = jnp.concatenate([z, jnp.sin(ang_r), z, jnp.sin(ang_c)], axis=-1)
    return cos, sina, sinb


ATTN_TQ = 128
ATTN_TKC = 1024


def _attn_kernel(q0_ref, qa_ref, qb_ref, k_ref, v_ref, o_ref, s_scr, mrun_scr, mcur_scr, acc_scr, *, tkc):
    t = pl.program_id(2)
    nt = pl.num_programs(2)
    tq = qa_ref.shape[1]
    nch = k_ref.shape[1] // tkc
    nsl = tkc // LANES
    ones = jnp.ones((tkc, LANES), BF16)

    def stack(q_ref):
        return jnp.concatenate([q_ref[0, :, g * HEAD_DIM:(g + 1) * HEAD_DIM] for g in range(GQA_GROUPS)], axis=0)

    def pass1(qs, c, slot):
        off = pl.multiple_of(c * tkc, tkc)
        s = lax.dot_general(qs, k_ref[0, pl.ds(off, tkc), :], (((1,), (1,)), ((), ())),
                            preferred_element_type=F32)
        s_scr[slot, c] = s
        m = s[:, :LANES]
        for i in range(1, nsl):
            m = jnp.maximum(m, s[:, i * LANES:(i + 1) * LANES])
        mrun_scr[...] = jnp.maximum(mrun_scr[...], m)

    def pass2(c, slot):
        off = pl.multiple_of(c * tkc, tkc)
        vext = jnp.concatenate([v_ref[0, pl.ds(off, tkc), :], ones], axis=1)
        m = mcur_scr[...]
        p = jnp.concatenate([jnp.exp2(s_scr[slot, c, :, i * LANES:(i + 1) * LANES] - m) for i in range(nsl)],
                            axis=1)
        acc_scr[...] += jnp.dot(p.astype(BF16), vext, preferred_element_type=F32)

    def begin_pass1():
        mrun_scr[...] = jnp.full(mrun_scr.shape, -jnp.inf, F32)

    def end_pass1():
        mcur_scr[...] = jnp.broadcast_to(jnp.max(mrun_scr[...], axis=1, keepdims=True), mcur_scr.shape)
        acc_scr[...] = jnp.zeros(acc_scr.shape, F32)

    def emit(half):
        o = acc_scr[:, :HEAD_DIM] / acc_scr[:, HEAD_DIM:]
        for g in range(GQA_GROUPS):
            o_ref[0, half * tq:(half + 1) * tq, g * HEAD_DIM:(g + 1) * HEAD_DIM] = (
                o[g * tq:(g + 1) * tq].astype(o_ref.dtype))

    @pl.when(t == 0)
    def _():
        qs0 = stack(q0_ref)
        begin_pass1()

        def body_0(c, carry):
            pass1(qs0, c, 0)
            return carry

        lax.fori_loop(0, nch, body_0, 0)
        end_pass1()

    qs = stack(qa_ref)
    begin_pass1()

    def body_x(c, carry):
        pass1(qs, c, 1)
        pass2(c, 0)
        return carry

    lax.fori_loop(0, nch, body_x, 0)
    emit(0)
    end_pass1()

    @pl.when(t < nt - 1)
    def _():
        qs2 = stack(qb_ref)
        begin_pass1()

        def body_y(c, carry):
            pass1(qs2, c, 0)
            pass2(c, 1)
            return carry

        lax.fori_loop(0, nch, body_y, 0)
        emit(1)
        end_pass1()

    @pl.when(t == nt - 1)
    def _():
        def body_z(c, carry):
            pass2(c, 1)
            return carry

        lax.fori_loop(0, nch, body_z, 0)
        emit(1)


def gqa_attention(q, k, v, tq=ATTN_TQ, tkc=ATTN_TKC):
    b, s, _ = q.shape
    tkc = min(tkc, s)
    nq = s // tq
    assert nq % 2 == 0 and s % tkc == 0
    gw = GQA_GROUPS * HEAD_DIM
    rows = GQA_GROUPS * tq
    qblk = (1, tq, gw)
    kvblk = (1, s, HEAD_DIM)
    return pl.pallas_call(
        functools.partial(_attn_kernel, tkc=tkc),
        grid=(b, N_KV_HEADS, nq // 2),
        in_specs=[pl.BlockSpec(qblk, lambda bi, h, t: (bi, 0, h)),
                  pl.BlockSpec(qblk, lambda bi, h, t: (bi, 2 * t + 1, h)),
                  pl.BlockSpec(qblk, lambda bi, h, t: (bi, jnp.minimum(2 * t + 2, nq - 1), h)),
                  pl.BlockSpec(kvblk, lambda bi, h, t: (bi, 0, h)),
                  pl.BlockSpec(kvblk, lambda bi, h, t: (bi, 0, h))],
        out_specs=pl.BlockSpec((1, 2 * tq, gw), lambda bi, h, t: (bi, t, h)),
        out_shape=jax.ShapeDtypeStruct(q.shape, BF16),
        scratch_shapes=[pltpu.VMEM((2, s // tkc, rows, tkc), F32), pltpu.VMEM((rows, LANES), F32),
                        pltpu.VMEM((rows, LANES), F32), pltpu.VMEM((rows, 2 * HEAD_DIM), F32)],
        compiler_params=_cparams(("parallel", "parallel", "arbitrary")),
        name="gqa_attention",
    )(q, q, q, k, v)


HGRN_CHUNK = 128
DIAG = SUBLANES


def _split3(x):
    hi = x.astype(BF16)
    r1 = x - hi.astype(F32)
    mid = r1.astype(BF16)
    lo = (r1 - mid.astype(F32)).astype(BF16)
    return hi, mid, lo


def _group_row(x, group, row):
    c, n = x.shape
    xr = x.reshape(c // group, group, n)
    return jnp.broadcast_to(xr[:, row:row + 1, :], xr.shape).reshape(c, n)


def _hgrn_chunk(q_ref, v_ref, lf_ref, o_ref, st_scr, r0, consts, *, rev):
    tri, ones_red, rowi, level_masks = consts
    c = HGRN_CHUNK
    q = q_ref[0, pl.ds(r0, c), :].astype(F32)
    vb = v_ref[0, pl.ds(r0, c), :]
    v = vb.astype(F32)
    lf = lf_ref[0, pl.ds(r0, c), :]
    k = 1.0 - jnp.exp(lf)

    hi, mid, lo = _split3(lf)
    a = (jnp.dot(tri, hi, preferred_element_type=F32) + jnp.dot(tri, mid, preferred_element_type=F32)
         + jnp.dot(tri, lo, preferred_element_type=F32))
    edge = 0 if rev else c - 1
    a_end = a[edge:edge + 1, :]

    st = st_scr[...]
    o = lax.dot_general((q * jnp.exp(a)).astype(BF16), st.astype(BF16), (((1,), (1,)), ((), ())),
                        preferred_element_type=F32)
    kdec = (k * jnp.exp(a_end - a)).astype(BF16)
    st_scr[...] = st * jnp.exp(a_end) + lax.dot_general(vb, kdec, (((0,), (0,)), ((), ())),
                                                        preferred_element_type=F32)

    p = jnp.zeros((c, c), F32)
    m = DIAG
    for mask in level_masks:
        ref_row = m if rev else m - 1
        e = jnp.exp(-jnp.abs(a - _group_row(a, 2 * m, ref_row)))
        sc = lax.dot_general((q * e).astype(BF16), (k * e).astype(BF16), (((1,), (1,)), ((), ())),
                             preferred_element_type=F32)
        p = p + jnp.where(mask, sc, 0.0)
        m *= 2
    o = o + jnp.dot(p.astype(BF16), vb, preferred_element_type=F32)

    for j in range(DIAG):
        d = a - _group_row(a, DIAG, j)
        valid = (rowi <= j) if rev else (rowi >= j)
        x = jnp.where(valid, q * jnp.exp(jnp.minimum(d, 0.0)) * _group_row(k, DIAG, j), 0.0)
        r = jnp.dot(x.astype(BF16), ones_red, preferred_element_type=F32)
        o = o + r * _group_row(v, DIAG, j)

    o_ref[0, pl.ds(r0, c), :] = o


def _hgrn_kernel(qf_ref, vf_ref, lff_ref, qb_ref, vb_ref, lfb_ref, of_ref, ob_ref, sf_scr, sb_scr):
    @pl.when(pl.program_id(2) == 0)
    def _():
        sf_scr[...] = jnp.zeros(sf_scr.shape, F32)
        sb_scr[...] = jnp.zeros(sb_scr.shape, F32)

    c = HGRN_CHUNK
    ti = lax.broadcasted_iota(jnp.int32, (c, c), 0)
    si = lax.broadcasted_iota(jnp.int32, (c, c), 1)
    rowi = lax.broadcasted_iota(jnp.int32, (c, LANES), 0) % DIAG
    ones_red = jnp.ones((LANES, LANES), BF16)

    def consts(rev):
        tri = ((si >= ti) if rev else (si <= ti)).astype(BF16)
        masks = []
        m = DIAG
        while m < c:
            same = (ti // (2 * m)) == (si // (2 * m))
            t_late = ((ti // m) % 2) == (0 if rev else 1)
            s_early = ((si // m) % 2) == (1 if rev else 0)
            masks.append(same & t_late & s_early)
            m *= 2
        return tri, ones_red, rowi, masks

    cf, cb = consts(False), consts(True)
    n = qf_ref.shape[1] // c

    def body(ci, carry):
        _hgrn_chunk(qf_ref, vf_ref, lff_ref, of_ref, sf_scr, pl.multiple_of(ci * c, c), cf, rev=False)
        _hgrn_chunk(qb_ref, vb_ref, lfb_ref, ob_ref, sb_scr, pl.multiple_of((n - 1 - ci) * c, c), cb, rev=True)
        return carry

    lax.fori_loop(0, n, body, 0)


def hgrn2_bidir(q, v, logf, ts=1024):
    b, s, hk = q.shape
    ts = min(ts, s)
    nt = s // ts
    nh = hk // HEAD_DIM
    blk = (1, ts, HEAD_DIM)
    fwd = lambda bi, h, i: (bi, i, h)
    bwd = lambda bi, h, i: (bi, nt - 1 - i, h)
    out = jax.ShapeDtypeStruct((b, s, hk), F32)
    return pl.pallas_call(
        _hgrn_kernel,
        grid=(b, nh, nt),
        in_specs=[pl.BlockSpec(blk, fwd), pl.BlockSpec(blk, fwd), pl.BlockSpec(blk, fwd),
                  pl.BlockSpec(blk, bwd), pl.BlockSpec(blk, bwd),
                  pl.BlockSpec(blk, lambda bi, h, i: (bi, nt - 1 - i, nh + h))],
        out_specs=[pl.BlockSpec(blk, fwd), pl.BlockSpec(blk, bwd)],
        out_shape=[out, out],
        scratch_shapes=[pltpu.VMEM((HEAD_DIM, HEAD_DIM), F32), pltpu.VMEM((HEAD_DIM, HEAD_DIM), F32)],
        compiler_params=_cparams(("parallel", "parallel", "arbitrary")),
        name="hgrn2",
    )(q, v, logf, q, v, logf)


def _merge_kernel(att_ref, of_ref, ob_ref, gr_ref, gn_ref, woa_ref, woh_ref, sa_ref, sh_ref, o_ref, or_scr):
    @pl.when(pl.program_id(1) == 0)
    def _():
        gn = gn_ref[...]
        for hd in range(or_scr.shape[1] // HEAD_DIM):
            sl = slice(hd * HEAD_DIM, (hd + 1) * HEAD_DIM)
            o = _rms(of_ref[:, sl] + ob_ref[:, sl], gn)
            or_scr[:, sl] = (o * gr_ref[:, sl].astype(F32)).astype(or_scr.dtype)

    ya = jnp.dot(att_ref[...], woa_ref[...], preferred_element_type=F32)
    yh = jnp.dot(or_scr[...], woh_ref[...], preferred_element_type=F32)
    o_ref[...] = (sa_ref[...].astype(F32) * ya + sh_ref[...].astype(F32) * yh).astype(o_ref.dtype)


def gated_merge(att, o_fw, o_bw, g_silu, g_norm, w_oa, w_oh, gates, tm=512, tn=512):
    m, d = att.shape
    tm = min(tm, m)
    nj = d // tn
    row = pl.BlockSpec((tm, d), lambda i, j: (i, 0))
    wsp = pl.BlockSpec((d, tn), lambda i, j: (0, j))
    return pl.pallas_call(
        _merge_kernel,
        grid=(m // tm, nj),
        in_specs=[row, row, row, row, pl.BlockSpec((1, HEAD_DIM), lambda i, j: (0, 0)), wsp, wsp,
                  pl.BlockSpec((tm, tn), lambda i, j: (i, j)),
                  pl.BlockSpec((tm, tn), lambda i, j: (i, nj + j))],
        out_specs=pl.BlockSpec((tm, tn), lambda i, j: (i, j)),
        out_shape=jax.ShapeDtypeStruct((m, d), BF16),
        scratch_shapes=[pltpu.VMEM((tm, d), BF16)],
        compiler_params=_cparams(("parallel", "arbitrary")),
        name="gated_merge",
    )(att, o_fw, o_bw, g_silu, g_norm.reshape(1, HEAD_DIM), w_oa, w_oh, gates, gates)


def _resid_proj_kernel(x_ref, a_ref, w_ref, o_ref):
    o_ref[...] = x_ref[...] + jnp.dot(a_ref[...], w_ref[...], preferred_element_type=F32)


def resid_proj(x, a, w, tm=1024, tn=512):
    m, d = x.shape
    tm = min(tm, m)
    return pl.pallas_call(
        _resid_proj_kernel,
        grid=(m // tm, d // tn),
        in_specs=[pl.BlockSpec((tm, tn), lambda i, j: (i, j)), pl.BlockSpec((tm, a.shape[1]), lambda i, j: (i, 0)),
                  pl.BlockSpec((a.shape[1], tn), lambda i, j: (0, j))],
        out_specs=pl.BlockSpec((tm, tn), lambda i, j: (i, j)),
        out_shape=jax.ShapeDtypeStruct((m, d), F32),
        compiler_params=_cparams(("parallel", "arbitrary")),
        name="out_proj",
    )(x, a, w)


def _mlp_kernel(x_ref, g_ref, wu_ref, wd_ref, o_ref, h_scr, acc_scr):
    f = pl.program_id(1)

    @pl.when(f == 0)
    def _():
        x = x_ref[...]
        h_scr[...] = _rms(x, g_ref[...]).astype(h_scr.dtype)
        acc_scr[...] = x

    u = jnp.maximum(jnp.dot(h_scr[...], wu_ref[...], preferred_element_type=F32), 0.0)
    acc_scr[...] += jnp.dot((u * u).astype(BF16), wd_ref[...], preferred_element_type=F32)

    @pl.when(f == pl.num_programs(1) - 1)
    def _():
        o_ref[...] = acc_scr[...]


def mlp_block(x, gain, w_up, w_down, tm=512, tf=512):
    m, d = x.shape
    ff = w_up.shape[1]
    tm = min(tm, m)
    return pl.pallas_call(
        _mlp_kernel,
        grid=(m // tm, ff // tf),
        in_specs=[pl.BlockSpec((tm, d), lambda i, f: (i, 0)), pl.BlockSpec((1, d), lambda i, f: (0, 0)),
                  pl.BlockSpec((d, tf), lambda i, f: (0, f)), pl.BlockSpec((tf, d), lambda i, f: (f, 0))],
        out_specs=pl.BlockSpec((tm, d), lambda i, f: (i, 0)),
        out_shape=jax.ShapeDtypeStruct((m, d), F32),
        scratch_shapes=[pltpu.VMEM((tm, d), BF16), pltpu.VMEM((tm, d), F32)],
        compiler_params=_cparams(("parallel", "arbitrary")),
        name="mlp",
    )(x, gain.reshape(1, d), w_up, w_down)


def _ple_kernel(x_ref, g_ref, wg_ref, p_ref, wp_ref, gf_ref, o_ref):
    x = x_ref[...]
    h = _rms(x, g_ref[...]).astype(BF16)
    gate = _sigmoid(jnp.dot(h, wg_ref[...], preferred_element_type=F32))
    emb = jnp.dot(p_ref[...].astype(BF16), wp_ref[...], preferred_element_type=F32)
    o_ref[...] = _rms(x + gate * emb, gf_ref[...])


def ple_final(x, gain, w_gate, p, w_p, g_final, tm=512):
    m, d = x.shape
    c = p.shape[1]
    tm = min(tm, m)
    const = lambda i: (0, 0)
    return pl.pallas_call(
        _ple_kernel,
        grid=(m // tm,),
        in_specs=[pl.BlockSpec((tm, d), lambda i: (i, 0)), pl.BlockSpec((1, d), const),
                  pl.BlockSpec((d, d), const), pl.BlockSpec((tm, c), lambda i: (i, 0)),
                  pl.BlockSpec((c, d), const), pl.BlockSpec((1, d), const)],
        out_specs=pl.BlockSpec((tm, d), lambda i: (i, 0)),
        out_shape=jax.ShapeDtypeStruct((m, d), F32),
        compiler_params=_cparams(("parallel",)),
        name="ple_final",
    )(x, gain.reshape(1, d), w_gate, p, w_p, g_final.reshape(1, d))


def kernel(x, p, g_mix, w_in, g_q, g_k, w_o_attn, hgrn_lb, g_hgrn, w_o_hgrn, w_out, g_mlp, w_up, w_down,
           g_ple, w_ple_gate, w_ple, g_final):
    b, s, d = x.shape
    m = b * s
    depth = w_in.shape[0]
    attn_q = N_Q_HEADS * HEAD_DIM
    attn_kv = N_KV_HEADS * HEAD_DIM
    hk = HGRN_HEADS * HEAD_DIM
    rope = rope_tables(s)
    lb_all = jnp.cumsum(jax.nn.softmax(hgrn_lb.astype(F32), axis=0), axis=0)

    xf = x.reshape(m, d)
    for i in range(depth):
        w = w_in[i].astype(BF16)
        h = rmsnorm_bf16(xf, g_mix[i])
        c0 = 0
        q_a = proj_qk(h, w, c0, attn_q, g_q[i], rope, s, HEAD_DIM ** -0.5 * LOG2_E); c0 += attn_q
        k_a = proj_qk(h, w, c0, attn_kv, g_k[i], rope, s, 1.0); c0 += attn_kv
        v_a = proj_plain(h, w, c0, attn_kv, None); c0 += attn_kv
        q_r = proj_plain(h, w, c0, hk, "silu"); c0 += hk
        logf = proj_logf(h, w, c0, 2 * hk, lb_all[i]); c0 += 2 * hk
        i_r = proj_plain(h, w, c0, hk, None); c0 += hk
        g_r = proj_plain(h, w, c0, hk, "silu"); c0 += hk
        gates = proj_plain(h, w, c0, 2 * d, "sigmoid"); c0 += 2 * d

        att = gqa_attention(q_a.reshape(b, s, attn_q), k_a.reshape(b, s, attn_kv), v_a.reshape(b, s, attn_kv))
        o_fw, o_bw = hgrn2_bidir(q_r.reshape(b, s, hk), i_r.reshape(b, s, hk), logf.reshape(b, s, 2 * hk))
        mixed = gated_merge(att.reshape(m, attn_q), o_fw.reshape(m, hk), o_bw.reshape(m, hk), g_r, g_hgrn[i],
                            w_o_attn[i].astype(BF16), w_o_hgrn[i].astype(BF16), gates)
        xf = resid_proj(xf, mixed, w_out[i].astype(BF16))
        xf = mlp_block(xf, g_mlp[i], w_up[i].astype(BF16), w_down[i].astype(BF16))
        assert depth == 1
        xf = ple_final(xf, g_ple[i], w_ple_gate[i].astype(BF16), p[i].reshape(m, -1), w_ple[i].astype(BF16),
                       g_final)
    return xf.reshape(b, s, d)
```

```python
import functools

import jax
import jax.numpy as jnp
from jax import lax
from jax.experimental import pallas as pl
from jax.experimental.pallas import tpu as pltpu

F32 = jnp.float32
BF16 = jnp.bfloat16

EPS = 1e-6
LOG2_E = 1.4426950408889634
HEAD_DIM = 128
N_Q_HEADS = 16
N_KV_HEADS = 4
GQA_GROUPS = N_Q_HEADS // N_KV_HEADS
GRID_W = 64
ROPE_THETA = 10000.0
HGRN_HEADS = 16
LANES = 128
SUBLANES = 8
VMEM_LIMIT = 56 * 1024 * 1024


def _cparams(sem):
    return pltpu.CompilerParams(dimension_semantics=sem, vmem_limit_bytes=VMEM_LIMIT)


def _sigmoid(x):
    return 1.0 / (1.0 + jnp.exp(-x))


def _rms(x, gain):
    ms = jnp.mean(x * x, axis=-1, keepdims=True)
    return x * lax.rsqrt(ms + EPS) * gain


def _rmsnorm_kernel(x_ref, g_ref, o_ref):
    o_ref[...] = _rms(x_ref[...], g_ref[...]).astype(o_ref.dtype)


def rmsnorm_bf16(x, gain, tm=512):
    m, d = x.shape
    tm = min(tm, m)
    return pl.pallas_call(
        _rmsnorm_kernel,
        grid=(m // tm,),
        in_specs=[pl.BlockSpec((tm, d), lambda i: (i, 0)), pl.BlockSpec((1, d), lambda i: (0, 0))],
        out_specs=pl.BlockSpec((tm, d), lambda i: (i, 0)),
        out_shape=jax.ShapeDtypeStruct((m, d), BF16),
        compiler_params=_cparams(("parallel",)),
        name="rmsnorm",
    )(x, gain.reshape(1, d))


def _proj_plain_kernel(h_ref, w_ref, o_ref, *, act):
    acc = jnp.dot(h_ref[...], w_ref[...], preferred_element_type=F32)
    if act == "silu":
        acc = acc * _sigmoid(acc)
    elif act == "sigmoid":
        acc = _sigmoid(acc)
    o_ref[...] = acc.astype(o_ref.dtype)


def _proj_logf_kernel(h_ref, w_ref, lb_ref, o_ref):
    acc = jnp.dot(h_ref[...], w_ref[...], preferred_element_type=F32)
    lb = lb_ref[...]
    o_ref[...] = jnp.log(lb + (1.0 - lb) * _sigmoid(acc)) * LOG2_E


def _proj_qk_kernel(h_ref, w_ref, g_ref, cos_ref, sina_ref, sinb_ref, o_ref, *, scale):
    acc = jnp.dot(h_ref[...], w_ref[...], preferred_element_type=F32)
    g = g_ref[...]
    cos, sina, sinb = cos_ref[...], sina_ref[...], sinb_ref[...]
    for hd in range(acc.shape[1] // HEAD_DIM):
        sl = slice(hd * HEAD_DIM, (hd + 1) * HEAD_DIM)
        y = _rms(acc[:, sl], g)
        y = (y * cos + pltpu.roll(y, HEAD_DIM - 32, 1) * sina + pltpu.roll(y, 32, 1) * sinb)
        o_ref[:, sl] = (y * scale).astype(o_ref.dtype)


def _proj_call(kernel, h, w, col0, ncols, out_dtype, extra=(), extra_specs=(), tm=1024, tn=512):
    m, d = h.shape
    tm = min(tm, m)
    assert col0 % tn == 0 and ncols % tn == 0 and m % tm == 0
    jb = col0 // tn
    return pl.pallas_call(
        kernel,
        grid=(m // tm, ncols // tn),
        in_specs=[pl.BlockSpec((tm, d), lambda i, j: (i, 0)),
                  pl.BlockSpec((d, tn), lambda i, j: (0, jb + j))] + list(extra_specs),
        out_specs=pl.BlockSpec((tm, tn), lambda i, j: (i, j)),
        out_shape=jax.ShapeDtypeStruct((m, ncols), out_dtype),
        compiler_params=_cparams(("parallel", "arbitrary")),
        name="in_proj",
    )(h, w, *extra)


def proj_plain(h, w, col0, ncols, act, out_dtype=BF16):
    return _proj_call(functools.partial(_proj_plain_kernel, act=act), h, w, col0, ncols, out_dtype)


def proj_logf(h, w, col0, ncols, lb, tn=512):
    return _proj_call(_proj_logf_kernel, h, w, col0, ncols, F32, extra=(lb.reshape(1, ncols),),
                      extra_specs=(pl.BlockSpec((1, tn), lambda i, j: (0, j)),), tn=tn)


def proj_qk(h, w, col0, ncols, gain, rope, seq, scale, tm=1024):
    tm = min(tm, seq)
    nsb = seq // tm
    tab = pl.BlockSpec((tm, HEAD_DIM), lambda i, j: (i % nsb, 0))
    return _proj_call(functools.partial(_proj_qk_kernel, scale=scale), h, w, col0, ncols, BF16,
                      extra=(gain.reshape(1, HEAD_DIM),) + tuple(rope),
                      extra_specs=(pl.BlockSpec((1, HEAD_DIM), lambda i, j: (0, 0)), tab, tab, tab), tm=tm)


def rope_tables(seq):
    half = HEAD_DIM // 2
    t = jnp.arange(seq, dtype=jnp.int32)
    row = (t // GRID_W).astype(F32)
    col = (t % GRID_W).astype(F32)
    inv_freq = ROPE_THETA ** (-jnp.arange(0, half, 2, dtype=F32) / half)
    ang_r = row[:, None] * inv_freq[None, :]
    ang_c = col[:, None] * inv_freq[None, :]
    z = jnp.zeros_like(ang_r)
    cos = jnp.concatenate([jnp.cos(ang_r)] * 2 + [jnp.cos(ang_c)] * 2, axis=-1)
    sina = jnp.concatenate([-jnp.sin(ang_r), z, -jnp.sin(ang_c), z], axis=-1)
    sinb = jnp.concatenate([z, jnp.sin(ang_r), z, jnp.sin(ang_c)], axis=-1)
    return cos, sina, sinb


ATTN_TQ = 128
ATTN_TKC = 1024


def _attn_kernel(q0_ref, qa_ref, qb_ref, k_ref, v_ref, o_ref, s_scr, mrun_scr, mcur_scr, acc_scr, *, tkc):
    t = pl.program_id(2)
    nt = pl.num_programs(2)
    tq = qa_ref.shape[1]
    nch = k_ref.shape[1] // tkc
    nsl = tkc // LANES
    ones = jnp.ones((tkc, LANES), BF16)

    def stack(q_ref):
        return jnp.concatenate([q_ref[0, :, g * HEAD_DIM:(g + 1) * HEAD_DIM] for g in range(GQA_GROUPS)], axis=0)

    def pass1(qs, c, slot):
        off = pl.multiple_of(c * tkc, tkc)
        s = lax.dot_general(qs, k_ref[0, pl.ds(off, tkc), :], (((1,), (1,)), ((), ())),
                            preferred_element_type=F32)
        s_scr[slot, c] = s
        m = s[:, :LANES]
        for i in range(1, nsl):
            m = jnp.maximum(m, s[:, i * LANES:(i + 1) * LANES])
        mrun_scr[...] = jnp.maximum(mrun_scr[...], m)

    def pass2(c, slot):
        off = pl.multiple_of(c * tkc, tkc)
        vext = jnp.concatenate([v_ref[0, pl.ds(off, tkc), :], ones], axis=1)
        m = mcur_scr[...]
        p = jnp.concatenate([jnp.exp2(s_scr[slot, c, :, i * LANES:(i + 1) * LANES] - m) for i in range(nsl)],
                            axis=1)
        acc_scr[...] += jnp.dot(p.astype(BF16), vext, preferred_element_type=F32)

    def begin_pass1():
        mrun_scr[...] = jnp.full(mrun_scr.shape, -jnp.inf, F32)

    def end_pass1():
        mcur_scr[...] = jnp.broadcast_to(jnp.max(mrun_scr[...], axis=1, keepdims=True), mcur_scr.shape)
        acc_scr[...] = jnp.zeros(acc_scr.shape, F32)

    def emit(half):
        o = acc_scr[:, :HEAD_DIM] / acc_scr[:, HEAD_DIM:]
        for g in range(GQA_GROUPS):
            o_ref[0, half * tq:(half + 1) * tq, g * HEAD_DIM:(g + 1) * HEAD_DIM] = (
                o[g * tq:(g + 1) * tq].astype(o_ref.dtype))

    @pl.when(t == 0)
    def _():
        qs0 = stack(q0_ref)
        begin_pass1()

        def body_0(c, carry):
            pass1(qs0, c, 0)
            return carry

        lax.fori_loop(0, nch, body_0, 0)
        end_pass1()

    qs = stack(qa_ref)
    begin_pass1()

    def body_x(c, carry):
        pass1(qs, c, 1)
        pass2(c, 0)
        return carry

    lax.fori_loop(0, nch, body_x, 0)
    emit(0)
    end_pass1()

    @pl.when(t < nt - 1)
    def _():
        qs2 = stack(qb_ref)
        begin_pass1()

        def body_y(c, carry):
            pass1(qs2, c, 0)
            pass2(c, 1)
            return carry

        lax.fori_loop(0, nch, body_y, 0)
        emit(1)
        end_pass1()

    @pl.when(t == nt - 1)
    def _():
        def body_z(c, carry):
            pass2(c, 1)
            return carry

        lax.fori_loop(0, nch, body_z, 0)
        emit(1)


def gqa_attention(q, k, v, tq=ATTN_TQ, tkc=ATTN_TKC):
    b, s, _ = q.shape
    tkc = min(tkc, s)
    nq = s // tq
    assert nq % 2 == 0 and s % tkc == 0
    gw = GQA_GROUPS * HEAD_DIM
    rows = GQA_GROUPS * tq
    qblk = (1, tq, gw)
    kvblk = (1, s, HEAD_DIM)
    return pl.pallas_call(
        functools.partial(_attn_kernel, tkc=tkc),
        grid=(b, N_KV_HEADS, nq // 2),
        in_specs=[pl.BlockSpec(qblk, lambda bi, h, t: (bi, 0, h)),
                  pl.BlockSpec(qblk, lambda bi, h, t: (bi, 2 * t + 1, h)),
                  pl.BlockSpec(qblk, lambda bi, h, t: (bi, jnp.minimum(2 * t + 2, nq - 1), h)),
                  pl.BlockSpec(kvblk, lambda bi, h, t: (bi, 0, h)),
                  pl.BlockSpec(kvblk, lambda bi, h, t: (bi, 0, h))],
        out_specs=pl.BlockSpec((1, 2 * tq, gw), lambda bi, h, t: (bi, t, h)),
        out_shape=jax.ShapeDtypeStruct(q.shape, BF16),
        scratch_shapes=[pltpu.VMEM((2, s // tkc, rows, tkc), F32), pltpu.VMEM((rows, LANES), F32),
                        pltpu.VMEM((rows, LANES), F32), pltpu.VMEM((rows, 2 * HEAD_DIM), F32)],
        compiler_params=_cparams(("parallel", "parallel", "arbitrary")),
        name="gqa_attention",
    )(q, q, q, k, v)


HGRN_CHUNK = 128


def _group_row(x, group, row):
    c, n = x.shape
    xr = x.reshape(c // group, group, n)
    return jnp.broadcast_to(xr[:, row:row + 1, :], xr.shape).reshape(c, n)


def _boundary_row(x, m, rev, rowi):
    ref_row = m if rev else m - 1
    if 2 * m >= SUBLANES:
        return _group_row(x, 2 * m, ref_row)
    reps = SUBLANES // (2 * m)
    out = _group_row(x, SUBLANES, ref_row)
    for g in range(1, reps):
        out = jnp.where(rowi >= g * 2 * m, _group_row(x, SUBLANES, g * 2 * m + ref_row), out)
    return out


def _neg_abs(x):
    return pltpu.bitcast(pltpu.bitcast(x, jnp.uint32) | jnp.uint32(0x80000000), F32)


def _nt(x, y):
    return lax.dot_general(x, y, (((1,), (1,)), ((), ())), preferred_element_type=F32)


def _hgrn_chunk(q_ref, v_ref, lf_ref, o_ref, st_scr, r0, col, consts, *, rev):
    tri, rowi, eye, level_masks = consts
    c = HGRN_CHUNK
    cols = slice(col * HEAD_DIM, (col + 1) * HEAD_DIM)
    qb = q_ref[0, pl.ds(r0, c), cols]
    vb = v_ref[0, pl.ds(r0, c), cols]
    lf = lf_ref[0, pl.ds(r0, c), cols]
    k = 1.0 - jnp.exp2(lf)

    hi = lf.astype(BF16)
    lo = (lf - hi.astype(F32)).astype(BF16)
    a2 = jnp.dot(tri, jnp.concatenate([hi, lo], axis=1), preferred_element_type=F32)
    sc = _nt(qb, k.astype(BF16))
    yield
    q = qb.astype(F32)
    a = a2[:, :LANES] + a2[:, LANES:]
    edge = 0 if rev else c - 1
    a_end = a[edge:edge + 1, :]

    st = st_scr[col]
    o = _nt((q * jnp.exp2(a)).astype(BF16), st.astype(BF16))
    kdec = (k * jnp.exp2(a_end - a)).astype(BF16)
    st_new = lax.dot_general(vb, kdec, (((0,), (0,)), ((), ())), preferred_element_type=F32)
    st_decayed = st * jnp.exp2(a_end)

    nstrip = c // SUBLANES
    strip = lambda x, i: x[i * SUBLANES:(i + 1) * SUBLANES]
    p = [None] * nstrip

    def apply(pending):
        mask, scores, rows = pending
        for j, i in enumerate(rows):
            p[i] = jnp.where(strip(mask, i), strip(scores, j), 0.0 if p[i] is None else p[i])

    pending = (eye, sc, list(range(nstrip)))
    m = 1
    for mask in level_masks:
        ref = _boundary_row(a, m, rev, rowi)
        query_first = rev
        if m < SUBLANES:
            is_query = ((rowi // m) % 2) == (0 if query_first else 1)
            z = (jnp.where(is_query, q, k) * jnp.exp2(_neg_abs(a - ref))).astype(BF16)
            sc = _nt(z, z)
            rows = list(range(nstrip))
        else:
            parts, qparts, rows = [], [], []
            for g in range(c // (2 * m)):
                first = slice(g * 2 * m, g * 2 * m + m)
                second = slice(g * 2 * m + m, (g + 1) * 2 * m)
                qs, ks = (first, second) if query_first else (second, first)
                zq = q[qs] * jnp.exp2(a[qs] - ref[qs])
                zk = k[ks] * jnp.exp2(ref[ks] - a[ks])
                parts += [zq, zk] if query_first else [zk, zq]
                qparts.append(zq)
                rows += list(range(qs.start // SUBLANES, qs.stop // SUBLANES))
            z = jnp.concatenate(parts, axis=0).astype(BF16)
            sc = _nt(jnp.concatenate(qparts, axis=0).astype(BF16), z)
        yield
        apply(pending)
        pending = (mask, sc, rows)
        m *= 2
    apply(pending)
    pv = jnp.dot(jnp.concatenate(p, axis=0).astype(BF16), vb, preferred_element_type=F32)
    st_scr[col] = st_decayed + st_new
    yield
    o_ref[0, pl.ds(r0, c), cols] = o + pv


def _interleave(chains):
    chains = list(chains)
    while chains:
        for ch in list(chains):
            try:
                next(ch)
            except StopIteration:
                chains.remove(ch)


def _hgrn_kernel(qf_ref, vf_ref, lff_ref, qb_ref, vb_ref, lfb_ref, of_ref, ob_ref, sf_scr, sb_scr):
    @pl.when(pl.program_id(2) == 0)
    def _():
        sf_scr[...] = jnp.zeros(sf_scr.shape, F32)
        sb_scr[...] = jnp.zeros(sb_scr.shape, F32)

    c = HGRN_CHUNK
    ti = lax.broadcasted_iota(jnp.int32, (c, c), 0)
    si = lax.broadcasted_iota(jnp.int32, (c, c), 1)
    rowi = lax.broadcasted_iota(jnp.int32, (c, LANES), 0) % SUBLANES

    def consts(rev):
        tri = ((si >= ti) if rev else (si <= ti)).astype(BF16)
        masks = []
        m = 1
        while m < c:
            same = (ti // (2 * m)) == (si // (2 * m))
            t_query = ((ti // m) % 2) == (0 if rev else 1)
            s_key = ((si // m) % 2) == (1 if rev else 0)
            masks.append(same & t_query & s_key)
            m *= 2
        return tri, rowi, ti == si, masks

    cf, cb = consts(False), consts(True)
    n = qf_ref.shape[1] // c
    heads = qf_ref.shape[2] // HEAD_DIM

    def body(ci, carry):
        rf = pl.multiple_of(ci * c, c)
        rb = pl.multiple_of((n - 1 - ci) * c, c)
        chains = []
        for hd in range(heads):
            chains.append(_hgrn_chunk(qf_ref, vf_ref, lff_ref, of_ref, sf_scr, rf, hd, cf, rev=False))
            chains.append(_hgrn_chunk(qb_ref, vb_ref, lfb_ref, ob_ref, sb_scr, rb, hd, cb, rev=True))
        _interleave(chains)
        return carry

    lax.fori_loop(0, n, body, 0)


HGRN_HEADS_PER_STEP = 2


def hgrn2_bidir(q, v, logf, ts=1024, hps=HGRN_HEADS_PER_STEP):
    b, s, hk = q.shape
    ts = min(ts, s)
    nt = s // ts
    ng = hk // (hps * HEAD_DIM)
    blk = (1, ts, hps * HEAD_DIM)
    fwd = lambda bi, h, i: (bi, i, h)
    bwd = lambda bi, h, i: (bi, nt - 1 - i, h)
    out = jax.ShapeDtypeStruct((b, s, hk), F32)
    state = pltpu.VMEM((hps, HEAD_DIM, HEAD_DIM), F32)
    return pl.pallas_call(
        _hgrn_kernel,
        grid=(b, ng, nt),
        in_specs=[pl.BlockSpec(blk, fwd), pl.BlockSpec(blk, fwd), pl.BlockSpec(blk, fwd),
                  pl.BlockSpec(blk, bwd), pl.BlockSpec(blk, bwd),
                  pl.BlockSpec(blk, lambda bi, h, i: (bi, nt - 1 - i, ng + h))],
        out_specs=[pl.BlockSpec(blk, fwd), pl.BlockSpec(blk, bwd)],
        out_shape=[out, out],
        scratch_shapes=[state, state],
        compiler_params=_cparams(("parallel", "parallel", "arbitrary")),
        name="hgrn2",
    )(q, v, logf, q, v, logf)


def _merge_kernel(att_ref, of_ref, ob_ref, gr_ref, gn_ref, woa_ref, woh_ref, sa_ref, sh_ref, o_ref, or_scr):
    @pl.when(pl.program_id(1) == 0)
    def _():
        gn = gn_ref[...]
        for hd in range(or_scr.shape[1] // HEAD_DIM):
            sl = slice(hd * HEAD_DIM, (hd + 1) * HEAD_DIM)
            o = _rms(of_ref[:, sl] + ob_ref[:, sl], gn)
            or_scr[:, sl] = (o * gr_ref[:, sl].astype(F32)).astype(or_scr.dtype)

    ya = jnp.dot(att_ref[...], woa_ref[...], preferred_element_type=F32)
    yh = jnp.dot(or_scr[...], woh_ref[...], preferred_element_type=F32)
    o_ref[...] = (sa_ref[...].astype(F32) * ya + sh_ref[...].astype(F32) * yh).astype(o_ref.dtype)


def gated_merge(att, o_fw, o_bw, g_silu, g_norm, w_oa, w_oh, gates, tm=512, tn=512):
    m, d = att.shape
    tm = min(tm, m)
    nj = d // tn
    row = pl.BlockSpec((tm, d), lambda i, j: (i, 0))
    wsp = pl.BlockSpec((d, tn), lambda i, j: (0, j))
    return pl.pallas_call(
        _merge_kernel,
        grid=(m // tm, nj),
        in_specs=[row, row, row, row, pl.BlockSpec((1, HEAD_DIM), lambda i, j: (0, 0)), wsp, wsp,
                  pl.BlockSpec((tm, tn), lambda i, j: (i, j)),
                  pl.BlockSpec((tm, tn), lambda i, j: (i, nj + j))],
        out_specs=pl.BlockSpec((tm, tn), lambda i, j: (i, j)),
        out_shape=jax.ShapeDtypeStruct((m, d), BF16),
        scratch_shapes=[pltpu.VMEM((tm, d), BF16)],
        compiler_params=_cparams(("parallel", "arbitrary")),
        name="gated_merge",
    )(att, o_fw, o_bw, g_silu, g_norm.reshape(1, HEAD_DIM), w_oa, w_oh, gates, gates)


def _resid_proj_kernel(x_ref, a_ref, w_ref, o_ref):
    o_ref[...] = x_ref[...] + jnp.dot(a_ref[...], w_ref[...], preferred_element_type=F32)


def resid_proj(x, a, w, tm=1024, tn=512):
    m, d = x.shape
    tm = min(tm, m)
    return pl.pallas_call(
        _resid_proj_kernel,
        grid=(m // tm, d // tn),
        in_specs=[pl.BlockSpec((tm, tn), lambda i, j: (i, j)), pl.BlockSpec((tm, a.shape[1]), lambda i, j: (i, 0)),
                  pl.BlockSpec((a.shape[1], tn), lambda i, j: (0, j))],
        out_specs=pl.BlockSpec((tm, tn), lambda i, j: (i, j)),
        out_shape=jax.ShapeDtypeStruct((m, d), F32),
        compiler_params=_cparams(("parallel", "arbitrary")),
        name="out_proj",
    )(x, a, w)


def _mlp_kernel(x_ref, g_ref, wu_ref, wd_ref, o_ref, h_scr, acc_scr):
    f = pl.program_id(1)

    @pl.when(f == 0)
    def _():
        x = x_ref[...]
        h_scr[...] = _rms(x, g_ref[...]).astype(h_scr.dtype)
        acc_scr[...] = x

    u = jnp.maximum(jnp.dot(h_scr[...], wu_ref[...], preferred_element_type=F32), 0.0)
    acc_scr[...] += jnp.dot((u * u).astype(BF16), wd_ref[...], preferred_element_type=F32)

    @pl.when(f == pl.num_programs(1) - 1)
    def _():
        o_ref[...] = acc_scr[...]


def mlp_block(x, gain, w_up, w_down, tm=512, tf=512):
    m, d = x.shape
    ff = w_up.shape[1]
    tm = min(tm, m)
    return pl.pallas_call(
        _mlp_kernel,
        grid=(m // tm, ff // tf),
        in_specs=[pl.BlockSpec((tm, d), lambda i, f: (i, 0)), pl.BlockSpec((1, d), lambda i, f: (0, 0)),
                  pl.BlockSpec((d, tf), lambda i, f: (0, f)), pl.BlockSpec((tf, d), lambda i, f: (f, 0))],
        out_specs=pl.BlockSpec((tm, d), lambda i, f: (i, 0)),
        out_shape=jax.ShapeDtypeStruct((m, d), F32),
        scratch_shapes=[pltpu.VMEM((tm, d), BF16), pltpu.VMEM((tm, d), F32)],
        compiler_params=_cparams(("parallel", "arbitrary")),
        name="mlp",
    )(x, gain.reshape(1, d), w_up, w_down)


def _ple_kernel(x_ref, g_ref, wg_ref, p_ref, wp_ref, gf_ref, o_ref):
    x = x_ref[...]
    h = _rms(x, g_ref[...]).astype(BF16)
    gate = _sigmoid(jnp.dot(h, wg_ref[...], preferred_element_type=F32))
    emb = jnp.dot(p_ref[...].astype(BF16), wp_ref[...], preferred_element_type=F32)
    o_ref[...] = _rms(x + gate * emb, gf_ref[...])


def ple_final(x, gain, w_gate, p, w_p, g_final, tm=512):
    m, d = x.shape
    c = p.shape[1]
    tm = min(tm, m)
    const = lambda i: (0, 0)
    return pl.pallas_call(
        _ple_kernel,
        grid=(m // tm,),
        in_specs=[pl.BlockSpec((tm, d), lambda i: (i, 0)), pl.BlockSpec((1, d), const),
                  pl.BlockSpec((d, d), const), pl.BlockSpec((tm, c), lambda i: (i, 0)),
                  pl.BlockSpec((c, d), const), pl.BlockSpec((1, d), const)],
        out_specs=pl.BlockSpec((tm, d), lambda i: (i, 0)),
        out_shape=jax.ShapeDtypeStruct((m, d), F32),
        compiler_params=_cparams(("parallel",)),
        name="ple_final",
    )(x, gain.reshape(1, d), w_gate, p, w_p, g_final.reshape(1, d))


def kernel(x, p, g_mix, w_in, g_q, g_k, w_o_attn, hgrn_lb, g_hgrn, w_o_hgrn, w_out, g_mlp, w_up, w_down,
           g_ple, w_ple_gate, w_ple, g_final):
    b, s, d = x.shape
    m = b * s
    depth = w_in.shape[0]
    attn_q = N_Q_HEADS * HEAD_DIM
    attn_kv = N_KV_HEADS * HEAD_DIM
    hk = HGRN_HEADS * HEAD_DIM
    rope = rope_tables(s)
    lb_all = jnp.cumsum(jax.nn.softmax(hgrn_lb.astype(F32), axis=0), axis=0)

    xf = x.reshape(m, d)
    for i in range(depth):
        w = w_in[i].astype(BF16)
        h = rmsnorm_bf16(xf, g_mix[i])
        c0 = 0
        q_a = proj_qk(h, w, c0, attn_q, g_q[i], rope, s, HEAD_DIM ** -0.5 * LOG2_E); c0 += attn_q
        k_a = proj_qk(h, w, c0, attn_kv, g_k[i], rope, s, 1.0); c0 += attn_kv
        v_a = proj_plain(h, w, c0, attn_kv, None); c0 += attn_kv
        q_r = proj_plain(h, w, c0, hk, "silu"); c0 += hk
        logf = proj_logf(h, w, c0, 2 * hk, lb_all[i]); c0 += 2 * hk
        i_r = proj_plain(h, w, c0, hk, None); c0 += hk
        g_r = proj_plain(h, w, c0, hk, "silu"); c0 += hk
        gates = proj_plain(h, w, c0, 2 * d, "sigmoid"); c0 += 2 * d

        att = gqa_attention(q_a.reshape(b, s, attn_q), k_a.reshape(b, s, attn_kv), v_a.reshape(b, s, attn_kv))
        o_fw, o_bw = hgrn2_bidir(q_r.reshape(b, s, hk), i_r.reshape(b, s, hk), logf.reshape(b, s, 2 * hk))
        mixed = gated_merge(att.reshape(m, attn_q), o_fw.reshape(m, hk), o_bw.reshape(m, hk), g_r, g_hgrn[i],
                            w_o_attn[i].astype(BF16), w_o_hgrn[i].astype(BF16), gates)
        xf = resid_proj(xf, mixed, w_out[i].astype(BF16))
        xf = mlp_block(xf, g_mlp[i], w_up[i].astype(BF16), w_down[i].astype(BF16))
        assert depth == 1
        xf = ple_final(xf, g_ple[i], w_ple_gate[i].astype(BF16), p[i].reshape(m, -1), w_ple[i].astype(BF16),
                       g_final)
    return xf.reshape(b, s, d)
```

```python
import functools

import jax
import jax.numpy as jnp
from jax import lax
from jax.experimental import pallas as pl
from jax.experimental.pallas import tpu as pltpu

F32 = jnp.float32
BF16 = jnp.bfloat16

EPS = 1e-6
LOG2_E = 1.4426950408889634
HEAD_DIM = 128
N_Q_HEADS = 16
N_KV_HEADS = 4
GQA_GROUPS = N_Q_HEADS // N_KV_HEADS
GRID_W = 64
ROPE_THETA = 10000.0
HGRN_HEADS = 16
LANES = 128
SUBLANES = 8
VMEM_LIMIT = 56 * 1024 * 1024


def _cparams(sem):
    return pltpu.CompilerParams(dimension_semantics=sem, vmem_limit_bytes=VMEM_LIMIT)


def _sigmoid(x):
    return 1.0 / (1.0 + jnp.exp(-x))


def _rms(x, gain):
    ms = jnp.mean(x * x, axis=-1, keepdims=True)
    return x * lax.rsqrt(ms + EPS) * gain


def _rmsnorm_kernel(x_ref, g_ref, o_ref):
    o_ref[...] = _rms(x_ref[...], g_ref[...]).astype(o_ref.dtype)


def rmsnorm_bf16(x, gain, tm=512):
    m, d = x.shape
    tm = min(tm, m)
    return pl.pallas_call(
        _rmsnorm_kernel,
        grid=(m // tm,),
        in_specs=[pl.BlockSpec((tm, d), lambda i: (i, 0)), pl.BlockSpec((1, d), lambda i: (0, 0))],
        out_specs=pl.BlockSpec((tm, d), lambda i: (i, 0)),
        out_shape=jax.ShapeDtypeStruct((m, d), BF16),
        compiler_params=_cparams(("parallel",)),
        name="rmsnorm",
    )(x, gain.reshape(1, d))


def _proj_plain_kernel(h_ref, w_ref, o_ref, *, act):
    acc = jnp.dot(h_ref[...], w_ref[...], preferred_element_type=F32)
    if act == "silu":
        acc = acc * _sigmoid(acc)
    elif act == "sigmoid":
        acc = _sigmoid(acc)
    o_ref[...] = acc.astype(o_ref.dtype)


def _proj_logf_kernel(h_ref, w_ref, lb_ref, o_ref):
    acc = jnp.dot(h_ref[...], w_ref[...], preferred_element_type=F32)
    lb = lb_ref[...]
    o_ref[...] = jnp.log(lb + (1.0 - lb) * _sigmoid(acc)) * LOG2_E


QK_SUB = 2 * HEAD_DIM


def _proj_qk_kernel(h_ref, w_ref, cg_ref, sg_ref, o_ref):
    h = h_ref[...]
    cg, sg = cg_ref[...], sg_ref[...]
    ones = jnp.ones((HEAD_DIM, HEAD_DIM), BF16)
    nsub = w_ref.shape[1] // QK_SUB

    def matmul(i):
        return jnp.dot(h, w_ref[:, i * QK_SUB:(i + 1) * QK_SUB], preferred_element_type=F32)

    def epilogue(i, acc):
        for hd in range(QK_SUB // HEAD_DIM):
            x = acc[:, hd * HEAD_DIM:(hd + 1) * HEAD_DIM]
            ssq = jnp.dot((x * x).astype(BF16), ones, preferred_element_type=F32)
            r = lax.rsqrt(ssq * (1.0 / HEAD_DIM) + EPS)
            y = r * (x * cg + pltpu.roll(x, HEAD_DIM // 2, 1) * sg)
            c0 = i * QK_SUB + hd * HEAD_DIM
            o_ref[:, c0:c0 + HEAD_DIM] = y.astype(o_ref.dtype)

    prev = matmul(0)
    for i in range(1, nsub):
        cur = matmul(i)
        epilogue(i - 1, prev)
        prev = cur
    epilogue(nsub - 1, prev)


def _proj_call(kernel, h, w, col0, ncols, out_dtype, extra=(), extra_specs=(), tm=1024, tn=1024):
    m, d = h.shape
    tm = min(tm, m)
    tn = min(tn, ncols)
    assert col0 % tn == 0 and ncols % tn == 0 and m % tm == 0
    jb = col0 // tn
    return pl.pallas_call(
        kernel,
        grid=(m // tm, ncols // tn),
        in_specs=[pl.BlockSpec((tm, d), lambda i, j: (i, 0)),
                  pl.BlockSpec((d, tn), lambda i, j: (0, jb + j))] + list(extra_specs),
        out_specs=pl.BlockSpec((tm, tn), lambda i, j: (i, j)),
        out_shape=jax.ShapeDtypeStruct((m, ncols), out_dtype),
        compiler_params=_cparams(("parallel", "arbitrary")),
        name="in_proj",
    )(h, w, *extra)


def proj_plain(h, w, col0, ncols, act, out_dtype=BF16):
    return _proj_call(functools.partial(_proj_plain_kernel, act=act), h, w, col0, ncols, out_dtype)


def proj_logf(h, w, col0, ncols, lb, tn=1024):
    return _proj_call(_proj_logf_kernel, h, w, col0, ncols, F32, extra=(lb.reshape(1, ncols),),
                      extra_specs=(pl.BlockSpec((1, tn), lambda i, j: (0, j)),), tn=tn)


def proj_qk(h, w, gain, rope, seq, scale, tm=1024, tn=1024):
    cos, sin = rope
    gp = gain.astype(F32)[jnp.array(ROPE_PERM)]
    cg = cos * (gp * scale)[None, :]
    sg = sin * (jnp.roll(gp, HEAD_DIM // 2) * scale)[None, :]
    tm = min(tm, seq)
    tn = min(tn, w.shape[1])
    nsb = seq // tm
    tab = pl.BlockSpec((tm, HEAD_DIM), lambda i, j: (i % nsb, 0))
    return _proj_call(_proj_qk_kernel, h, w, 0, w.shape[1], BF16, extra=(cg, sg), extra_specs=(tab, tab),
                      tm=tm, tn=tn)


ROPE_PERM = tuple(list(range(0, 32)) + list(range(64, 96)) + list(range(32, 64)) + list(range(96, 128)))


def permute_heads(w):
    d, n = w.shape
    return w.reshape(d, n // HEAD_DIM, HEAD_DIM)[:, :, jnp.array(ROPE_PERM)].reshape(d, n)


def rope_tables(seq):
    half = HEAD_DIM // 2
    t = jnp.arange(seq, dtype=jnp.int32)
    row = (t // GRID_W).astype(F32)
    col = (t % GRID_W).astype(F32)
    inv_freq = ROPE_THETA ** (-jnp.arange(0, half, 2, dtype=F32) / half)
    ang = jnp.concatenate([row[:, None] * inv_freq[None, :], col[:, None] * inv_freq[None, :]], axis=-1)
    cos = jnp.concatenate([jnp.cos(ang), jnp.cos(ang)], axis=-1)
    sin = jnp.concatenate([-jnp.sin(ang), jnp.sin(ang)], axis=-1)
    return cos, sin


ATTN_TQ = 128
ATTN_TKC = 8192


def _attn_kernel(q0_ref, qa_ref, qb_ref, k_ref, v_ref, o_ref, s_scr, mrun_scr, mcur_scr, acc_scr, *, tkc):
    t = pl.program_id(2)
    nt = pl.num_programs(2)
    tq = qa_ref.shape[1]
    nch = k_ref.shape[1] // tkc
    nsl = tkc // LANES
    ones = jnp.ones((tkc, LANES), BF16)

    def stack(q_ref):
        return jnp.concatenate([q_ref[0, :, g * HEAD_DIM:(g + 1) * HEAD_DIM] for g in range(GQA_GROUPS)], axis=0)

    def pass1(qs, c, slot):
        off = pl.multiple_of(c * tkc, tkc)
        s = lax.dot_general(qs, k_ref[0, pl.ds(off, tkc), :], (((1,), (1,)), ((), ())),
                            preferred_element_type=F32)
        s_scr[slot, c] = s
        m = s[:, :LANES]
        for i in range(1, nsl):
            m = jnp.maximum(m, s[:, i * LANES:(i + 1) * LANES])
        mrun_scr[...] = jnp.maximum(mrun_scr[...], m)

    def pass2(c, slot):
        off = pl.multiple_of(c * tkc, tkc)
        vext = jnp.concatenate([v_ref[0, pl.ds(off, tkc), :], ones], axis=1)
        m = mcur_scr[...]
        p = jnp.concatenate([jnp.exp2(s_scr[slot, c, :, i * LANES:(i + 1) * LANES] - m) for i in range(nsl)],
                            axis=1)
        acc_scr[...] += jnp.dot(p.astype(BF16), vext, preferred_element_type=F32)

    def begin_pass1():
        mrun_scr[...] = jnp.full(mrun_scr.shape, -jnp.inf, F32)

    def end_pass1():
        mcur_scr[...] = jnp.broadcast_to(jnp.max(mrun_scr[...], axis=1, keepdims=True), mcur_scr.shape)
        acc_scr[...] = jnp.zeros(acc_scr.shape, F32)

    def emit(half):
        o = acc_scr[:, :HEAD_DIM] / acc_scr[:, HEAD_DIM:]
        for g in range(GQA_GROUPS):
            o_ref[0, half * tq:(half + 1) * tq, g * HEAD_DIM:(g + 1) * HEAD_DIM] = (
                o[g * tq:(g + 1) * tq].astype(o_ref.dtype))

    @pl.when(t == 0)
    def _():
        qs0 = stack(q0_ref)
        begin_pass1()

        def body_0(c, carry):
            pass1(qs0, c, 0)
            return carry

        lax.fori_loop(0, nch, body_0, 0)
        end_pass1()

    qs = stack(qa_ref)
    begin_pass1()

    def body_x(c, carry):
        pass1(qs, c, 1)
        pass2(c, 0)
        return carry

    lax.fori_loop(0, nch, body_x, 0)
    emit(0)
    end_pass1()

    @pl.when(t < nt - 1)
    def _():
        qs2 = stack(qb_ref)
        begin_pass1()

        def body_y(c, carry):
            pass1(qs2, c, 0)
            pass2(c, 1)
            return carry

        lax.fori_loop(0, nch, body_y, 0)
        emit(1)
        end_pass1()

    @pl.when(t == nt - 1)
    def _():
        def body_z(c, carry):
            pass2(c, 1)
            return carry

        lax.fori_loop(0, nch, body_z, 0)
        emit(1)


def gqa_attention(q, k, v, tq=ATTN_TQ, tkc=ATTN_TKC):
    b, s, _ = q.shape
    tkc = min(tkc, s)
    nq = s // tq
    assert nq % 2 == 0 and s % tkc == 0
    gw = GQA_GROUPS * HEAD_DIM
    rows = GQA_GROUPS * tq
    qblk = (1, tq, gw)
    kvblk = (1, s, HEAD_DIM)
    return pl.pallas_call(
        functools.partial(_attn_kernel, tkc=tkc),
        grid=(b, N_KV_HEADS, nq // 2),
        in_specs=[pl.BlockSpec(qblk, lambda bi, h, t: (bi, 0, h)),
                  pl.BlockSpec(qblk, lambda bi, h, t: (bi, 2 * t + 1, h)),
                  pl.BlockSpec(qblk, lambda bi, h, t: (bi, jnp.minimum(2 * t + 2, nq - 1), h)),
                  pl.BlockSpec(kvblk, lambda bi, h, t: (bi, 0, h)),
                  pl.BlockSpec(kvblk, lambda bi, h, t: (bi, 0, h))],
        out_specs=pl.BlockSpec((1, 2 * tq, gw), lambda bi, h, t: (bi, t, h)),
        out_shape=jax.ShapeDtypeStruct(q.shape, BF16),
        scratch_shapes=[pltpu.VMEM((2, s // tkc, rows, tkc), F32), pltpu.VMEM((rows, LANES), F32),
                        pltpu.VMEM((rows, LANES), F32), pltpu.VMEM((rows, 2 * HEAD_DIM), F32)],
        compiler_params=_cparams(("parallel", "parallel", "arbitrary")),
        name="gqa_attention",
    )(q, q, q, k, v)


HGRN_CHUNK = 128


def _group_row(x, group, row):
    c, n = x.shape
    xr = x.reshape(c // group, group, n)
    return jnp.broadcast_to(xr[:, row:row + 1, :], xr.shape).reshape(c, n)


def _boundary_row(x, m, rev, rowi):
    ref_row = m if rev else m - 1
    if 2 * m >= SUBLANES:
        return _group_row(x, 2 * m, ref_row)
    reps = SUBLANES // (2 * m)
    out = _group_row(x, SUBLANES, ref_row)
    for g in range(1, reps):
        out = jnp.where(rowi >= g * 2 * m, _group_row(x, SUBLANES, g * 2 * m + ref_row), out)
    return out


def _neg_abs(x):
    return pltpu.bitcast(pltpu.bitcast(x, jnp.uint32) | jnp.uint32(0x80000000), F32)


def _nt(x, y):
    return lax.dot_general(x, y, (((1,), (1,)), ((), ())), preferred_element_type=F32)


def _hgrn_chunk(q_ref, v_ref, lf_ref, o_ref, st_scr, r0, col, consts, *, rev):
    tri, rowi, eye, level_masks = consts
    c = HGRN_CHUNK
    cols = slice(col * HEAD_DIM, (col + 1) * HEAD_DIM)
    qb = q_ref[0, pl.ds(r0, c), cols]
    vb = v_ref[0, pl.ds(r0, c), cols]
    lf = lf_ref[0, pl.ds(r0, c), cols]
    k = 1.0 - jnp.exp2(lf)

    hi = lf.astype(BF16)
    lo = (lf - hi.astype(F32)).astype(BF16)
    a2 = jnp.dot(tri, jnp.concatenate([hi, lo], axis=1), preferred_element_type=F32)
    sc = _nt(qb, k.astype(BF16))
    yield
    q = qb.astype(F32)
    a = a2[:, :LANES] + a2[:, LANES:]
    edge = 0 if rev else c - 1
    a_end = a[edge:edge + 1, :]

    st = st_scr[col]
    o = _nt((q * jnp.exp2(a)).astype(BF16), st.astype(BF16))
    kdec = (k * jnp.exp2(a_end - a)).astype(BF16)
    st_new = lax.dot_general(vb, kdec, (((0,), (0,)), ((), ())), preferred_element_type=F32)
    st_decayed = st * jnp.exp2(a_end)

    nstrip = c // SUBLANES
    strip = lambda x, i: x[i * SUBLANES:(i + 1) * SUBLANES]
    p = [None] * nstrip

    def apply(pending):
        mask, scores, rows = pending
        for j, i in enumerate(rows):
            p[i] = jnp.where(strip(mask, i), strip(scores, j), 0.0 if p[i] is None else p[i])

    pending = (eye, sc, list(range(nstrip)))
    m = 1
    for mask in level_masks:
        ref = _boundary_row(a, m, rev, rowi)
        query_first = rev
        if m < SUBLANES:
            is_query = ((rowi // m) % 2) == (0 if query_first else 1)
            z = (jnp.where(is_query, q, k) * jnp.exp2(_neg_abs(a - ref))).astype(BF16)
            sc = _nt(z, z)
            rows = list(range(nstrip))
        else:
            parts, qparts, rows = [], [], []
            for g in range(c // (2 * m)):
                first = slice(g * 2 * m, g * 2 * m + m)
                second = slice(g * 2 * m + m, (g + 1) * 2 * m)
                qs, ks = (first, second) if query_first else (second, first)
                zq = q[qs] * jnp.exp2(a[qs] - ref[qs])
                zk = k[ks] * jnp.exp2(ref[ks] - a[ks])
                parts += [zq, zk] if query_first else [zk, zq]
                qparts.append(zq)
                rows += list(range(qs.start // SUBLANES, qs.stop // SUBLANES))
            z = jnp.concatenate(parts, axis=0).astype(BF16)
            sc = _nt(jnp.concatenate(qparts, axis=0).astype(BF16), z)
        yield
        apply(pending)
        pending = (mask, sc, rows)
        m *= 2
    apply(pending)
    pv = jnp.dot(jnp.concatenate(p, axis=0).astype(BF16), vb, preferred_element_type=F32)
    st_scr[col] = st_decayed + st_new
    yield
    o_ref[0, pl.ds(r0, c), cols] = o + pv


def _interleave(chains):
    chains = list(chains)
    while chains:
        for ch in list(chains):
            try:
                next(ch)
            except StopIteration:
                chains.remove(ch)


def _hgrn_kernel(qf_ref, vf_ref, lff_ref, qb_ref, vb_ref, lfb_ref, of_ref, ob_ref, sf_scr, sb_scr):
    @pl.when(pl.program_id(2) == 0)
    def _():
        sf_scr[...] = jnp.zeros(sf_scr.shape, F32)
        sb_scr[...] = jnp.zeros(sb_scr.shape, F32)

    c = HGRN_CHUNK
    ti = lax.broadcasted_iota(jnp.int32, (c, c), 0)
    si = lax.broadcasted_iota(jnp.int32, (c, c), 1)
    rowi = lax.broadcasted_iota(jnp.int32, (c, LANES), 0) % SUBLANES

    def consts(rev):
        tri = ((si >= ti) if rev else (si <= ti)).astype(BF16)
        masks = []
        m = 1
        while m < c:
            same = (ti // (2 * m)) == (si // (2 * m))
            t_query = ((ti // m) % 2) == (0 if rev else 1)
            s_key = ((si // m) % 2) == (1 if rev else 0)
            masks.append(same & t_query & s_key)
            m *= 2
        return tri, rowi, ti == si, masks

    cf, cb = consts(False), consts(True)
    n = qf_ref.shape[1] // c
    heads = qf_ref.shape[2] // HEAD_DIM

    def body(ci, carry):
        rf = pl.multiple_of(ci * c, c)
        rb = pl.multiple_of((n - 1 - ci) * c, c)
        chains = []
        for hd in range(heads):
            chains.append(_hgrn_chunk(qf_ref, vf_ref, lff_ref, of_ref, sf_scr, rf, hd, cf, rev=False))
            chains.append(_hgrn_chunk(qb_ref, vb_ref, lfb_ref, ob_ref, sb_scr, rb, hd, cb, rev=True))
        _interleave(chains)
        return carry

    lax.fori_loop(0, n, body, 0)


HGRN_HEADS_PER_STEP = 2


def hgrn2_bidir(q, v, logf, ts=1024, hps=HGRN_HEADS_PER_STEP):
    b, s, hk = q.shape
    ts = min(ts, s)
    nt = s // ts
    ng = hk // (hps * HEAD_DIM)
    blk = (1, ts, hps * HEAD_DIM)
    fwd = lambda bi, h, i: (bi, i, h)
    bwd = lambda bi, h, i: (bi, nt - 1 - i, h)
    out = jax.ShapeDtypeStruct((b, s, hk), F32)
    state = pltpu.VMEM((hps, HEAD_DIM, HEAD_DIM), F32)
    return pl.pallas_call(
        _hgrn_kernel,
        grid=(b, ng, nt),
        in_specs=[pl.BlockSpec(blk, fwd), pl.BlockSpec(blk, fwd), pl.BlockSpec(blk, fwd),
                  pl.BlockSpec(blk, bwd), pl.BlockSpec(blk, bwd),
                  pl.BlockSpec(blk, lambda bi, h, i: (bi, nt - 1 - i, ng + h))],
        out_specs=[pl.BlockSpec(blk, fwd), pl.BlockSpec(blk, bwd)],
        out_shape=[out, out],
        scratch_shapes=[state, state],
        compiler_params=_cparams(("parallel", "parallel", "arbitrary")),
        name="hgrn2",
    )(q, v, logf, q, v, logf)


def _merge_kernel(att_ref, of_ref, ob_ref, gr_ref, gn_ref, woa_ref, woh_ref, sa_ref, sh_ref, o_ref, or_scr):
    @pl.when(pl.program_id(1) == 0)
    def _():
        gn = gn_ref[...]
        for hd in range(or_scr.shape[1] // HEAD_DIM):
            sl = slice(hd * HEAD_DIM, (hd + 1) * HEAD_DIM)
            o = _rms(of_ref[:, sl] + ob_ref[:, sl], gn)
            or_scr[:, sl] = (o * gr_ref[:, sl].astype(F32)).astype(or_scr.dtype)

    ya = jnp.dot(att_ref[...], woa_ref[...], preferred_element_type=F32)
    yh = jnp.dot(or_scr[...], woh_ref[...], preferred_element_type=F32)
    o_ref[...] = (sa_ref[...].astype(F32) * ya + sh_ref[...].astype(F32) * yh).astype(o_ref.dtype)


def gated_merge(att, o_fw, o_bw, g_silu, g_norm, w_oa, w_oh, gates, tm=512, tn=512):
    m, d = att.shape
    tm = min(tm, m)
    nj = d // tn
    row = pl.BlockSpec((tm, d), lambda i, j: (i, 0))
    wsp = pl.BlockSpec((d, tn), lambda i, j: (0, j))
    return pl.pallas_call(
        _merge_kernel,
        grid=(m // tm, nj),
        in_specs=[row, row, row, row, pl.BlockSpec((1, HEAD_DIM), lambda i, j: (0, 0)), wsp, wsp,
                  pl.BlockSpec((tm, tn), lambda i, j: (i, j)),
                  pl.BlockSpec((tm, tn), lambda i, j: (i, nj + j))],
        out_specs=pl.BlockSpec((tm, tn), lambda i, j: (i, j)),
        out_shape=jax.ShapeDtypeStruct((m, d), BF16),
        scratch_shapes=[pltpu.VMEM((tm, d), BF16)],
        compiler_params=_cparams(("parallel", "arbitrary")),
        name="gated_merge",
    )(att, o_fw, o_bw, g_silu, g_norm.reshape(1, HEAD_DIM), w_oa, w_oh, gates, gates)


def _resid_proj_kernel(x_ref, a_ref, w_ref, o_ref):
    o_ref[...] = x_ref[...] + jnp.dot(a_ref[...], w_ref[...], preferred_element_type=F32)


def resid_proj(x, a, w, tm=1024, tn=512):
    m, d = x.shape
    tm = min(tm, m)
    return pl.pallas_call(
        _resid_proj_kernel,
        grid=(m // tm, d // tn),
        in_specs=[pl.BlockSpec((tm, tn), lambda i, j: (i, j)), pl.BlockSpec((tm, a.shape[1]), lambda i, j: (i, 0)),
                  pl.BlockSpec((a.shape[1], tn), lambda i, j: (0, j))],
        out_specs=pl.BlockSpec((tm, tn), lambda i, j: (i, j)),
        out_shape=jax.ShapeDtypeStruct((m, d), F32),
        compiler_params=_cparams(("parallel", "arbitrary")),
        name="out_proj",
    )(x, a, w)


def _mlp_kernel(x_ref, g_ref, wu_ref, wd_ref, o_ref, h_scr, acc_scr):
    f = pl.program_id(1)

    @pl.when(f == 0)
    def _():
        x = x_ref[...]
        h_scr[...] = _rms(x, g_ref[...]).astype(h_scr.dtype)
        acc_scr[...] = x

    u = jnp.maximum(jnp.dot(h_scr[...], wu_ref[...], preferred_element_type=F32), 0.0)
    acc_scr[...] += jnp.dot((u * u).astype(BF16), wd_ref[...], preferred_element_type=F32)

    @pl.when(f == pl.num_programs(1) - 1)
    def _():
        o_ref[...] = acc_scr[...]


def mlp_block(x, gain, w_up, w_down, tm=512, tf=512):
    m, d = x.shape
    ff = w_up.shape[1]
    tm = min(tm, m)
    return pl.pallas_call(
        _mlp_kernel,
        grid=(m // tm, ff // tf),
        in_specs=[pl.BlockSpec((tm, d), lambda i, f: (i, 0)), pl.BlockSpec((1, d), lambda i, f: (0, 0)),
                  pl.BlockSpec((d, tf), lambda i, f: (0, f)), pl.BlockSpec((tf, d), lambda i, f: (f, 0))],
        out_specs=pl.BlockSpec((tm, d), lambda i, f: (i, 0)),
        out_shape=jax.ShapeDtypeStruct((m, d), F32),
        scratch_shapes=[pltpu.VMEM((tm, d), BF16), pltpu.VMEM((tm, d), F32)],
        compiler_params=_cparams(("parallel", "arbitrary")),
        name="mlp",
    )(x, gain.reshape(1, d), w_up, w_down)


def _ple_kernel(x_ref, g_ref, wg_ref, p_ref, wp_ref, gf_ref, o_ref):
    x = x_ref[...]
    h = _rms(x, g_ref[...]).astype(BF16)
    gate = _sigmoid(jnp.dot(h, wg_ref[...], preferred_element_type=F32))
    emb = jnp.dot(p_ref[...].astype(BF16), wp_ref[...], preferred_element_type=F32)
    o_ref[...] = _rms(x + gate * emb, gf_ref[...])


def ple_final(x, gain, w_gate, p, w_p, g_final, tm=512):
    m, d = x.shape
    c = p.shape[1]
    tm = min(tm, m)
    const = lambda i: (0, 0)
    return pl.pallas_call(
        _ple_kernel,
        grid=(m // tm,),
        in_specs=[pl.BlockSpec((tm, d), lambda i: (i, 0)), pl.BlockSpec((1, d), const),
                  pl.BlockSpec((d, d), const), pl.BlockSpec((tm, c), lambda i: (i, 0)),
                  pl.BlockSpec((c, d), const), pl.BlockSpec((1, d), const)],
        out_specs=pl.BlockSpec((tm, d), lambda i: (i, 0)),
        out_shape=jax.ShapeDtypeStruct((m, d), F32),
        compiler_params=_cparams(("parallel",)),
        name="ple_final",
    )(x, gain.reshape(1, d), w_gate, p, w_p, g_final.reshape(1, d))


def kernel(x, p, g_mix, w_in, g_q, g_k, w_o_attn, hgrn_lb, g_hgrn, w_o_hgrn, w_out, g_mlp, w_up, w_down,
           g_ple, w_ple_gate, w_ple, g_final):
    b, s, d = x.shape
    m = b * s
    depth = w_in.shape[0]
    attn_q = N_Q_HEADS * HEAD_DIM
    attn_kv = N_KV_HEADS * HEAD_DIM
    hk = HGRN_HEADS * HEAD_DIM
    rope = rope_tables(s)
    lb_all = jnp.cumsum(jax.nn.softmax(hgrn_lb.astype(F32), axis=0), axis=0)

    xf = x.reshape(m, d)
    for i in range(depth):
        w = w_in[i].astype(BF16)
        h = rmsnorm_bf16(xf, g_mix[i])
        c0 = 0
        w_qk = permute_heads(w[:, :attn_q + attn_kv])
        q_a = proj_qk(h, w_qk[:, :attn_q], g_q[i], rope, s, HEAD_DIM ** -0.5 * LOG2_E); c0 += attn_q
        k_a = proj_qk(h, w_qk[:, attn_q:], g_k[i], rope, s, 1.0); c0 += attn_kv
        v_a = proj_plain(h, w, c0, attn_kv, None); c0 += attn_kv
        q_r = proj_plain(h, w, c0, hk, "silu"); c0 += hk
        logf = proj_logf(h, w, c0, 2 * hk, lb_all[i]); c0 += 2 * hk
        i_r = proj_plain(h, w, c0, hk, None); c0 += hk
        g_r = proj_plain(h, w, c0, hk, "silu"); c0 += hk
        gates = proj_plain(h, w, c0, 2 * d, "sigmoid"); c0 += 2 * d

        att = gqa_attention(q_a.reshape(b, s, attn_q), k_a.reshape(b, s, attn_kv), v_a.reshape(b, s, attn_kv))
        o_fw, o_bw = hgrn2_bidir(q_r.reshape(b, s, hk), i_r.reshape(b, s, hk), logf.reshape(b, s, 2 * hk))
        mixed = gated_merge(att.reshape(m, attn_q), o_fw.reshape(m, hk), o_bw.reshape(m, hk), g_r, g_hgrn[i],
                            w_o_attn[i].astype(BF16), w_o_hgrn[i].astype(BF16), gates)
        xf = resid_proj(xf, mixed, w_out[i].astype(BF16))
        xf = mlp_block(xf, g_mlp[i], w_up[i].astype(BF16), w_down[i].astype(BF16))
        assert depth == 1
        xf = ple_final(xf, g_ple[i], w_ple_gate[i].astype(BF16), p[i].reshape(m, -1), w_ple[i].astype(BF16),
                       g_final)
    return xf.reshape(b, s, d)
```

```python
import functools

import jax
import jax.numpy as jnp
from jax import lax
from jax.experimental import pallas as pl
from jax.experimental.pallas import tpu as pltpu

F32 = jnp.float32
BF16 = jnp.bfloat16

EPS = 1e-6
LOG2_E = 1.4426950408889634
HEAD_DIM = 128
N_Q_HEADS = 16
N_KV_HEADS = 4
GQA_GROUPS = N_Q_HEADS // N_KV_HEADS
GRID_W = 64
ROPE_THETA = 10000.0
HGRN_HEADS = 16
LANES = 128
SUBLANES = 8
VMEM_LIMIT = 56 * 1024 * 1024


def _cparams(sem):
    return pltpu.CompilerParams(dimension_semantics=sem, vmem_limit_bytes=VMEM_LIMIT)


def _sigmoid(x):
    return 1.0 / (1.0 + jnp.exp(-x))


def _rms(x, gain):
    ms = jnp.mean(x * x, axis=-1, keepdims=True)
    return x * lax.rsqrt(ms + EPS) * gain


def _rmsnorm_kernel(x_ref, g_ref, o_ref):
    o_ref[...] = _rms(x_ref[...], g_ref[...]).astype(o_ref.dtype)


def rmsnorm_bf16(x, gain, tm=512):
    m, d = x.shape
    tm = min(tm, m)
    return pl.pallas_call(
        _rmsnorm_kernel,
        grid=(m // tm,),
        in_specs=[pl.BlockSpec((tm, d), lambda i: (i, 0)), pl.BlockSpec((1, d), lambda i: (0, 0))],
        out_specs=pl.BlockSpec((tm, d), lambda i: (i, 0)),
        out_shape=jax.ShapeDtypeStruct((m, d), BF16),
        compiler_params=_cparams(("parallel",)),
        name="rmsnorm",
    )(x, gain.reshape(1, d))


def _proj_plain_kernel(h_ref, w_ref, o_ref, *, act):
    acc = jnp.dot(h_ref[...], w_ref[...], preferred_element_type=F32)
    if act == "silu":
        acc = acc * _sigmoid(acc)
    elif act == "sigmoid":
        acc = _sigmoid(acc)
    o_ref[...] = acc.astype(o_ref.dtype)


def _proj_logf_kernel(h_ref, w_ref, lb_ref, o_ref):
    acc = jnp.dot(h_ref[...], w_ref[...], preferred_element_type=F32)
    lb = lb_ref[...]
    o_ref[...] = jnp.log(lb + (1.0 - lb) * _sigmoid(acc)) * LOG2_E


QK_SUB = 2 * HEAD_DIM


def _proj_qk_kernel(h_ref, w_ref, cg_ref, sg_ref, o_ref):
    h = h_ref[...]
    cg, sg = cg_ref[...], sg_ref[...]
    ones = jnp.ones((HEAD_DIM, HEAD_DIM), BF16)
    nsub = w_ref.shape[1] // QK_SUB

    def matmul(i):
        return jnp.dot(h, w_ref[:, i * QK_SUB:(i + 1) * QK_SUB], preferred_element_type=F32)

    def epilogue(i, acc):
        for hd in range(QK_SUB // HEAD_DIM):
            x = acc[:, hd * HEAD_DIM:(hd + 1) * HEAD_DIM]
            ssq = jnp.dot((x * x).astype(BF16), ones, preferred_element_type=F32)
            r = lax.rsqrt(ssq * (1.0 / HEAD_DIM) + EPS)
            y = r * (x * cg + pltpu.roll(x, HEAD_DIM // 2, 1) * sg)
            c0 = i * QK_SUB + hd * HEAD_DIM
            o_ref[:, c0:c0 + HEAD_DIM] = y.astype(o_ref.dtype)

    prev = matmul(0)
    for i in range(1, nsub):
        cur = matmul(i)
        epilogue(i - 1, prev)
        prev = cur
    epilogue(nsub - 1, prev)


def _proj_call(kernel, h, w, col0, ncols, out_dtype, extra=(), extra_specs=(), tm=1024, tn=1024):
    m, d = h.shape
    tm = min(tm, m)
    tn = min(tn, ncols)
    assert col0 % tn == 0 and ncols % tn == 0 and m % tm == 0
    jb = col0 // tn
    return pl.pallas_call(
        kernel,
        grid=(m // tm, ncols // tn),
        in_specs=[pl.BlockSpec((tm, d), lambda i, j: (i, 0)),
                  pl.BlockSpec((d, tn), lambda i, j: (0, jb + j))] + list(extra_specs),
        out_specs=pl.BlockSpec((tm, tn), lambda i, j: (i, j)),
        out_shape=jax.ShapeDtypeStruct((m, ncols), out_dtype),
        compiler_params=_cparams(("parallel", "arbitrary")),
        name="in_proj",
    )(h, w, *extra)


def proj_plain(h, w, col0, ncols, act, out_dtype=BF16):
    return _proj_call(functools.partial(_proj_plain_kernel, act=act), h, w, col0, ncols, out_dtype)


def proj_logf(h, w, col0, ncols, lb, tn=1024):
    return _proj_call(_proj_logf_kernel, h, w, col0, ncols, F32, extra=(lb.reshape(1, ncols),),
                      extra_specs=(pl.BlockSpec((1, tn), lambda i, j: (0, j)),), tn=tn)


def proj_qk(h, w, gain, rope, seq, scale, tm=1024, tn=1024):
    cos, sin = rope
    gp = gain.astype(F32)[jnp.array(ROPE_PERM)]
    cg = cos * (gp * scale)[None, :]
    sg = sin * (jnp.roll(gp, HEAD_DIM // 2) * scale)[None, :]
    tm = min(tm, seq)
    tn = min(tn, w.shape[1])
    nsb = seq // tm
    tab = pl.BlockSpec((tm, HEAD_DIM), lambda i, j: (i % nsb, 0))
    return _proj_call(_proj_qk_kernel, h, w, 0, w.shape[1], BF16, extra=(cg, sg), extra_specs=(tab, tab),
                      tm=tm, tn=tn)


ROPE_PERM = tuple(list(range(0, 32)) + list(range(64, 96)) + list(range(32, 64)) + list(range(96, 128)))


def permute_heads(w):
    d, n = w.shape
    return w.reshape(d, n // HEAD_DIM, HEAD_DIM)[:, :, jnp.array(ROPE_PERM)].reshape(d, n)


def rope_tables(seq):
    half = HEAD_DIM // 2
    t = jnp.arange(seq, dtype=jnp.int32)
    row = (t // GRID_W).astype(F32)
    col = (t % GRID_W).astype(F32)
    inv_freq = ROPE_THETA ** (-jnp.arange(0, half, 2, dtype=F32) / half)
    ang = jnp.concatenate([row[:, None] * inv_freq[None, :], col[:, None] * inv_freq[None, :]], axis=-1)
    cos = jnp.concatenate([jnp.cos(ang), jnp.cos(ang)], axis=-1)
    sin = jnp.concatenate([-jnp.sin(ang), jnp.sin(ang)], axis=-1)
    return cos, sin


ATTN_TQ = 128
ATTN_TKC = 8192


def _attn_kernel(q0_ref, qa_ref, qb_ref, k_ref, v_ref, o_ref, s_scr, mrun_scr, mcur_scr, acc_scr, *, tkc):
    t = pl.program_id(2)
    nt = pl.num_programs(2)
    tq = qa_ref.shape[1]
    nch = k_ref.shape[1] // tkc
    nsl = tkc // LANES
    ones = jnp.ones((tkc, LANES), BF16)

    def stack(q_ref):
        return jnp.concatenate([q_ref[0, :, g * HEAD_DIM:(g + 1) * HEAD_DIM] for g in range(GQA_GROUPS)], axis=0)

    def pass1(qs, c, slot):
        off = pl.multiple_of(c * tkc, tkc)
        s = lax.dot_general(qs, k_ref[0, pl.ds(off, tkc), :], (((1,), (1,)), ((), ())),
                            preferred_element_type=F32)
        s_scr[slot, c] = s
        m = s[:, :LANES]
        for i in range(1, nsl):
            m = jnp.maximum(m, s[:, i * LANES:(i + 1) * LANES])
        mrun_scr[...] = jnp.maximum(mrun_scr[...], m)

    def pass2(c, slot):
        off = pl.multiple_of(c * tkc, tkc)
        vext = jnp.concatenate([v_ref[0, pl.ds(off, tkc), :], ones], axis=1)
        m = mcur_scr[...]
        p = jnp.concatenate([jnp.exp2(s_scr[slot, c, :, i * LANES:(i + 1) * LANES] - m) for i in range(nsl)],
                            axis=1)
        acc_scr[...] += jnp.dot(p.astype(BF16), vext, preferred_element_type=F32)

    def begin_pass1():
        mrun_scr[...] = jnp.full(mrun_scr.shape, -jnp.inf, F32)

    def end_pass1():
        mcur_scr[...] = jnp.broadcast_to(jnp.max(mrun_scr[...], axis=1, keepdims=True), mcur_scr.shape)
        acc_scr[...] = jnp.zeros(acc_scr.shape, F32)

    def emit(half):
        o = acc_scr[:, :HEAD_DIM] / acc_scr[:, HEAD_DIM:]
        for g in range(GQA_GROUPS):
            o_ref[0, half * tq:(half + 1) * tq, g * HEAD_DIM:(g + 1) * HEAD_DIM] = (
                o[g * tq:(g + 1) * tq].astype(o_ref.dtype))

    @pl.when(t == 0)
    def _():
        qs0 = stack(q0_ref)
        begin_pass1()

        def body_0(c, carry):
            pass1(qs0, c, 0)
            return carry

        lax.fori_loop(0, nch, body_0, 0)
        end_pass1()

    qs = stack(qa_ref)
    begin_pass1()

    def body_x(c, carry):
        pass1(qs, c, 1)
        pass2(c, 0)
        return carry

    lax.fori_loop(0, nch, body_x, 0)
    emit(0)
    end_pass1()

    @pl.when(t < nt - 1)
    def _():
        qs2 = stack(qb_ref)
        begin_pass1()

        def body_y(c, carry):
            pass1(qs2, c, 0)
            pass2(c, 1)
            return carry

        lax.fori_loop(0, nch, body_y, 0)
        emit(1)
        end_pass1()

    @pl.when(t == nt - 1)
    def _():
        def body_z(c, carry):
            pass2(c, 1)
            return carry

        lax.fori_loop(0, nch, body_z, 0)
        emit(1)


def gqa_attention(q, k, v, tq=ATTN_TQ, tkc=ATTN_TKC):
    b, s, _ = q.shape
    tkc = min(tkc, s)
    nq = s // tq
    assert nq % 2 == 0 and s % tkc == 0
    gw = GQA_GROUPS * HEAD_DIM
    rows = GQA_GROUPS * tq
    qblk = (1, tq, gw)
    kvblk = (1, s, HEAD_DIM)
    return pl.pallas_call(
        functools.partial(_attn_kernel, tkc=tkc),
        grid=(b, N_KV_HEADS, nq // 2),
        in_specs=[pl.BlockSpec(qblk, lambda bi, h, t: (bi, 0, h)),
                  pl.BlockSpec(qblk, lambda bi, h, t: (bi, 2 * t + 1, h)),
                  pl.BlockSpec(qblk, lambda bi, h, t: (bi, jnp.minimum(2 * t + 2, nq - 1), h)),
                  pl.BlockSpec(kvblk, lambda bi, h, t: (bi, 0, h)),
                  pl.BlockSpec(kvblk, lambda bi, h, t: (bi, 0, h))],
        out_specs=pl.BlockSpec((1, 2 * tq, gw), lambda bi, h, t: (bi, t, h)),
        out_shape=jax.ShapeDtypeStruct(q.shape, BF16),
        scratch_shapes=[pltpu.VMEM((2, s // tkc, rows, tkc), F32), pltpu.VMEM((rows, LANES), F32),
                        pltpu.VMEM((rows, LANES), F32), pltpu.VMEM((rows, 2 * HEAD_DIM), F32)],
        compiler_params=_cparams(("parallel", "parallel", "arbitrary")),
        name="gqa_attention",
    )(q, q, q, k, v)


HGRN_CHUNK = 128
HGRN_LAG = 2


def _group_row(x, group, row):
    c, n = x.shape
    xr = x.reshape(c // group, group, n)
    return jnp.broadcast_to(xr[:, row:row + 1, :], xr.shape).reshape(c, n)


def _boundary_row(x, m, rev, odd_block):
    ref_row = m if rev else m - 1
    if 2 * m >= SUBLANES:
        return _group_row(x, 2 * m, ref_row)
    if m == 1:
        c, n = x.shape
        x3 = x.reshape(c // SUBLANES, SUBLANES, n)
        other = pltpu.roll(x3, SUBLANES - 1 if rev else 1, 1).reshape(c, n)
        return jnp.where(odd_block(1), x, other) if rev else jnp.where(odd_block(1), other, x)
    assert 4 * m == SUBLANES
    return jnp.where(odd_block(2 * m), _group_row(x, SUBLANES, 2 * m + ref_row), _group_row(x, SUBLANES, ref_row))


def hgrn_constants():
    import numpy as np
    c = HGRN_CHUNK
    t = np.arange(c)[:, None]
    s = np.arange(c)[None, :]
    pair, tri = [], []
    for rev in (False, True):
        masks = [t == s]
        m = 1
        while m < c:
            same = (t // (2 * m)) == (s // (2 * m))
            t_query = ((t // m) % 2) == (0 if rev else 1)
            s_key = ((s // m) % 2) == (1 if rev else 0)
            masks.append(same & t_query & s_key)
            m *= 2
        pair.append(np.stack(masks))
        tri.append((s >= t) if rev else (s <= t))
    rows = np.stack([np.broadcast_to(((t // n) % 2) == 1, (c, LANES)) for n in (1, 2, 4)])
    return (jnp.asarray(np.stack(pair), jnp.int32), jnp.asarray(rows, jnp.int32),
            jnp.asarray(np.stack(tri), BF16))


def _neg_abs(x):
    return pltpu.bitcast(pltpu.bitcast(x, jnp.uint32) | jnp.uint32(0x80000000), F32)


def _nt(x, y):
    return lax.dot_general(x, y, (((1,), (1,)), ((), ())), preferred_element_type=F32)


def _hgrn_chunk(q_ref, v_ref, lf_ref, o_ref, st_scr, r0, col, consts, *, rev):
    pair_ref, rows_ref, tri_ref = consts
    d = 1 if rev else 0
    tri = tri_ref[d]
    odd_block = lambda n: rows_ref[{1: 0, 2: 1, 4: 2}[n]] != 0
    c = HGRN_CHUNK
    cols = slice(col * HEAD_DIM, (col + 1) * HEAD_DIM)
    qb = q_ref[0, pl.ds(r0, c), cols]
    vb = v_ref[0, pl.ds(r0, c), cols]
    lf = lf_ref[0, pl.ds(r0, c), cols]
    k = 1.0 - jnp.exp2(lf)

    hi = lf.astype(BF16)
    lo = (lf - hi.astype(F32)).astype(BF16)
    a2 = jnp.dot(tri, jnp.concatenate([hi, lo], axis=1), preferred_element_type=F32)
    sc = _nt(qb, k.astype(BF16))
    yield
    q = qb.astype(F32)
    a = a2[:, :LANES] + a2[:, LANES:]
    edge = 0 if rev else c - 1
    a_end = a[edge:edge + 1, :]

    st = st_scr[col]
    o = _nt((q * jnp.exp2(a)).astype(BF16), st.astype(BF16))
    kdec = (k * jnp.exp2(a_end - a)).astype(BF16)
    st_new = lax.dot_general(vb, kdec, (((0,), (0,)), ((), ())), preferred_element_type=F32)
    st_decayed = st * jnp.exp2(a_end)

    nstrip = c // SUBLANES
    strip = lambda x, i: x[i * SUBLANES:(i + 1) * SUBLANES]
    p = [None] * nstrip

    def apply(pending):
        level, scores, rows = pending
        for j, i in enumerate(rows):
            mask = pair_ref[d, level, i * SUBLANES:(i + 1) * SUBLANES, :] != 0
            p[i] = jnp.where(mask, strip(scores, j), 0.0 if p[i] is None else p[i])

    pending = [(0, sc, list(range(nstrip)))]
    m = 1
    for level in range(1, pair_ref.shape[1]):
        ref = _boundary_row(a, m, rev, odd_block)
        query_first = rev
        if m < SUBLANES:
            qk = jnp.where(odd_block(m), k, q) if query_first else jnp.where(odd_block(m), q, k)
            z = (qk * jnp.exp2(_neg_abs(a - ref))).astype(BF16)
            sc = _nt(z, z)
            rows = list(range(nstrip))
        else:
            parts, qparts, rows = [], [], []
            for g in range(c // (2 * m)):
                first = slice(g * 2 * m, g * 2 * m + m)
                second = slice(g * 2 * m + m, (g + 1) * 2 * m)
                qs, ks = (first, second) if query_first else (second, first)
                zq = q[qs] * jnp.exp2(a[qs] - ref[qs])
                zk = k[ks] * jnp.exp2(ref[ks] - a[ks])
                parts += [zq, zk] if query_first else [zk, zq]
                qparts.append(zq)
                rows += list(range(qs.start // SUBLANES, qs.stop // SUBLANES))
            z = jnp.concatenate(parts, axis=0).astype(BF16)
            sc = _nt(jnp.concatenate(qparts, axis=0).astype(BF16), z)
        yield
        pending.append((level, sc, rows))
        if len(pending) > HGRN_LAG:
            apply(pending.pop(0))
        m *= 2
    for item in pending:
        apply(item)
    pv = jnp.dot(jnp.concatenate(p, axis=0).astype(BF16), vb, preferred_element_type=F32)
    st_scr[col] = st_decayed + st_new
    yield
    o_ref[0, pl.ds(r0, c), cols] = o + pv


def _interleave(chains):
    chains = list(chains)
    while chains:
        for ch in list(chains):
            try:
                next(ch)
            except StopIteration:
                chains.remove(ch)


def _hgrn_kernel(qf_ref, vf_ref, lff_ref, qb_ref, vb_ref, lfb_ref, pair_ref, rows_ref, tri_ref,
                 of_ref, ob_ref, sf_scr, sb_scr):
    @pl.when(pl.program_id(2) == 0)
    def _():
        sf_scr[...] = jnp.zeros(sf_scr.shape, F32)
        sb_scr[...] = jnp.zeros(sb_scr.shape, F32)

    c = HGRN_CHUNK
    cf = cb = (pair_ref, rows_ref, tri_ref)
    n = qf_ref.shape[1] // c
    heads = qf_ref.shape[2] // HEAD_DIM

    def body(ci, carry):
        rf = pl.multiple_of(ci * c, c)
        rb = pl.multiple_of((n - 1 - ci) * c, c)
        chains = []
        for hd in range(heads):
            chains.append(_hgrn_chunk(qf_ref, vf_ref, lff_ref, of_ref, sf_scr, rf, hd, cf, rev=False))
            chains.append(_hgrn_chunk(qb_ref, vb_ref, lfb_ref, ob_ref, sb_scr, rb, hd, cb, rev=True))
        _interleave(chains)
        return carry

    lax.fori_loop(0, n, body, 0)


HGRN_HEADS_PER_STEP = 4


def hgrn2_bidir(q, v, logf, ts=1024, hps=HGRN_HEADS_PER_STEP):
    b, s, hk = q.shape
    ts = min(ts, s)
    nt = s // ts
    ng = hk // (hps * HEAD_DIM)
    blk = (1, ts, hps * HEAD_DIM)
    fwd = lambda bi, h, i: (bi, i, h)
    bwd = lambda bi, h, i: (bi, nt - 1 - i, h)
    out = jax.ShapeDtypeStruct((b, s, hk), F32)
    state = pltpu.VMEM((hps, HEAD_DIM, HEAD_DIM), F32)
    consts = hgrn_constants()
    whole = lambda x: pl.BlockSpec(x.shape, lambda bi, h, i: (0,) * x.ndim)
    return pl.pallas_call(
        _hgrn_kernel,
        grid=(b, ng, nt),
        in_specs=[pl.BlockSpec(blk, fwd), pl.BlockSpec(blk, fwd), pl.BlockSpec(blk, fwd),
                  pl.BlockSpec(blk, bwd), pl.BlockSpec(blk, bwd),
                  pl.BlockSpec(blk, lambda bi, h, i: (bi, nt - 1 - i, ng + h))] + [whole(x) for x in consts],
        out_specs=[pl.BlockSpec(blk, fwd), pl.BlockSpec(blk, bwd)],
        out_shape=[out, out],
        scratch_shapes=[state, state],
        compiler_params=_cparams(("parallel", "parallel", "arbitrary")),
        name="hgrn2",
    )(q, v, logf, q, v, logf, *consts)


def _merge_kernel(att_ref, of_ref, ob_ref, gr_ref, gn_ref, woa_ref, woh_ref, sa_ref, sh_ref, o_ref, or_scr):
    @pl.when(pl.program_id(1) == 0)
    def _():
        gn = gn_ref[...]
        for hd in range(or_scr.shape[1] // HEAD_DIM):
            sl = slice(hd * HEAD_DIM, (hd + 1) * HEAD_DIM)
            o = _rms(of_ref[:, sl] + ob_ref[:, sl], gn)
            or_scr[:, sl] = (o * gr_ref[:, sl].astype(F32)).astype(or_scr.dtype)

    ya = jnp.dot(att_ref[...], woa_ref[...], preferred_element_type=F32)
    yh = jnp.dot(or_scr[...], woh_ref[...], preferred_element_type=F32)
    o_ref[...] = (sa_ref[...].astype(F32) * ya + sh_ref[...].astype(F32) * yh).astype(o_ref.dtype)


def gated_merge(att, o_fw, o_bw, g_silu, g_norm, w_oa, w_oh, gates, tm=512, tn=512):
    m, d = att.shape
    tm = min(tm, m)
    nj = d // tn
    row = pl.BlockSpec((tm, d), lambda i, j: (i, 0))
    wsp = pl.BlockSpec((d, tn), lambda i, j: (0, j))
    return pl.pallas_call(
        _merge_kernel,
        grid=(m // tm, nj),
        in_specs=[row, row, row, row, pl.BlockSpec((1, HEAD_DIM), lambda i, j: (0, 0)), wsp, wsp,
                  pl.BlockSpec((tm, tn), lambda i, j: (i, j)),
                  pl.BlockSpec((tm, tn), lambda i, j: (i, nj + j))],
        out_specs=pl.BlockSpec((tm, tn), lambda i, j: (i, j)),
        out_shape=jax.ShapeDtypeStruct((m, d), BF16),
        scratch_shapes=[pltpu.VMEM((tm, d), BF16)],
        compiler_params=_cparams(("parallel", "arbitrary")),
        name="gated_merge",
    )(att, o_fw, o_bw, g_silu, g_norm.reshape(1, HEAD_DIM), w_oa, w_oh, gates, gates)


def _resid_proj_kernel(x_ref, a_ref, w_ref, o_ref):
    o_ref[...] = x_ref[...] + jnp.dot(a_ref[...], w_ref[...], preferred_element_type=F32)


def resid_proj(x, a, w, tm=1024, tn=512):
    m, d = x.shape
    tm = min(tm, m)
    return pl.pallas_call(
        _resid_proj_kernel,
        grid=(m // tm, d // tn),
        in_specs=[pl.BlockSpec((tm, tn), lambda i, j: (i, j)), pl.BlockSpec((tm, a.shape[1]), lambda i, j: (i, 0)),
                  pl.BlockSpec((a.shape[1], tn), lambda i, j: (0, j))],
        out_specs=pl.BlockSpec((tm, tn), lambda i, j: (i, j)),
        out_shape=jax.ShapeDtypeStruct((m, d), F32),
        compiler_params=_cparams(("parallel", "arbitrary")),
        name="out_proj",
    )(x, a, w)


def _mlp_kernel(x_ref, g_ref, wu_ref, wd_ref, o_ref, h_scr, acc_scr):
    f = pl.program_id(1)

    @pl.when(f == 0)
    def _():
        x = x_ref[...]
        h_scr[...] = _rms(x, g_ref[...]).astype(h_scr.dtype)
        acc_scr[...] = x

    u = jnp.maximum(jnp.dot(h_scr[...], wu_ref[...], preferred_element_type=F32), 0.0)
    acc_scr[...] += jnp.dot((u * u).astype(BF16), wd_ref[...], preferred_element_type=F32)

    @pl.when(f == pl.num_programs(1) - 1)
    def _():
        o_ref[...] = acc_scr[...]


def mlp_block(x, gain, w_up, w_down, tm=512, tf=512):
    m, d = x.shape
    ff = w_up.shape[1]
    tm = min(tm, m)
    return pl.pallas_call(
        _mlp_kernel,
        grid=(m // tm, ff // tf),
        in_specs=[pl.BlockSpec((tm, d), lambda i, f: (i, 0)), pl.BlockSpec((1, d), lambda i, f: (0, 0)),
                  pl.BlockSpec((d, tf), lambda i, f: (0, f)), pl.BlockSpec((tf, d), lambda i, f: (f, 0))],
        out_specs=pl.BlockSpec((tm, d), lambda i, f: (i, 0)),
        out_shape=jax.ShapeDtypeStruct((m, d), F32),
        scratch_shapes=[pltpu.VMEM((tm, d), BF16), pltpu.VMEM((tm, d), F32)],
        compiler_params=_cparams(("parallel", "arbitrary")),
        name="mlp",
    )(x, gain.reshape(1, d), w_up, w_down)


def _ple_kernel(x_ref, g_ref, wg_ref, p_ref, wp_ref, gf_ref, o_ref):
    x = x_ref[...]
    h = _rms(x, g_ref[...]).astype(BF16)
    gate = _sigmoid(jnp.dot(h, wg_ref[...], preferred_element_type=F32))
    emb = jnp.dot(p_ref[...].astype(BF16), wp_ref[...], preferred_element_type=F32)
    o_ref[...] = _rms(x + gate * emb, gf_ref[...])


def ple_final(x, gain, w_gate, p, w_p, g_final, tm=512):
    m, d = x.shape
    c = p.shape[1]
    tm = min(tm, m)
    const = lambda i: (0, 0)
    return pl.pallas_call(
        _ple_kernel,
        grid=(m // tm,),
        in_specs=[pl.BlockSpec((tm, d), lambda i: (i, 0)), pl.BlockSpec((1, d), const),
                  pl.BlockSpec((d, d), const), pl.BlockSpec((tm, c), lambda i: (i, 0)),
                  pl.BlockSpec((c, d), const), pl.BlockSpec((1, d), const)],
        out_specs=pl.BlockSpec((tm, d), lambda i: (i, 0)),
        out_shape=jax.ShapeDtypeStruct((m, d), F32),
        compiler_params=_cparams(("parallel",)),
        name="ple_final",
    )(x, gain.reshape(1, d), w_gate, p, w_p, g_final.reshape(1, d))


def kernel(x, p, g_mix, w_in, g_q, g_k, w_o_attn, hgrn_lb, g_hgrn, w_o_hgrn, w_out, g_mlp, w_up, w_down,
           g_ple, w_ple_gate, w_ple, g_final):
    b, s, d = x.shape
    m = b * s
    depth = w_in.shape[0]
    attn_q = N_Q_HEADS * HEAD_DIM
    attn_kv = N_KV_HEADS * HEAD_DIM
    hk = HGRN_HEADS * HEAD_DIM
    rope = rope_tables(s)
    lb_all = jnp.cumsum(jax.nn.softmax(hgrn_lb.astype(F32), axis=0), axis=0)

    xf = x.reshape(m, d)
    for i in range(depth):
        w = w_in[i].astype(BF16)
        h = rmsnorm_bf16(xf, g_mix[i])
        c0 = 0
        w_qk = permute_heads(w[:, :attn_q + attn_kv])
        q_a = proj_qk(h, w_qk[:, :attn_q], g_q[i], rope, s, HEAD_DIM ** -0.5 * LOG2_E); c0 += attn_q
        k_a = proj_qk(h, w_qk[:, attn_q:], g_k[i], rope, s, 1.0); c0 += attn_kv
        v_a = proj_plain(h, w, c0, attn_kv, None); c0 += attn_kv
        q_r = proj_plain(h, w, c0, hk, "silu"); c0 += hk
        logf = proj_logf(h, w, c0, 2 * hk, lb_all[i]); c0 += 2 * hk
        i_r = proj_plain(h, w, c0, hk, None); c0 += hk
        g_r = proj_plain(h, w, c0, hk, "silu"); c0 += hk
        gates = proj_plain(h, w, c0, 2 * d, "sigmoid"); c0 += 2 * d

        att = gqa_attention(q_a.reshape(b, s, attn_q), k_a.reshape(b, s, attn_kv), v_a.reshape(b, s, attn_kv))
        o_fw, o_bw = hgrn2_bidir(q_r.reshape(b, s, hk), i_r.reshape(b, s, hk), logf.reshape(b, s, 2 * hk))
        mixed = gated_merge(att.reshape(m, attn_q), o_fw.reshape(m, hk), o_bw.reshape(m, hk), g_r, g_hgrn[i],
                            w_o_attn[i].astype(BF16), w_o_hgrn[i].astype(BF16), gates)
        xf = resid_proj(xf, mixed, w_out[i].astype(BF16))
        xf = mlp_block(xf, g_mlp[i], w_up[i].astype(BF16), w_down[i].astype(BF16))
        assert depth == 1
        xf = ple_final(xf, g_ple[i], w_ple_gate[i].astype(BF16), p[i].reshape(m, -1), w_ple[i].astype(BF16),
                       g_final)
    return xf.reshape(b, s, d)
```

```python
import functools

import jax
import jax.numpy as jnp
from jax import lax
from jax.experimental import pallas as pl
from jax.experimental.pallas import tpu as pltpu

F32 = jnp.float32
BF16 = jnp.bfloat16

EPS = 1e-6
LOG2_E = 1.4426950408889634
HEAD_DIM = 128
N_Q_HEADS = 16
N_KV_HEADS = 4
GQA_GROUPS = N_Q_HEADS // N_KV_HEADS
GRID_W = 64
ROPE_THETA = 10000.0
HGRN_HEADS = 16
LANES = 128
SUBLANES = 8
VMEM_LIMIT = 56 * 1024 * 1024


def _cparams(sem):
    return pltpu.CompilerParams(dimension_semantics=sem, vmem_limit_bytes=VMEM_LIMIT)


def _sigmoid(x):
    return 1.0 / (1.0 + jnp.exp(-x))


def _rms(x, gain):
    ms = jnp.mean(x * x, axis=-1, keepdims=True)
    return x * lax.rsqrt(ms + EPS) * gain


def _rmsnorm_kernel(x_ref, g_ref, o_ref):
    o_ref[...] = _rms(x_ref[...], g_ref[...]).astype(o_ref.dtype)


def rmsnorm_bf16(x, gain, tm=512):
    m, d = x.shape
    tm = min(tm, m)
    return pl.pallas_call(
        _rmsnorm_kernel,
        grid=(m // tm,),
        in_specs=[pl.BlockSpec((tm, d), lambda i: (i, 0)), pl.BlockSpec((1, d), lambda i: (0, 0))],
        out_specs=pl.BlockSpec((tm, d), lambda i: (i, 0)),
        out_shape=jax.ShapeDtypeStruct((m, d), BF16),
        compiler_params=_cparams(("parallel",)),
        name="rmsnorm",
    )(x, gain.reshape(1, d))


QK_SUB = 2 * HEAD_DIM


MM_ROWS = 512


def _subtiled_matmul(h_ref, w_ref, epilogue):
    tm = h_ref.shape[0]
    rows_per = min(MM_ROWS, tm)
    units = [(slice(r, r + rows_per), slice(c0, c0 + QK_SUB))
             for c0 in range(0, w_ref.shape[1], QK_SUB) for r in range(0, tm, rows_per)]
    matmul = lambda u: jnp.dot(h_ref[u[0], :], w_ref[:, u[1]], preferred_element_type=F32)
    prev = matmul(units[0])
    for j in range(1, len(units)):
        cur = matmul(units[j])
        epilogue(prev, *units[j - 1])
        prev = cur
    epilogue(prev, *units[-1])


def _proj_plain_kernel(h_ref, w_ref, o_ref, *, act):
    def epilogue(acc, rs, cs):
        if act == "silu":
            acc = acc * _sigmoid(acc)
        elif act == "sigmoid":
            acc = _sigmoid(acc)
        o_ref[rs, cs] = acc.astype(o_ref.dtype)

    _subtiled_matmul(h_ref, w_ref, epilogue)


def _proj_logf_kernel(h_ref, w_ref, lb_ref, o_ref):
    def epilogue(acc, rs, cs):
        lb = lb_ref[:, cs]
        o_ref[rs, cs] = jnp.log(lb + (1.0 - lb) * _sigmoid(acc)) * LOG2_E

    _subtiled_matmul(h_ref, w_ref, epilogue)


def _proj_qk_kernel(h_ref, w_ref, cg_ref, sg_ref, o_ref):
    ones = jnp.ones((HEAD_DIM, HEAD_DIM), BF16)

    def epilogue(acc, rs, cs):
        cg, sg = cg_ref[rs, :], sg_ref[rs, :]
        for hd in range(QK_SUB // HEAD_DIM):
            x = acc[:, hd * HEAD_DIM:(hd + 1) * HEAD_DIM]
            ssq = jnp.dot((x * x).astype(BF16), ones, preferred_element_type=F32)
            r = lax.rsqrt(ssq * (1.0 / HEAD_DIM) + EPS)
            y = r * (x * cg + pltpu.roll(x, HEAD_DIM // 2, 1) * sg)
            c0 = cs.start + hd * HEAD_DIM
            o_ref[rs, c0:c0 + HEAD_DIM] = y.astype(o_ref.dtype)

    _subtiled_matmul(h_ref, w_ref, epilogue)


def _proj_call(kernel, h, w, col0, ncols, out_dtype, extra=(), extra_specs=(), tm=1024, tn=1024):
    m, d = h.shape
    tm = min(tm, m)
    tn = min(tn, ncols)
    assert col0 % tn == 0 and ncols % tn == 0 and m % tm == 0
    jb = col0 // tn
    return pl.pallas_call(
        kernel,
        grid=(m // tm, ncols // tn),
        in_specs=[pl.BlockSpec((tm, d), lambda i, j: (i, 0)),
                  pl.BlockSpec((d, tn), lambda i, j: (0, jb + j))] + list(extra_specs),
        out_specs=pl.BlockSpec((tm, tn), lambda i, j: (i, j)),
        out_shape=jax.ShapeDtypeStruct((m, ncols), out_dtype),
        compiler_params=_cparams(("parallel", "arbitrary")),
        name="in_proj",
    )(h, w, *extra)


def proj_plain(h, w, col0, ncols, act, out_dtype=BF16):
    return _proj_call(functools.partial(_proj_plain_kernel, act=act), h, w, col0, ncols, out_dtype)


def proj_logf(h, w, col0, ncols, lb, tn=1024):
    return _proj_call(_proj_logf_kernel, h, w, col0, ncols, F32, extra=(lb.reshape(1, ncols),),
                      extra_specs=(pl.BlockSpec((1, tn), lambda i, j: (0, j)),), tn=tn)


def proj_qk(h, w, gain, rope, seq, scale, tm=1024, tn=1024):
    cos, sin = rope
    gp = gain.astype(F32)[jnp.array(ROPE_PERM)]
    cg = cos * (gp * scale)[None, :]
    sg = sin * (jnp.roll(gp, HEAD_DIM // 2) * scale)[None, :]
    tm = min(tm, seq)
    tn = min(tn, w.shape[1])
    nsb = seq // tm
    tab = pl.BlockSpec((tm, HEAD_DIM), lambda i, j: (i % nsb, 0))
    return _proj_call(_proj_qk_kernel, h, w, 0, w.shape[1], BF16, extra=(cg, sg), extra_specs=(tab, tab),
                      tm=tm, tn=tn)


ROPE_PERM = tuple(list(range(0, 32)) + list(range(64, 96)) + list(range(32, 64)) + list(range(96, 128)))


def permute_heads(w):
    d, n = w.shape
    return w.reshape(d, n // HEAD_DIM, HEAD_DIM)[:, :, jnp.array(ROPE_PERM)].reshape(d, n)


def rope_tables(seq):
    half = HEAD_DIM // 2
    t = jnp.arange(seq, dtype=jnp.int32)
    row = (t // GRID_W).astype(F32)
    col = (t % GRID_W).astype(F32)
    inv_freq = ROPE_THETA ** (-jnp.arange(0, half, 2, dtype=F32) / half)
    ang = jnp.concatenate([row[:, None] * inv_freq[None, :], col[:, None] * inv_freq[None, :]], axis=-1)
    cos = jnp.concatenate([jnp.cos(ang), jnp.cos(ang)], axis=-1)
    sin = jnp.concatenate([-jnp.sin(ang), jnp.sin(ang)], axis=-1)
    return cos, sin


ATTN_TQ = 128
ATTN_TKC = 8192


def _attn_kernel(q0_ref, qa_ref, qb_ref, k_ref, v_ref, o_ref, s_scr, mrun_scr, mcur_scr, acc_scr, *, tkc):
    t = pl.program_id(2)
    nt = pl.num_programs(2)
    tq = qa_ref.shape[1]
    nch = k_ref.shape[1] // tkc
    nsl = tkc // LANES
    ones = jnp.ones((tkc, LANES), BF16)

    def stack(q_ref):
        return jnp.concatenate([q_ref[0, :, g * HEAD_DIM:(g + 1) * HEAD_DIM] for g in range(GQA_GROUPS)], axis=0)

    def pass1(qs, c, slot):
        off = pl.multiple_of(c * tkc, tkc)
        s = lax.dot_general(qs, k_ref[0, pl.ds(off, tkc), :], (((1,), (1,)), ((), ())),
                            preferred_element_type=F32)
        s_scr[slot, c] = s
        m = s[:, :LANES]
        for i in range(1, nsl):
            m = jnp.maximum(m, s[:, i * LANES:(i + 1) * LANES])
        mrun_scr[...] = jnp.maximum(mrun_scr[...], m)

    def pass2(c, slot):
        off = pl.multiple_of(c * tkc, tkc)
        vext = jnp.concatenate([v_ref[0, pl.ds(off, tkc), :], ones], axis=1)
        m = mcur_scr[...]
        p = jnp.concatenate([jnp.exp2(s_scr[slot, c, :, i * LANES:(i + 1) * LANES] - m) for i in range(nsl)],
                            axis=1)
        acc_scr[...] += jnp.dot(p.astype(BF16), vext, preferred_element_type=F32)

    def begin_pass1():
        mrun_scr[...] = jnp.full(mrun_scr.shape, -jnp.inf, F32)

    def end_pass1():
        mcur_scr[...] = jnp.broadcast_to(jnp.max(mrun_scr[...], axis=1, keepdims=True), mcur_scr.shape)
        acc_scr[...] = jnp.zeros(acc_scr.shape, F32)

    def emit(half):
        o = acc_scr[:, :HEAD_DIM] / acc_scr[:, HEAD_DIM:]
        for g in range(GQA_GROUPS):
            o_ref[0, half * tq:(half + 1) * tq, g * HEAD_DIM:(g + 1) * HEAD_DIM] = (
                o[g * tq:(g + 1) * tq].astype(o_ref.dtype))

    @pl.when(t == 0)
    def _():
        qs0 = stack(q0_ref)
        begin_pass1()

        def body_0(c, carry):
            pass1(qs0, c, 0)
            return carry

        lax.fori_loop(0, nch, body_0, 0)
        end_pass1()

    qs = stack(qa_ref)
    begin_pass1()

    def body_x(c, carry):
        pass1(qs, c, 1)
        pass2(c, 0)
        return carry

    lax.fori_loop(0, nch, body_x, 0)
    emit(0)
    end_pass1()

    @pl.when(t < nt - 1)
    def _():
        qs2 = stack(qb_ref)
        begin_pass1()

        def body_y(c, carry):
            pass1(qs2, c, 0)
            pass2(c, 1)
            return carry

        lax.fori_loop(0, nch, body_y, 0)
        emit(1)
        end_pass1()

    @pl.when(t == nt - 1)
    def _():
        def body_z(c, carry):
            pass2(c, 1)
            return carry

        lax.fori_loop(0, nch, body_z, 0)
        emit(1)


def gqa_attention(q, k, v, tq=ATTN_TQ, tkc=ATTN_TKC):
    b, s, _ = q.shape
    tkc = min(tkc, s)
    nq = s // tq
    assert nq % 2 == 0 and s % tkc == 0
    gw = GQA_GROUPS * HEAD_DIM
    rows = GQA_GROUPS * tq
    qblk = (1, tq, gw)
    kvblk = (1, s, HEAD_DIM)
    return pl.pallas_call(
        functools.partial(_attn_kernel, tkc=tkc),
        grid=(b, N_KV_HEADS, nq // 2),
        in_specs=[pl.BlockSpec(qblk, lambda bi, h, t: (bi, 0, h)),
                  pl.BlockSpec(qblk, lambda bi, h, t: (bi, 2 * t + 1, h)),
                  pl.BlockSpec(qblk, lambda bi, h, t: (bi, jnp.minimum(2 * t + 2, nq - 1), h)),
                  pl.BlockSpec(kvblk, lambda bi, h, t: (bi, 0, h)),
                  pl.BlockSpec(kvblk, lambda bi, h, t: (bi, 0, h))],
        out_specs=pl.BlockSpec((1, 2 * tq, gw), lambda bi, h, t: (bi, t, h)),
        out_shape=jax.ShapeDtypeStruct(q.shape, BF16),
        scratch_shapes=[pltpu.VMEM((2, s // tkc, rows, tkc), F32), pltpu.VMEM((rows, LANES), F32),
                        pltpu.VMEM((rows, LANES), F32), pltpu.VMEM((rows, 2 * HEAD_DIM), F32)],
        compiler_params=_cparams(("parallel", "parallel", "arbitrary")),
        name="gqa_attention",
    )(q, q, q, k, v)


HGRN_CHUNK = 128
HGRN_LAG = 2


def _group_row(x, group, row):
    c, n = x.shape
    xr = x.reshape(c // group, group, n)
    return jnp.broadcast_to(xr[:, row:row + 1, :], xr.shape).reshape(c, n)


def _boundary_row(x, m, rev, odd_block):
    ref_row = m if rev else m - 1
    if 2 * m >= SUBLANES:
        return _group_row(x, 2 * m, ref_row)
    if m == 1:
        c, n = x.shape
        x3 = x.reshape(c // SUBLANES, SUBLANES, n)
        other = pltpu.roll(x3, SUBLANES - 1 if rev else 1, 1).reshape(c, n)
        return jnp.where(odd_block(1), x, other) if rev else jnp.where(odd_block(1), other, x)
    assert 4 * m == SUBLANES
    return jnp.where(odd_block(2 * m), _group_row(x, SUBLANES, 2 * m + ref_row), _group_row(x, SUBLANES, ref_row))


def hgrn_constants():
    import numpy as np
    c = HGRN_CHUNK
    t = np.arange(c)[:, None]
    s = np.arange(c)[None, :]
    pair, tri = [], []
    for rev in (False, True):
        masks = [t == s]
        m = 1
        while m < c:
            same = (t // (2 * m)) == (s // (2 * m))
            t_query = ((t // m) % 2) == (0 if rev else 1)
            s_key = ((s // m) % 2) == (1 if rev else 0)
            masks.append(same & t_query & s_key)
            m *= 2
        pair.append(np.stack(masks))
        tri.append((s >= t) if rev else (s <= t))
    rows = np.stack([np.broadcast_to(((t // n) % 2) == 1, (c, LANES)) for n in (1, 2, 4)])
    return (jnp.asarray(np.stack(pair), jnp.int32), jnp.asarray(rows, jnp.int32),
            jnp.asarray(np.stack(tri), BF16))


def _neg_abs(x):
    return pltpu.bitcast(pltpu.bitcast(x, jnp.uint32) | jnp.uint32(0x80000000), F32)


def _nt(x, y):
    return lax.dot_general(x, y, (((1,), (1,)), ((), ())), preferred_element_type=F32)


def _hgrn_chunk(q_ref, v_ref, lf_ref, o_ref, st_scr, a_scr, p_scr, rows3, col, slot, consts, *, rev):
    pair_ref, rows_ref, tri_ref = consts
    r_prev, r0, r_next = rows3
    d = 1 if rev else 0
    odd_block = lambda n: rows_ref[{1: 0, 2: 1, 4: 2}[n]] != 0
    c = HGRN_CHUNK
    cols = slice(col * HEAD_DIM, (col + 1) * HEAD_DIM)

    pv = jnp.dot(p_scr[slot], v_ref[0, pl.ds(r_prev, c), cols], preferred_element_type=F32)
    a2 = _cumsum_issue(lf_ref[0, pl.ds(r_next, c), cols], tri_ref[d])

    qb = q_ref[0, pl.ds(r0, c), cols]
    vb = v_ref[0, pl.ds(r0, c), cols]
    k = 1.0 - jnp.exp2(lf_ref[0, pl.ds(r0, c), cols])
    sc = _nt(qb, k.astype(BF16))
    q = qb.astype(F32)
    a = a_scr[slot]
    edge = 0 if rev else c - 1
    a_end = a[edge:edge + 1, :]

    st = st_scr[col]
    o = _nt((q * jnp.exp2(a)).astype(BF16), st.astype(BF16))
    kdec = (k * jnp.exp2(a_end - a)).astype(BF16)
    st_new = lax.dot_general(vb, kdec, (((0,), (0,)), ((), ())), preferred_element_type=F32)
    st_decayed = st * jnp.exp2(a_end)

    nstrip = c // SUBLANES
    strip = lambda x, i: x[i * SUBLANES:(i + 1) * SUBLANES]
    p = [None] * nstrip

    def apply(pending):
        level, scores, rows = pending
        for j, i in enumerate(rows):
            mask = pair_ref[d, level, i * SUBLANES:(i + 1) * SUBLANES, :] != 0
            p[i] = jnp.where(mask, strip(scores, j), 0.0 if p[i] is None else p[i])

    query_first = rev

    def halves(m):
        out = []
        for g in range(c // (2 * m)):
            first = slice(g * 2 * m, g * 2 * m + m)
            second = slice(g * 2 * m + m, (g + 1) * 2 * m)
            out.append((first, second) if query_first else (second, first))
        return out

    def factor(m):
        ref = _boundary_row(a, m, rev, odd_block)
        if m < SUBLANES:
            return jnp.exp2(_neg_abs(a - ref))
        parts = []
        for qs, ks in halves(m):
            dq, dk = a[qs] - ref[qs], ref[ks] - a[ks]
            parts += [dq, dk] if query_first else [dk, dq]
        return jnp.exp2(jnp.concatenate(parts, axis=0))

    pending = [(0, sc, list(range(nstrip)))]
    nlevel = pair_ref.shape[1] - 1
    m = 1
    e = factor(m)
    for level in range(1, nlevel + 1):
        e_next = factor(2 * m) if level < nlevel else None
        if m < SUBLANES:
            qk = jnp.where(odd_block(m), k, q) if query_first else jnp.where(odd_block(m), q, k)
            z = (qk * e).astype(BF16)
            sc = _nt(z, z)
            rows = list(range(nstrip))
        else:
            parts, qparts, rows = [], [], []
            for qs, ks in halves(m):
                zq, zk = q[qs] * e[qs], k[ks] * e[ks]
                parts += [zq, zk] if query_first else [zk, zq]
                qparts.append(zq)
                rows += list(range(qs.start // SUBLANES, qs.stop // SUBLANES))
            z = jnp.concatenate(parts, axis=0).astype(BF16)
            sc = _nt(jnp.concatenate(qparts, axis=0).astype(BF16), z)
        e = e_next
        yield
        if level == 2:
            o_ref[0, pl.ds(r0, c), cols] = o
            st_scr[col] = st_decayed + st_new
        pending.append((level, sc, rows))
        if len(pending) > HGRN_LAG:
            apply(pending.pop(0))
        m *= 2
    a_scr[slot] = a2[:, :LANES] + a2[:, LANES:]
    o_ref[0, pl.ds(r_prev, c), cols] += pv
    yield
    for item in pending:
        apply(item)
    p_scr[slot] = jnp.concatenate(p, axis=0).astype(BF16)


def _cumsum_issue(lf, tri):
    hi = lf.astype(BF16)
    lo = (lf - hi.astype(F32)).astype(BF16)
    return jnp.dot(tri, jnp.concatenate([hi, lo], axis=1), preferred_element_type=F32)


def _interleave(chains):
    chains = list(chains)
    while chains:
        for ch in list(chains):
            try:
                next(ch)
            except StopIteration:
                chains.remove(ch)


def _hgrn_kernel(qf_ref, vf_ref, lff_ref, qb_ref, vb_ref, lfb_ref, pair_ref, rows_ref, tri_ref,
                 of_ref, ob_ref, sf_scr, sb_scr, a_scr, p_scr):
    @pl.when(pl.program_id(2) == 0)
    def _():
        sf_scr[...] = jnp.zeros(sf_scr.shape, F32)
        sb_scr[...] = jnp.zeros(sb_scr.shape, F32)

    c = HGRN_CHUNK
    consts = (pair_ref, rows_ref, tri_ref)
    n = qf_ref.shape[1] // c
    heads = qf_ref.shape[2] // HEAD_DIM
    row = lambda idx: pl.multiple_of(idx * c, c)
    dirs = ((qf_ref, vf_ref, lff_ref, of_ref, sf_scr, False), (qb_ref, vb_ref, lfb_ref, ob_ref, sb_scr, True))
    first = lambda rev: n - 1 if rev else 0
    last = lambda rev: 0 if rev else n - 1

    for hd in range(heads):
        cols = slice(hd * HEAD_DIM, (hd + 1) * HEAD_DIM)
        for d, (_, _, lf_ref, _, _, rev) in enumerate(dirs):
            a2 = _cumsum_issue(lf_ref[0, pl.ds(first(rev) * c, c), cols], tri_ref[d])
            a_scr[2 * hd + d] = a2[:, :LANES] + a2[:, LANES:]
    p_scr[...] = jnp.zeros(p_scr.shape, BF16)

    def body(ci, carry):
        idx = {False: (jnp.maximum(ci - 1, 0), ci, jnp.minimum(ci + 1, n - 1)),
               True: (jnp.minimum(n - ci, n - 1), n - 1 - ci, jnp.maximum(n - 2 - ci, 0))}
        chains = []
        for hd in range(heads):
            for d, (q_ref, v_ref, lf_ref, o_ref, st_scr, rev) in enumerate(dirs):
                rows3 = tuple(row(i) for i in idx[rev])
                chains.append(_hgrn_chunk(q_ref, v_ref, lf_ref, o_ref, st_scr, a_scr, p_scr, rows3, hd,
                                          2 * hd + d, consts, rev=rev))
        _interleave(chains)
        return carry

    lax.fori_loop(0, n, body, 0)

    for hd in range(heads):
        cols = slice(hd * HEAD_DIM, (hd + 1) * HEAD_DIM)
        for d, (_, v_ref, _, o_ref, _, rev) in enumerate(dirs):
            rws = pl.ds(last(rev) * c, c)
            o_ref[0, rws, cols] += jnp.dot(p_scr[2 * hd + d], v_ref[0, rws, cols], preferred_element_type=F32)


HGRN_HEADS_PER_STEP = 4


def hgrn2_bidir(q, v, logf, ts=1024, hps=HGRN_HEADS_PER_STEP):
    b, s, hk = q.shape
    ts = min(ts, s)
    nt = s // ts
    ng = hk // (hps * HEAD_DIM)
    blk = (1, ts, hps * HEAD_DIM)
    fwd = lambda bi, h, i: (bi, i, h)
    bwd = lambda bi, h, i: (bi, nt - 1 - i, h)
    out = jax.ShapeDtypeStruct((b, s, hk), F32)
    state = pltpu.VMEM((hps, HEAD_DIM, HEAD_DIM), F32)
    consts = hgrn_constants()
    whole = lambda x: pl.BlockSpec(x.shape, lambda bi, h, i: (0,) * x.ndim)
    return pl.pallas_call(
        _hgrn_kernel,
        grid=(b, ng, nt),
        in_specs=[pl.BlockSpec(blk, fwd), pl.BlockSpec(blk, fwd), pl.BlockSpec(blk, fwd),
                  pl.BlockSpec(blk, bwd), pl.BlockSpec(blk, bwd),
                  pl.BlockSpec(blk, lambda bi, h, i: (bi, nt - 1 - i, ng + h))] + [whole(x) for x in consts],
        out_specs=[pl.BlockSpec(blk, fwd), pl.BlockSpec(blk, bwd)],
        out_shape=[out, out],
        scratch_shapes=[state, state, pltpu.VMEM((2 * hps, HGRN_CHUNK, HEAD_DIM), F32),
                        pltpu.VMEM((2 * hps, HGRN_CHUNK, HGRN_CHUNK), BF16)],
        compiler_params=_cparams(("parallel", "parallel", "arbitrary")),
        name="hgrn2",
    )(q, v, logf, q, v, logf, *consts)


def _merge_kernel(att_ref, of_ref, ob_ref, gr_ref, gn_ref, woa_ref, woh_ref, sa_ref, sh_ref, o_ref, or_scr):
    @pl.when(pl.program_id(1) == 0)
    def _():
        gn = gn_ref[...]
        for hd in range(or_scr.shape[1] // HEAD_DIM):
            sl = slice(hd * HEAD_DIM, (hd + 1) * HEAD_DIM)
            o = _rms(of_ref[:, sl] + ob_ref[:, sl], gn)
            or_scr[:, sl] = (o * gr_ref[:, sl].astype(F32)).astype(or_scr.dtype)

    ya = jnp.dot(att_ref[...], woa_ref[...], preferred_element_type=F32)
    yh = jnp.dot(or_scr[...], woh_ref[...], preferred_element_type=F32)
    o_ref[...] = (sa_ref[...].astype(F32) * ya + sh_ref[...].astype(F32) * yh).astype(o_ref.dtype)


def gated_merge(att, o_fw, o_bw, g_silu, g_norm, w_oa, w_oh, gates, tm=512, tn=512):
    m, d = att.shape
    tm = min(tm, m)
    nj = d // tn
    row = pl.BlockSpec((tm, d), lambda i, j: (i, 0))
    wsp = pl.BlockSpec((d, tn), lambda i, j: (0, j))
    return pl.pallas_call(
        _merge_kernel,
        grid=(m // tm, nj),
        in_specs=[row, row, row, row, pl.BlockSpec((1, HEAD_DIM), lambda i, j: (0, 0)), wsp, wsp,
                  pl.BlockSpec((tm, tn), lambda i, j: (i, j)),
                  pl.BlockSpec((tm, tn), lambda i, j: (i, nj + j))],
        out_specs=pl.BlockSpec((tm, tn), lambda i, j: (i, j)),
        out_shape=jax.ShapeDtypeStruct((m, d), BF16),
        scratch_shapes=[pltpu.VMEM((tm, d), BF16)],
        compiler_params=_cparams(("parallel", "arbitrary")),
        name="gated_merge",
    )(att, o_fw, o_bw, g_silu, g_norm.reshape(1, HEAD_DIM), w_oa, w_oh, gates, gates)


def _resid_proj_kernel(x_ref, a_ref, w_ref, o_ref):
    o_ref[...] = x_ref[...] + jnp.dot(a_ref[...], w_ref[...], preferred_element_type=F32)


def resid_proj(x, a, w, tm=1024, tn=512):
    m, d = x.shape
    tm = min(tm, m)
    return pl.pallas_call(
        _resid_proj_kernel,
        grid=(m // tm, d // tn),
        in_specs=[pl.BlockSpec((tm, tn), lambda i, j: (i, j)), pl.BlockSpec((tm, a.shape[1]), lambda i, j: (i, 0)),
                  pl.BlockSpec((a.shape[1], tn), lambda i, j: (0, j))],
        out_specs=pl.BlockSpec((tm, tn), lambda i, j: (i, j)),
        out_shape=jax.ShapeDtypeStruct((m, d), F32),
        compiler_params=_cparams(("parallel", "arbitrary")),
        name="out_proj",
    )(x, a, w)


def _mlp_kernel(x_ref, g_ref, wu_ref, wd_ref, o_ref, h_scr, acc_scr):
    f = pl.program_id(1)

    @pl.when(f == 0)
    def _():
        x = x_ref[...]
        h_scr[...] = _rms(x, g_ref[...]).astype(h_scr.dtype)
        acc_scr[...] = x

    u = jnp.maximum(jnp.dot(h_scr[...], wu_ref[...], preferred_element_type=F32), 0.0)
    acc_scr[...] += jnp.dot((u * u).astype(BF16), wd_ref[...], preferred_element_type=F32)

    @pl.when(f == pl.num_programs(1) - 1)
    def _():
        o_ref[...] = acc_scr[...]


def mlp_block(x, gain, w_up, w_down, tm=512, tf=512):
    m, d = x.shape
    ff = w_up.shape[1]
    tm = min(tm, m)
    return pl.pallas_call(
        _mlp_kernel,
        grid=(m // tm, ff // tf),
        in_specs=[pl.BlockSpec((tm, d), lambda i, f: (i, 0)), pl.BlockSpec((1, d), lambda i, f: (0, 0)),
                  pl.BlockSpec((d, tf), lambda i, f: (0, f)), pl.BlockSpec((tf, d), lambda i, f: (f, 0))],
        out_specs=pl.BlockSpec((tm, d), lambda i, f: (i, 0)),
        out_shape=jax.ShapeDtypeStruct((m, d), F32),
        scratch_shapes=[pltpu.VMEM((tm, d), BF16), pltpu.VMEM((tm, d), F32)],
        compiler_params=_cparams(("parallel", "arbitrary")),
        name="mlp",
    )(x, gain.reshape(1, d), w_up, w_down)


def _ple_kernel(x_ref, g_ref, wg_ref, p_ref, wp_ref, gf_ref, o_ref):
    x = x_ref[...]
    h = _rms(x, g_ref[...]).astype(BF16)
    gate = _sigmoid(jnp.dot(h, wg_ref[...], preferred_element_type=F32))
    emb = jnp.dot(p_ref[...].astype(BF16), wp_ref[...], preferred_element_type=F32)
    o_ref[...] = _rms(x + gate * emb, gf_ref[...])


def ple_final(x, gain, w_gate, p, w_p, g_final, tm=512):
    m, d = x.shape
    c = p.shape[1]
    tm = min(tm, m)
    const = lambda i: (0, 0)
    return pl.pallas_call(
        _ple_kernel,
        grid=(m // tm,),
        in_specs=[pl.BlockSpec((tm, d), lambda i: (i, 0)), pl.BlockSpec((1, d), const),
                  pl.BlockSpec((d, d), const), pl.BlockSpec((tm, c), lambda i: (i, 0)),
                  pl.BlockSpec((c, d), const), pl.BlockSpec((1, d), const)],
        out_specs=pl.BlockSpec((tm, d), lambda i: (i, 0)),
        out_shape=jax.ShapeDtypeStruct((m, d), F32),
        compiler_params=_cparams(("parallel",)),
        name="ple_final",
    )(x, gain.reshape(1, d), w_gate, p, w_p, g_final.reshape(1, d))


def kernel(x, p, g_mix, w_in, g_q, g_k, w_o_attn, hgrn_lb, g_hgrn, w_o_hgrn, w_out, g_mlp, w_up, w_down,
           g_ple, w_ple_gate, w_ple, g_final):
    b, s, d = x.shape
    m = b * s
    depth = w_in.shape[0]
    attn_q = N_Q_HEADS * HEAD_DIM
    attn_kv = N_KV_HEADS * HEAD_DIM
    hk = HGRN_HEADS * HEAD_DIM
    rope = rope_tables(s)
    lb_all = jnp.cumsum(jax.nn.softmax(hgrn_lb.astype(F32), axis=0), axis=0)

    xf = x.reshape(m, d)
    for i in range(depth):
        w = w_in[i].astype(BF16)
        h = rmsnorm_bf16(xf, g_mix[i])
        c0 = 0
        w_qk = permute_heads(w[:, :attn_q + attn_kv])
        q_a = proj_qk(h, w_qk[:, :attn_q], g_q[i], rope, s, HEAD_DIM ** -0.5 * LOG2_E); c0 += attn_q
        k_a = proj_qk(h, w_qk[:, attn_q:], g_k[i], rope, s, 1.0); c0 += attn_kv
        v_a = proj_plain(h, w, c0, attn_kv, None); c0 += attn_kv
        q_r = proj_plain(h, w, c0, hk, "silu"); c0 += hk
        logf = proj_logf(h, w, c0, 2 * hk, lb_all[i]); c0 += 2 * hk
        i_r = proj_plain(h, w, c0, hk, None); c0 += hk
        g_r = proj_plain(h, w, c0, hk, "silu"); c0 += hk
        gates = proj_plain(h, w, c0, 2 * d, "sigmoid"); c0 += 2 * d

        att = gqa_attention(q_a.reshape(b, s, attn_q), k_a.reshape(b, s, attn_kv), v_a.reshape(b, s, attn_kv))
        o_fw, o_bw = hgrn2_bidir(q_r.reshape(b, s, hk), i_r.reshape(b, s, hk), logf.reshape(b, s, 2 * hk))
        mixed = gated_merge(att.reshape(m, attn_q), o_fw.reshape(m, hk), o_bw.reshape(m, hk), g_r, g_hgrn[i],
                            w_o_attn[i].astype(BF16), w_o_hgrn[i].astype(BF16), gates)
        xf = resid_proj(xf, mixed, w_out[i].astype(BF16))
        xf = mlp_block(xf, g_mlp[i], w_up[i].astype(BF16), w_down[i].astype(BF16))
        assert depth == 1
        xf = ple_final(xf, g_ple[i], w_ple_gate[i].astype(BF16), p[i].reshape(m, -1), w_ple[i].astype(BF16),
                       g_final)
    return xf.reshape(b, s, d)
```

```python
import functools

import jax
import jax.numpy as jnp
from jax import lax
from jax.experimental import pallas as pl
from jax.experimental.pallas import tpu as pltpu

F32 = jnp.float32
BF16 = jnp.bfloat16

EPS = 1e-6
LOG2_E = 1.4426950408889634
HEAD_DIM = 128
N_Q_HEADS = 16
N_KV_HEADS = 4
GQA_GROUPS = N_Q_HEADS // N_KV_HEADS
GRID_W = 64
ROPE_THETA = 10000.0
HGRN_HEADS = 16
LANES = 128
SUBLANES = 8
VMEM_LIMIT = 56 * 1024 * 1024


def _cparams(sem):
    return pltpu.CompilerParams(dimension_semantics=sem, vmem_limit_bytes=VMEM_LIMIT)


def _sigmoid(x):
    return 1.0 / (1.0 + jnp.exp(-x))


def _rms(x, gain):
    ms = jnp.mean(x * x, axis=-1, keepdims=True)
    return x * lax.rsqrt(ms + EPS) * gain


def _rmsnorm_kernel(x_ref, g_ref, o_ref):
    o_ref[...] = _rms(x_ref[...], g_ref[...]).astype(o_ref.dtype)


def rmsnorm_bf16(x, gain, tm=512):
    m, d = x.shape
    tm = min(tm, m)
    return pl.pallas_call(
        _rmsnorm_kernel,
        grid=(m // tm,),
        in_specs=[pl.BlockSpec((tm, d), lambda i: (i, 0)), pl.BlockSpec((1, d), lambda i: (0, 0))],
        out_specs=pl.BlockSpec((tm, d), lambda i: (i, 0)),
        out_shape=jax.ShapeDtypeStruct((m, d), BF16),
        compiler_params=_cparams(("parallel",)),
        name="rmsnorm",
    )(x, gain.reshape(1, d))


QK_SUB = 2 * HEAD_DIM


MM_ROWS = 512


def _subtiled_matmul(h_ref, w_ref, epilogue):
    tm = h_ref.shape[0]
    rows_per = min(MM_ROWS, tm)
    units = [(slice(r, r + rows_per), slice(c0, c0 + QK_SUB))
             for c0 in range(0, w_ref.shape[1], QK_SUB) for r in range(0, tm, rows_per)]
    matmul = lambda u: jnp.dot(h_ref[u[0], :], w_ref[:, u[1]], preferred_element_type=F32)
    prev = matmul(units[0])
    for j in range(1, len(units)):
        cur = matmul(units[j])
        epilogue(prev, *units[j - 1])
        prev = cur
    epilogue(prev, *units[-1])


def _proj_plain_kernel(h_ref, w_ref, o_ref, *, act):
    def epilogue(acc, rs, cs):
        if act == "silu":
            acc = acc * _sigmoid(acc)
        elif act == "sigmoid":
            acc = _sigmoid(acc)
        o_ref[rs, cs] = acc.astype(o_ref.dtype)

    _subtiled_matmul(h_ref, w_ref, epilogue)


def _proj_logf_kernel(h_ref, w_ref, lb_ref, o_ref):
    def epilogue(acc, rs, cs):
        lb = lb_ref[:, cs]
        o_ref[rs, cs] = jnp.log(lb + (1.0 - lb) * _sigmoid(acc)) * LOG2_E

    _subtiled_matmul(h_ref, w_ref, epilogue)


def _proj_qk_kernel(h_ref, w_ref, cg_ref, sg_ref, o_ref):
    ones = jnp.ones((HEAD_DIM, HEAD_DIM), BF16)

    def epilogue(acc, rs, cs):
        cg, sg = cg_ref[rs, :], sg_ref[rs, :]
        for hd in range(QK_SUB // HEAD_DIM):
            x = acc[:, hd * HEAD_DIM:(hd + 1) * HEAD_DIM]
            ssq = jnp.dot((x * x).astype(BF16), ones, preferred_element_type=F32)
            r = lax.rsqrt(ssq * (1.0 / HEAD_DIM) + EPS)
            y = r * (x * cg + pltpu.roll(x, HEAD_DIM // 2, 1) * sg)
            c0 = cs.start + hd * HEAD_DIM
            o_ref[rs, c0:c0 + HEAD_DIM] = y.astype(o_ref.dtype)

    _subtiled_matmul(h_ref, w_ref, epilogue)


def _proj_call(kernel, h, w, col0, ncols, out_dtype, extra=(), extra_specs=(), tm=1024, tn=1024):
    m, d = h.shape
    tm = min(tm, m)
    tn = min(tn, ncols)
    assert col0 % tn == 0 and ncols % tn == 0 and m % tm == 0
    jb = col0 // tn
    return pl.pallas_call(
        kernel,
        grid=(m // tm, ncols // tn),
        in_specs=[pl.BlockSpec((tm, d), lambda i, j: (i, 0)),
                  pl.BlockSpec((d, tn), lambda i, j: (0, jb + j))] + list(extra_specs),
        out_specs=pl.BlockSpec((tm, tn), lambda i, j: (i, j)),
        out_shape=jax.ShapeDtypeStruct((m, ncols), out_dtype),
        compiler_params=_cparams(("parallel", "arbitrary")),
        name="in_proj",
    )(h, w, *extra)


def proj_plain(h, w, col0, ncols, act, out_dtype=BF16):
    return _proj_call(functools.partial(_proj_plain_kernel, act=act), h, w, col0, ncols, out_dtype)


def proj_logf(h, w, col0, ncols, lb, tn=1024):
    return _proj_call(_proj_logf_kernel, h, w, col0, ncols, F32, extra=(lb.reshape(1, ncols),),
                      extra_specs=(pl.BlockSpec((1, tn), lambda i, j: (0, j)),), tn=tn)


def proj_qk(h, w, gain, rope, seq, scale, tm=1024, tn=1024):
    cos, sin = rope
    gp = gain.astype(F32)[jnp.array(ROPE_PERM)]
    cg = cos * (gp * scale)[None, :]
    sg = sin * (jnp.roll(gp, HEAD_DIM // 2) * scale)[None, :]
    tm = min(tm, seq)
    tn = min(tn, w.shape[1])
    nsb = seq // tm
    tab = pl.BlockSpec((tm, HEAD_DIM), lambda i, j: (i % nsb, 0))
    return _proj_call(_proj_qk_kernel, h, w, 0, w.shape[1], BF16, extra=(cg, sg), extra_specs=(tab, tab),
                      tm=tm, tn=tn)


ROPE_PERM = tuple(list(range(0, 32)) + list(range(64, 96)) + list(range(32, 64)) + list(range(96, 128)))


def permute_heads(w):
    d, n = w.shape
    return w.reshape(d, n // HEAD_DIM, HEAD_DIM)[:, :, jnp.array(ROPE_PERM)].reshape(d, n)


def rope_tables(seq):
    half = HEAD_DIM // 2
    t = jnp.arange(seq, dtype=jnp.int32)
    row = (t // GRID_W).astype(F32)
    col = (t % GRID_W).astype(F32)
    inv_freq = ROPE_THETA ** (-jnp.arange(0, half, 2, dtype=F32) / half)
    ang = jnp.concatenate([row[:, None] * inv_freq[None, :], col[:, None] * inv_freq[None, :]], axis=-1)
    cos = jnp.concatenate([jnp.cos(ang), jnp.cos(ang)], axis=-1)
    sin = jnp.concatenate([-jnp.sin(ang), jnp.sin(ang)], axis=-1)
    return cos, sin


ATTN_TQ = 128
ATTN_TKC = 8192


def _attn_kernel(q0_ref, qa_ref, qb_ref, k_ref, v_ref, o_ref, s_scr, mrun_scr, mcur_scr, acc_scr, *, tkc):
    t = pl.program_id(2)
    nt = pl.num_programs(2)
    tq = qa_ref.shape[1]
    nch = k_ref.shape[1] // tkc
    nsl = tkc // LANES
    ones = jnp.ones((tkc, LANES), BF16)

    def stack(q_ref):
        return jnp.concatenate([q_ref[0, :, g * HEAD_DIM:(g + 1) * HEAD_DIM] for g in range(GQA_GROUPS)], axis=0)

    def pass1(qs, c, slot):
        off = pl.multiple_of(c * tkc, tkc)
        s = lax.dot_general(qs, k_ref[0, pl.ds(off, tkc), :], (((1,), (1,)), ((), ())),
                            preferred_element_type=F32)
        s_scr[slot, c] = s
        m = s[:, :LANES]
        for i in range(1, nsl):
            m = jnp.maximum(m, s[:, i * LANES:(i + 1) * LANES])
        mrun_scr[...] = jnp.maximum(mrun_scr[...], m)

    def pass2(c, slot):
        off = pl.multiple_of(c * tkc, tkc)
        vext = jnp.concatenate([v_ref[0, pl.ds(off, tkc), :], ones], axis=1)
        m = mcur_scr[...]
        p = jnp.concatenate([jnp.exp2(s_scr[slot, c, :, i * LANES:(i + 1) * LANES] - m) for i in range(nsl)],
                            axis=1)
        acc_scr[...] += jnp.dot(p.astype(BF16), vext, preferred_element_type=F32)

    def begin_pass1():
        mrun_scr[...] = jnp.full(mrun_scr.shape, -jnp.inf, F32)

    def end_pass1():
        mcur_scr[...] = jnp.broadcast_to(jnp.max(mrun_scr[...], axis=1, keepdims=True), mcur_scr.shape)
        acc_scr[...] = jnp.zeros(acc_scr.shape, F32)

    def emit(half):
        o = acc_scr[:, :HEAD_DIM] / acc_scr[:, HEAD_DIM:]
        for g in range(GQA_GROUPS):
            o_ref[0, half * tq:(half + 1) * tq, g * HEAD_DIM:(g + 1) * HEAD_DIM] = (
                o[g * tq:(g + 1) * tq].astype(o_ref.dtype))

    @pl.when(t == 0)
    def _():
        qs0 = stack(q0_ref)
        begin_pass1()

        def body_0(c, carry):
            pass1(qs0, c, 0)
            return carry

        lax.fori_loop(0, nch, body_0, 0)
        end_pass1()

    qs = stack(qa_ref)
    begin_pass1()

    def body_x(c, carry):
        pass1(qs, c, 1)
        pass2(c, 0)
        return carry

    lax.fori_loop(0, nch, body_x, 0)
    emit(0)
    end_pass1()

    @pl.when(t < nt - 1)
    def _():
        qs2 = stack(qb_ref)
        begin_pass1()

        def body_y(c, carry):
            pass1(qs2, c, 0)
            pass2(c, 1)
            return carry

        lax.fori_loop(0, nch, body_y, 0)
        emit(1)
        end_pass1()

    @pl.when(t == nt - 1)
    def _():
        def body_z(c, carry):
            pass2(c, 1)
            return carry

        lax.fori_loop(0, nch, body_z, 0)
        emit(1)


def gqa_attention(q, k, v, tq=ATTN_TQ, tkc=ATTN_TKC):
    b, s, _ = q.shape
    tkc = min(tkc, s)
    nq = s // tq
    assert nq % 2 == 0 and s % tkc == 0
    gw = GQA_GROUPS * HEAD_DIM
    rows = GQA_GROUPS * tq
    qblk = (1, tq, gw)
    kvblk = (1, s, HEAD_DIM)
    return pl.pallas_call(
        functools.partial(_attn_kernel, tkc=tkc),
        grid=(b, N_KV_HEADS, nq // 2),
        in_specs=[pl.BlockSpec(qblk, lambda bi, h, t: (bi, 0, h)),
                  pl.BlockSpec(qblk, lambda bi, h, t: (bi, 2 * t + 1, h)),
                  pl.BlockSpec(qblk, lambda bi, h, t: (bi, jnp.minimum(2 * t + 2, nq - 1), h)),
                  pl.BlockSpec(kvblk, lambda bi, h, t: (bi, 0, h)),
                  pl.BlockSpec(kvblk, lambda bi, h, t: (bi, 0, h))],
        out_specs=pl.BlockSpec((1, 2 * tq, gw), lambda bi, h, t: (bi, t, h)),
        out_shape=jax.ShapeDtypeStruct(q.shape, BF16),
        scratch_shapes=[pltpu.VMEM((2, s // tkc, rows, tkc), F32), pltpu.VMEM((rows, LANES), F32),
                        pltpu.VMEM((rows, LANES), F32), pltpu.VMEM((rows, 2 * HEAD_DIM), F32)],
        compiler_params=_cparams(("parallel", "parallel", "arbitrary")),
        name="gqa_attention",
    )(q, q, q, k, v)


HGRN_CHUNK = 128
HGRN_LAG = 2


def _group_row(x, group, row):
    c, n = x.shape
    xr = x.reshape(c // group, group, n)
    return jnp.broadcast_to(xr[:, row:row + 1, :], xr.shape).reshape(c, n)


def _boundary_row(x, m, rev, odd_block):
    ref_row = m if rev else m - 1
    if 2 * m >= SUBLANES:
        return _group_row(x, 2 * m, ref_row)
    if m == 1:
        c, n = x.shape
        x3 = x.reshape(c // SUBLANES, SUBLANES, n)
        other = pltpu.roll(x3, SUBLANES - 1 if rev else 1, 1).reshape(c, n)
        return jnp.where(odd_block(1), x, other) if rev else jnp.where(odd_block(1), other, x)
    assert 4 * m == SUBLANES
    return jnp.where(odd_block(2 * m), _group_row(x, SUBLANES, 2 * m + ref_row), _group_row(x, SUBLANES, ref_row))


def hgrn_constants():
    import numpy as np
    c = HGRN_CHUNK
    t = np.arange(c)[:, None]
    s = np.arange(c)[None, :]
    pair, tri = [], []
    for rev in (False, True):
        masks = [t == s]
        m = 1
        while m < c:
            same = (t // (2 * m)) == (s // (2 * m))
            t_query = ((t // m) % 2) == (0 if rev else 1)
            s_key = ((s // m) % 2) == (1 if rev else 0)
            masks.append(same & t_query & s_key)
            m *= 2
        pair.append(np.stack(masks))
        tri.append((s >= t) if rev else (s <= t))
    rows = np.stack([np.broadcast_to(((t // n) % 2) == 1, (c, LANES)) for n in (1, 2, 4)])
    return (jnp.asarray(np.stack(pair), jnp.int32), jnp.asarray(rows, jnp.int32),
            jnp.asarray(np.stack(tri), BF16))


def _neg_abs(x):
    return pltpu.bitcast(pltpu.bitcast(x, jnp.uint32) | jnp.uint32(0x80000000), F32)


def _nt(x, y):
    return lax.dot_general(x, y, (((1,), (1,)), ((), ())), preferred_element_type=F32)


def _hgrn_chunk(q_ref, v_ref, lf_ref, o_ref, st_scr, a_scr, p_scr, rows3, col, slot, consts, *, rev):
    pair_ref, rows_ref, tri_ref = consts
    r_prev, r0, r_next = rows3
    d = 1 if rev else 0
    odd_block = lambda n: rows_ref[{1: 0, 2: 1, 4: 2}[n]] != 0
    c = HGRN_CHUNK
    cols = slice(col * HEAD_DIM, (col + 1) * HEAD_DIM)

    pv = jnp.dot(p_scr[slot], v_ref[0, pl.ds(r_prev, c), cols], preferred_element_type=F32)
    a2 = _cumsum_issue(lf_ref[0, pl.ds(r_next, c), cols], tri_ref[d])

    qb = q_ref[0, pl.ds(r0, c), cols]
    vb = v_ref[0, pl.ds(r0, c), cols]
    k = 1.0 - jnp.exp2(lf_ref[0, pl.ds(r0, c), cols])
    sc = _nt(qb, k.astype(BF16))
    q = qb.astype(F32)
    a = a_scr[slot]
    edge = 0 if rev else c - 1
    a_end = a[edge:edge + 1, :]

    st = st_scr[col]
    o = _nt((q * jnp.exp2(a)).astype(BF16), st.astype(BF16))
    kdec = (k * jnp.exp2(a_end - a)).astype(BF16)
    st_new = lax.dot_general(vb, kdec, (((0,), (0,)), ((), ())), preferred_element_type=F32)
    st_decayed = st * jnp.exp2(a_end)

    nstrip = c // SUBLANES
    strip = lambda x, i: x[i * SUBLANES:(i + 1) * SUBLANES]
    p = [None] * nstrip

    def apply(pending):
        level, scores, rows = pending
        for j, i in enumerate(rows):
            mask = pair_ref[d, level, i * SUBLANES:(i + 1) * SUBLANES, :] != 0
            p[i] = jnp.where(mask, strip(scores, j), 0.0 if p[i] is None else p[i])

    query_first = rev

    def halves(m):
        out = []
        for g in range(c // (2 * m)):
            first = slice(g * 2 * m, g * 2 * m + m)
            second = slice(g * 2 * m + m, (g + 1) * 2 * m)
            out.append((first, second) if query_first else (second, first))
        return out

    def factor(m):
        ref = _boundary_row(a, m, rev, odd_block)
        if m < SUBLANES:
            return jnp.exp2(_neg_abs(a - ref))
        parts = []
        for qs, ks in halves(m):
            dq, dk = a[qs] - ref[qs], ref[ks] - a[ks]
            parts += [dq, dk] if query_first else [dk, dq]
        return jnp.exp2(jnp.concatenate(parts, axis=0))

    pending = [(0, sc, list(range(nstrip)))]
    nlevel = pair_ref.shape[1] - 1
    m = 1
    e = factor(m)
    for level in range(1, nlevel + 1):
        e_next = factor(2 * m) if level < nlevel else None
        if m < SUBLANES:
            qk = jnp.where(odd_block(m), k, q) if query_first else jnp.where(odd_block(m), q, k)
            z = (qk * e).astype(BF16)
            sc = _nt(z, z)
            rows = list(range(nstrip))
        else:
            parts, qparts, rows = [], [], []
            for qs, ks in halves(m):
                zq, zk = q[qs] * e[qs], k[ks] * e[ks]
                parts += [zq, zk] if query_first else [zk, zq]
                qparts.append(zq)
                rows += list(range(qs.start // SUBLANES, qs.stop // SUBLANES))
            z = jnp.concatenate(parts, axis=0).astype(BF16)
            sc = _nt(jnp.concatenate(qparts, axis=0).astype(BF16), z)
        e = e_next
        yield
        if level == 2:
            o_ref[0, pl.ds(r0, c), cols] = o
            st_scr[col] = st_decayed + st_new
        pending.append((level, sc, rows))
        if len(pending) > HGRN_LAG:
            apply(pending.pop(0))
        m *= 2
    a_scr[slot] = a2[:, :LANES] + a2[:, LANES:]
    o_ref[0, pl.ds(r_prev, c), cols] += pv
    yield
    for item in pending:
        apply(item)
    p_scr[slot] = jnp.concatenate(p, axis=0).astype(BF16)


def _cumsum_issue(lf, tri):
    hi = lf.astype(BF16)
    lo = (lf - hi.astype(F32)).astype(BF16)
    return jnp.dot(tri, jnp.concatenate([hi, lo], axis=1), preferred_element_type=F32)


def _interleave(chains):
    chains = list(chains)
    while chains:
        for ch in list(chains):
            try:
                next(ch)
            except StopIteration:
                chains.remove(ch)


def _hgrn_kernel(qf_ref, vf_ref, lff_ref, qb_ref, vb_ref, lfb_ref, pair_ref, rows_ref, tri_ref,
                 of_ref, ob_ref, sf_scr, sb_scr, a_scr, p_scr):
    @pl.when(pl.program_id(2) == 0)
    def _():
        sf_scr[...] = jnp.zeros(sf_scr.shape, F32)
        sb_scr[...] = jnp.zeros(sb_scr.shape, F32)

    c = HGRN_CHUNK
    consts = (pair_ref, rows_ref, tri_ref)
    n = qf_ref.shape[1] // c
    heads = qf_ref.shape[2] // HEAD_DIM
    row = lambda idx: pl.multiple_of(idx * c, c)
    dirs = ((qf_ref, vf_ref, lff_ref, of_ref, sf_scr, False), (qb_ref, vb_ref, lfb_ref, ob_ref, sb_scr, True))
    first = lambda rev: n - 1 if rev else 0
    last = lambda rev: 0 if rev else n - 1

    for hd in range(heads):
        cols = slice(hd * HEAD_DIM, (hd + 1) * HEAD_DIM)
        for d, (_, _, lf_ref, _, _, rev) in enumerate(dirs):
            a2 = _cumsum_issue(lf_ref[0, pl.ds(first(rev) * c, c), cols], tri_ref[d])
            a_scr[2 * hd + d] = a2[:, :LANES] + a2[:, LANES:]
    p_scr[...] = jnp.zeros(p_scr.shape, BF16)

    def body(ci, carry):
        idx = {False: (jnp.maximum(ci - 1, 0), ci, jnp.minimum(ci + 1, n - 1)),
               True: (jnp.minimum(n - ci, n - 1), n - 1 - ci, jnp.maximum(n - 2 - ci, 0))}
        chains = []
        for hd in range(heads):
            for d, (q_ref, v_ref, lf_ref, o_ref, st_scr, rev) in enumerate(dirs):
                rows3 = tuple(row(i) for i in idx[rev])
                chains.append(_hgrn_chunk(q_ref, v_ref, lf_ref, o_ref, st_scr, a_scr, p_scr, rows3, hd,
                                          2 * hd + d, consts, rev=rev))
        _interleave(chains)
        return carry

    lax.fori_loop(0, n, body, 0)

    for hd in range(heads):
        cols = slice(hd * HEAD_DIM, (hd + 1) * HEAD_DIM)
        for d, (_, v_ref, _, o_ref, _, rev) in enumerate(dirs):
            rws = pl.ds(last(rev) * c, c)
            o_ref[0, rws, cols] += jnp.dot(p_scr[2 * hd + d], v_ref[0, rws, cols], preferred_element_type=F32)


HGRN_HEADS_PER_STEP = 4


def hgrn2_bidir(q, v, logf, ts=1024, hps=HGRN_HEADS_PER_STEP):
    b, s, hk = q.shape
    ts = min(ts, s)
    nt = s // ts
    ng = hk // (hps * HEAD_DIM)
    blk = (1, ts, hps * HEAD_DIM)
    fwd = lambda bi, h, i: (bi, i, h)
    bwd = lambda bi, h, i: (bi, nt - 1 - i, h)
    out = jax.ShapeDtypeStruct((b, s, hk), F32)
    state = pltpu.VMEM((hps, HEAD_DIM, HEAD_DIM), F32)
    consts = hgrn_constants()
    whole = lambda x: pl.BlockSpec(x.shape, lambda bi, h, i: (0,) * x.ndim)
    return pl.pallas_call(
        _hgrn_kernel,
        grid=(b, ng, nt),
        in_specs=[pl.BlockSpec(blk, fwd), pl.BlockSpec(blk, fwd), pl.BlockSpec(blk, fwd),
                  pl.BlockSpec(blk, bwd), pl.BlockSpec(blk, bwd),
                  pl.BlockSpec(blk, lambda bi, h, i: (bi, nt - 1 - i, ng + h))] + [whole(x) for x in consts],
        out_specs=[pl.BlockSpec(blk, fwd), pl.BlockSpec(blk, bwd)],
        out_shape=[out, out],
        scratch_shapes=[state, state, pltpu.VMEM((2 * hps, HGRN_CHUNK, HEAD_DIM), F32),
                        pltpu.VMEM((2 * hps, HGRN_CHUNK, HGRN_CHUNK), BF16)],
        compiler_params=_cparams(("parallel", "parallel", "arbitrary")),
        name="hgrn2",
    )(q, v, logf, q, v, logf, *consts)


def _hgrn_post_kernel(of_ref, ob_ref, gr_ref, gn_ref, o_ref):
    gn = gn_ref[...]
    for hd in range(o_ref.shape[1] // HEAD_DIM):
        sl = slice(hd * HEAD_DIM, (hd + 1) * HEAD_DIM)
        o = _rms(of_ref[:, sl] + ob_ref[:, sl], gn)
        o_ref[:, sl] = (o * gr_ref[:, sl].astype(F32)).astype(o_ref.dtype)


def hgrn_post(o_fw, o_bw, g_silu, g_norm, tm=512):
    m, d = o_fw.shape
    tm = min(tm, m)
    row = pl.BlockSpec((tm, d), lambda i: (i, 0))
    return pl.pallas_call(
        _hgrn_post_kernel,
        grid=(m // tm,),
        in_specs=[row, row, row, pl.BlockSpec((1, HEAD_DIM), lambda i: (0, 0))],
        out_specs=row,
        out_shape=jax.ShapeDtypeStruct((m, d), BF16),
        compiler_params=_cparams(("parallel",)),
        name="hgrn_post",
    )(o_fw, o_bw, g_silu, g_norm.reshape(1, HEAD_DIM))


def _merge_kernel(att_ref, or_ref, woa_ref, woh_ref, sa_ref, sh_ref, o_ref):
    tm = o_ref.shape[0]
    rows_per = min(MM_ROWS, tm)
    for r in range(0, tm, rows_per):
        rs = slice(r, r + rows_per)
        ya = jnp.dot(att_ref[rs, :], woa_ref[...], preferred_element_type=F32)
        yh = jnp.dot(or_ref[rs, :], woh_ref[...], preferred_element_type=F32)
        o_ref[rs, :] = (sa_ref[rs, :].astype(F32) * ya + sh_ref[rs, :].astype(F32) * yh).astype(o_ref.dtype)


def gated_merge(att, o_r, w_oa, w_oh, gates, tm=1024, tn=512):
    m, d = att.shape
    tm = min(tm, m)
    nj = d // tn
    row = pl.BlockSpec((tm, d), lambda i, j: (i, 0))
    wsp = pl.BlockSpec((d, tn), lambda i, j: (0, j))
    return pl.pallas_call(
        _merge_kernel,
        grid=(m // tm, nj),
        in_specs=[row, row, wsp, wsp,
                  pl.BlockSpec((tm, tn), lambda i, j: (i, j)),
                  pl.BlockSpec((tm, tn), lambda i, j: (i, nj + j))],
        out_specs=pl.BlockSpec((tm, tn), lambda i, j: (i, j)),
        out_shape=jax.ShapeDtypeStruct((m, d), BF16),
        compiler_params=_cparams(("parallel", "arbitrary")),
        name="gated_merge",
    )(att, o_r, w_oa, w_oh, gates, gates)


def _resid_proj_kernel(x_ref, a_ref, w_ref, o_ref):
    tm = o_ref.shape[0]
    rows_per = min(MM_ROWS, tm)
    for r in range(0, tm, rows_per):
        rs = slice(r, r + rows_per)
        o_ref[rs, :] = x_ref[rs, :] + jnp.dot(a_ref[rs, :], w_ref[...], preferred_element_type=F32)


def resid_proj(x, a, w, tm=1024, tn=512):
    m, d = x.shape
    tm = min(tm, m)
    return pl.pallas_call(
        _resid_proj_kernel,
        grid=(m // tm, d // tn),
        in_specs=[pl.BlockSpec((tm, tn), lambda i, j: (i, j)), pl.BlockSpec((tm, a.shape[1]), lambda i, j: (i, 0)),
                  pl.BlockSpec((a.shape[1], tn), lambda i, j: (0, j))],
        out_specs=pl.BlockSpec((tm, tn), lambda i, j: (i, j)),
        out_shape=jax.ShapeDtypeStruct((m, d), F32),
        compiler_params=_cparams(("parallel", "arbitrary")),
        name="out_proj",
    )(x, a, w)


def _mlp_kernel(x_ref, g_ref, wu_ref, wd_ref, o_ref, h_scr):
    @pl.when(pl.program_id(1) == 0)
    def _():
        x = x_ref[...]
        h_scr[...] = _rms(x, g_ref[...]).astype(h_scr.dtype)
        o_ref[...] = x

    tm = o_ref.shape[0]
    rows_per = min(MM_ROWS, tm)
    for r in range(0, tm, rows_per):
        rs = slice(r, r + rows_per)
        u = jnp.maximum(jnp.dot(h_scr[rs, :], wu_ref[...], preferred_element_type=F32), 0.0)
        o_ref[rs, :] += jnp.dot((u * u).astype(BF16), wd_ref[...], preferred_element_type=F32)


def mlp_block(x, gain, w_up, w_down, tm=1024, tf=512):
    m, d = x.shape
    ff = w_up.shape[1]
    tm = min(tm, m)
    return pl.pallas_call(
        _mlp_kernel,
        grid=(m // tm, ff // tf),
        in_specs=[pl.BlockSpec((tm, d), lambda i, f: (i, 0)), pl.BlockSpec((1, d), lambda i, f: (0, 0)),
                  pl.BlockSpec((d, tf), lambda i, f: (0, f)), pl.BlockSpec((tf, d), lambda i, f: (f, 0))],
        out_specs=pl.BlockSpec((tm, d), lambda i, f: (i, 0)),
        out_shape=jax.ShapeDtypeStruct((m, d), F32),
        scratch_shapes=[pltpu.VMEM((tm, d), BF16)],
        compiler_params=_cparams(("parallel", "arbitrary")),
        name="mlp",
    )(x, gain.reshape(1, d), w_up, w_down)


def _ple_kernel(x_ref, g_ref, wg_ref, p_ref, wp_ref, gf_ref, o_ref):
    x = x_ref[...]
    h = _rms(x, g_ref[...]).astype(BF16)
    gate = _sigmoid(jnp.dot(h, wg_ref[...], preferred_element_type=F32))
    emb = jnp.dot(p_ref[...].astype(BF16), wp_ref[...], preferred_element_type=F32)
    o_ref[...] = _rms(x + gate * emb, gf_ref[...])


def ple_final(x, gain, w_gate, p, w_p, g_final, tm=512):
    m, d = x.shape
    c = p.shape[1]
    tm = min(tm, m)
    const = lambda i: (0, 0)
    return pl.pallas_call(
        _ple_kernel,
        grid=(m // tm,),
        in_specs=[pl.BlockSpec((tm, d), lambda i: (i, 0)), pl.BlockSpec((1, d), const),
                  pl.BlockSpec((d, d), const), pl.BlockSpec((tm, c), lambda i: (i, 0)),
                  pl.BlockSpec((c, d), const), pl.BlockSpec((1, d), const)],
        out_specs=pl.BlockSpec((tm, d), lambda i: (i, 0)),
        out_shape=jax.ShapeDtypeStruct((m, d), F32),
        compiler_params=_cparams(("parallel",)),
        name="ple_final",
    )(x, gain.reshape(1, d), w_gate, p, w_p, g_final.reshape(1, d))


def kernel(x, p, g_mix, w_in, g_q, g_k, w_o_attn, hgrn_lb, g_hgrn, w_o_hgrn, w_out, g_mlp, w_up, w_down,
           g_ple, w_ple_gate, w_ple, g_final):
    b, s, d = x.shape
    m = b * s
    depth = w_in.shape[0]
    attn_q = N_Q_HEADS * HEAD_DIM
    attn_kv = N_KV_HEADS * HEAD_DIM
    hk = HGRN_HEADS * HEAD_DIM
    rope = rope_tables(s)
    lb_all = jnp.cumsum(jax.nn.softmax(hgrn_lb.astype(F32), axis=0), axis=0)

    xf = x.reshape(m, d)
    for i in range(depth):
        w = w_in[i].astype(BF16)
        h = rmsnorm_bf16(xf, g_mix[i])
        c0 = 0
        w_qk = permute_heads(w[:, :attn_q + attn_kv])
        q_a = proj_qk(h, w_qk[:, :attn_q], g_q[i], rope, s, HEAD_DIM ** -0.5 * LOG2_E); c0 += attn_q
        k_a = proj_qk(h, w_qk[:, attn_q:], g_k[i], rope, s, 1.0); c0 += attn_kv
        v_a = proj_plain(h, w, c0, attn_kv, None); c0 += attn_kv
        q_r = proj_plain(h, w, c0, hk, "silu"); c0 += hk
        logf = proj_logf(h, w, c0, 2 * hk, lb_all[i]); c0 += 2 * hk
        i_r = proj_plain(h, w, c0, hk, None); c0 += hk
        g_r = proj_plain(h, w, c0, hk, "silu"); c0 += hk
        gates = proj_plain(h, w, c0, 2 * d, "sigmoid"); c0 += 2 * d

        att = gqa_attention(q_a.reshape(b, s, attn_q), k_a.reshape(b, s, attn_kv), v_a.reshape(b, s, attn_kv))
        o_fw, o_bw = hgrn2_bidir(q_r.reshape(b, s, hk), i_r.reshape(b, s, hk), logf.reshape(b, s, 2 * hk))
        o_r = hgrn_post(o_fw.reshape(m, hk), o_bw.reshape(m, hk), g_r, g_hgrn[i])
        mixed = gated_merge(att.reshape(m, attn_q), o_r, w_o_attn[i].astype(BF16), w_o_hgrn[i].astype(BF16), gates)
        xf = resid_proj(xf, mixed, w_out[i].astype(BF16))
        xf = mlp_block(xf, g_mlp[i], w_up[i].astype(BF16), w_down[i].astype(BF16))
        assert depth == 1
        xf = ple_final(xf, g_ple[i], w_ple_gate[i].astype(BF16), p[i].reshape(m, -1), w_ple[i].astype(BF16),
                       g_final)
    return xf.reshape(b, s, d)
```

```python
import functools

import jax
import jax.numpy as jnp
from jax import lax
from jax.experimental import pallas as pl
from jax.experimental.pallas import tpu as pltpu

F32 = jnp.float32
BF16 = jnp.bfloat16

EPS = 1e-6
LOG2_E = 1.4426950408889634
HEAD_DIM = 128
N_Q_HEADS = 16
N_KV_HEADS = 4
GQA_GROUPS = N_Q_HEADS // N_KV_HEADS
GRID_W = 64
ROPE_THETA = 10000.0
HGRN_HEADS = 16
LANES = 128
SUBLANES = 8
VMEM_LIMIT = 56 * 1024 * 1024


def _cparams(sem):
    return pltpu.CompilerParams(dimension_semantics=sem, vmem_limit_bytes=VMEM_LIMIT)


def _sigmoid(x):
    return 1.0 / (1.0 + jnp.exp(-x))


def _rms(x, gain):
    ms = jnp.mean(x * x, axis=-1, keepdims=True)
    return x * lax.rsqrt(ms + EPS) * gain


def _rmsnorm_kernel(x_ref, g_ref, o_ref):
    o_ref[...] = _rms(x_ref[...], g_ref[...]).astype(o_ref.dtype)


def rmsnorm_bf16(x, gain, tm=512):
    m, d = x.shape
    tm = min(tm, m)
    return pl.pallas_call(
        _rmsnorm_kernel,
        grid=(m // tm,),
        in_specs=[pl.BlockSpec((tm, d), lambda i: (i, 0)), pl.BlockSpec((1, d), lambda i: (0, 0))],
        out_specs=pl.BlockSpec((tm, d), lambda i: (i, 0)),
        out_shape=jax.ShapeDtypeStruct((m, d), BF16),
        compiler_params=_cparams(("parallel",)),
        name="rmsnorm",
    )(x, gain.reshape(1, d))


QK_SUB = 2 * HEAD_DIM


MM_ROWS = 512


def _subtiled_matmul(h_ref, w_ref, epilogue):
    tm = h_ref.shape[0]
    rows_per = min(MM_ROWS, tm)
    units = [(slice(r, r + rows_per), slice(c0, c0 + QK_SUB))
             for c0 in range(0, w_ref.shape[1], QK_SUB) for r in range(0, tm, rows_per)]
    weights = {}

    def matmul(u):
        rs, cs = u
        if cs.start not in weights:
            weights.clear()
            weights[cs.start] = w_ref[:, cs].astype(BF16)
        return jnp.dot(h_ref[rs, :], weights[cs.start], preferred_element_type=F32)

    prev = matmul(units[0])
    for j in range(1, len(units)):
        cur = matmul(units[j])
        epilogue(prev, *units[j - 1])
        prev = cur
    epilogue(prev, *units[-1])


def _proj_plain_kernel(h_ref, w_ref, o_ref, *, act):
    def epilogue(acc, rs, cs):
        if act == "silu":
            acc = acc * _sigmoid(acc)
        elif act == "sigmoid":
            acc = _sigmoid(acc)
        o_ref[rs, cs] = acc.astype(o_ref.dtype)

    _subtiled_matmul(h_ref, w_ref, epilogue)


def _proj_logf_kernel(h_ref, w_ref, lb_ref, o_ref):
    def epilogue(acc, rs, cs):
        lb = lb_ref[:, cs]
        o_ref[rs, cs] = jnp.log(lb + (1.0 - lb) * _sigmoid(acc)) * LOG2_E

    _subtiled_matmul(h_ref, w_ref, epilogue)


def _proj_qk_kernel(h_ref, w_ref, cg_ref, sg_ref, o_ref):
    ones = jnp.ones((HEAD_DIM, HEAD_DIM), BF16)

    def epilogue(acc, rs, cs):
        cg, sg = cg_ref[rs, :], sg_ref[rs, :]
        for hd in range(QK_SUB // HEAD_DIM):
            x = acc[:, hd * HEAD_DIM:(hd + 1) * HEAD_DIM]
            ssq = jnp.dot((x * x).astype(BF16), ones, preferred_element_type=F32)
            r = lax.rsqrt(ssq * (1.0 / HEAD_DIM) + EPS)
            y = r * (x * cg + pltpu.roll(x, HEAD_DIM // 2, 1) * sg)
            c0 = cs.start + hd * HEAD_DIM
            o_ref[rs, c0:c0 + HEAD_DIM] = y.astype(o_ref.dtype)

    _subtiled_matmul(h_ref, w_ref, epilogue)


def _proj_call(kernel, h, w, col0, ncols, out_dtype, extra=(), extra_specs=(), tm=1024, tn=1024):
    m, d = h.shape
    tm = min(tm, m)
    tn = min(tn, ncols)
    assert col0 % tn == 0 and ncols % tn == 0 and m % tm == 0
    jb = col0 // tn
    return pl.pallas_call(
        kernel,
        grid=(m // tm, ncols // tn),
        in_specs=[pl.BlockSpec((tm, d), lambda i, j: (i, 0)),
                  pl.BlockSpec((d, tn), lambda i, j: (0, jb + j))] + list(extra_specs),
        out_specs=pl.BlockSpec((tm, tn), lambda i, j: (i, j)),
        out_shape=jax.ShapeDtypeStruct((m, ncols), out_dtype),
        compiler_params=_cparams(("parallel", "arbitrary")),
        name="in_proj",
    )(h, w, *extra)


def proj_plain(h, w, col0, ncols, act, out_dtype=BF16):
    return _proj_call(functools.partial(_proj_plain_kernel, act=act), h, w, col0, ncols, out_dtype)


def proj_logf(h, w, col0, ncols, lb, tn=1024):
    return _proj_call(_proj_logf_kernel, h, w, col0, ncols, F32, extra=(lb.reshape(1, ncols),),
                      extra_specs=(pl.BlockSpec((1, tn), lambda i, j: (0, j)),), tn=tn)


def proj_qk(h, w, col0, ncols, gain, rope, seq, scale, tm=1024, tn=1024):
    cos, sin = rope
    gp = gain.astype(F32)[jnp.array(ROPE_PERM)]
    cg = cos * (gp * scale)[None, :]
    sg = sin * (jnp.roll(gp, HEAD_DIM // 2) * scale)[None, :]
    tm = min(tm, seq)
    tn = min(tn, ncols)
    nsb = seq // tm
    tab = pl.BlockSpec((tm, HEAD_DIM), lambda i, j: (i % nsb, 0))
    return _proj_call(_proj_qk_kernel, h, w, col0, ncols, BF16, extra=(cg, sg), extra_specs=(tab, tab),
                      tm=tm, tn=tn)


ROPE_PERM = tuple(list(range(0, 32)) + list(range(64, 96)) + list(range(32, 64)) + list(range(96, 128)))


def permute_heads(w):
    d, n = w.shape
    quarter = HEAD_DIM // 4
    return w.reshape(d, n // HEAD_DIM, 2, 2, quarter).transpose(0, 1, 3, 2, 4).reshape(d, n)


def rope_tables(seq):
    half = HEAD_DIM // 2
    t = jnp.arange(seq, dtype=jnp.int32)
    row = (t // GRID_W).astype(F32)
    col = (t % GRID_W).astype(F32)
    inv_freq = ROPE_THETA ** (-jnp.arange(0, half, 2, dtype=F32) / half)
    ang = jnp.concatenate([row[:, None] * inv_freq[None, :], col[:, None] * inv_freq[None, :]], axis=-1)
    cos = jnp.concatenate([jnp.cos(ang), jnp.cos(ang)], axis=-1)
    sin = jnp.concatenate([-jnp.sin(ang), jnp.sin(ang)], axis=-1)
    return cos, sin


ATTN_TQ = 128
ATTN_TKC = 8192


def _attn_kernel(q0_ref, qa_ref, qb_ref, k_ref, v_ref, o_ref, s_scr, mrun_scr, mcur_scr, acc_scr, *, tkc):
    t = pl.program_id(2)
    nt = pl.num_programs(2)
    tq = qa_ref.shape[1]
    nch = k_ref.shape[1] // tkc
    nsl = tkc // LANES
    ones = jnp.ones((tkc, LANES), BF16)

    def stack(q_ref):
        return jnp.concatenate([q_ref[0, :, g * HEAD_DIM:(g + 1) * HEAD_DIM] for g in range(GQA_GROUPS)], axis=0)

    def pass1(qs, c, slot):
        off = pl.multiple_of(c * tkc, tkc)
        s = lax.dot_general(qs, k_ref[0, pl.ds(off, tkc), :], (((1,), (1,)), ((), ())),
                            preferred_element_type=F32)
        s_scr[slot, c] = s
        m = s[:, :LANES]
        for i in range(1, nsl):
            m = jnp.maximum(m, s[:, i * LANES:(i + 1) * LANES])
        mrun_scr[...] = jnp.maximum(mrun_scr[...], m)

    def pass2(c, slot):
        off = pl.multiple_of(c * tkc, tkc)
        vext = jnp.concatenate([v_ref[0, pl.ds(off, tkc), :], ones], axis=1)
        m = mcur_scr[...]
        p = jnp.concatenate([jnp.exp2(s_scr[slot, c, :, i * LANES:(i + 1) * LANES] - m) for i in range(nsl)],
                            axis=1)
        acc_scr[...] += jnp.dot(p.astype(BF16), vext, preferred_element_type=F32)

    def begin_pass1():
        mrun_scr[...] = jnp.full(mrun_scr.shape, -jnp.inf, F32)

    def end_pass1():
        mcur_scr[...] = jnp.broadcast_to(jnp.max(mrun_scr[...], axis=1, keepdims=True), mcur_scr.shape)
        acc_scr[...] = jnp.zeros(acc_scr.shape, F32)

    def emit(half):
        o = acc_scr[:, :HEAD_DIM] / acc_scr[:, HEAD_DIM:]
        for g in range(GQA_GROUPS):
            o_ref[0, half * tq:(half + 1) * tq, g * HEAD_DIM:(g + 1) * HEAD_DIM] = (
                o[g * tq:(g + 1) * tq].astype(o_ref.dtype))

    @pl.when(t == 0)
    def _():
        qs0 = stack(q0_ref)
        begin_pass1()

        def body_0(c, carry):
            pass1(qs0, c, 0)
            return carry

        lax.fori_loop(0, nch, body_0, 0)
        end_pass1()

    qs = stack(qa_ref)
    begin_pass1()

    def body_x(c, carry):
        pass1(qs, c, 1)
        pass2(c, 0)
        return carry

    lax.fori_loop(0, nch, body_x, 0)
    emit(0)
    end_pass1()

    @pl.when(t < nt - 1)
    def _():
        qs2 = stack(qb_ref)
        begin_pass1()

        def body_y(c, carry):
            pass1(qs2, c, 0)
            pass2(c, 1)
            return carry

        lax.fori_loop(0, nch, body_y, 0)
        emit(1)
        end_pass1()

    @pl.when(t == nt - 1)
    def _():
        def body_z(c, carry):
            pass2(c, 1)
            return carry

        lax.fori_loop(0, nch, body_z, 0)
        emit(1)


def gqa_attention(q, k, v, tq=ATTN_TQ, tkc=ATTN_TKC):
    b, s, _ = q.shape
    tkc = min(tkc, s)
    nq = s // tq
    assert nq % 2 == 0 and s % tkc == 0
    gw = GQA_GROUPS * HEAD_DIM
    rows = GQA_GROUPS * tq
    qblk = (1, tq, gw)
    kvblk = (1, s, HEAD_DIM)
    return pl.pallas_call(
        functools.partial(_attn_kernel, tkc=tkc),
        grid=(b, N_KV_HEADS, nq // 2),
        in_specs=[pl.BlockSpec(qblk, lambda bi, h, t: (bi, 0, h)),
                  pl.BlockSpec(qblk, lambda bi, h, t: (bi, 2 * t + 1, h)),
                  pl.BlockSpec(qblk, lambda bi, h, t: (bi, jnp.minimum(2 * t + 2, nq - 1), h)),
                  pl.BlockSpec(kvblk, lambda bi, h, t: (bi, 0, h)),
                  pl.BlockSpec(kvblk, lambda bi, h, t: (bi, 0, h))],
        out_specs=pl.BlockSpec((1, 2 * tq, gw), lambda bi, h, t: (bi, t, h)),
        out_shape=jax.ShapeDtypeStruct(q.shape, BF16),
        scratch_shapes=[pltpu.VMEM((2, s // tkc, rows, tkc), F32), pltpu.VMEM((rows, LANES), F32),
                        pltpu.VMEM((rows, LANES), F32), pltpu.VMEM((rows, 2 * HEAD_DIM), F32)],
        compiler_params=_cparams(("parallel", "parallel", "arbitrary")),
        name="gqa_attention",
    )(q, q, q, k, v)


HGRN_CHUNK = 128
HGRN_LAG = 2


def _group_row(x, group, row):
    c, n = x.shape
    xr = x.reshape(c // group, group, n)
    return jnp.broadcast_to(xr[:, row:row + 1, :], xr.shape).reshape(c, n)


def _boundary_row(x, m, rev, odd_block):
    ref_row = m if rev else m - 1
    if 2 * m >= SUBLANES:
        return _group_row(x, 2 * m, ref_row)
    if m == 1:
        c, n = x.shape
        x3 = x.reshape(c // SUBLANES, SUBLANES, n)
        other = pltpu.roll(x3, SUBLANES - 1 if rev else 1, 1).reshape(c, n)
        return jnp.where(odd_block(1), x, other) if rev else jnp.where(odd_block(1), other, x)
    assert 4 * m == SUBLANES
    return jnp.where(odd_block(2 * m), _group_row(x, SUBLANES, 2 * m + ref_row), _group_row(x, SUBLANES, ref_row))


def hgrn_constants():
    import numpy as np
    c = HGRN_CHUNK
    t = np.arange(c)[:, None]
    s = np.arange(c)[None, :]
    pair, tri = [], []
    for rev in (False, True):
        masks = [t == s]
        m = 1
        while m < c:
            same = (t // (2 * m)) == (s // (2 * m))
            t_query = ((t // m) % 2) == (0 if rev else 1)
            s_key = ((s // m) % 2) == (1 if rev else 0)
            masks.append(same & t_query & s_key)
            m *= 2
        pair.append(np.stack(masks))
        tri.append((s >= t) if rev else (s <= t))
    rows = np.stack([np.broadcast_to(((t // n) % 2) == 1, (c, LANES)) for n in (1, 2, 4)])
    return (jnp.asarray(np.stack(pair), jnp.int32), jnp.asarray(rows, jnp.int32),
            jnp.asarray(np.stack(tri), BF16))


def _neg_abs(x):
    return pltpu.bitcast(pltpu.bitcast(x, jnp.uint32) | jnp.uint32(0x80000000), F32)


def _nt(x, y):
    return lax.dot_general(x, y, (((1,), (1,)), ((), ())), preferred_element_type=F32)


def _hgrn_chunk(q_ref, v_ref, lf_ref, o_ref, st_scr, a_scr, p_scr, rows3, col, slot, consts, *, rev):
    pair_ref, rows_ref, tri_ref = consts
    r_prev, r0, r_next = rows3
    d = 1 if rev else 0
    odd_block = lambda n: rows_ref[{1: 0, 2: 1, 4: 2}[n]] != 0
    c = HGRN_CHUNK
    cols = slice(col * HEAD_DIM, (col + 1) * HEAD_DIM)

    pv = jnp.dot(p_scr[slot], v_ref[0, pl.ds(r_prev, c), cols], preferred_element_type=F32)
    a2 = _cumsum_issue(lf_ref[0, pl.ds(r_next, c), cols], tri_ref[d])

    qb = q_ref[0, pl.ds(r0, c), cols]
    vb = v_ref[0, pl.ds(r0, c), cols]
    k = 1.0 - jnp.exp2(lf_ref[0, pl.ds(r0, c), cols])
    sc = _nt(qb, k.astype(BF16))
    q = qb.astype(F32)
    a = a_scr[slot]
    edge = 0 if rev else c - 1
    a_end = a[edge:edge + 1, :]

    st = st_scr[col]
    o = _nt((q * jnp.exp2(a)).astype(BF16), st.astype(BF16))
    kdec = (k * jnp.exp2(a_end - a)).astype(BF16)
    st_new = lax.dot_general(vb, kdec, (((0,), (0,)), ((), ())), preferred_element_type=F32)
    st_decayed = st * jnp.exp2(a_end)

    nstrip = c // SUBLANES
    strip = lambda x, i: x[i * SUBLANES:(i + 1) * SUBLANES]
    p = [None] * nstrip

    def apply(pending):
        level, scores, rows = pending
        for j, i in enumerate(rows):
            mask = pair_ref[d, level, i * SUBLANES:(i + 1) * SUBLANES, :] != 0
            p[i] = jnp.where(mask, strip(scores, j), 0.0 if p[i] is None else p[i])

    query_first = rev

    def halves(m):
        out = []
        for g in range(c // (2 * m)):
            first = slice(g * 2 * m, g * 2 * m + m)
            second = slice(g * 2 * m + m, (g + 1) * 2 * m)
            out.append((first, second) if query_first else (second, first))
        return out

    def factor(m):
        ref = _boundary_row(a, m, rev, odd_block)
        if m < SUBLANES:
            return jnp.exp2(_neg_abs(a - ref))
        parts = []
        for qs, ks in halves(m):
            dq, dk = a[qs] - ref[qs], ref[ks] - a[ks]
            parts += [dq, dk] if query_first else [dk, dq]
        return jnp.exp2(jnp.concatenate(parts, axis=0))

    pending = [(0, sc, list(range(nstrip)))]
    nlevel = pair_ref.shape[1] - 1
    m = 1
    e = factor(m)
    for level in range(1, nlevel + 1):
        e_next = factor(2 * m) if level < nlevel else None
        if m < SUBLANES:
            qk = jnp.where(odd_block(m), k, q) if query_first else jnp.where(odd_block(m), q, k)
            z = (qk * e).astype(BF16)
            sc = _nt(z, z)
            rows = list(range(nstrip))
        else:
            parts, qparts, rows = [], [], []
            for qs, ks in halves(m):
                zq, zk = q[qs] * e[qs], k[ks] * e[ks]
                parts += [zq, zk] if query_first else [zk, zq]
                qparts.append(zq)
                rows += list(range(qs.start // SUBLANES, qs.stop // SUBLANES))
            z = jnp.concatenate(parts, axis=0).astype(BF16)
            sc = _nt(jnp.concatenate(qparts, axis=0).astype(BF16), z)
        e = e_next
        yield
        if level == 2:
            o_ref[0, pl.ds(r0, c), cols] = o
            st_scr[col] = st_decayed + st_new
        pending.append((level, sc, rows))
        if len(pending) > HGRN_LAG:
            apply(pending.pop(0))
        m *= 2
    a_scr[slot] = a2[:, :LANES] + a2[:, LANES:]
    o_ref[0, pl.ds(r_prev, c), cols] += pv
    yield
    for item in pending:
        apply(item)
    p_scr[slot] = jnp.concatenate(p, axis=0).astype(BF16)


def _cumsum_issue(lf, tri):
    hi = lf.astype(BF16)
    lo = (lf - hi.astype(F32)).astype(BF16)
    return jnp.dot(tri, jnp.concatenate([hi, lo], axis=1), preferred_element_type=F32)


def _interleave(chains):
    chains = list(chains)
    while chains:
        for ch in list(chains):
            try:
                next(ch)
            except StopIteration:
                chains.remove(ch)


def _hgrn_kernel(qf_ref, vf_ref, lff_ref, qb_ref, vb_ref, lfb_ref, pair_ref, rows_ref, tri_ref,
                 of_ref, ob_ref, sf_scr, sb_scr, a_scr, p_scr):
    @pl.when(pl.program_id(2) == 0)
    def _():
        sf_scr[...] = jnp.zeros(sf_scr.shape, F32)
        sb_scr[...] = jnp.zeros(sb_scr.shape, F32)

    c = HGRN_CHUNK
    consts = (pair_ref, rows_ref, tri_ref)
    n = qf_ref.shape[1] // c
    heads = qf_ref.shape[2] // HEAD_DIM
    row = lambda idx: pl.multiple_of(idx * c, c)
    dirs = ((qf_ref, vf_ref, lff_ref, of_ref, sf_scr, False), (qb_ref, vb_ref, lfb_ref, ob_ref, sb_scr, True))
    first = lambda rev: n - 1 if rev else 0
    last = lambda rev: 0 if rev else n - 1

    for hd in range(heads):
        cols = slice(hd * HEAD_DIM, (hd + 1) * HEAD_DIM)
        for d, (_, _, lf_ref, _, _, rev) in enumerate(dirs):
            a2 = _cumsum_issue(lf_ref[0, pl.ds(first(rev) * c, c), cols], tri_ref[d])
            a_scr[2 * hd + d] = a2[:, :LANES] + a2[:, LANES:]
    p_scr[...] = jnp.zeros(p_scr.shape, BF16)

    def body(ci, carry):
        idx = {False: (jnp.maximum(ci - 1, 0), ci, jnp.minimum(ci + 1, n - 1)),
               True: (jnp.minimum(n - ci, n - 1), n - 1 - ci, jnp.maximum(n - 2 - ci, 0))}
        chains = []
        for hd in range(heads):
            for d, (q_ref, v_ref, lf_ref, o_ref, st_scr, rev) in enumerate(dirs):
                rows3 = tuple(row(i) for i in idx[rev])
                chains.append(_hgrn_chunk(q_ref, v_ref, lf_ref, o_ref, st_scr, a_scr, p_scr, rows3, hd,
                                          2 * hd + d, consts, rev=rev))
        _interleave(chains)
        return carry

    lax.fori_loop(0, n, body, 0)

    for hd in range(heads):
        cols = slice(hd * HEAD_DIM, (hd + 1) * HEAD_DIM)
        for d, (_, v_ref, _, o_ref, _, rev) in enumerate(dirs):
            rws = pl.ds(last(rev) * c, c)
            o_ref[0, rws, cols] += jnp.dot(p_scr[2 * hd + d], v_ref[0, rws, cols], preferred_element_type=F32)


HGRN_HEADS_PER_STEP = 4


def hgrn2_bidir(q, v, logf, ts=1024, hps=HGRN_HEADS_PER_STEP):
    b, s, hk = q.shape
    ts = min(ts, s)
    nt = s // ts
    ng = hk // (hps * HEAD_DIM)
    blk = (1, ts, hps * HEAD_DIM)
    fwd = lambda bi, h, i: (bi, i, h)
    bwd = lambda bi, h, i: (bi, nt - 1 - i, h)
    out = jax.ShapeDtypeStruct((b, s, hk), F32)
    state = pltpu.VMEM((hps, HEAD_DIM, HEAD_DIM), F32)
    consts = hgrn_constants()
    whole = lambda x: pl.BlockSpec(x.shape, lambda bi, h, i: (0,) * x.ndim)
    return pl.pallas_call(
        _hgrn_kernel,
        grid=(b, ng, nt),
        in_specs=[pl.BlockSpec(blk, fwd), pl.BlockSpec(blk, fwd), pl.BlockSpec(blk, fwd),
                  pl.BlockSpec(blk, bwd), pl.BlockSpec(blk, bwd),
                  pl.BlockSpec(blk, lambda bi, h, i: (bi, nt - 1 - i, ng + h))] + [whole(x) for x in consts],
        out_specs=[pl.BlockSpec(blk, fwd), pl.BlockSpec(blk, bwd)],
        out_shape=[out, out],
        scratch_shapes=[state, state, pltpu.VMEM((2 * hps, HGRN_CHUNK, HEAD_DIM), F32),
                        pltpu.VMEM((2 * hps, HGRN_CHUNK, HGRN_CHUNK), BF16)],
        compiler_params=_cparams(("parallel", "parallel", "arbitrary")),
        name="hgrn2",
    )(q, v, logf, q, v, logf, *consts)


def _hgrn_post_kernel(of_ref, ob_ref, gr_ref, gn_ref, o_ref):
    gn = gn_ref[...]
    for hd in range(o_ref.shape[1] // HEAD_DIM):
        sl = slice(hd * HEAD_DIM, (hd + 1) * HEAD_DIM)
        o = _rms(of_ref[:, sl] + ob_ref[:, sl], gn)
        o_ref[:, sl] = (o * gr_ref[:, sl].astype(F32)).astype(o_ref.dtype)


def hgrn_post(o_fw, o_bw, g_silu, g_norm, tm=512):
    m, d = o_fw.shape
    tm = min(tm, m)
    row = pl.BlockSpec((tm, d), lambda i: (i, 0))
    return pl.pallas_call(
        _hgrn_post_kernel,
        grid=(m // tm,),
        in_specs=[row, row, row, pl.BlockSpec((1, HEAD_DIM), lambda i: (0, 0))],
        out_specs=row,
        out_shape=jax.ShapeDtypeStruct((m, d), BF16),
        compiler_params=_cparams(("parallel",)),
        name="hgrn_post",
    )(o_fw, o_bw, g_silu, g_norm.reshape(1, HEAD_DIM))


def _merge_kernel(att_ref, or_ref, woa_ref, woh_ref, sa_ref, sh_ref, o_ref):
    tm = o_ref.shape[0]
    rows_per = min(MM_ROWS, tm)
    woa, woh = woa_ref[...].astype(BF16), woh_ref[...].astype(BF16)
    for r in range(0, tm, rows_per):
        rs = slice(r, r + rows_per)
        ya = jnp.dot(att_ref[rs, :], woa, preferred_element_type=F32)
        yh = jnp.dot(or_ref[rs, :], woh, preferred_element_type=F32)
        o_ref[rs, :] = (sa_ref[rs, :].astype(F32) * ya + sh_ref[rs, :].astype(F32) * yh).astype(o_ref.dtype)


def gated_merge(att, o_r, w_oa, w_oh, gates, tm=1024, tn=512):
    m, d = att.shape
    tm = min(tm, m)
    nj = d // tn
    row = pl.BlockSpec((tm, d), lambda i, j: (i, 0))
    wsp = pl.BlockSpec((d, tn), lambda i, j: (0, j))
    return pl.pallas_call(
        _merge_kernel,
        grid=(m // tm, nj),
        in_specs=[row, row, wsp, wsp,
                  pl.BlockSpec((tm, tn), lambda i, j: (i, j)),
                  pl.BlockSpec((tm, tn), lambda i, j: (i, nj + j))],
        out_specs=pl.BlockSpec((tm, tn), lambda i, j: (i, j)),
        out_shape=jax.ShapeDtypeStruct((m, d), BF16),
        compiler_params=_cparams(("parallel", "arbitrary")),
        name="gated_merge",
    )(att, o_r, w_oa, w_oh, gates, gates)


def _resid_proj_kernel(x_ref, a_ref, w_ref, o_ref):
    tm = o_ref.shape[0]
    rows_per = min(MM_ROWS, tm)
    w = w_ref[...].astype(BF16)
    for r in range(0, tm, rows_per):
        rs = slice(r, r + rows_per)
        o_ref[rs, :] = x_ref[rs, :] + jnp.dot(a_ref[rs, :], w, preferred_element_type=F32)


def resid_proj(x, a, w, tm=1024, tn=512):
    m, d = x.shape
    tm = min(tm, m)
    return pl.pallas_call(
        _resid_proj_kernel,
        grid=(m // tm, d // tn),
        in_specs=[pl.BlockSpec((tm, tn), lambda i, j: (i, j)), pl.BlockSpec((tm, a.shape[1]), lambda i, j: (i, 0)),
                  pl.BlockSpec((a.shape[1], tn), lambda i, j: (0, j))],
        out_specs=pl.BlockSpec((tm, tn), lambda i, j: (i, j)),
        out_shape=jax.ShapeDtypeStruct((m, d), F32),
        compiler_params=_cparams(("parallel", "arbitrary")),
        name="out_proj",
    )(x, a, w)


def _mlp_kernel(x_ref, g_ref, wu_ref, wd_ref, o_ref, h_scr):
    @pl.when(pl.program_id(1) == 0)
    def _():
        x = x_ref[...]
        h_scr[...] = _rms(x, g_ref[...]).astype(h_scr.dtype)
        o_ref[...] = x

    tm = o_ref.shape[0]
    rows_per = min(MM_ROWS, tm)
    wu, wd = wu_ref[...].astype(BF16), wd_ref[...].astype(BF16)
    for r in range(0, tm, rows_per):
        rs = slice(r, r + rows_per)
        u = jnp.maximum(jnp.dot(h_scr[rs, :], wu, preferred_element_type=F32), 0.0)
        o_ref[rs, :] += jnp.dot((u * u).astype(BF16), wd, preferred_element_type=F32)


def mlp_block(x, gain, w_up, w_down, tm=1024, tf=512):
    m, d = x.shape
    ff = w_up.shape[1]
    tm = min(tm, m)
    return pl.pallas_call(
        _mlp_kernel,
        grid=(m // tm, ff // tf),
        in_specs=[pl.BlockSpec((tm, d), lambda i, f: (i, 0)), pl.BlockSpec((1, d), lambda i, f: (0, 0)),
                  pl.BlockSpec((d, tf), lambda i, f: (0, f)), pl.BlockSpec((tf, d), lambda i, f: (f, 0))],
        out_specs=pl.BlockSpec((tm, d), lambda i, f: (i, 0)),
        out_shape=jax.ShapeDtypeStruct((m, d), F32),
        scratch_shapes=[pltpu.VMEM((tm, d), BF16)],
        compiler_params=_cparams(("parallel", "arbitrary")),
        name="mlp",
    )(x, gain.reshape(1, d), w_up, w_down)


def _ple_kernel(x_ref, g_ref, wg_ref, p_ref, wp_ref, gf_ref, o_ref):
    x = x_ref[...]
    h = _rms(x, g_ref[...]).astype(BF16)
    gate = _sigmoid(jnp.dot(h, wg_ref[...], preferred_element_type=F32))
    emb = jnp.dot(p_ref[...].astype(BF16), wp_ref[...], preferred_element_type=F32)
    o_ref[...] = _rms(x + gate * emb, gf_ref[...])


def ple_final(x, gain, w_gate, p, w_p, g_final, tm=512):
    m, d = x.shape
    c = p.shape[1]
    tm = min(tm, m)
    const = lambda i: (0, 0)
    return pl.pallas_call(
        _ple_kernel,
        grid=(m // tm,),
        in_specs=[pl.BlockSpec((tm, d), lambda i: (i, 0)), pl.BlockSpec((1, d), const),
                  pl.BlockSpec((d, d), const), pl.BlockSpec((tm, c), lambda i: (i, 0)),
                  pl.BlockSpec((c, d), const), pl.BlockSpec((1, d), const)],
        out_specs=pl.BlockSpec((tm, d), lambda i: (i, 0)),
        out_shape=jax.ShapeDtypeStruct((m, d), F32),
        compiler_params=_cparams(("parallel",)),
        name="ple_final",
    )(x, gain.reshape(1, d), w_gate, p, w_p, g_final.reshape(1, d))


def kernel(x, p, g_mix, w_in, g_q, g_k, w_o_attn, hgrn_lb, g_hgrn, w_o_hgrn, w_out, g_mlp, w_up, w_down,
           g_ple, w_ple_gate, w_ple, g_final):
    b, s, d = x.shape
    m = b * s
    depth = w_in.shape[0]
    attn_q = N_Q_HEADS * HEAD_DIM
    attn_kv = N_KV_HEADS * HEAD_DIM
    hk = HGRN_HEADS * HEAD_DIM
    rope = rope_tables(s)
    lb_all = jnp.cumsum(jax.nn.softmax(hgrn_lb.astype(F32), axis=0), axis=0)

    xf = x.reshape(m, d)
    for i in range(depth):
        w = w_in[i]
        h = rmsnorm_bf16(xf, g_mix[i])
        c0 = 0
        w_qk = permute_heads(w[:, :attn_q + attn_kv]).astype(BF16)
        q_a = proj_qk(h, w_qk, 0, attn_q, g_q[i], rope, s, HEAD_DIM ** -0.5 * LOG2_E); c0 += attn_q
        k_a = proj_qk(h, w_qk, attn_q, attn_kv, g_k[i], rope, s, 1.0); c0 += attn_kv
        v_a = proj_plain(h, w, c0, attn_kv, None); c0 += attn_kv
        q_r = proj_plain(h, w, c0, hk, "silu"); c0 += hk
        logf = proj_logf(h, w, c0, 2 * hk, lb_all[i]); c0 += 2 * hk
        i_r = proj_plain(h, w, c0, hk, None); c0 += hk
        g_r = proj_plain(h, w, c0, hk, "silu"); c0 += hk
        gates = proj_plain(h, w, c0, 2 * d, "sigmoid"); c0 += 2 * d

        att = gqa_attention(q_a.reshape(b, s, attn_q), k_a.reshape(b, s, attn_kv), v_a.reshape(b, s, attn_kv))
        o_fw, o_bw = hgrn2_bidir(q_r.reshape(b, s, hk), i_r.reshape(b, s, hk), logf.reshape(b, s, 2 * hk))
        o_r = hgrn_post(o_fw.reshape(m, hk), o_bw.reshape(m, hk), g_r, g_hgrn[i])
        mixed = gated_merge(att.reshape(m, attn_q), o_r, w_o_attn[i], w_o_hgrn[i], gates)
        xf = resid_proj(xf, mixed, w_out[i])
        xf = mlp_block(xf, g_mlp[i], w_up[i].astype(BF16), w_down[i])
        assert depth == 1
        xf = ple_final(xf, g_ple[i], w_ple_gate[i].astype(BF16), p[i].reshape(m, -1), w_ple[i].astype(BF16),
                       g_final)
    return xf.reshape(b, s, d)
```

```python
import functools

import jax
import jax.numpy as jnp
from jax import lax
from jax.experimental import pallas as pl
from jax.experimental.pallas import tpu as pltpu

F32 = jnp.float32
BF16 = jnp.bfloat16

EPS = 1e-6
LOG2_E = 1.4426950408889634
HEAD_DIM = 128
N_Q_HEADS = 16
N_KV_HEADS = 4
GQA_GROUPS = N_Q_HEADS // N_KV_HEADS
GRID_W = 64
ROPE_THETA = 10000.0
HGRN_HEADS = 16
LANES = 128
SUBLANES = 8
VMEM_LIMIT = 56 * 1024 * 1024


def _cparams(sem):
    return pltpu.CompilerParams(dimension_semantics=sem, vmem_limit_bytes=VMEM_LIMIT)


def _sigmoid(x):
    return 1.0 / (1.0 + jnp.exp(-x))


def _rms(x, gain):
    ms = jnp.mean(x * x, axis=-1, keepdims=True)
    return x * lax.rsqrt(ms + EPS) * gain


def _rmsnorm_kernel(x_ref, g_ref, o_ref):
    o_ref[...] = _rms(x_ref[...], g_ref[...]).astype(o_ref.dtype)


def rmsnorm_bf16(x, gain, tm=512):
    m, d = x.shape
    tm = min(tm, m)
    return pl.pallas_call(
        _rmsnorm_kernel,
        grid=(m // tm,),
        in_specs=[pl.BlockSpec((tm, d), lambda i: (i, 0)), pl.BlockSpec((1, d), lambda i: (0, 0))],
        out_specs=pl.BlockSpec((tm, d), lambda i: (i, 0)),
        out_shape=jax.ShapeDtypeStruct((m, d), BF16),
        compiler_params=_cparams(("parallel",)),
        name="rmsnorm",
    )(x, gain.reshape(1, d))


QK_SUB = 2 * HEAD_DIM


MM_ROWS = 512


def _subtiled_matmul(h_ref, w_ref, epilogue):
    tm = h_ref.shape[0]
    rows_per = min(MM_ROWS, tm)
    units = [(slice(r, r + rows_per), slice(c0, c0 + QK_SUB))
             for c0 in range(0, w_ref.shape[1], QK_SUB) for r in range(0, tm, rows_per)]
    weights = {}

    def matmul(u):
        rs, cs = u
        if cs.start not in weights:
            weights.clear()
            weights[cs.start] = w_ref[:, cs].astype(BF16)
        return jnp.dot(h_ref[rs, :], weights[cs.start], preferred_element_type=F32)

    prev = matmul(units[0])
    for j in range(1, len(units)):
        cur = matmul(units[j])
        epilogue(prev, *units[j - 1])
        prev = cur
    epilogue(prev, *units[-1])


def _stationary_weights(w_ref, wb_scr):
    @pl.when(pl.program_id(1) == 0)
    def _():
        wb_scr[...] = w_ref[...].astype(wb_scr.dtype)


def _proj_plain_kernel(h_ref, w_ref, o_ref, wb_scr, *, act):
    def epilogue(acc, rs, cs):
        if act == "silu":
            acc = acc * _sigmoid(acc)
        elif act == "sigmoid":
            acc = _sigmoid(acc)
        o_ref[rs, cs] = acc.astype(o_ref.dtype)

    _stationary_weights(w_ref, wb_scr)
    _subtiled_matmul(h_ref, wb_scr, epilogue)


def _proj_logf_kernel(h_ref, w_ref, lb_ref, o_ref, wb_scr):
    def epilogue(acc, rs, cs):
        lb = lb_ref[:, cs]
        o_ref[rs, cs] = jnp.log(lb + (1.0 - lb) * _sigmoid(acc)) * LOG2_E

    _stationary_weights(w_ref, wb_scr)
    _subtiled_matmul(h_ref, wb_scr, epilogue)


def _proj_qk_kernel(h_ref, w_ref, cg_ref, sg_ref, o_ref, wb_scr):
    ones = jnp.ones((HEAD_DIM, HEAD_DIM), BF16)
    _stationary_weights(w_ref, wb_scr)

    def epilogue(acc, rs, cs):
        cg, sg = cg_ref[rs, :], sg_ref[rs, :]
        for hd in range(QK_SUB // HEAD_DIM):
            x = acc[:, hd * HEAD_DIM:(hd + 1) * HEAD_DIM]
            ssq = jnp.dot((x * x).astype(BF16), ones, preferred_element_type=F32)
            r = lax.rsqrt(ssq * (1.0 / HEAD_DIM) + EPS)
            y = r * (x * cg + pltpu.roll(x, HEAD_DIM // 2, 1) * sg)
            c0 = cs.start + hd * HEAD_DIM
            o_ref[rs, c0:c0 + HEAD_DIM] = y.astype(o_ref.dtype)

    _subtiled_matmul(h_ref, wb_scr, epilogue)


def _proj_call(kernel, h, w, col0, ncols, out_dtype, extra=(), extra_specs=(), tm=1024, tn=1024):
    m, d = h.shape
    tm = min(tm, m)
    tn = min(tn, ncols)
    assert col0 % tn == 0 and ncols % tn == 0 and m % tm == 0
    jb = col0 // tn
    return pl.pallas_call(
        kernel,
        grid=(ncols // tn, m // tm),
        in_specs=[pl.BlockSpec((tm, d), lambda j, i: (i, 0)),
                  pl.BlockSpec((d, tn), lambda j, i: (0, jb + j))] + list(extra_specs),
        out_specs=pl.BlockSpec((tm, tn), lambda j, i: (i, j)),
        out_shape=jax.ShapeDtypeStruct((m, ncols), out_dtype),
        scratch_shapes=[pltpu.VMEM((d, tn), BF16)],
        compiler_params=_cparams(("parallel", "arbitrary")),
        name="in_proj",
    )(h, w, *extra)


def proj_plain(h, w, col0, ncols, act, out_dtype=BF16):
    return _proj_call(functools.partial(_proj_plain_kernel, act=act), h, w, col0, ncols, out_dtype)


def proj_logf(h, w, col0, ncols, lb, tn=1024):
    return _proj_call(_proj_logf_kernel, h, w, col0, ncols, F32, extra=(lb.reshape(1, ncols),),
                      extra_specs=(pl.BlockSpec((1, tn), lambda j, i: (0, j)),), tn=tn)


def proj_qk(h, w, col0, ncols, gain, rope, seq, scale, tm=1024, tn=1024):
    cos, sin = rope
    gp = gain.astype(F32)[jnp.array(ROPE_PERM)]
    cg = cos * (gp * scale)[None, :]
    sg = sin * (jnp.roll(gp, HEAD_DIM // 2) * scale)[None, :]
    tm = min(tm, seq)
    tn = min(tn, ncols)
    nsb = seq // tm
    tab = pl.BlockSpec((tm, HEAD_DIM), lambda j, i: (i % nsb, 0))
    return _proj_call(_proj_qk_kernel, h, w, col0, ncols, BF16, extra=(cg, sg), extra_specs=(tab, tab),
                      tm=tm, tn=tn)


ROPE_PERM = tuple(list(range(0, 32)) + list(range(64, 96)) + list(range(32, 64)) + list(range(96, 128)))


def permute_heads(w):
    d, n = w.shape
    quarter = HEAD_DIM // 4
    return w.reshape(d, n // HEAD_DIM, 2, 2, quarter).transpose(0, 1, 3, 2, 4).reshape(d, n)


def rope_tables(seq):
    half = HEAD_DIM // 2
    t = jnp.arange(seq, dtype=jnp.int32)
    row = (t // GRID_W).astype(F32)
    col = (t % GRID_W).astype(F32)
    inv_freq = ROPE_THETA ** (-jnp.arange(0, half, 2, dtype=F32) / half)
    ang = jnp.concatenate([row[:, None] * inv_freq[None, :], col[:, None] * inv_freq[None, :]], axis=-1)
    cos = jnp.concatenate([jnp.cos(ang), jnp.cos(ang)], axis=-1)
    sin = jnp.concatenate([-jnp.sin(ang), jnp.sin(ang)], axis=-1)
    return cos, sin


ATTN_TQ = 128
ATTN_TKC = 8192


def _attn_kernel(q0_ref, qa_ref, qb_ref, k_ref, v_ref, o_ref, s_scr, mrun_scr, mcur_scr, acc_scr, *, tkc):
    t = pl.program_id(2)
    nt = pl.num_programs(2)
    tq = qa_ref.shape[1]
    nch = k_ref.shape[1] // tkc
    nsl = tkc // LANES
    ones = jnp.ones((tkc, LANES), BF16)

    def stack(q_ref):
        return jnp.concatenate([q_ref[0, :, g * HEAD_DIM:(g + 1) * HEAD_DIM] for g in range(GQA_GROUPS)], axis=0)

    def pass1(qs, c, slot):
        off = pl.multiple_of(c * tkc, tkc)
        s = lax.dot_general(qs, k_ref[0, pl.ds(off, tkc), :], (((1,), (1,)), ((), ())),
                            preferred_element_type=F32)
        s_scr[slot, c] = s
        m = s[:, :LANES]
        for i in range(1, nsl):
            m = jnp.maximum(m, s[:, i * LANES:(i + 1) * LANES])
        mrun_scr[...] = jnp.maximum(mrun_scr[...], m)

    def pass2(c, slot):
        off = pl.multiple_of(c * tkc, tkc)
        vext = jnp.concatenate([v_ref[0, pl.ds(off, tkc), :], ones], axis=1)
        m = mcur_scr[...]
        p = jnp.concatenate([jnp.exp2(s_scr[slot, c, :, i * LANES:(i + 1) * LANES] - m) for i in range(nsl)],
                            axis=1)
        acc_scr[...] += jnp.dot(p.astype(BF16), vext, preferred_element_type=F32)

    def begin_pass1():
        mrun_scr[...] = jnp.full(mrun_scr.shape, -jnp.inf, F32)

    def end_pass1():
        mcur_scr[...] = jnp.broadcast_to(jnp.max(mrun_scr[...], axis=1, keepdims=True), mcur_scr.shape)
        acc_scr[...] = jnp.zeros(acc_scr.shape, F32)

    def emit(half):
        o = acc_scr[:, :HEAD_DIM] / acc_scr[:, HEAD_DIM:]
        for g in range(GQA_GROUPS):
            o_ref[0, half * tq:(half + 1) * tq, g * HEAD_DIM:(g + 1) * HEAD_DIM] = (
                o[g * tq:(g + 1) * tq].astype(o_ref.dtype))

    @pl.when(t == 0)
    def _():
        qs0 = stack(q0_ref)
        begin_pass1()

        def body_0(c, carry):
            pass1(qs0, c, 0)
            return carry

        lax.fori_loop(0, nch, body_0, 0)
        end_pass1()

    qs = stack(qa_ref)
    begin_pass1()

    def body_x(c, carry):
        pass1(qs, c, 1)
        pass2(c, 0)
        return carry

    lax.fori_loop(0, nch, body_x, 0)
    emit(0)
    end_pass1()

    @pl.when(t < nt - 1)
    def _():
        qs2 = stack(qb_ref)
        begin_pass1()

        def body_y(c, carry):
            pass1(qs2, c, 0)
            pass2(c, 1)
            return carry

        lax.fori_loop(0, nch, body_y, 0)
        emit(1)
        end_pass1()

    @pl.when(t == nt - 1)
    def _():
        def body_z(c, carry):
            pass2(c, 1)
            return carry

        lax.fori_loop(0, nch, body_z, 0)
        emit(1)


def gqa_attention(q, k, v, tq=ATTN_TQ, tkc=ATTN_TKC):
    b, s, _ = q.shape
    tkc = min(tkc, s)
    nq = s // tq
    assert nq % 2 == 0 and s % tkc == 0
    gw = GQA_GROUPS * HEAD_DIM
    rows = GQA_GROUPS * tq
    qblk = (1, tq, gw)
    kvblk = (1, s, HEAD_DIM)
    return pl.pallas_call(
        functools.partial(_attn_kernel, tkc=tkc),
        grid=(b, N_KV_HEADS, nq // 2),
        in_specs=[pl.BlockSpec(qblk, lambda bi, h, t: (bi, 0, h)),
                  pl.BlockSpec(qblk, lambda bi, h, t: (bi, 2 * t + 1, h)),
                  pl.BlockSpec(qblk, lambda bi, h, t: (bi, jnp.minimum(2 * t + 2, nq - 1), h)),
                  pl.BlockSpec(kvblk, lambda bi, h, t: (bi, 0, h)),
                  pl.BlockSpec(kvblk, lambda bi, h, t: (bi, 0, h))],
        out_specs=pl.BlockSpec((1, 2 * tq, gw), lambda bi, h, t: (bi, t, h)),
        out_shape=jax.ShapeDtypeStruct(q.shape, BF16),
        scratch_shapes=[pltpu.VMEM((2, s // tkc, rows, tkc), F32), pltpu.VMEM((rows, LANES), F32),
                        pltpu.VMEM((rows, LANES), F32), pltpu.VMEM((rows, 2 * HEAD_DIM), F32)],
        compiler_params=_cparams(("parallel", "parallel", "arbitrary")),
        name="gqa_attention",
    )(q, q, q, k, v)


HGRN_CHUNK = 128
HGRN_LAG = 2


def _group_row(x, group, row):
    c, n = x.shape
    xr = x.reshape(c // group, group, n)
    return jnp.broadcast_to(xr[:, row:row + 1, :], xr.shape).reshape(c, n)


def _boundary_row(x, m, rev, odd_block):
    ref_row = m if rev else m - 1
    if 2 * m >= SUBLANES:
        return _group_row(x, 2 * m, ref_row)
    if m == 1:
        c, n = x.shape
        x3 = x.reshape(c // SUBLANES, SUBLANES, n)
        other = pltpu.roll(x3, SUBLANES - 1 if rev else 1, 1).reshape(c, n)
        return jnp.where(odd_block(1), x, other) if rev else jnp.where(odd_block(1), other, x)
    assert 4 * m == SUBLANES
    return jnp.where(odd_block(2 * m), _group_row(x, SUBLANES, 2 * m + ref_row), _group_row(x, SUBLANES, ref_row))


def hgrn_constants():
    import numpy as np
    c = HGRN_CHUNK
    t = np.arange(c)[:, None]
    s = np.arange(c)[None, :]
    pair, tri = [], []
    for rev in (False, True):
        masks = [t == s]
        m = 1
        while m < c:
            same = (t // (2 * m)) == (s // (2 * m))
            t_query = ((t // m) % 2) == (0 if rev else 1)
            s_key = ((s // m) % 2) == (1 if rev else 0)
            masks.append(same & t_query & s_key)
            m *= 2
        pair.append(np.stack(masks))
        tri.append((s >= t) if rev else (s <= t))
    rows = np.stack([np.broadcast_to(((t // n) % 2) == 1, (c, LANES)) for n in (1, 2, 4)])
    return (jnp.asarray(np.stack(pair), jnp.int32), jnp.asarray(rows, jnp.int32),
            jnp.asarray(np.stack(tri), BF16))


def _neg_abs(x):
    return pltpu.bitcast(pltpu.bitcast(x, jnp.uint32) | jnp.uint32(0x80000000), F32)


def _nt(x, y):
    return lax.dot_general(x, y, (((1,), (1,)), ((), ())), preferred_element_type=F32)


def _hgrn_chunk(q_ref, v_ref, lf_ref, o_ref, st_scr, a_scr, p_scr, rows3, col, slot, consts, *, rev):
    pair_ref, rows_ref, tri_ref = consts
    r_prev, r0, r_next = rows3
    d = 1 if rev else 0
    odd_block = lambda n: rows_ref[{1: 0, 2: 1, 4: 2}[n]] != 0
    c = HGRN_CHUNK
    cols = slice(col * HEAD_DIM, (col + 1) * HEAD_DIM)

    pv = jnp.dot(p_scr[slot], v_ref[0, pl.ds(r_prev, c), cols], preferred_element_type=F32)
    a2 = _cumsum_issue(lf_ref[0, pl.ds(r_next, c), cols], tri_ref[d])

    qb = q_ref[0, pl.ds(r0, c), cols]
    vb = v_ref[0, pl.ds(r0, c), cols]
    k = 1.0 - jnp.exp2(lf_ref[0, pl.ds(r0, c), cols])
    sc = _nt(qb, k.astype(BF16))
    q = qb.astype(F32)
    a = a_scr[slot]
    edge = 0 if rev else c - 1
    a_end = a[edge:edge + 1, :]

    st = st_scr[col]
    o = _nt((q * jnp.exp2(a)).astype(BF16), st.astype(BF16))
    kdec = (k * jnp.exp2(a_end - a)).astype(BF16)
    st_new = lax.dot_general(vb, kdec, (((0,), (0,)), ((), ())), preferred_element_type=F32)
    st_decayed = st * jnp.exp2(a_end)

    nstrip = c // SUBLANES
    strip = lambda x, i: x[i * SUBLANES:(i + 1) * SUBLANES]
    p = [None] * nstrip

    def apply(pending):
        level, scores, rows = pending
        for j, i in enumerate(rows):
            mask = pair_ref[d, level, i * SUBLANES:(i + 1) * SUBLANES, :] != 0
            p[i] = jnp.where(mask, strip(scores, j), 0.0 if p[i] is None else p[i])

    query_first = rev

    def halves(m):
        out = []
        for g in range(c // (2 * m)):
            first = slice(g * 2 * m, g * 2 * m + m)
            second = slice(g * 2 * m + m, (g + 1) * 2 * m)
            out.append((first, second) if query_first else (second, first))
        return out

    def factor(m):
        ref = _boundary_row(a, m, rev, odd_block)
        if m < SUBLANES:
            return jnp.exp2(_neg_abs(a - ref))
        parts = []
        for qs, ks in halves(m):
            dq, dk = a[qs] - ref[qs], ref[ks] - a[ks]
            parts += [dq, dk] if query_first else [dk, dq]
        return jnp.exp2(jnp.concatenate(parts, axis=0))

    pending = [(0, sc, list(range(nstrip)))]
    nlevel = pair_ref.shape[1] - 1
    m = 1
    e = factor(m)
    for level in range(1, nlevel + 1):
        e_next = factor(2 * m) if level < nlevel else None
        if m < SUBLANES:
            qk = jnp.where(odd_block(m), k, q) if query_first else jnp.where(odd_block(m), q, k)
            z = (qk * e).astype(BF16)
            sc = _nt(z, z)
            rows = list(range(nstrip))
        else:
            parts, qparts, rows = [], [], []
            for qs, ks in halves(m):
                zq, zk = q[qs] * e[qs], k[ks] * e[ks]
                parts += [zq, zk] if query_first else [zk, zq]
                qparts.append(zq)
                rows += list(range(qs.start // SUBLANES, qs.stop // SUBLANES))
            z = jnp.concatenate(parts, axis=0).astype(BF16)
            sc = _nt(jnp.concatenate(qparts, axis=0).astype(BF16), z)
        e = e_next
        yield
        if level == 2:
            o_ref[0, pl.ds(r0, c), cols] = o
            st_scr[col] = st_decayed + st_new
        pending.append((level, sc, rows))
        if len(pending) > HGRN_LAG:
            apply(pending.pop(0))
        m *= 2
    a_scr[slot] = a2[:, :LANES] + a2[:, LANES:]
    o_ref[0, pl.ds(r_prev, c), cols] += pv
    yield
    for item in pending:
        apply(item)
    p_scr[slot] = jnp.concatenate(p, axis=0).astype(BF16)


def _cumsum_issue(lf, tri):
    hi = lf.astype(BF16)
    lo = (lf - hi.astype(F32)).astype(BF16)
    return jnp.dot(tri, jnp.concatenate([hi, lo], axis=1), preferred_element_type=F32)


def _interleave(chains):
    chains = list(chains)
    while chains:
        for ch in list(chains):
            try:
                next(ch)
            except StopIteration:
                chains.remove(ch)


def _hgrn_kernel(qf_ref, vf_ref, lff_ref, qb_ref, vb_ref, lfb_ref, pair_ref, rows_ref, tri_ref,
                 of_ref, ob_ref, sf_scr, sb_scr, a_scr, p_scr):
    @pl.when(pl.program_id(2) == 0)
    def _():
        sf_scr[...] = jnp.zeros(sf_scr.shape, F32)
        sb_scr[...] = jnp.zeros(sb_scr.shape, F32)

    c = HGRN_CHUNK
    consts = (pair_ref, rows_ref, tri_ref)
    n = qf_ref.shape[1] // c
    heads = qf_ref.shape[2] // HEAD_DIM
    row = lambda idx: pl.multiple_of(idx * c, c)
    dirs = ((qf_ref, vf_ref, lff_ref, of_ref, sf_scr, False), (qb_ref, vb_ref, lfb_ref, ob_ref, sb_scr, True))
    first = lambda rev: n - 1 if rev else 0
    last = lambda rev: 0 if rev else n - 1

    for hd in range(heads):
        cols = slice(hd * HEAD_DIM, (hd + 1) * HEAD_DIM)
        for d, (_, _, lf_ref, _, _, rev) in enumerate(dirs):
            a2 = _cumsum_issue(lf_ref[0, pl.ds(first(rev) * c, c), cols], tri_ref[d])
            a_scr[2 * hd + d] = a2[:, :LANES] + a2[:, LANES:]
    p_scr[...] = jnp.zeros(p_scr.shape, BF16)

    def body(ci, carry):
        idx = {False: (jnp.maximum(ci - 1, 0), ci, jnp.minimum(ci + 1, n - 1)),
               True: (jnp.minimum(n - ci, n - 1), n - 1 - ci, jnp.maximum(n - 2 - ci, 0))}
        chains = []
        for hd in range(heads):
            for d, (q_ref, v_ref, lf_ref, o_ref, st_scr, rev) in enumerate(dirs):
                rows3 = tuple(row(i) for i in idx[rev])
                chains.append(_hgrn_chunk(q_ref, v_ref, lf_ref, o_ref, st_scr, a_scr, p_scr, rows3, hd,
                                          2 * hd + d, consts, rev=rev))
        _interleave(chains)
        return carry

    lax.fori_loop(0, n, body, 0)

    for hd in range(heads):
        cols = slice(hd * HEAD_DIM, (hd + 1) * HEAD_DIM)
        for d, (_, v_ref, _, o_ref, _, rev) in enumerate(dirs):
            rws = pl.ds(last(rev) * c, c)
            o_ref[0, rws, cols] += jnp.dot(p_scr[2 * hd + d], v_ref[0, rws, cols], preferred_element_type=F32)


HGRN_HEADS_PER_STEP = 4


def hgrn2_bidir(q, v, logf, ts=1024, hps=HGRN_HEADS_PER_STEP):
    b, s, hk = q.shape
    ts = min(ts, s)
    nt = s // ts
    ng = hk // (hps * HEAD_DIM)
    blk = (1, ts, hps * HEAD_DIM)
    fwd = lambda bi, h, i: (bi, i, h)
    bwd = lambda bi, h, i: (bi, nt - 1 - i, h)
    out = jax.ShapeDtypeStruct((b, s, hk), F32)
    state = pltpu.VMEM((hps, HEAD_DIM, HEAD_DIM), F32)
    consts = hgrn_constants()
    whole = lambda x: pl.BlockSpec(x.shape, lambda bi, h, i: (0,) * x.ndim)
    return pl.pallas_call(
        _hgrn_kernel,
        grid=(b, ng, nt),
        in_specs=[pl.BlockSpec(blk, fwd), pl.BlockSpec(blk, fwd), pl.BlockSpec(blk, fwd),
                  pl.BlockSpec(blk, bwd), pl.BlockSpec(blk, bwd),
                  pl.BlockSpec(blk, lambda bi, h, i: (bi, nt - 1 - i, ng + h))] + [whole(x) for x in consts],
        out_specs=[pl.BlockSpec(blk, fwd), pl.BlockSpec(blk, bwd)],
        out_shape=[out, out],
        scratch_shapes=[state, state, pltpu.VMEM((2 * hps, HGRN_CHUNK, HEAD_DIM), F32),
                        pltpu.VMEM((2 * hps, HGRN_CHUNK, HGRN_CHUNK), BF16)],
        compiler_params=_cparams(("parallel", "parallel", "arbitrary")),
        name="hgrn2",
    )(q, v, logf, q, v, logf, *consts)


def _hgrn_post_kernel(of_ref, ob_ref, gr_ref, gn_ref, o_ref):
    gn = gn_ref[...]
    for hd in range(o_ref.shape[1] // HEAD_DIM):
        sl = slice(hd * HEAD_DIM, (hd + 1) * HEAD_DIM)
        o = _rms(of_ref[:, sl] + ob_ref[:, sl], gn)
        o_ref[:, sl] = (o * gr_ref[:, sl].astype(F32)).astype(o_ref.dtype)


def hgrn_post(o_fw, o_bw, g_silu, g_norm, tm=512):
    m, d = o_fw.shape
    tm = min(tm, m)
    row = pl.BlockSpec((tm, d), lambda i: (i, 0))
    return pl.pallas_call(
        _hgrn_post_kernel,
        grid=(m // tm,),
        in_specs=[row, row, row, pl.BlockSpec((1, HEAD_DIM), lambda i: (0, 0))],
        out_specs=row,
        out_shape=jax.ShapeDtypeStruct((m, d), BF16),
        compiler_params=_cparams(("parallel",)),
        name="hgrn_post",
    )(o_fw, o_bw, g_silu, g_norm.reshape(1, HEAD_DIM))


def _merge_kernel(att_ref, or_ref, woa_ref, woh_ref, sa_ref, sh_ref, o_ref):
    tm = o_ref.shape[0]
    rows_per = min(MM_ROWS, tm)
    woa, woh = woa_ref[...].astype(BF16), woh_ref[...].astype(BF16)
    for r in range(0, tm, rows_per):
        rs = slice(r, r + rows_per)
        ya = jnp.dot(att_ref[rs, :], woa, preferred_element_type=F32)
        yh = jnp.dot(or_ref[rs, :], woh, preferred_element_type=F32)
        o_ref[rs, :] = (sa_ref[rs, :].astype(F32) * ya + sh_ref[rs, :].astype(F32) * yh).astype(o_ref.dtype)


def gated_merge(att, o_r, w_oa, w_oh, gates, tm=1024, tn=512):
    m, d = att.shape
    tm = min(tm, m)
    nj = d // tn
    row = pl.BlockSpec((tm, d), lambda i, j: (i, 0))
    wsp = pl.BlockSpec((d, tn), lambda i, j: (0, j))
    return pl.pallas_call(
        _merge_kernel,
        grid=(m // tm, nj),
        in_specs=[row, row, wsp, wsp,
                  pl.BlockSpec((tm, tn), lambda i, j: (i, j)),
                  pl.BlockSpec((tm, tn), lambda i, j: (i, nj + j))],
        out_specs=pl.BlockSpec((tm, tn), lambda i, j: (i, j)),
        out_shape=jax.ShapeDtypeStruct((m, d), BF16),
        compiler_params=_cparams(("parallel", "arbitrary")),
        name="gated_merge",
    )(att, o_r, w_oa, w_oh, gates, gates)


def _resid_proj_kernel(x_ref, a_ref, w_ref, o_ref):
    tm = o_ref.shape[0]
    rows_per = min(MM_ROWS, tm)
    w = w_ref[...].astype(BF16)
    for r in range(0, tm, rows_per):
        rs = slice(r, r + rows_per)
        o_ref[rs, :] = x_ref[rs, :] + jnp.dot(a_ref[rs, :], w, preferred_element_type=F32)


def resid_proj(x, a, w, tm=1024, tn=512):
    m, d = x.shape
    tm = min(tm, m)
    return pl.pallas_call(
        _resid_proj_kernel,
        grid=(m // tm, d // tn),
        in_specs=[pl.BlockSpec((tm, tn), lambda i, j: (i, j)), pl.BlockSpec((tm, a.shape[1]), lambda i, j: (i, 0)),
                  pl.BlockSpec((a.shape[1], tn), lambda i, j: (0, j))],
        out_specs=pl.BlockSpec((tm, tn), lambda i, j: (i, j)),
        out_shape=jax.ShapeDtypeStruct((m, d), F32),
        compiler_params=_cparams(("parallel", "arbitrary")),
        name="out_proj",
    )(x, a, w)


def _mlp_kernel(x_ref, g_ref, wu_ref, wd_ref, o_ref, h_scr):
    @pl.when(pl.program_id(1) == 0)
    def _():
        x = x_ref[...]
        h_scr[...] = _rms(x, g_ref[...]).astype(h_scr.dtype)
        o_ref[...] = x

    tm = o_ref.shape[0]
    rows_per = min(MM_ROWS, tm)
    wu, wd = wu_ref[...].astype(BF16), wd_ref[...].astype(BF16)
    for r in range(0, tm, rows_per):
        rs = slice(r, r + rows_per)
        u = jnp.maximum(jnp.dot(h_scr[rs, :], wu, preferred_element_type=F32), 0.0)
        o_ref[rs, :] += jnp.dot((u * u).astype(BF16), wd, preferred_element_type=F32)


def mlp_block(x, gain, w_up, w_down, tm=1024, tf=512):
    m, d = x.shape
    ff = w_up.shape[1]
    tm = min(tm, m)
    return pl.pallas_call(
        _mlp_kernel,
        grid=(m // tm, ff // tf),
        in_specs=[pl.BlockSpec((tm, d), lambda i, f: (i, 0)), pl.BlockSpec((1, d), lambda i, f: (0, 0)),
                  pl.BlockSpec((d, tf), lambda i, f: (0, f)), pl.BlockSpec((tf, d), lambda i, f: (f, 0))],
        out_specs=pl.BlockSpec((tm, d), lambda i, f: (i, 0)),
        out_shape=jax.ShapeDtypeStruct((m, d), F32),
        scratch_shapes=[pltpu.VMEM((tm, d), BF16)],
        compiler_params=_cparams(("parallel", "arbitrary")),
        name="mlp",
    )(x, gain.reshape(1, d), w_up, w_down)


def _ple_kernel(x_ref, g_ref, wg_ref, p_ref, wp_ref, gf_ref, o_ref):
    x = x_ref[...]
    h = _rms(x, g_ref[...]).astype(BF16)
    gate = _sigmoid(jnp.dot(h, wg_ref[...], preferred_element_type=F32))
    emb = jnp.dot(p_ref[...].astype(BF16), wp_ref[...], preferred_element_type=F32)
    o_ref[...] = _rms(x + gate * emb, gf_ref[...])


def ple_final(x, gain, w_gate, p, w_p, g_final, tm=512):
    m, d = x.shape
    c = p.shape[1]
    tm = min(tm, m)
    const = lambda i: (0, 0)
    return pl.pallas_call(
        _ple_kernel,
        grid=(m // tm,),
        in_specs=[pl.BlockSpec((tm, d), lambda i: (i, 0)), pl.BlockSpec((1, d), const),
                  pl.BlockSpec((d, d), const), pl.BlockSpec((tm, c), lambda i: (i, 0)),
                  pl.BlockSpec((c, d), const), pl.BlockSpec((1, d), const)],
        out_specs=pl.BlockSpec((tm, d), lambda i: (i, 0)),
        out_shape=jax.ShapeDtypeStruct((m, d), F32),
        compiler_params=_cparams(("parallel",)),
        name="ple_final",
    )(x, gain.reshape(1, d), w_gate, p, w_p, g_final.reshape(1, d))


def kernel(x, p, g_mix, w_in, g_q, g_k, w_o_attn, hgrn_lb, g_hgrn, w_o_hgrn, w_out, g_mlp, w_up, w_down,
           g_ple, w_ple_gate, w_ple, g_final):
    b, s, d = x.shape
    m = b * s
    depth = w_in.shape[0]
    attn_q = N_Q_HEADS * HEAD_DIM
    attn_kv = N_KV_HEADS * HEAD_DIM
    hk = HGRN_HEADS * HEAD_DIM
    rope = rope_tables(s)
    lb_all = jnp.cumsum(jax.nn.softmax(hgrn_lb.astype(F32), axis=0), axis=0)

    xf = x.reshape(m, d)
    for i in range(depth):
        w = w_in[i]
        h = rmsnorm_bf16(xf, g_mix[i])
        c0 = 0
        w_qk = permute_heads(w[:, :attn_q + attn_kv]).astype(BF16)
        q_a = proj_qk(h, w_qk, 0, attn_q, g_q[i], rope, s, HEAD_DIM ** -0.5 * LOG2_E); c0 += attn_q
        k_a = proj_qk(h, w_qk, attn_q, attn_kv, g_k[i], rope, s, 1.0); c0 += attn_kv
        v_a = proj_plain(h, w, c0, attn_kv, None); c0 += attn_kv
        q_r = proj_plain(h, w, c0, hk, "silu"); c0 += hk
        logf = proj_logf(h, w, c0, 2 * hk, lb_all[i]); c0 += 2 * hk
        i_r = proj_plain(h, w, c0, hk, None); c0 += hk
        g_r = proj_plain(h, w, c0, hk, "silu"); c0 += hk
        gates = proj_plain(h, w, c0, 2 * d, "sigmoid"); c0 += 2 * d

        att = gqa_attention(q_a.reshape(b, s, attn_q), k_a.reshape(b, s, attn_kv), v_a.reshape(b, s, attn_kv))
        o_fw, o_bw = hgrn2_bidir(q_r.reshape(b, s, hk), i_r.reshape(b, s, hk), logf.reshape(b, s, 2 * hk))
        o_r = hgrn_post(o_fw.reshape(m, hk), o_bw.reshape(m, hk), g_r, g_hgrn[i])
        mixed = gated_merge(att.reshape(m, attn_q), o_r, w_o_attn[i].astype(BF16), w_o_hgrn[i].astype(BF16), gates)
        xf = resid_proj(xf, mixed, w_out[i].astype(BF16))
        xf = mlp_block(xf, g_mlp[i], w_up[i].astype(BF16), w_down[i])
        assert depth == 1
        xf = ple_final(xf, g_ple[i], w_ple_gate[i].astype(BF16), p[i].reshape(m, -1), w_ple[i].astype(BF16),
                       g_final)
    return xf.reshape(b, s, d)
```

```python
import functools

import jax
import jax.numpy as jnp
from jax import lax
from jax.experimental import pallas as pl
from jax.experimental.pallas import tpu as pltpu

F32 = jnp.float32
BF16 = jnp.bfloat16

EPS = 1e-6
LOG2_E = 1.4426950408889634
HEAD_DIM = 128
N_Q_HEADS = 16
N_KV_HEADS = 4
GQA_GROUPS = N_Q_HEADS // N_KV_HEADS
GRID_W = 64
ROPE_THETA = 10000.0
HGRN_HEADS = 16
LANES = 128
SUBLANES = 8
VMEM_LIMIT = 56 * 1024 * 1024


def _cparams(sem):
    return pltpu.CompilerParams(dimension_semantics=sem, vmem_limit_bytes=VMEM_LIMIT)


def _sigmoid(x):
    return 1.0 / (1.0 + jnp.exp(-x))


def _rms(x, gain):
    ms = jnp.mean(x * x, axis=-1, keepdims=True)
    return x * lax.rsqrt(ms + EPS) * gain


def _rmsnorm_kernel(x_ref, g_ref, o_ref):
    o_ref[...] = _rms(x_ref[...], g_ref[...]).astype(o_ref.dtype)


def rmsnorm_bf16(x, gain, tm=512):
    m, d = x.shape
    tm = min(tm, m)
    return pl.pallas_call(
        _rmsnorm_kernel,
        grid=(m // tm,),
        in_specs=[pl.BlockSpec((tm, d), lambda i: (i, 0)), pl.BlockSpec((1, d), lambda i: (0, 0))],
        out_specs=pl.BlockSpec((tm, d), lambda i: (i, 0)),
        out_shape=jax.ShapeDtypeStruct((m, d), BF16),
        compiler_params=_cparams(("parallel",)),
        name="rmsnorm",
    )(x, gain.reshape(1, d))


QK_SUB = 2 * HEAD_DIM


MM_ROWS = 512


def _subtiled_matmul(h_ref, w_ref, epilogue):
    tm = h_ref.shape[0]
    rows_per = min(MM_ROWS, tm)
    units = [(slice(r, r + rows_per), slice(c0, c0 + QK_SUB))
             for c0 in range(0, w_ref.shape[1], QK_SUB) for r in range(0, tm, rows_per)]
    weights = {}

    def matmul(u):
        rs, cs = u
        if cs.start not in weights:
            weights.clear()
            weights[cs.start] = w_ref[:, cs].astype(BF16)
        return jnp.dot(h_ref[rs, :], weights[cs.start], preferred_element_type=F32)

    prev = matmul(units[0])
    for j in range(1, len(units)):
        cur = matmul(units[j])
        epilogue(prev, *units[j - 1])
        prev = cur
    epilogue(prev, *units[-1])


def _stationary_weights(w_ref, wb_scr, rope_order=False):
    @pl.when(pl.program_id(1) == 0)
    def _():
        w = w_ref[...]
        if rope_order:
            tn = w.shape[1]
            quarter = (lax.broadcasted_iota(jnp.int32, w.shape, 1) // (HEAD_DIM // 4)) % 4
            w = jnp.where(quarter == 1, pltpu.roll(w, tn - HEAD_DIM // 4, 1),
                          jnp.where(quarter == 2, pltpu.roll(w, HEAD_DIM // 4, 1), w))
        wb_scr[...] = w.astype(wb_scr.dtype)


def _proj_plain_kernel(h_ref, w_ref, o_ref, wb_scr, *, act):
    def epilogue(acc, rs, cs):
        if act == "silu":
            acc = acc * _sigmoid(acc)
        elif act == "sigmoid":
            acc = _sigmoid(acc)
        o_ref[rs, cs] = acc.astype(o_ref.dtype)

    _stationary_weights(w_ref, wb_scr)
    _subtiled_matmul(h_ref, wb_scr, epilogue)


def _proj_logf_kernel(h_ref, w_ref, lb_ref, o_ref, wb_scr):
    def epilogue(acc, rs, cs):
        lb = lb_ref[:, cs]
        o_ref[rs, cs] = jnp.log(lb + (1.0 - lb) * _sigmoid(acc)) * LOG2_E

    _stationary_weights(w_ref, wb_scr)
    _subtiled_matmul(h_ref, wb_scr, epilogue)


def _proj_qk_kernel(h_ref, w_ref, cg_ref, sg_ref, o_ref, wb_scr):
    ones = jnp.ones((HEAD_DIM, HEAD_DIM), BF16)
    _stationary_weights(w_ref, wb_scr, rope_order=True)

    def epilogue(acc, rs, cs):
        cg, sg = cg_ref[rs, :], sg_ref[rs, :]
        for hd in range(QK_SUB // HEAD_DIM):
            x = acc[:, hd * HEAD_DIM:(hd + 1) * HEAD_DIM]
            ssq = jnp.dot((x * x).astype(BF16), ones, preferred_element_type=F32)
            r = lax.rsqrt(ssq * (1.0 / HEAD_DIM) + EPS)
            y = r * (x * cg + pltpu.roll(x, HEAD_DIM // 2, 1) * sg)
            c0 = cs.start + hd * HEAD_DIM
            o_ref[rs, c0:c0 + HEAD_DIM] = y.astype(o_ref.dtype)

    _subtiled_matmul(h_ref, wb_scr, epilogue)


def _proj_call(kernel, h, w, col0, ncols, out_dtype, extra=(), extra_specs=(), tm=1024, tn=1024):
    m, d = h.shape
    tm = min(tm, m)
    tn = min(tn, ncols)
    assert col0 % tn == 0 and ncols % tn == 0 and m % tm == 0
    jb = col0 // tn
    return pl.pallas_call(
        kernel,
        grid=(ncols // tn, m // tm),
        in_specs=[pl.BlockSpec((tm, d), lambda j, i: (i, 0)),
                  pl.BlockSpec((d, tn), lambda j, i: (0, jb + j))] + list(extra_specs),
        out_specs=pl.BlockSpec((tm, tn), lambda j, i: (i, j)),
        out_shape=jax.ShapeDtypeStruct((m, ncols), out_dtype),
        scratch_shapes=[pltpu.VMEM((d, tn), BF16)],
        compiler_params=_cparams(("parallel", "arbitrary")),
        name="in_proj",
    )(h, w, *extra)


def proj_plain(h, w, col0, ncols, act, out_dtype=BF16):
    return _proj_call(functools.partial(_proj_plain_kernel, act=act), h, w, col0, ncols, out_dtype)


def proj_logf(h, w, col0, ncols, lb, tn=1024):
    return _proj_call(_proj_logf_kernel, h, w, col0, ncols, F32, extra=(lb.reshape(1, ncols),),
                      extra_specs=(pl.BlockSpec((1, tn), lambda j, i: (0, j)),), tn=tn)


def proj_qk(h, w, col0, ncols, gain, rope, seq, scale, tm=1024, tn=1024):
    cos, sin = rope
    gp = gain.astype(F32)[jnp.array(ROPE_PERM)]
    cg = cos * (gp * scale)[None, :]
    sg = sin * (jnp.roll(gp, HEAD_DIM // 2) * scale)[None, :]
    tm = min(tm, seq)
    tn = min(tn, ncols)
    nsb = seq // tm
    tab = pl.BlockSpec((tm, HEAD_DIM), lambda j, i: (i % nsb, 0))
    return _proj_call(_proj_qk_kernel, h, w, col0, ncols, BF16, extra=(cg, sg), extra_specs=(tab, tab),
                      tm=tm, tn=tn)


ROPE_PERM = tuple(list(range(0, 32)) + list(range(64, 96)) + list(range(32, 64)) + list(range(96, 128)))


def rope_tables(seq):
    half = HEAD_DIM // 2
    t = jnp.arange(seq, dtype=jnp.int32)
    row = (t // GRID_W).astype(F32)
    col = (t % GRID_W).astype(F32)
    inv_freq = ROPE_THETA ** (-jnp.arange(0, half, 2, dtype=F32) / half)
    ang = jnp.concatenate([row[:, None] * inv_freq[None, :], col[:, None] * inv_freq[None, :]], axis=-1)
    cos = jnp.concatenate([jnp.cos(ang), jnp.cos(ang)], axis=-1)
    sin = jnp.concatenate([-jnp.sin(ang), jnp.sin(ang)], axis=-1)
    return cos, sin


ATTN_TQ = 128
ATTN_TKC = 8192


def _attn_kernel(q0_ref, qa_ref, qb_ref, k_ref, v_ref, o_ref, s_scr, mrun_scr, mcur_scr, acc_scr, *, tkc):
    t = pl.program_id(2)
    nt = pl.num_programs(2)
    tq = qa_ref.shape[1]
    nch = k_ref.shape[1] // tkc
    nsl = tkc // LANES
    ones = jnp.ones((tkc, LANES), BF16)

    def stack(q_ref):
        return jnp.concatenate([q_ref[0, :, g * HEAD_DIM:(g + 1) * HEAD_DIM] for g in range(GQA_GROUPS)], axis=0)

    def pass1(qs, c, slot):
        off = pl.multiple_of(c * tkc, tkc)
        s = lax.dot_general(qs, k_ref[0, pl.ds(off, tkc), :], (((1,), (1,)), ((), ())),
                            preferred_element_type=F32)
        s_scr[slot, c] = s
        m = s[:, :LANES]
        for i in range(1, nsl):
            m = jnp.maximum(m, s[:, i * LANES:(i + 1) * LANES])
        mrun_scr[...] = jnp.maximum(mrun_scr[...], m)

    def pass2(c, slot):
        off = pl.multiple_of(c * tkc, tkc)
        vext = jnp.concatenate([v_ref[0, pl.ds(off, tkc), :], ones], axis=1)
        m = mcur_scr[...]
        p = jnp.concatenate([jnp.exp2(s_scr[slot, c, :, i * LANES:(i + 1) * LANES] - m) for i in range(nsl)],
                            axis=1)
        acc_scr[...] += jnp.dot(p.astype(BF16), vext, preferred_element_type=F32)

    def begin_pass1():
        mrun_scr[...] = jnp.full(mrun_scr.shape, -jnp.inf, F32)

    def end_pass1():
        mcur_scr[...] = jnp.broadcast_to(jnp.max(mrun_scr[...], axis=1, keepdims=True), mcur_scr.shape)
        acc_scr[...] = jnp.zeros(acc_scr.shape, F32)

    def emit(half):
        o = acc_scr[:, :HEAD_DIM] / acc_scr[:, HEAD_DIM:]
        for g in range(GQA_GROUPS):
            o_ref[0, half * tq:(half + 1) * tq, g * HEAD_DIM:(g + 1) * HEAD_DIM] = (
                o[g * tq:(g + 1) * tq].astype(o_ref.dtype))

    @pl.when(t == 0)
    def _():
        qs0 = stack(q0_ref)
        begin_pass1()

        def body_0(c, carry):
            pass1(qs0, c, 0)
            return carry

        lax.fori_loop(0, nch, body_0, 0)
        end_pass1()

    qs = stack(qa_ref)
    begin_pass1()

    def body_x(c, carry):
        pass1(qs, c, 1)
        pass2(c, 0)
        return carry

    lax.fori_loop(0, nch, body_x, 0)
    emit(0)
    end_pass1()

    @pl.when(t < nt - 1)
    def _():
        qs2 = stack(qb_ref)
        begin_pass1()

        def body_y(c, carry):
            pass1(qs2, c, 0)
            pass2(c, 1)
            return carry

        lax.fori_loop(0, nch, body_y, 0)
        emit(1)
        end_pass1()

    @pl.when(t == nt - 1)
    def _():
        def body_z(c, carry):
            pass2(c, 1)
            return carry

        lax.fori_loop(0, nch, body_z, 0)
        emit(1)


def gqa_attention(q, k, v, tq=ATTN_TQ, tkc=ATTN_TKC):
    b, s, _ = q.shape
    tkc = min(tkc, s)
    nq = s // tq
    assert nq % 2 == 0 and s % tkc == 0
    gw = GQA_GROUPS * HEAD_DIM
    rows = GQA_GROUPS * tq
    qblk = (1, tq, gw)
    kvblk = (1, s, HEAD_DIM)
    return pl.pallas_call(
        functools.partial(_attn_kernel, tkc=tkc),
        grid=(b, N_KV_HEADS, nq // 2),
        in_specs=[pl.BlockSpec(qblk, lambda bi, h, t: (bi, 0, h)),
                  pl.BlockSpec(qblk, lambda bi, h, t: (bi, 2 * t + 1, h)),
                  pl.BlockSpec(qblk, lambda bi, h, t: (bi, jnp.minimum(2 * t + 2, nq - 1), h)),
                  pl.BlockSpec(kvblk, lambda bi, h, t: (bi, 0, h)),
                  pl.BlockSpec(kvblk, lambda bi, h, t: (bi, 0, h))],
        out_specs=pl.BlockSpec((1, 2 * tq, gw), lambda bi, h, t: (bi, t, h)),
        out_shape=jax.ShapeDtypeStruct(q.shape, BF16),
        scratch_shapes=[pltpu.VMEM((2, s // tkc, rows, tkc), F32), pltpu.VMEM((rows, LANES), F32),
                        pltpu.VMEM((rows, LANES), F32), pltpu.VMEM((rows, 2 * HEAD_DIM), F32)],
        compiler_params=_cparams(("parallel", "parallel", "arbitrary")),
        name="gqa_attention",
    )(q, q, q, k, v)


HGRN_CHUNK = 128
HGRN_LAG = 2


def _group_row(x, group, row):
    c, n = x.shape
    xr = x.reshape(c // group, group, n)
    return jnp.broadcast_to(xr[:, row:row + 1, :], xr.shape).reshape(c, n)


def _boundary_row(x, m, rev, odd_block):
    ref_row = m if rev else m - 1
    if 2 * m >= SUBLANES:
        return _group_row(x, 2 * m, ref_row)
    if m == 1:
        c, n = x.shape
        x3 = x.reshape(c // SUBLANES, SUBLANES, n)
        other = pltpu.roll(x3, SUBLANES - 1 if rev else 1, 1).reshape(c, n)
        return jnp.where(odd_block(1), x, other) if rev else jnp.where(odd_block(1), other, x)
    assert 4 * m == SUBLANES
    return jnp.where(odd_block(2 * m), _group_row(x, SUBLANES, 2 * m + ref_row), _group_row(x, SUBLANES, ref_row))


def hgrn_constants():
    import numpy as np
    c = HGRN_CHUNK
    t = np.arange(c)[:, None]
    s = np.arange(c)[None, :]
    pair, tri = [], []
    for rev in (False, True):
        masks = [t == s]
        m = 1
        while m < c:
            same = (t // (2 * m)) == (s // (2 * m))
            t_query = ((t // m) % 2) == (0 if rev else 1)
            s_key = ((s // m) % 2) == (1 if rev else 0)
            masks.append(same & t_query & s_key)
            m *= 2
        pair.append(np.stack(masks))
        tri.append((s >= t) if rev else (s <= t))
    rows = np.stack([np.broadcast_to(((t // n) % 2) == 1, (c, LANES)) for n in (1, 2, 4)])
    return (jnp.asarray(np.stack(pair), jnp.int32), jnp.asarray(rows, jnp.int32),
            jnp.asarray(np.stack(tri), BF16))


def _neg_abs(x):
    return pltpu.bitcast(pltpu.bitcast(x, jnp.uint32) | jnp.uint32(0x80000000), F32)


def _nt(x, y):
    return lax.dot_general(x, y, (((1,), (1,)), ((), ())), preferred_element_type=F32)


def _hgrn_chunk(q_ref, v_ref, lf_ref, o_ref, st_scr, a_scr, p_scr, rows3, col, slot, consts, *, rev):
    pair_ref, rows_ref, tri_ref = consts
    r_prev, r0, r_next = rows3
    d = 1 if rev else 0
    odd_block = lambda n: rows_ref[{1: 0, 2: 1, 4: 2}[n]] != 0
    c = HGRN_CHUNK
    cols = slice(col * HEAD_DIM, (col + 1) * HEAD_DIM)

    pv = jnp.dot(p_scr[slot], v_ref[0, pl.ds(r_prev, c), cols], preferred_element_type=F32)
    a2 = _cumsum_issue(lf_ref[0, pl.ds(r_next, c), cols], tri_ref[d])

    qb = q_ref[0, pl.ds(r0, c), cols]
    vb = v_ref[0, pl.ds(r0, c), cols]
    k = 1.0 - jnp.exp2(lf_ref[0, pl.ds(r0, c), cols])
    sc = _nt(qb, k.astype(BF16))
    q = qb.astype(F32)
    a = a_scr[slot]
    edge = 0 if rev else c - 1
    a_end = a[edge:edge + 1, :]

    st = st_scr[col]
    o = _nt((q * jnp.exp2(a)).astype(BF16), st.astype(BF16))
    kdec = (k * jnp.exp2(a_end - a)).astype(BF16)
    st_new = lax.dot_general(vb, kdec, (((0,), (0,)), ((), ())), preferred_element_type=F32)
    st_decayed = st * jnp.exp2(a_end)

    nstrip = c // SUBLANES
    strip = lambda x, i: x[i * SUBLANES:(i + 1) * SUBLANES]
    p = [None] * nstrip

    def apply(pending):
        level, scores, rows = pending
        for j, i in enumerate(rows):
            mask = pair_ref[d, level, i * SUBLANES:(i + 1) * SUBLANES, :] != 0
            p[i] = jnp.where(mask, strip(scores, j), 0.0 if p[i] is None else p[i])

    query_first = rev

    def halves(m):
        out = []
        for g in range(c // (2 * m)):
            first = slice(g * 2 * m, g * 2 * m + m)
            second = slice(g * 2 * m + m, (g + 1) * 2 * m)
            out.append((first, second) if query_first else (second, first))
        return out

    def factor(m):
        ref = _boundary_row(a, m, rev, odd_block)
        if m < SUBLANES:
            return jnp.exp2(_neg_abs(a - ref))
        parts = []
        for qs, ks in halves(m):
            dq, dk = a[qs] - ref[qs], ref[ks] - a[ks]
            parts += [dq, dk] if query_first else [dk, dq]
        return jnp.exp2(jnp.concatenate(parts, axis=0))

    pending = [(0, sc, list(range(nstrip)))]
    nlevel = pair_ref.shape[1] - 1
    m = 1
    e = factor(m)
    for level in range(1, nlevel + 1):
        e_next = factor(2 * m) if level < nlevel else None
        if m < SUBLANES:
            qk = jnp.where(odd_block(m), k, q) if query_first else jnp.where(odd_block(m), q, k)
            z = (qk * e).astype(BF16)
            sc = _nt(z, z)
            rows = list(range(nstrip))
        else:
            parts, qparts, rows = [], [], []
            for qs, ks in halves(m):
                zq, zk = q[qs] * e[qs], k[ks] * e[ks]
                parts += [zq, zk] if query_first else [zk, zq]
                qparts.append(zq)
                rows += list(range(qs.start // SUBLANES, qs.stop // SUBLANES))
            z = jnp.concatenate(parts, axis=0).astype(BF16)
            sc = _nt(jnp.concatenate(qparts, axis=0).astype(BF16), z)
        e = e_next
        yield
        if level == 2:
            o_ref[0, pl.ds(r0, c), cols] = o
            st_scr[col] = st_decayed + st_new
        pending.append((level, sc, rows))
        if len(pending) > HGRN_LAG:
            apply(pending.pop(0))
        m *= 2
    a_scr[slot] = a2[:, :LANES] + a2[:, LANES:]
    o_ref[0, pl.ds(r_prev, c), cols] += pv
    yield
    for item in pending:
        apply(item)
    p_scr[slot] = jnp.concatenate(p, axis=0).astype(BF16)


def _cumsum_issue(lf, tri):
    hi = lf.astype(BF16)
    lo = (lf - hi.astype(F32)).astype(BF16)
    return jnp.dot(tri, jnp.concatenate([hi, lo], axis=1), preferred_element_type=F32)


def _interleave(chains):
    chains = list(chains)
    while chains:
        for ch in list(chains):
            try:
                next(ch)
            except StopIteration:
                chains.remove(ch)


def _hgrn_kernel(qf_ref, vf_ref, lff_ref, qb_ref, vb_ref, lfb_ref, pair_ref, rows_ref, tri_ref,
                 of_ref, ob_ref, sf_scr, sb_scr, a_scr, p_scr):
    @pl.when(pl.program_id(2) == 0)
    def _():
        sf_scr[...] = jnp.zeros(sf_scr.shape, F32)
        sb_scr[...] = jnp.zeros(sb_scr.shape, F32)

    c = HGRN_CHUNK
    consts = (pair_ref, rows_ref, tri_ref)
    n = qf_ref.shape[1] // c
    heads = qf_ref.shape[2] // HEAD_DIM
    row = lambda idx: pl.multiple_of(idx * c, c)
    dirs = ((qf_ref, vf_ref, lff_ref, of_ref, sf_scr, False), (qb_ref, vb_ref, lfb_ref, ob_ref, sb_scr, True))
    first = lambda rev: n - 1 if rev else 0
    last = lambda rev: 0 if rev else n - 1

    for hd in range(heads):
        cols = slice(hd * HEAD_DIM, (hd + 1) * HEAD_DIM)
        for d, (_, _, lf_ref, _, _, rev) in enumerate(dirs):
            a2 = _cumsum_issue(lf_ref[0, pl.ds(first(rev) * c, c), cols], tri_ref[d])
            a_scr[2 * hd + d] = a2[:, :LANES] + a2[:, LANES:]
    p_scr[...] = jnp.zeros(p_scr.shape, BF16)

    def body(ci, carry):
        idx = {False: (jnp.maximum(ci - 1, 0), ci, jnp.minimum(ci + 1, n - 1)),
               True: (jnp.minimum(n - ci, n - 1), n - 1 - ci, jnp.maximum(n - 2 - ci, 0))}
        chains = []
        for hd in range(heads):
            for d, (q_ref, v_ref, lf_ref, o_ref, st_scr, rev) in enumerate(dirs):
                rows3 = tuple(row(i) for i in idx[rev])
                chains.append(_hgrn_chunk(q_ref, v_ref, lf_ref, o_ref, st_scr, a_scr, p_scr, rows3, hd,
                                          2 * hd + d, consts, rev=rev))
        _interleave(chains)
        return carry

    lax.fori_loop(0, n, body, 0)

    for hd in range(heads):
        cols = slice(hd * HEAD_DIM, (hd + 1) * HEAD_DIM)
        for d, (_, v_ref, _, o_ref, _, rev) in enumerate(dirs):
            rws = pl.ds(last(rev) * c, c)
            o_ref[0, rws, cols] += jnp.dot(p_scr[2 * hd + d], v_ref[0, rws, cols], preferred_element_type=F32)


HGRN_HEADS_PER_STEP = 4


def hgrn2_bidir(q, v, logf, ts=1024, hps=HGRN_HEADS_PER_STEP):
    b, s, hk = q.shape
    ts = min(ts, s)
    nt = s // ts
    ng = hk // (hps * HEAD_DIM)
    blk = (1, ts, hps * HEAD_DIM)
    fwd = lambda bi, h, i: (bi, i, h)
    bwd = lambda bi, h, i: (bi, nt - 1 - i, h)
    out = jax.ShapeDtypeStruct((b, s, hk), F32)
    state = pltpu.VMEM((hps, HEAD_DIM, HEAD_DIM), F32)
    consts = hgrn_constants()
    whole = lambda x: pl.BlockSpec(x.shape, lambda bi, h, i: (0,) * x.ndim)
    return pl.pallas_call(
        _hgrn_kernel,
        grid=(b, ng, nt),
        in_specs=[pl.BlockSpec(blk, fwd), pl.BlockSpec(blk, fwd), pl.BlockSpec(blk, fwd),
                  pl.BlockSpec(blk, bwd), pl.BlockSpec(blk, bwd),
                  pl.BlockSpec(blk, lambda bi, h, i: (bi, nt - 1 - i, ng + h))] + [whole(x) for x in consts],
        out_specs=[pl.BlockSpec(blk, fwd), pl.BlockSpec(blk, bwd)],
        out_shape=[out, out],
        scratch_shapes=[state, state, pltpu.VMEM((2 * hps, HGRN_CHUNK, HEAD_DIM), F32),
                        pltpu.VMEM((2 * hps, HGRN_CHUNK, HGRN_CHUNK), BF16)],
        compiler_params=_cparams(("parallel", "parallel", "arbitrary")),
        name="hgrn2",
    )(q, v, logf, q, v, logf, *consts)


def _hgrn_post_kernel(of_ref, ob_ref, gr_ref, gn_ref, o_ref):
    gn = gn_ref[...]
    for hd in range(o_ref.shape[1] // HEAD_DIM):
        sl = slice(hd * HEAD_DIM, (hd + 1) * HEAD_DIM)
        o = _rms(of_ref[:, sl] + ob_ref[:, sl], gn)
        o_ref[:, sl] = (o * gr_ref[:, sl].astype(F32)).astype(o_ref.dtype)


def hgrn_post(o_fw, o_bw, g_silu, g_norm, tm=512):
    m, d = o_fw.shape
    tm = min(tm, m)
    row = pl.BlockSpec((tm, d), lambda i: (i, 0))
    return pl.pallas_call(
        _hgrn_post_kernel,
        grid=(m // tm,),
        in_specs=[row, row, row, pl.BlockSpec((1, HEAD_DIM), lambda i: (0, 0))],
        out_specs=row,
        out_shape=jax.ShapeDtypeStruct((m, d), BF16),
        compiler_params=_cparams(("parallel",)),
        name="hgrn_post",
    )(o_fw, o_bw, g_silu, g_norm.reshape(1, HEAD_DIM))


def _merge_kernel(att_ref, or_ref, woa_ref, woh_ref, sa_ref, sh_ref, o_ref):
    tm = o_ref.shape[0]
    rows_per = min(MM_ROWS, tm)
    woa, woh = woa_ref[...].astype(BF16), woh_ref[...].astype(BF16)
    for r in range(0, tm, rows_per):
        rs = slice(r, r + rows_per)
        ya = jnp.dot(att_ref[rs, :], woa, preferred_element_type=F32)
        yh = jnp.dot(or_ref[rs, :], woh, preferred_element_type=F32)
        o_ref[rs, :] = (sa_ref[rs, :].astype(F32) * ya + sh_ref[rs, :].astype(F32) * yh).astype(o_ref.dtype)


def gated_merge(att, o_r, w_oa, w_oh, gates, tm=1024, tn=512):
    m, d = att.shape
    tm = min(tm, m)
    nj = d // tn
    row = pl.BlockSpec((tm, d), lambda i, j: (i, 0))
    wsp = pl.BlockSpec((d, tn), lambda i, j: (0, j))
    return pl.pallas_call(
        _merge_kernel,
        grid=(m // tm, nj),
        in_specs=[row, row, wsp, wsp,
                  pl.BlockSpec((tm, tn), lambda i, j: (i, j)),
                  pl.BlockSpec((tm, tn), lambda i, j: (i, nj + j))],
        out_specs=pl.BlockSpec((tm, tn), lambda i, j: (i, j)),
        out_shape=jax.ShapeDtypeStruct((m, d), BF16),
        compiler_params=_cparams(("parallel", "arbitrary")),
        name="gated_merge",
    )(att, o_r, w_oa, w_oh, gates, gates)


def _resid_proj_kernel(x_ref, a_ref, w_ref, o_ref):
    tm = o_ref.shape[0]
    rows_per = min(MM_ROWS, tm)
    w = w_ref[...].astype(BF16)
    for r in range(0, tm, rows_per):
        rs = slice(r, r + rows_per)
        o_ref[rs, :] = x_ref[rs, :] + jnp.dot(a_ref[rs, :], w, preferred_element_type=F32)


def resid_proj(x, a, w, tm=1024, tn=512):
    m, d = x.shape
    tm = min(tm, m)
    return pl.pallas_call(
        _resid_proj_kernel,
        grid=(m // tm, d // tn),
        in_specs=[pl.BlockSpec((tm, tn), lambda i, j: (i, j)), pl.BlockSpec((tm, a.shape[1]), lambda i, j: (i, 0)),
                  pl.BlockSpec((a.shape[1], tn), lambda i, j: (0, j))],
        out_specs=pl.BlockSpec((tm, tn), lambda i, j: (i, j)),
        out_shape=jax.ShapeDtypeStruct((m, d), F32),
        compiler_params=_cparams(("parallel", "arbitrary")),
        name="out_proj",
    )(x, a, w)


def _mlp_kernel(x_ref, g_ref, wu_ref, wd_ref, o_ref, h_scr):
    @pl.when(pl.program_id(1) == 0)
    def _():
        x = x_ref[...]
        h_scr[...] = _rms(x, g_ref[...]).astype(h_scr.dtype)
        o_ref[...] = x

    tm = o_ref.shape[0]
    rows_per = min(MM_ROWS, tm)
    wu, wd = wu_ref[...].astype(BF16), wd_ref[...].astype(BF16)
    for r in range(0, tm, rows_per):
        rs = slice(r, r + rows_per)
        u = jnp.maximum(jnp.dot(h_scr[rs, :], wu, preferred_element_type=F32), 0.0)
        o_ref[rs, :] += jnp.dot((u * u).astype(BF16), wd, preferred_element_type=F32)


def mlp_block(x, gain, w_up, w_down, tm=1024, tf=512):
    m, d = x.shape
    ff = w_up.shape[1]
    tm = min(tm, m)
    return pl.pallas_call(
        _mlp_kernel,
        grid=(m // tm, ff // tf),
        in_specs=[pl.BlockSpec((tm, d), lambda i, f: (i, 0)), pl.BlockSpec((1, d), lambda i, f: (0, 0)),
                  pl.BlockSpec((d, tf), lambda i, f: (0, f)), pl.BlockSpec((tf, d), lambda i, f: (f, 0))],
        out_specs=pl.BlockSpec((tm, d), lambda i, f: (i, 0)),
        out_shape=jax.ShapeDtypeStruct((m, d), F32),
        scratch_shapes=[pltpu.VMEM((tm, d), BF16)],
        compiler_params=_cparams(("parallel", "arbitrary")),
        name="mlp",
    )(x, gain.reshape(1, d), w_up, w_down)


def _ple_kernel(x_ref, g_ref, wg_ref, p_ref, wp_ref, gf_ref, o_ref):
    x = x_ref[...]
    h = _rms(x, g_ref[...]).astype(BF16)
    gate = _sigmoid(jnp.dot(h, wg_ref[...], preferred_element_type=F32))
    emb = jnp.dot(p_ref[...].astype(BF16), wp_ref[...], preferred_element_type=F32)
    o_ref[...] = _rms(x + gate * emb, gf_ref[...])


def ple_final(x, gain, w_gate, p, w_p, g_final, tm=512):
    m, d = x.shape
    c = p.shape[1]
    tm = min(tm, m)
    const = lambda i: (0, 0)
    return pl.pallas_call(
        _ple_kernel,
        grid=(m // tm,),
        in_specs=[pl.BlockSpec((tm, d), lambda i: (i, 0)), pl.BlockSpec((1, d), const),
                  pl.BlockSpec((d, d), const), pl.BlockSpec((tm, c), lambda i: (i, 0)),
                  pl.BlockSpec((c, d), const), pl.BlockSpec((1, d), const)],
        out_specs=pl.BlockSpec((tm, d), lambda i: (i, 0)),
        out_shape=jax.ShapeDtypeStruct((m, d), F32),
        compiler_params=_cparams(("parallel",)),
        name="ple_final",
    )(x, gain.reshape(1, d), w_gate, p, w_p, g_final.reshape(1, d))


def kernel(x, p, g_mix, w_in, g_q, g_k, w_o_attn, hgrn_lb, g_hgrn, w_o_hgrn, w_out, g_mlp, w_up, w_down,
           g_ple, w_ple_gate, w_ple, g_final):
    b, s, d = x.shape
    m = b * s
    depth = w_in.shape[0]
    attn_q = N_Q_HEADS * HEAD_DIM
    attn_kv = N_KV_HEADS * HEAD_DIM
    hk = HGRN_HEADS * HEAD_DIM
    rope = rope_tables(s)
    lb_all = jnp.cumsum(jax.nn.softmax(hgrn_lb.astype(F32), axis=0), axis=0)

    xf = x.reshape(m, d)
    for i in range(depth):
        w = w_in[i]
        h = rmsnorm_bf16(xf, g_mix[i])
        c0 = 0
        q_a = proj_qk(h, w, c0, attn_q, g_q[i], rope, s, HEAD_DIM ** -0.5 * LOG2_E); c0 += attn_q
        k_a = proj_qk(h, w, c0, attn_kv, g_k[i], rope, s, 1.0); c0 += attn_kv
        v_a = proj_plain(h, w, c0, attn_kv, None); c0 += attn_kv
        q_r = proj_plain(h, w, c0, hk, "silu"); c0 += hk
        logf = proj_logf(h, w, c0, 2 * hk, lb_all[i]); c0 += 2 * hk
        i_r = proj_plain(h, w, c0, hk, None); c0 += hk
        g_r = proj_plain(h, w, c0, hk, "silu"); c0 += hk
        gates = proj_plain(h, w, c0, 2 * d, "sigmoid"); c0 += 2 * d

        att = gqa_attention(q_a.reshape(b, s, attn_q), k_a.reshape(b, s, attn_kv), v_a.reshape(b, s, attn_kv))
        o_fw, o_bw = hgrn2_bidir(q_r.reshape(b, s, hk), i_r.reshape(b, s, hk), logf.reshape(b, s, 2 * hk))
        o_r = hgrn_post(o_fw.reshape(m, hk), o_bw.reshape(m, hk), g_r, g_hgrn[i])
        mixed = gated_merge(att.reshape(m, attn_q), o_r, w_o_attn[i].astype(BF16), w_o_hgrn[i].astype(BF16), gates)
        xf = resid_proj(xf, mixed, w_out[i].astype(BF16))
        xf = mlp_block(xf, g_mlp[i], w_up[i].astype(BF16), w_down[i])
        assert depth == 1
        xf = ple_final(xf, g_ple[i], w_ple_gate[i].astype(BF16), p[i].reshape(m, -1), w_ple[i].astype(BF16),
                       g_final)
    return xf.reshape(b, s, d)
```

```python
import functools

import jax
import jax.numpy as jnp
from jax import lax
from jax.experimental import pallas as pl
from jax.experimental.pallas import tpu as pltpu

F32 = jnp.float32
BF16 = jnp.bfloat16

EPS = 1e-6
LOG2_E = 1.4426950408889634
HEAD_DIM = 128
N_Q_HEADS = 16
N_KV_HEADS = 4
GQA_GROUPS = N_Q_HEADS // N_KV_HEADS
GRID_W = 64
ROPE_THETA = 10000.0
HGRN_HEADS = 16
LANES = 128
SUBLANES = 8
VMEM_LIMIT = 56 * 1024 * 1024


def _cparams(sem):
    return pltpu.CompilerParams(dimension_semantics=sem, vmem_limit_bytes=VMEM_LIMIT)


def _sigmoid(x):
    return 1.0 / (1.0 + jnp.exp(-x))


def _rms(x, gain):
    ms = jnp.mean(x * x, axis=-1, keepdims=True)
    return x * lax.rsqrt(ms + EPS) * gain


def _rmsnorm_kernel(x_ref, g_ref, o_ref):
    o_ref[...] = _rms(x_ref[...], g_ref[...]).astype(o_ref.dtype)


def rmsnorm_bf16(x, gain, tm=512):
    m, d = x.shape
    tm = min(tm, m)
    return pl.pallas_call(
        _rmsnorm_kernel,
        grid=(m // tm,),
        in_specs=[pl.BlockSpec((tm, d), lambda i: (i, 0)), pl.BlockSpec((1, d), lambda i: (0, 0))],
        out_specs=pl.BlockSpec((tm, d), lambda i: (i, 0)),
        out_shape=jax.ShapeDtypeStruct((m, d), BF16),
        compiler_params=_cparams(("parallel",)),
        name="rmsnorm",
    )(x, gain.reshape(1, d))


QK_SUB = 2 * HEAD_DIM


MM_ROWS = 512


def _subtiled_matmul(h_ref, w_ref, epilogue):
    tm = h_ref.shape[0]
    rows_per = min(MM_ROWS, tm)
    units = [(slice(r, r + rows_per), slice(c0, c0 + QK_SUB))
             for c0 in range(0, w_ref.shape[1], QK_SUB) for r in range(0, tm, rows_per)]
    weights = {}

    def matmul(u):
        rs, cs = u
        if cs.start not in weights:
            weights.clear()
            weights[cs.start] = w_ref[:, cs].astype(BF16)
        return jnp.dot(h_ref[rs, :], weights[cs.start], preferred_element_type=F32)

    prev = matmul(units[0])
    for j in range(1, len(units)):
        cur = matmul(units[j])
        epilogue(prev, *units[j - 1])
        prev = cur
    epilogue(prev, *units[-1])


def _stationary_weights(w_ref, wb_scr, rope_order=False):
    @pl.when(pl.program_id(1) == 0)
    def _():
        w = w_ref[...]
        if rope_order:
            tn = w.shape[1]
            quarter = (lax.broadcasted_iota(jnp.int32, w.shape, 1) // (HEAD_DIM // 4)) % 4
            w = jnp.where(quarter == 1, pltpu.roll(w, tn - HEAD_DIM // 4, 1),
                          jnp.where(quarter == 2, pltpu.roll(w, HEAD_DIM // 4, 1), w))
        wb_scr[...] = w.astype(wb_scr.dtype)


def _proj_plain_kernel(h_ref, w_ref, o_ref, wb_scr, *, act):
    def epilogue(acc, rs, cs):
        if act == "silu":
            acc = acc * _sigmoid(acc)
        elif act == "sigmoid":
            acc = _sigmoid(acc)
        o_ref[rs, cs] = acc.astype(o_ref.dtype)

    _stationary_weights(w_ref, wb_scr)
    _subtiled_matmul(h_ref, wb_scr, epilogue)


def _proj_logf_kernel(h_ref, w_ref, lb_ref, o_ref, wb_scr):
    def epilogue(acc, rs, cs):
        lb = lb_ref[:, cs]
        o_ref[rs, cs] = jnp.log(lb + (1.0 - lb) * _sigmoid(acc)) * LOG2_E

    _stationary_weights(w_ref, wb_scr)
    _subtiled_matmul(h_ref, wb_scr, epilogue)


def _proj_qk_kernel(h_ref, w_ref, cg_ref, sg_ref, o_ref, wb_scr):
    ones = jnp.ones((HEAD_DIM, HEAD_DIM), BF16)
    _stationary_weights(w_ref, wb_scr, rope_order=True)

    def epilogue(acc, rs, cs):
        cg, sg = cg_ref[rs, :], sg_ref[rs, :]
        for hd in range(QK_SUB // HEAD_DIM):
            x = acc[:, hd * HEAD_DIM:(hd + 1) * HEAD_DIM]
            ssq = jnp.dot((x * x).astype(BF16), ones, preferred_element_type=F32)
            r = lax.rsqrt(ssq * (1.0 / HEAD_DIM) + EPS)
            y = r * (x * cg + pltpu.roll(x, HEAD_DIM // 2, 1) * sg)
            c0 = cs.start + hd * HEAD_DIM
            o_ref[rs, c0:c0 + HEAD_DIM] = y.astype(o_ref.dtype)

    _subtiled_matmul(h_ref, wb_scr, epilogue)


def _proj_call(kernel, h, w, col0, ncols, out_dtype, extra=(), extra_specs=(), tm=1024, tn=1024):
    m, d = h.shape
    tm = min(tm, m)
    tn = min(tn, ncols)
    assert col0 % tn == 0 and ncols % tn == 0 and m % tm == 0
    jb = col0 // tn
    return pl.pallas_call(
        kernel,
        grid=(ncols // tn, m // tm),
        in_specs=[pl.BlockSpec((tm, d), lambda j, i: (i, 0)),
                  pl.BlockSpec((d, tn), lambda j, i: (0, jb + j))] + list(extra_specs),
        out_specs=pl.BlockSpec((tm, tn), lambda j, i: (i, j)),
        out_shape=jax.ShapeDtypeStruct((m, ncols), out_dtype),
        scratch_shapes=[pltpu.VMEM((d, tn), BF16)],
        compiler_params=_cparams(("parallel", "arbitrary")),
        name="in_proj",
    )(h, w, *extra)


def proj_plain(h, w, col0, ncols, act, out_dtype=BF16):
    return _proj_call(functools.partial(_proj_plain_kernel, act=act), h, w, col0, ncols, out_dtype)


def proj_logf(h, w, col0, ncols, lb, tn=1024):
    return _proj_call(_proj_logf_kernel, h, w, col0, ncols, F32, extra=(lb.reshape(1, ncols),),
                      extra_specs=(pl.BlockSpec((1, tn), lambda j, i: (0, j)),), tn=tn)


def proj_qk(h, w, col0, ncols, gain, rope, seq, scale, tm=1024, tn=1024):
    cos, sin = rope
    gp = gain.astype(F32)[jnp.array(ROPE_PERM)]
    cg = cos * (gp * scale)[None, :]
    sg = sin * (jnp.roll(gp, HEAD_DIM // 2) * scale)[None, :]
    tm = min(tm, seq)
    tn = min(tn, ncols)
    nsb = seq // tm
    tab = pl.BlockSpec((tm, HEAD_DIM), lambda j, i: (i % nsb, 0))
    return _proj_call(_proj_qk_kernel, h, w, col0, ncols, BF16, extra=(cg, sg), extra_specs=(tab, tab),
                      tm=tm, tn=tn)


ROPE_PERM = tuple(list(range(0, 32)) + list(range(64, 96)) + list(range(32, 64)) + list(range(96, 128)))


def rope_tables(seq):
    half = HEAD_DIM // 2
    t = jnp.arange(seq, dtype=jnp.int32)
    row = (t // GRID_W).astype(F32)
    col = (t % GRID_W).astype(F32)
    inv_freq = ROPE_THETA ** (-jnp.arange(0, half, 2, dtype=F32) / half)
    ang = jnp.concatenate([row[:, None] * inv_freq[None, :], col[:, None] * inv_freq[None, :]], axis=-1)
    cos = jnp.concatenate([jnp.cos(ang), jnp.cos(ang)], axis=-1)
    sin = jnp.concatenate([-jnp.sin(ang), jnp.sin(ang)], axis=-1)
    return cos, sin


ATTN_TQ = 128
ATTN_TKC = 8192


def _attn_kernel(q0_ref, qa_ref, qb_ref, k_ref, v_ref, o_ref, s_scr, mrun_scr, mcur_scr, acc_scr, *, tkc):
    t = pl.program_id(2)
    nt = pl.num_programs(2)
    tq = qa_ref.shape[1]
    nch = k_ref.shape[1] // tkc
    nsl = tkc // LANES
    ones = jnp.ones((tkc, LANES), BF16)

    def stack(q_ref):
        return jnp.concatenate([q_ref[0, :, g * HEAD_DIM:(g + 1) * HEAD_DIM] for g in range(GQA_GROUPS)], axis=0)

    def pass1(qs, c, slot):
        off = pl.multiple_of(c * tkc, tkc)
        s = lax.dot_general(qs, k_ref[0, pl.ds(off, tkc), :], (((1,), (1,)), ((), ())),
                            preferred_element_type=F32)
        s_scr[slot, c] = s
        m = s[:, :LANES]
        for i in range(1, nsl):
            m = jnp.maximum(m, s[:, i * LANES:(i + 1) * LANES])
        mrun_scr[...] = jnp.maximum(mrun_scr[...], m)

    def pass2(c, slot):
        off = pl.multiple_of(c * tkc, tkc)
        vext = jnp.concatenate([v_ref[0, pl.ds(off, tkc), :], ones], axis=1)
        m = mcur_scr[...]
        p = jnp.concatenate([jnp.exp2(s_scr[slot, c, :, i * LANES:(i + 1) * LANES] - m) for i in range(nsl)],
                            axis=1)
        acc_scr[...] += jnp.dot(p.astype(BF16), vext, preferred_element_type=F32)

    def begin_pass1():
        mrun_scr[...] = jnp.full(mrun_scr.shape, -jnp.inf, F32)

    def end_pass1():
        mcur_scr[...] = jnp.broadcast_to(jnp.max(mrun_scr[...], axis=1, keepdims=True), mcur_scr.shape)
        acc_scr[...] = jnp.zeros(acc_scr.shape, F32)

    def emit(half):
        o = acc_scr[:, :HEAD_DIM] / acc_scr[:, HEAD_DIM:]
        for g in range(GQA_GROUPS):
            o_ref[0, half * tq:(half + 1) * tq, g * HEAD_DIM:(g + 1) * HEAD_DIM] = (
                o[g * tq:(g + 1) * tq].astype(o_ref.dtype))

    @pl.when(t == 0)
    def _():
        qs0 = stack(q0_ref)
        begin_pass1()

        def body_0(c, carry):
            pass1(qs0, c, 0)
            return carry

        lax.fori_loop(0, nch, body_0, 0)
        end_pass1()

    qs = stack(qa_ref)
    begin_pass1()

    def body_x(c, carry):
        pass1(qs, c, 1)
        pass2(c, 0)
        return carry

    lax.fori_loop(0, nch, body_x, 0)
    emit(0)
    end_pass1()

    @pl.when(t < nt - 1)
    def _():
        qs2 = stack(qb_ref)
        begin_pass1()

        def body_y(c, carry):
            pass1(qs2, c, 0)
            pass2(c, 1)
            return carry

        lax.fori_loop(0, nch, body_y, 0)
        emit(1)
        end_pass1()

    @pl.when(t == nt - 1)
    def _():
        def body_z(c, carry):
            pass2(c, 1)
            return carry

        lax.fori_loop(0, nch, body_z, 0)
        emit(1)


def gqa_attention(q, k, v, tq=ATTN_TQ, tkc=ATTN_TKC):
    b, s, _ = q.shape
    tkc = min(tkc, s)
    nq = s // tq
    assert nq % 2 == 0 and s % tkc == 0
    gw = GQA_GROUPS * HEAD_DIM
    rows = GQA_GROUPS * tq
    qblk = (1, tq, gw)
    kvblk = (1, s, HEAD_DIM)
    return pl.pallas_call(
        functools.partial(_attn_kernel, tkc=tkc),
        grid=(b, N_KV_HEADS, nq // 2),
        in_specs=[pl.BlockSpec(qblk, lambda bi, h, t: (bi, 0, h)),
                  pl.BlockSpec(qblk, lambda bi, h, t: (bi, 2 * t + 1, h)),
                  pl.BlockSpec(qblk, lambda bi, h, t: (bi, jnp.minimum(2 * t + 2, nq - 1), h)),
                  pl.BlockSpec(kvblk, lambda bi, h, t: (bi, 0, h)),
                  pl.BlockSpec(kvblk, lambda bi, h, t: (bi, 0, h))],
        out_specs=pl.BlockSpec((1, 2 * tq, gw), lambda bi, h, t: (bi, t, h)),
        out_shape=jax.ShapeDtypeStruct(q.shape, BF16),
        scratch_shapes=[pltpu.VMEM((2, s // tkc, rows, tkc), F32), pltpu.VMEM((rows, LANES), F32),
                        pltpu.VMEM((rows, LANES), F32), pltpu.VMEM((rows, 2 * HEAD_DIM), F32)],
        compiler_params=_cparams(("parallel", "parallel", "arbitrary")),
        name="gqa_attention",
    )(q, q, q, k, v)


HGRN_CHUNK = 128
HGRN_LAG = 2


def _group_row(x, group, row):
    c, n = x.shape
    xr = x.reshape(c // group, group, n)
    return jnp.broadcast_to(xr[:, row:row + 1, :], xr.shape).reshape(c, n)


def _boundary_row(x, m, rev, odd_block):
    ref_row = m if rev else m - 1
    if 2 * m >= SUBLANES:
        return _group_row(x, 2 * m, ref_row)
    if m == 1:
        c, n = x.shape
        x3 = x.reshape(c // SUBLANES, SUBLANES, n)
        other = pltpu.roll(x3, SUBLANES - 1 if rev else 1, 1).reshape(c, n)
        return jnp.where(odd_block(1), x, other) if rev else jnp.where(odd_block(1), other, x)
    assert 4 * m == SUBLANES
    return jnp.where(odd_block(2 * m), _group_row(x, SUBLANES, 2 * m + ref_row), _group_row(x, SUBLANES, ref_row))


def hgrn_constants():
    import numpy as np
    c = HGRN_CHUNK
    t = np.arange(c)[:, None]
    s = np.arange(c)[None, :]
    pair, tri = [], []
    for rev in (False, True):
        masks = [t == s]
        m = 1
        while m < c:
            same = (t // (2 * m)) == (s // (2 * m))
            t_query = ((t // m) % 2) == (0 if rev else 1)
            s_key = ((s // m) % 2) == (1 if rev else 0)
            masks.append(same & t_query & s_key)
            m *= 2
        pair.append(np.stack(masks))
        tri.append((s >= t) if rev else (s <= t))
    rows = np.stack([np.broadcast_to(((t // n) % 2) == 1, (c, LANES)) for n in (1, 2, 4)])
    return (jnp.asarray(np.stack(pair), jnp.int32), jnp.asarray(rows, jnp.int32),
            jnp.asarray(np.stack(tri), BF16))


def _neg_abs(x):
    return pltpu.bitcast(pltpu.bitcast(x, jnp.uint32) | jnp.uint32(0x80000000), F32)


def _nt(x, y):
    return lax.dot_general(x, y, (((1,), (1,)), ((), ())), preferred_element_type=F32)


def _hgrn_chunk(q_ref, v_ref, lf_ref, o_ref, st_scr, a_scr, p_scr, oi_scr, rows3, col, slot, consts, *, rev):
    pair_ref, rows_ref, tri_ref = consts
    r_prev, r0, r_next = rows3
    d = 1 if rev else 0
    odd_block = lambda n: rows_ref[{1: 0, 2: 1, 4: 2}[n]] != 0
    c = HGRN_CHUNK
    cols = slice(col * HEAD_DIM, (col + 1) * HEAD_DIM)

    pv = jnp.dot(p_scr[slot], v_ref[0, pl.ds(r_prev, c), cols], preferred_element_type=F32)
    a2 = _cumsum_issue(lf_ref[0, pl.ds(r_next, c), cols], tri_ref[d])

    qb = q_ref[0, pl.ds(r0, c), cols]
    vb = v_ref[0, pl.ds(r0, c), cols]
    k = 1.0 - jnp.exp2(lf_ref[0, pl.ds(r0, c), cols])
    sc = _nt(qb, k.astype(BF16))
    q = qb.astype(F32)
    a = a_scr[slot]
    edge = 0 if rev else c - 1
    a_end = a[edge:edge + 1, :]

    st = st_scr[col]
    o = _nt((q * jnp.exp2(a)).astype(BF16), st.astype(BF16))
    kdec = (k * jnp.exp2(a_end - a)).astype(BF16)
    st_new = lax.dot_general(vb, kdec, (((0,), (0,)), ((), ())), preferred_element_type=F32)
    st_decayed = st * jnp.exp2(a_end)

    nstrip = c // SUBLANES
    strip = lambda x, i: x[i * SUBLANES:(i + 1) * SUBLANES]
    p = [None] * nstrip

    def apply(pending):
        level, scores, rows = pending
        for j, i in enumerate(rows):
            mask = pair_ref[d, level, i * SUBLANES:(i + 1) * SUBLANES, :] != 0
            p[i] = jnp.where(mask, strip(scores, j), 0.0 if p[i] is None else p[i])

    query_first = rev

    def halves(m):
        out = []
        for g in range(c // (2 * m)):
            first = slice(g * 2 * m, g * 2 * m + m)
            second = slice(g * 2 * m + m, (g + 1) * 2 * m)
            out.append((first, second) if query_first else (second, first))
        return out

    def factor(m):
        ref = _boundary_row(a, m, rev, odd_block)
        if m < SUBLANES:
            return jnp.exp2(_neg_abs(a - ref))
        parts = []
        for qs, ks in halves(m):
            dq, dk = a[qs] - ref[qs], ref[ks] - a[ks]
            parts += [dq, dk] if query_first else [dk, dq]
        return jnp.exp2(jnp.concatenate(parts, axis=0))

    pending = [(0, sc, list(range(nstrip)))]
    nlevel = pair_ref.shape[1] - 1
    m = 1
    e = factor(m)
    for level in range(1, nlevel + 1):
        e_next = factor(2 * m) if level < nlevel else None
        if m < SUBLANES:
            qk = jnp.where(odd_block(m), k, q) if query_first else jnp.where(odd_block(m), q, k)
            z = (qk * e).astype(BF16)
            sc = _nt(z, z)
            rows = list(range(nstrip))
        else:
            parts, qparts, rows = [], [], []
            for qs, ks in halves(m):
                zq, zk = q[qs] * e[qs], k[ks] * e[ks]
                parts += [zq, zk] if query_first else [zk, zq]
                qparts.append(zq)
                rows += list(range(qs.start // SUBLANES, qs.stop // SUBLANES))
            z = jnp.concatenate(parts, axis=0).astype(BF16)
            sc = _nt(jnp.concatenate(qparts, axis=0).astype(BF16), z)
        e = e_next
        yield
        if level == 2:
            o_prev = oi_scr[slot]
            oi_scr[slot] = o
            st_scr[col] = st_decayed + st_new
        pending.append((level, sc, rows))
        if len(pending) > HGRN_LAG:
            apply(pending.pop(0))
        m *= 2
    a_scr[slot] = a2[:, :LANES] + a2[:, LANES:]
    o_ref[0, pl.ds(r_prev, c), cols] = (o_prev + pv).astype(o_ref.dtype)
    yield
    for item in pending:
        apply(item)
    p_scr[slot] = jnp.concatenate(p, axis=0).astype(BF16)


def _cumsum_issue(lf, tri):
    hi = lf.astype(BF16)
    lo = (lf - hi.astype(F32)).astype(BF16)
    return jnp.dot(tri, jnp.concatenate([hi, lo], axis=1), preferred_element_type=F32)


def _interleave(chains):
    chains = list(chains)
    while chains:
        for ch in list(chains):
            try:
                next(ch)
            except StopIteration:
                chains.remove(ch)


def _hgrn_kernel(qf_ref, vf_ref, lff_ref, qb_ref, vb_ref, lfb_ref, pair_ref, rows_ref, tri_ref,
                 of_ref, ob_ref, sf_scr, sb_scr, a_scr, p_scr, oi_scr):
    @pl.when(pl.program_id(2) == 0)
    def _():
        sf_scr[...] = jnp.zeros(sf_scr.shape, F32)
        sb_scr[...] = jnp.zeros(sb_scr.shape, F32)

    c = HGRN_CHUNK
    consts = (pair_ref, rows_ref, tri_ref)
    n = qf_ref.shape[1] // c
    heads = qf_ref.shape[2] // HEAD_DIM
    row = lambda idx: pl.multiple_of(idx * c, c)
    dirs = ((qf_ref, vf_ref, lff_ref, of_ref, sf_scr, False), (qb_ref, vb_ref, lfb_ref, ob_ref, sb_scr, True))
    first = lambda rev: n - 1 if rev else 0
    last = lambda rev: 0 if rev else n - 1

    for hd in range(heads):
        cols = slice(hd * HEAD_DIM, (hd + 1) * HEAD_DIM)
        for d, (_, _, lf_ref, _, _, rev) in enumerate(dirs):
            a2 = _cumsum_issue(lf_ref[0, pl.ds(first(rev) * c, c), cols], tri_ref[d])
            a_scr[2 * hd + d] = a2[:, :LANES] + a2[:, LANES:]
    p_scr[...] = jnp.zeros(p_scr.shape, BF16)
    oi_scr[...] = jnp.zeros(oi_scr.shape, F32)

    def body(ci, carry):
        idx = {False: (jnp.maximum(ci - 1, 0), ci, jnp.minimum(ci + 1, n - 1)),
               True: (jnp.minimum(n - ci, n - 1), n - 1 - ci, jnp.maximum(n - 2 - ci, 0))}
        chains = []
        for hd in range(heads):
            for d, (q_ref, v_ref, lf_ref, o_ref, st_scr, rev) in enumerate(dirs):
                rows3 = tuple(row(i) for i in idx[rev])
                chains.append(_hgrn_chunk(q_ref, v_ref, lf_ref, o_ref, st_scr, a_scr, p_scr, oi_scr, rows3, hd,
                                          2 * hd + d, consts, rev=rev))
        _interleave(chains)
        return carry

    lax.fori_loop(0, n, body, 0)

    for hd in range(heads):
        cols = slice(hd * HEAD_DIM, (hd + 1) * HEAD_DIM)
        for d, (_, v_ref, _, o_ref, _, rev) in enumerate(dirs):
            rws = pl.ds(last(rev) * c, c)
            pv = jnp.dot(p_scr[2 * hd + d], v_ref[0, rws, cols], preferred_element_type=F32)
            o_ref[0, rws, cols] = (oi_scr[2 * hd + d] + pv).astype(o_ref.dtype)


HGRN_HEADS_PER_STEP = 4


def hgrn2_bidir(q, v, logf, ts=1024, hps=HGRN_HEADS_PER_STEP):
    b, s, hk = q.shape
    ts = min(ts, s)
    nt = s // ts
    ng = hk // (hps * HEAD_DIM)
    blk = (1, ts, hps * HEAD_DIM)
    fwd = lambda bi, h, i: (bi, i, h)
    bwd = lambda bi, h, i: (bi, nt - 1 - i, h)
    out = jax.ShapeDtypeStruct((b, s, hk), BF16)
    state = pltpu.VMEM((hps, HEAD_DIM, HEAD_DIM), F32)
    consts = hgrn_constants()
    whole = lambda x: pl.BlockSpec(x.shape, lambda bi, h, i: (0,) * x.ndim)
    return pl.pallas_call(
        _hgrn_kernel,
        grid=(b, ng, nt),
        in_specs=[pl.BlockSpec(blk, fwd), pl.BlockSpec(blk, fwd), pl.BlockSpec(blk, fwd),
                  pl.BlockSpec(blk, bwd), pl.BlockSpec(blk, bwd),
                  pl.BlockSpec(blk, lambda bi, h, i: (bi, nt - 1 - i, ng + h))] + [whole(x) for x in consts],
        out_specs=[pl.BlockSpec(blk, fwd), pl.BlockSpec(blk, bwd)],
        out_shape=[out, out],
        scratch_shapes=[state, state, pltpu.VMEM((2 * hps, HGRN_CHUNK, HEAD_DIM), F32),
                        pltpu.VMEM((2 * hps, HGRN_CHUNK, HGRN_CHUNK), BF16),
                        pltpu.VMEM((2 * hps, HGRN_CHUNK, HEAD_DIM), F32)],
        compiler_params=_cparams(("parallel", "parallel", "arbitrary")),
        name="hgrn2",
    )(q, v, logf, q, v, logf, *consts)


def _hgrn_post_kernel(of_ref, ob_ref, gr_ref, gn_ref, o_ref):
    gn = gn_ref[...]
    for hd in range(o_ref.shape[1] // HEAD_DIM):
        sl = slice(hd * HEAD_DIM, (hd + 1) * HEAD_DIM)
        o = _rms(of_ref[:, sl].astype(F32) + ob_ref[:, sl].astype(F32), gn)
        o_ref[:, sl] = (o * gr_ref[:, sl].astype(F32)).astype(o_ref.dtype)


def hgrn_post(o_fw, o_bw, g_silu, g_norm, tm=512):
    m, d = o_fw.shape
    tm = min(tm, m)
    row = pl.BlockSpec((tm, d), lambda i: (i, 0))
    return pl.pallas_call(
        _hgrn_post_kernel,
        grid=(m // tm,),
        in_specs=[row, row, row, pl.BlockSpec((1, HEAD_DIM), lambda i: (0, 0))],
        out_specs=row,
        out_shape=jax.ShapeDtypeStruct((m, d), BF16),
        compiler_params=_cparams(("parallel",)),
        name="hgrn_post",
    )(o_fw, o_bw, g_silu, g_norm.reshape(1, HEAD_DIM))


def _merge_kernel(att_ref, or_ref, woa_ref, woh_ref, sa_ref, sh_ref, o_ref):
    tm = o_ref.shape[0]
    rows_per = min(MM_ROWS, tm)
    woa, woh = woa_ref[...].astype(BF16), woh_ref[...].astype(BF16)
    for r in range(0, tm, rows_per):
        rs = slice(r, r + rows_per)
        ya = jnp.dot(att_ref[rs, :], woa, preferred_element_type=F32)
        yh = jnp.dot(or_ref[rs, :], woh, preferred_element_type=F32)
        o_ref[rs, :] = (sa_ref[rs, :].astype(F32) * ya + sh_ref[rs, :].astype(F32) * yh).astype(o_ref.dtype)


def gated_merge(att, o_r, w_oa, w_oh, gates, tm=1024, tn=512):
    m, d = att.shape
    tm = min(tm, m)
    nj = d // tn
    row = pl.BlockSpec((tm, d), lambda i, j: (i, 0))
    wsp = pl.BlockSpec((d, tn), lambda i, j: (0, j))
    return pl.pallas_call(
        _merge_kernel,
        grid=(m // tm, nj),
        in_specs=[row, row, wsp, wsp,
                  pl.BlockSpec((tm, tn), lambda i, j: (i, j)),
                  pl.BlockSpec((tm, tn), lambda i, j: (i, nj + j))],
        out_specs=pl.BlockSpec((tm, tn), lambda i, j: (i, j)),
        out_shape=jax.ShapeDtypeStruct((m, d), BF16),
        compiler_params=_cparams(("parallel", "arbitrary")),
        name="gated_merge",
    )(att, o_r, w_oa, w_oh, gates, gates)


def _resid_proj_kernel(x_ref, a_ref, w_ref, o_ref):
    tm = o_ref.shape[0]
    rows_per = min(MM_ROWS, tm)
    w = w_ref[...].astype(BF16)
    for r in range(0, tm, rows_per):
        rs = slice(r, r + rows_per)
        o_ref[rs, :] = x_ref[rs, :] + jnp.dot(a_ref[rs, :], w, preferred_element_type=F32)


def resid_proj(x, a, w, tm=1024, tn=512):
    m, d = x.shape
    tm = min(tm, m)
    return pl.pallas_call(
        _resid_proj_kernel,
        grid=(m // tm, d // tn),
        in_specs=[pl.BlockSpec((tm, tn), lambda i, j: (i, j)), pl.BlockSpec((tm, a.shape[1]), lambda i, j: (i, 0)),
                  pl.BlockSpec((a.shape[1], tn), lambda i, j: (0, j))],
        out_specs=pl.BlockSpec((tm, tn), lambda i, j: (i, j)),
        out_shape=jax.ShapeDtypeStruct((m, d), F32),
        compiler_params=_cparams(("parallel", "arbitrary")),
        name="out_proj",
    )(x, a, w)


def _mlp_kernel(x_ref, g_ref, wu_ref, wd_ref, o_ref, h_scr):
    @pl.when(pl.program_id(1) == 0)
    def _():
        x = x_ref[...]
        h_scr[...] = _rms(x, g_ref[...]).astype(h_scr.dtype)
        o_ref[...] = x

    tm = o_ref.shape[0]
    rows_per = min(MM_ROWS, tm)
    wu, wd = wu_ref[...].astype(BF16), wd_ref[...].astype(BF16)
    for r in range(0, tm, rows_per):
        rs = slice(r, r + rows_per)
        u = jnp.maximum(jnp.dot(h_scr[rs, :], wu, preferred_element_type=F32), 0.0)
        o_ref[rs, :] += jnp.dot((u * u).astype(BF16), wd, preferred_element_type=F32)


def mlp_block(x, gain, w_up, w_down, tm=1024, tf=512):
    m, d = x.shape
    ff = w_up.shape[1]
    tm = min(tm, m)
    return pl.pallas_call(
        _mlp_kernel,
        grid=(m // tm, ff // tf),
        in_specs=[pl.BlockSpec((tm, d), lambda i, f: (i, 0)), pl.BlockSpec((1, d), lambda i, f: (0, 0)),
                  pl.BlockSpec((d, tf), lambda i, f: (0, f)), pl.BlockSpec((tf, d), lambda i, f: (f, 0))],
        out_specs=pl.BlockSpec((tm, d), lambda i, f: (i, 0)),
        out_shape=jax.ShapeDtypeStruct((m, d), F32),
        scratch_shapes=[pltpu.VMEM((tm, d), BF16)],
        compiler_params=_cparams(("parallel", "arbitrary")),
        name="mlp",
    )(x, gain.reshape(1, d), w_up, w_down)


def _ple_kernel(x_ref, g_ref, wg_ref, p_ref, wp_ref, gf_ref, o_ref):
    x = x_ref[...]
    h = _rms(x, g_ref[...]).astype(BF16)
    gate = _sigmoid(jnp.dot(h, wg_ref[...], preferred_element_type=F32))
    emb = jnp.dot(p_ref[...].astype(BF16), wp_ref[...], preferred_element_type=F32)
    o_ref[...] = _rms(x + gate * emb, gf_ref[...])


def ple_final(x, gain, w_gate, p, w_p, g_final, tm=512):
    m, d = x.shape
    c = p.shape[1]
    tm = min(tm, m)
    const = lambda i: (0, 0)
    return pl.pallas_call(
        _ple_kernel,
        grid=(m // tm,),
        in_specs=[pl.BlockSpec((tm, d), lambda i: (i, 0)), pl.BlockSpec((1, d), const),
                  pl.BlockSpec((d, d), const), pl.BlockSpec((tm, c), lambda i: (i, 0)),
                  pl.BlockSpec((c, d), const), pl.BlockSpec((1, d), const)],
        out_specs=pl.BlockSpec((tm, d), lambda i: (i, 0)),
        out_shape=jax.ShapeDtypeStruct((m, d), F32),
        compiler_params=_cparams(("parallel",)),
        name="ple_final",
    )(x, gain.reshape(1, d), w_gate, p, w_p, g_final.reshape(1, d))


def kernel(x, p, g_mix, w_in, g_q, g_k, w_o_attn, hgrn_lb, g_hgrn, w_o_hgrn, w_out, g_mlp, w_up, w_down,
           g_ple, w_ple_gate, w_ple, g_final):
    b, s, d = x.shape
    m = b * s
    depth = w_in.shape[0]
    attn_q = N_Q_HEADS * HEAD_DIM
    attn_kv = N_KV_HEADS * HEAD_DIM
    hk = HGRN_HEADS * HEAD_DIM
    rope = rope_tables(s)
    lb_all = jnp.cumsum(jax.nn.softmax(hgrn_lb.astype(F32), axis=0), axis=0)

    xf = x.reshape(m, d)
    for i in range(depth):
        w = w_in[i]
        h = rmsnorm_bf16(xf, g_mix[i])
        c0 = 0
        q_a = proj_qk(h, w, c0, attn_q, g_q[i], rope, s, HEAD_DIM ** -0.5 * LOG2_E); c0 += attn_q
        k_a = proj_qk(h, w, c0, attn_kv, g_k[i], rope, s, 1.0); c0 += attn_kv
        v_a = proj_plain(h, w, c0, attn_kv, None); c0 += attn_kv
        q_r = proj_plain(h, w, c0, hk, "silu"); c0 += hk
        logf = proj_logf(h, w, c0, 2 * hk, lb_all[i]); c0 += 2 * hk
        i_r = proj_plain(h, w, c0, hk, None); c0 += hk
        g_r = proj_plain(h, w, c0, hk, "silu"); c0 += hk
        gates = proj_plain(h, w, c0, 2 * d, "sigmoid"); c0 += 2 * d

        att = gqa_attention(q_a.reshape(b, s, attn_q), k_a.reshape(b, s, attn_kv), v_a.reshape(b, s, attn_kv))
        o_fw, o_bw = hgrn2_bidir(q_r.reshape(b, s, hk), i_r.reshape(b, s, hk), logf.reshape(b, s, 2 * hk))
        o_r = hgrn_post(o_fw.reshape(m, hk), o_bw.reshape(m, hk), g_r, g_hgrn[i])
        mixed = gated_merge(att.reshape(m, attn_q), o_r, w_o_attn[i].astype(BF16), w_o_hgrn[i].astype(BF16), gates)
        xf = resid_proj(xf, mixed, w_out[i].astype(BF16))
        xf = mlp_block(xf, g_mlp[i], w_up[i].astype(BF16), w_down[i])
        assert depth == 1
        xf = ple_final(xf, g_ple[i], w_ple_gate[i].astype(BF16), p[i].reshape(m, -1), w_ple[i].astype(BF16),
                       g_final)
    return xf.reshape(b, s, d)
```

```python
import functools

import jax
import jax.numpy as jnp
from jax import lax
from jax.experimental import pallas as pl
from jax.experimental.pallas import tpu as pltpu

F32 = jnp.float32
BF16 = jnp.bfloat16

EPS = 1e-6
LOG2_E = 1.4426950408889634
HEAD_DIM = 128
N_Q_HEADS = 16
N_KV_HEADS = 4
GQA_GROUPS = N_Q_HEADS // N_KV_HEADS
GRID_W = 64
ROPE_THETA = 10000.0
HGRN_HEADS = 16
LANES = 128
SUBLANES = 8
VMEM_LIMIT = 56 * 1024 * 1024


def _cparams(sem):
    return pltpu.CompilerParams(dimension_semantics=sem, vmem_limit_bytes=VMEM_LIMIT)


def _sigmoid(x):
    return 1.0 / (1.0 + jnp.exp(-x))


def _rms(x, gain):
    ms = jnp.mean(x * x, axis=-1, keepdims=True)
    return x * lax.rsqrt(ms + EPS) * gain


def _rmsnorm_kernel(x_ref, g_ref, o_ref):
    o_ref[...] = _rms(x_ref[...], g_ref[...]).astype(o_ref.dtype)


def rmsnorm_bf16(x, gain, tm=512):
    m, d = x.shape
    tm = min(tm, m)
    return pl.pallas_call(
        _rmsnorm_kernel,
        grid=(m // tm,),
        in_specs=[pl.BlockSpec((tm, d), lambda i: (i, 0)), pl.BlockSpec((1, d), lambda i: (0, 0))],
        out_specs=pl.BlockSpec((tm, d), lambda i: (i, 0)),
        out_shape=jax.ShapeDtypeStruct((m, d), BF16),
        compiler_params=_cparams(("parallel",)),
        name="rmsnorm",
    )(x, gain.reshape(1, d))


QK_SUB = 2 * HEAD_DIM


MM_ROWS = 512


def _subtiled_matmul(h_ref, w_ref, epilogue):
    tm = h_ref.shape[0]
    rows_per = min(MM_ROWS, tm)
    units = [(slice(r, r + rows_per), slice(c0, c0 + QK_SUB))
             for c0 in range(0, w_ref.shape[1], QK_SUB) for r in range(0, tm, rows_per)]
    weights = {}

    def matmul(u):
        rs, cs = u
        if cs.start not in weights:
            weights.clear()
            weights[cs.start] = w_ref[:, cs].astype(BF16)
        return jnp.dot(h_ref[rs, :], weights[cs.start], preferred_element_type=F32)

    prev = matmul(units[0])
    for j in range(1, len(units)):
        cur = matmul(units[j])
        epilogue(prev, *units[j - 1])
        prev = cur
    epilogue(prev, *units[-1])


def _stationary_weights(w_ref, wb_scr, rope_order=False):
    @pl.when(pl.program_id(1) == 0)
    def _():
        w = w_ref[...]
        if rope_order:
            tn = w.shape[1]
            quarter = (lax.broadcasted_iota(jnp.int32, w.shape, 1) // (HEAD_DIM // 4)) % 4
            w = jnp.where(quarter == 1, pltpu.roll(w, tn - HEAD_DIM // 4, 1),
                          jnp.where(quarter == 2, pltpu.roll(w, HEAD_DIM // 4, 1), w))
        wb_scr[...] = w.astype(wb_scr.dtype)


def _proj_plain_kernel(h_ref, w_ref, o_ref, wb_scr, *, act):
    def epilogue(acc, rs, cs):
        if act == "silu":
            acc = acc * _sigmoid(acc)
        elif act == "sigmoid":
            acc = _sigmoid(acc)
        o_ref[rs, cs] = acc.astype(o_ref.dtype)

    _stationary_weights(w_ref, wb_scr)
    _subtiled_matmul(h_ref, wb_scr, epilogue)


def _proj_logf_kernel(h_ref, w_ref, lb_ref, o_ref, wb_scr):
    def epilogue(acc, rs, cs):
        lb = lb_ref[:, cs]
        o_ref[rs, cs] = jnp.log(lb + (1.0 - lb) * _sigmoid(acc)) * LOG2_E

    _stationary_weights(w_ref, wb_scr)
    _subtiled_matmul(h_ref, wb_scr, epilogue)


def _proj_qk_kernel(h_ref, w_ref, cg_ref, sg_ref, o_ref, wb_scr):
    ones = jnp.ones((HEAD_DIM, HEAD_DIM), BF16)
    _stationary_weights(w_ref, wb_scr, rope_order=True)

    def epilogue(acc, rs, cs):
        cg, sg = cg_ref[rs, :], sg_ref[rs, :]
        for hd in range(QK_SUB // HEAD_DIM):
            x = acc[:, hd * HEAD_DIM:(hd + 1) * HEAD_DIM]
            ssq = jnp.dot((x * x).astype(BF16), ones, preferred_element_type=F32)
            r = lax.rsqrt(ssq * (1.0 / HEAD_DIM) + EPS)
            y = r * (x * cg + pltpu.roll(x, HEAD_DIM // 2, 1) * sg)
            c0 = cs.start + hd * HEAD_DIM
            o_ref[rs, c0:c0 + HEAD_DIM] = y.astype(o_ref.dtype)

    _subtiled_matmul(h_ref, wb_scr, epilogue)


def _proj_call(kernel, h, w, col0, ncols, out_dtype, extra=(), extra_specs=(), tm=1024, tn=1024):
    m, d = h.shape
    tm = min(tm, m)
    tn = min(tn, ncols)
    assert col0 % tn == 0 and ncols % tn == 0 and m % tm == 0
    jb = col0 // tn
    return pl.pallas_call(
        kernel,
        grid=(ncols // tn, m // tm),
        in_specs=[pl.BlockSpec((tm, d), lambda j, i: (i, 0)),
                  pl.BlockSpec((d, tn), lambda j, i: (0, jb + j))] + list(extra_specs),
        out_specs=pl.BlockSpec((tm, tn), lambda j, i: (i, j)),
        out_shape=jax.ShapeDtypeStruct((m, ncols), out_dtype),
        scratch_shapes=[pltpu.VMEM((d, tn), BF16)],
        compiler_params=_cparams(("parallel", "arbitrary")),
        name="in_proj",
    )(h, w, *extra)


def proj_plain(h, w, col0, ncols, act, out_dtype=BF16):
    return _proj_call(functools.partial(_proj_plain_kernel, act=act), h, w, col0, ncols, out_dtype)


def proj_logf(h, w, col0, ncols, lb, tn=1024):
    return _proj_call(_proj_logf_kernel, h, w, col0, ncols, F32, extra=(lb.reshape(1, ncols),),
                      extra_specs=(pl.BlockSpec((1, tn), lambda j, i: (0, j)),), tn=tn)


def proj_qk(h, w, col0, ncols, gain, rope, seq, scale, tm=1024, tn=1024):
    cos, sin = rope
    gp = gain.astype(F32)[jnp.array(ROPE_PERM)]
    cg = cos * (gp * scale)[None, :]
    sg = sin * (jnp.roll(gp, HEAD_DIM // 2) * scale)[None, :]
    tm = min(tm, seq)
    tn = min(tn, ncols)
    nsb = seq // tm
    tab = pl.BlockSpec((tm, HEAD_DIM), lambda j, i: (i % nsb, 0))
    return _proj_call(_proj_qk_kernel, h, w, col0, ncols, BF16, extra=(cg, sg), extra_specs=(tab, tab),
                      tm=tm, tn=tn)


ROPE_PERM = tuple(list(range(0, 32)) + list(range(64, 96)) + list(range(32, 64)) + list(range(96, 128)))


def rope_tables(seq):
    half = HEAD_DIM // 2
    t = jnp.arange(seq, dtype=jnp.int32)
    row = (t // GRID_W).astype(F32)
    col = (t % GRID_W).astype(F32)
    inv_freq = ROPE_THETA ** (-jnp.arange(0, half, 2, dtype=F32) / half)
    ang = jnp.concatenate([row[:, None] * inv_freq[None, :], col[:, None] * inv_freq[None, :]], axis=-1)
    cos = jnp.concatenate([jnp.cos(ang), jnp.cos(ang)], axis=-1)
    sin = jnp.concatenate([-jnp.sin(ang), jnp.sin(ang)], axis=-1)
    return cos, sin


ATTN_TQ = 128
ATTN_TKC = 8192


def _attn_kernel(q0_ref, qa_ref, qb_ref, k_ref, v_ref, o_ref, s_scr, mrun_scr, mcur_scr, acc_scr, *, tkc):
    t = pl.program_id(2)
    nt = pl.num_programs(2)
    tq = qa_ref.shape[1]
    nch = k_ref.shape[1] // tkc
    nsl = tkc // LANES
    ones = jnp.ones((tkc, LANES), BF16)

    def stack(q_ref):
        return jnp.concatenate([q_ref[0, :, g * HEAD_DIM:(g + 1) * HEAD_DIM] for g in range(GQA_GROUPS)], axis=0)

    def pass1(qs, c, slot):
        off = pl.multiple_of(c * tkc, tkc)
        s = lax.dot_general(qs, k_ref[0, pl.ds(off, tkc), :], (((1,), (1,)), ((), ())),
                            preferred_element_type=F32)
        s_scr[slot, c] = s
        m = s[:, :LANES]
        for i in range(1, nsl):
            m = jnp.maximum(m, s[:, i * LANES:(i + 1) * LANES])
        mrun_scr[...] = jnp.maximum(mrun_scr[...], m)

    def pass2(c, slot):
        off = pl.multiple_of(c * tkc, tkc)
        vext = jnp.concatenate([v_ref[0, pl.ds(off, tkc), :], ones], axis=1)
        m = mcur_scr[...]
        p = jnp.concatenate([jnp.exp2(s_scr[slot, c, :, i * LANES:(i + 1) * LANES] - m) for i in range(nsl)],
                            axis=1)
        acc_scr[...] += jnp.dot(p.astype(BF16), vext, preferred_element_type=F32)

    def begin_pass1():
        mrun_scr[...] = jnp.full(mrun_scr.shape, -jnp.inf, F32)

    def end_pass1():
        mcur_scr[...] = jnp.broadcast_to(jnp.max(mrun_scr[...], axis=1, keepdims=True), mcur_scr.shape)
        acc_scr[...] = jnp.zeros(acc_scr.shape, F32)

    def emit(half):
        o = acc_scr[:, :HEAD_DIM] / acc_scr[:, HEAD_DIM:]
        for g in range(GQA_GROUPS):
            o_ref[0, half * tq:(half + 1) * tq, g * HEAD_DIM:(g + 1) * HEAD_DIM] = (
                o[g * tq:(g + 1) * tq].astype(o_ref.dtype))

    @pl.when(t == 0)
    def _():
        qs0 = stack(q0_ref)
        begin_pass1()

        def body_0(c, carry):
            pass1(qs0, c, 0)
            return carry

        lax.fori_loop(0, nch, body_0, 0)
        end_pass1()

    qs = stack(qa_ref)
    begin_pass1()

    def body_x(c, carry):
        pass1(qs, c, 1)
        pass2(c, 0)
        return carry

    lax.fori_loop(0, nch, body_x, 0)
    emit(0)
    end_pass1()

    @pl.when(t < nt - 1)
    def _():
        qs2 = stack(qb_ref)
        begin_pass1()

        def body_y(c, carry):
            pass1(qs2, c, 0)
            pass2(c, 1)
            return carry

        lax.fori_loop(0, nch, body_y, 0)
        emit(1)
        end_pass1()

    @pl.when(t == nt - 1)
    def _():
        def body_z(c, carry):
            pass2(c, 1)
            return carry

        lax.fori_loop(0, nch, body_z, 0)
        emit(1)


def gqa_attention(q, k, v, tq=ATTN_TQ, tkc=ATTN_TKC):
    b, s, _ = q.shape
    tkc = min(tkc, s)
    nq = s // tq
    assert nq % 2 == 0 and s % tkc == 0
    gw = GQA_GROUPS * HEAD_DIM
    rows = GQA_GROUPS * tq
    qblk = (1, tq, gw)
    kvblk = (1, s, HEAD_DIM)
    return pl.pallas_call(
        functools.partial(_attn_kernel, tkc=tkc),
        grid=(b, N_KV_HEADS, nq // 2),
        in_specs=[pl.BlockSpec(qblk, lambda bi, h, t: (bi, 0, h)),
                  pl.BlockSpec(qblk, lambda bi, h, t: (bi, 2 * t + 1, h)),
                  pl.BlockSpec(qblk, lambda bi, h, t: (bi, jnp.minimum(2 * t + 2, nq - 1), h)),
                  pl.BlockSpec(kvblk, lambda bi, h, t: (bi, 0, h)),
                  pl.BlockSpec(kvblk, lambda bi, h, t: (bi, 0, h))],
        out_specs=pl.BlockSpec((1, 2 * tq, gw), lambda bi, h, t: (bi, t, h)),
        out_shape=jax.ShapeDtypeStruct(q.shape, BF16),
        scratch_shapes=[pltpu.VMEM((2, s // tkc, rows, tkc), F32), pltpu.VMEM((rows, LANES), F32),
                        pltpu.VMEM((rows, LANES), F32), pltpu.VMEM((rows, 2 * HEAD_DIM), F32)],
        compiler_params=_cparams(("parallel", "parallel", "arbitrary")),
        name="gqa_attention",
    )(q, q, q, k, v)


HGRN_CHUNK = 128
HGRN_LAG = 2


def _group_row(x, group, row):
    c, n = x.shape
    xr = x.reshape(c // group, group, n)
    return jnp.broadcast_to(xr[:, row:row + 1, :], xr.shape).reshape(c, n)


def _boundary_row(x, m, rev, odd_block):
    ref_row = m if rev else m - 1
    if 2 * m >= SUBLANES:
        return _group_row(x, 2 * m, ref_row)
    if m == 1:
        c, n = x.shape
        x3 = x.reshape(c // SUBLANES, SUBLANES, n)
        other = pltpu.roll(x3, SUBLANES - 1 if rev else 1, 1).reshape(c, n)
        return jnp.where(odd_block(1), x, other) if rev else jnp.where(odd_block(1), other, x)
    assert 4 * m == SUBLANES
    return jnp.where(odd_block(2 * m), _group_row(x, SUBLANES, 2 * m + ref_row), _group_row(x, SUBLANES, ref_row))


def hgrn_constants():
    import numpy as np
    c = HGRN_CHUNK
    t = np.arange(c)[:, None]
    s = np.arange(c)[None, :]
    pair, tri = [], []
    for rev in (False, True):
        masks = [t == s]
        m = 1
        while m < c:
            same = (t // (2 * m)) == (s // (2 * m))
            t_query = ((t // m) % 2) == (0 if rev else 1)
            s_key = ((s // m) % 2) == (1 if rev else 0)
            masks.append(same & t_query & s_key)
            m *= 2
        pair.append(np.stack(masks))
        tri.append((s >= t) if rev else (s <= t))
    rows = np.stack([np.broadcast_to(((t // n) % 2) == 1, (c, LANES)) for n in (1, 2, 4)])
    return (jnp.asarray(np.stack(pair), jnp.int32), jnp.asarray(rows, jnp.int32),
            jnp.asarray(np.stack(tri), BF16))


def _neg_abs(x):
    return pltpu.bitcast(pltpu.bitcast(x, jnp.uint32) | jnp.uint32(0x80000000), F32)


def _nt(x, y):
    return lax.dot_general(x, y, (((1,), (1,)), ((), ())), preferred_element_type=F32)


def _hgrn_chunk(q_ref, v_ref, lf_ref, o_ref, st_scr, a_scr, p_scr, oi_scr, rows3, col, slot, consts, *, rev):
    pair_ref, rows_ref, tri_ref = consts
    r_prev, r0, r_next = rows3
    d = 1 if rev else 0
    odd_block = lambda n: rows_ref[{1: 0, 2: 1, 4: 2}[n]] != 0
    c = HGRN_CHUNK
    cols = slice(col * HEAD_DIM, (col + 1) * HEAD_DIM)

    pv = jnp.dot(p_scr[slot], v_ref[0, pl.ds(r_prev, c), cols], preferred_element_type=F32)
    a2 = _cumsum_issue(lf_ref[0, pl.ds(r_next, c), cols], tri_ref[d])

    qb = q_ref[0, pl.ds(r0, c), cols]
    vb = v_ref[0, pl.ds(r0, c), cols]
    k = 1.0 - jnp.exp2(lf_ref[0, pl.ds(r0, c), cols])
    q = qb.astype(F32)
    sc = jnp.sum(q * k, axis=1, keepdims=True)
    a = a_scr[slot]
    edge = 0 if rev else c - 1
    a_end = a[edge:edge + 1, :]

    st = st_scr[col]
    o = _nt((q * jnp.exp2(a)).astype(BF16), st.astype(BF16))
    kdec = (k * jnp.exp2(a_end - a)).astype(BF16)
    st_new = lax.dot_general(vb, kdec, (((0,), (0,)), ((), ())), preferred_element_type=F32)
    st_decayed = st * jnp.exp2(a_end)

    nstrip = c // SUBLANES
    strip = lambda x, i: x[i * SUBLANES:(i + 1) * SUBLANES]
    p = [None] * nstrip

    def apply(pending):
        level, scores, rows = pending
        for j, i in enumerate(rows):
            mask = pair_ref[d, level, i * SUBLANES:(i + 1) * SUBLANES, :] != 0
            p[i] = jnp.where(mask, strip(scores, j), 0.0 if p[i] is None else p[i])

    query_first = rev

    def halves(m):
        out = []
        for g in range(c // (2 * m)):
            first = slice(g * 2 * m, g * 2 * m + m)
            second = slice(g * 2 * m + m, (g + 1) * 2 * m)
            out.append((first, second) if query_first else (second, first))
        return out

    def factor(m):
        ref = _boundary_row(a, m, rev, odd_block)
        if m < SUBLANES:
            return jnp.exp2(_neg_abs(a - ref))
        parts = []
        for qs, ks in halves(m):
            dq, dk = a[qs] - ref[qs], ref[ks] - a[ks]
            parts += [dq, dk] if query_first else [dk, dq]
        return jnp.exp2(jnp.concatenate(parts, axis=0))

    pending = [(0, sc, list(range(nstrip)))]
    nlevel = pair_ref.shape[1] - 1
    m = 1
    e = factor(m)
    for level in range(1, nlevel + 1):
        e_next = factor(2 * m) if level < nlevel else None
        if m < SUBLANES:
            qk = jnp.where(odd_block(m), k, q) if query_first else jnp.where(odd_block(m), q, k)
            z = (qk * e).astype(BF16)
            sc = _nt(z, z)
            rows = list(range(nstrip))
        else:
            parts, qparts, rows = [], [], []
            for qs, ks in halves(m):
                zq, zk = q[qs] * e[qs], k[ks] * e[ks]
                parts += [zq, zk] if query_first else [zk, zq]
                qparts.append(zq)
                rows += list(range(qs.start // SUBLANES, qs.stop // SUBLANES))
            z = jnp.concatenate(parts, axis=0).astype(BF16)
            sc = _nt(jnp.concatenate(qparts, axis=0).astype(BF16), z)
        e = e_next
        yield
        if level == 2:
            o_prev = oi_scr[slot]
            oi_scr[slot] = o
            st_scr[col] = st_decayed + st_new
        pending.append((level, sc, rows))
        if len(pending) > HGRN_LAG:
            apply(pending.pop(0))
        m *= 2
    a_scr[slot] = a2[:, :LANES] + a2[:, LANES:]
    o_ref[0, pl.ds(r_prev, c), cols] = (o_prev + pv).astype(o_ref.dtype)
    yield
    for item in pending:
        apply(item)
    p_scr[slot] = jnp.concatenate(p, axis=0).astype(BF16)


def _cumsum_issue(lf, tri):
    hi = lf.astype(BF16)
    lo = (lf - hi.astype(F32)).astype(BF16)
    return jnp.dot(tri, jnp.concatenate([hi, lo], axis=1), preferred_element_type=F32)


def _interleave(chains):
    chains = list(chains)
    while chains:
        for ch in list(chains):
            try:
                next(ch)
            except StopIteration:
                chains.remove(ch)


def _hgrn_kernel(qf_ref, vf_ref, lff_ref, qb_ref, vb_ref, lfb_ref, pair_ref, rows_ref, tri_ref,
                 of_ref, ob_ref, sf_scr, sb_scr, a_scr, p_scr, oi_scr):
    @pl.when(pl.program_id(2) == 0)
    def _():
        sf_scr[...] = jnp.zeros(sf_scr.shape, F32)
        sb_scr[...] = jnp.zeros(sb_scr.shape, F32)

    c = HGRN_CHUNK
    consts = (pair_ref, rows_ref, tri_ref)
    n = qf_ref.shape[1] // c
    heads = qf_ref.shape[2] // HEAD_DIM
    row = lambda idx: pl.multiple_of(idx * c, c)
    dirs = ((qf_ref, vf_ref, lff_ref, of_ref, sf_scr, False), (qb_ref, vb_ref, lfb_ref, ob_ref, sb_scr, True))
    first = lambda rev: n - 1 if rev else 0
    last = lambda rev: 0 if rev else n - 1

    for hd in range(heads):
        cols = slice(hd * HEAD_DIM, (hd + 1) * HEAD_DIM)
        for d, (_, _, lf_ref, _, _, rev) in enumerate(dirs):
            a2 = _cumsum_issue(lf_ref[0, pl.ds(first(rev) * c, c), cols], tri_ref[d])
            a_scr[2 * hd + d] = a2[:, :LANES] + a2[:, LANES:]
    p_scr[...] = jnp.zeros(p_scr.shape, BF16)
    oi_scr[...] = jnp.zeros(oi_scr.shape, F32)

    def body(ci, carry):
        idx = {False: (jnp.maximum(ci - 1, 0), ci, jnp.minimum(ci + 1, n - 1)),
               True: (jnp.minimum(n - ci, n - 1), n - 1 - ci, jnp.maximum(n - 2 - ci, 0))}
        chains = []
        for hd in range(heads):
            for d, (q_ref, v_ref, lf_ref, o_ref, st_scr, rev) in enumerate(dirs):
                rows3 = tuple(row(i) for i in idx[rev])
                chains.append(_hgrn_chunk(q_ref, v_ref, lf_ref, o_ref, st_scr, a_scr, p_scr, oi_scr, rows3, hd,
                                          2 * hd + d, consts, rev=rev))
        _interleave(chains)
        return carry

    lax.fori_loop(0, n, body, 0)

    for hd in range(heads):
        cols = slice(hd * HEAD_DIM, (hd + 1) * HEAD_DIM)
        for d, (_, v_ref, _, o_ref, _, rev) in enumerate(dirs):
            rws = pl.ds(last(rev) * c, c)
            pv = jnp.dot(p_scr[2 * hd + d], v_ref[0, rws, cols], preferred_element_type=F32)
            o_ref[0, rws, cols] = (oi_scr[2 * hd + d] + pv).astype(o_ref.dtype)


HGRN_HEADS_PER_STEP = 4


def hgrn2_bidir(q, v, logf, ts=1024, hps=HGRN_HEADS_PER_STEP):
    b, s, hk = q.shape
    ts = min(ts, s)
    nt = s // ts
    ng = hk // (hps * HEAD_DIM)
    blk = (1, ts, hps * HEAD_DIM)
    fwd = lambda bi, h, i: (bi, i, h)
    bwd = lambda bi, h, i: (bi, nt - 1 - i, h)
    out = jax.ShapeDtypeStruct((b, s, hk), BF16)
    state = pltpu.VMEM((hps, HEAD_DIM, HEAD_DIM), F32)
    consts = hgrn_constants()
    whole = lambda x: pl.BlockSpec(x.shape, lambda bi, h, i: (0,) * x.ndim)
    return pl.pallas_call(
        _hgrn_kernel,
        grid=(b, ng, nt),
        in_specs=[pl.BlockSpec(blk, fwd), pl.BlockSpec(blk, fwd), pl.BlockSpec(blk, fwd),
                  pl.BlockSpec(blk, bwd), pl.BlockSpec(blk, bwd),
                  pl.BlockSpec(blk, lambda bi, h, i: (bi, nt - 1 - i, ng + h))] + [whole(x) for x in consts],
        out_specs=[pl.BlockSpec(blk, fwd), pl.BlockSpec(blk, bwd)],
        out_shape=[out, out],
        scratch_shapes=[state, state, pltpu.VMEM((2 * hps, HGRN_CHUNK, HEAD_DIM), F32),
                        pltpu.VMEM((2 * hps, HGRN_CHUNK, HGRN_CHUNK), BF16),
                        pltpu.VMEM((2 * hps, HGRN_CHUNK, HEAD_DIM), F32)],
        compiler_params=_cparams(("parallel", "parallel", "arbitrary")),
        name="hgrn2",
    )(q, v, logf, q, v, logf, *consts)


def _hgrn_post_kernel(of_ref, ob_ref, gr_ref, gn_ref, o_ref):
    gn = gn_ref[...]
    for hd in range(o_ref.shape[1] // HEAD_DIM):
        sl = slice(hd * HEAD_DIM, (hd + 1) * HEAD_DIM)
        o = _rms(of_ref[:, sl].astype(F32) + ob_ref[:, sl].astype(F32), gn)
        o_ref[:, sl] = (o * gr_ref[:, sl].astype(F32)).astype(o_ref.dtype)


def hgrn_post(o_fw, o_bw, g_silu, g_norm, tm=512):
    m, d = o_fw.shape
    tm = min(tm, m)
    row = pl.BlockSpec((tm, d), lambda i: (i, 0))
    return pl.pallas_call(
        _hgrn_post_kernel,
        grid=(m // tm,),
        in_specs=[row, row, row, pl.BlockSpec((1, HEAD_DIM), lambda i: (0, 0))],
        out_specs=row,
        out_shape=jax.ShapeDtypeStruct((m, d), BF16),
        compiler_params=_cparams(("parallel",)),
        name="hgrn_post",
    )(o_fw, o_bw, g_silu, g_norm.reshape(1, HEAD_DIM))


def _merge_kernel(att_ref, or_ref, woa_ref, woh_ref, sa_ref, sh_ref, o_ref):
    tm = o_ref.shape[0]
    rows_per = min(MM_ROWS, tm)
    woa, woh = woa_ref[...].astype(BF16), woh_ref[...].astype(BF16)
    for r in range(0, tm, rows_per):
        rs = slice(r, r + rows_per)
        ya = jnp.dot(att_ref[rs, :], woa, preferred_element_type=F32)
        yh = jnp.dot(or_ref[rs, :], woh, preferred_element_type=F32)
        o_ref[rs, :] = (sa_ref[rs, :].astype(F32) * ya + sh_ref[rs, :].astype(F32) * yh).astype(o_ref.dtype)


def gated_merge(att, o_r, w_oa, w_oh, gates, tm=1024, tn=512):
    m, d = att.shape
    tm = min(tm, m)
    nj = d // tn
    row = pl.BlockSpec((tm, d), lambda i, j: (i, 0))
    wsp = pl.BlockSpec((d, tn), lambda i, j: (0, j))
    return pl.pallas_call(
        _merge_kernel,
        grid=(m // tm, nj),
        in_specs=[row, row, wsp, wsp,
                  pl.BlockSpec((tm, tn), lambda i, j: (i, j)),
                  pl.BlockSpec((tm, tn), lambda i, j: (i, nj + j))],
        out_specs=pl.BlockSpec((tm, tn), lambda i, j: (i, j)),
        out_shape=jax.ShapeDtypeStruct((m, d), BF16),
        compiler_params=_cparams(("parallel", "arbitrary")),
        name="gated_merge",
    )(att, o_r, w_oa, w_oh, gates, gates)


def _resid_proj_kernel(x_ref, a_ref, w_ref, o_ref):
    tm = o_ref.shape[0]
    rows_per = min(MM_ROWS, tm)
    w = w_ref[...].astype(BF16)
    for r in range(0, tm, rows_per):
        rs = slice(r, r + rows_per)
        o_ref[rs, :] = x_ref[rs, :] + jnp.dot(a_ref[rs, :], w, preferred_element_type=F32)


def resid_proj(x, a, w, tm=1024, tn=512):
    m, d = x.shape
    tm = min(tm, m)
    return pl.pallas_call(
        _resid_proj_kernel,
        grid=(m // tm, d // tn),
        in_specs=[pl.BlockSpec((tm, tn), lambda i, j: (i, j)), pl.BlockSpec((tm, a.shape[1]), lambda i, j: (i, 0)),
                  pl.BlockSpec((a.shape[1], tn), lambda i, j: (0, j))],
        out_specs=pl.BlockSpec((tm, tn), lambda i, j: (i, j)),
        out_shape=jax.ShapeDtypeStruct((m, d), F32),
        compiler_params=_cparams(("parallel", "arbitrary")),
        name="out_proj",
    )(x, a, w)


def _mlp_kernel(x_ref, g_ref, wu_ref, wd_ref, o_ref, h_scr):
    @pl.when(pl.program_id(1) == 0)
    def _():
        x = x_ref[...]
        h_scr[...] = _rms(x, g_ref[...]).astype(h_scr.dtype)
        o_ref[...] = x

    tm = o_ref.shape[0]
    rows_per = min(MM_ROWS, tm)
    wu, wd = wu_ref[...].astype(BF16), wd_ref[...].astype(BF16)
    for r in range(0, tm, rows_per):
        rs = slice(r, r + rows_per)
        u = jnp.maximum(jnp.dot(h_scr[rs, :], wu, preferred_element_type=F32), 0.0)
        o_ref[rs, :] += jnp.dot((u * u).astype(BF16), wd, preferred_element_type=F32)


def mlp_block(x, gain, w_up, w_down, tm=1024, tf=512):
    m, d = x.shape
    ff = w_up.shape[1]
    tm = min(tm, m)
    return pl.pallas_call(
        _mlp_kernel,
        grid=(m // tm, ff // tf),
        in_specs=[pl.BlockSpec((tm, d), lambda i, f: (i, 0)), pl.BlockSpec((1, d), lambda i, f: (0, 0)),
                  pl.BlockSpec((d, tf), lambda i, f: (0, f)), pl.BlockSpec((tf, d), lambda i, f: (f, 0))],
        out_specs=pl.BlockSpec((tm, d), lambda i, f: (i, 0)),
        out_shape=jax.ShapeDtypeStruct((m, d), F32),
        scratch_shapes=[pltpu.VMEM((tm, d), BF16)],
        compiler_params=_cparams(("parallel", "arbitrary")),
        name="mlp",
    )(x, gain.reshape(1, d), w_up, w_down)


PLE_ROWS = 512


def _ple_kernel(x_ref, g_ref, wg_ref, p_ref, wp_ref, gf_ref, o_ref):
    tm = o_ref.shape[0]
    units = [slice(r, r + min(PLE_ROWS, tm)) for r in range(0, tm, min(PLE_ROWS, tm))]

    def issue(rs):
        x = x_ref[rs, :]
        h = _rms(x, g_ref[...]).astype(BF16)
        logit = jnp.dot(h, wg_ref[...], preferred_element_type=F32)
        emb = jnp.dot(p_ref[rs, :].astype(BF16), wp_ref[...], preferred_element_type=F32)
        return x, logit, emb

    def finish(rs, x, logit, emb):
        o_ref[rs, :] = _rms(x + _sigmoid(logit) * emb, gf_ref[...])

    prev = issue(units[0])
    for u in range(1, len(units)):
        cur = issue(units[u])
        finish(units[u - 1], *prev)
        prev = cur
    finish(units[-1], *prev)


def ple_final(x, gain, w_gate, p, w_p, g_final, tm=512):
    m, d = x.shape
    c = p.shape[1]
    tm = min(tm, m)
    const = lambda i: (0, 0)
    return pl.pallas_call(
        _ple_kernel,
        grid=(m // tm,),
        in_specs=[pl.BlockSpec((tm, d), lambda i: (i, 0)), pl.BlockSpec((1, d), const),
                  pl.BlockSpec((d, d), const), pl.BlockSpec((tm, c), lambda i: (i, 0)),
                  pl.BlockSpec((c, d), const), pl.BlockSpec((1, d), const)],
        out_specs=pl.BlockSpec((tm, d), lambda i: (i, 0)),
        out_shape=jax.ShapeDtypeStruct((m, d), F32),
        compiler_params=_cparams(("parallel",)),
        name="ple_final",
    )(x, gain.reshape(1, d), w_gate, p, w_p, g_final.reshape(1, d))


def kernel(x, p, g_mix, w_in, g_q, g_k, w_o_attn, hgrn_lb, g_hgrn, w_o_hgrn, w_out, g_mlp, w_up, w_down,
           g_ple, w_ple_gate, w_ple, g_final):
    b, s, d = x.shape
    m = b * s
    depth = w_in.shape[0]
    attn_q = N_Q_HEADS * HEAD_DIM
    attn_kv = N_KV_HEADS * HEAD_DIM
    hk = HGRN_HEADS * HEAD_DIM
    rope = rope_tables(s)
    lb_all = jnp.cumsum(jax.nn.softmax(hgrn_lb.astype(F32), axis=0), axis=0)

    xf = x.reshape(m, d)
    for i in range(depth):
        w = w_in[i]
        h = rmsnorm_bf16(xf, g_mix[i])
        c0 = 0
        q_a = proj_qk(h, w, c0, attn_q, g_q[i], rope, s, HEAD_DIM ** -0.5 * LOG2_E); c0 += attn_q
        k_a = proj_qk(h, w, c0, attn_kv, g_k[i], rope, s, 1.0); c0 += attn_kv
        v_a = proj_plain(h, w, c0, attn_kv, None); c0 += attn_kv
        q_r = proj_plain(h, w, c0, hk, "silu"); c0 += hk
        logf = proj_logf(h, w, c0, 2 * hk, lb_all[i]); c0 += 2 * hk
        i_r = proj_plain(h, w, c0, hk, None); c0 += hk
        g_r = proj_plain(h, w, c0, hk, "silu"); c0 += hk
        gates = proj_plain(h, w, c0, 2 * d, "sigmoid"); c0 += 2 * d

        att = gqa_attention(q_a.reshape(b, s, attn_q), k_a.reshape(b, s, attn_kv), v_a.reshape(b, s, attn_kv))
        o_fw, o_bw = hgrn2_bidir(q_r.reshape(b, s, hk), i_r.reshape(b, s, hk), logf.reshape(b, s, 2 * hk))
        o_r = hgrn_post(o_fw.reshape(m, hk), o_bw.reshape(m, hk), g_r, g_hgrn[i])
        mixed = gated_merge(att.reshape(m, attn_q), o_r, w_o_attn[i].astype(BF16), w_o_hgrn[i].astype(BF16), gates)
        xf = resid_proj(xf, mixed, w_out[i].astype(BF16))
        xf = mlp_block(xf, g_mlp[i], w_up[i].astype(BF16), w_down[i])
        assert depth == 1
        xf = ple_final(xf, g_ple[i], w_ple_gate[i].astype(BF16), p[i].reshape(m, -1), w_ple[i].astype(BF16),
                       g_final)
    return xf.reshape(b, s, d)
```

```python
import functools

import jax
import jax.numpy as jnp
from jax import lax
from jax.experimental import pallas as pl
from jax.experimental.pallas import tpu as pltpu

F32 = jnp.float32
BF16 = jnp.bfloat16

EPS = 1e-6
LOG2_E = 1.4426950408889634
HEAD_DIM = 128
N_Q_HEADS = 16
N_KV_HEADS = 4
GQA_GROUPS = N_Q_HEADS // N_KV_HEADS
GRID_W = 64
ROPE_THETA = 10000.0
HGRN_HEADS = 16
LANES = 128
SUBLANES = 8
VMEM_LIMIT = 56 * 1024 * 1024


def _cparams(sem):
    return pltpu.CompilerParams(dimension_semantics=sem, vmem_limit_bytes=VMEM_LIMIT)


def _sigmoid(x):
    return 1.0 / (1.0 + jnp.exp(-x))


def _rms(x, gain):
    ms = jnp.mean(x * x, axis=-1, keepdims=True)
    return x * lax.rsqrt(ms + EPS) * gain


def _rmsnorm_kernel(x_ref, g_ref, o_ref):
    o_ref[...] = _rms(x_ref[...], g_ref[...]).astype(o_ref.dtype)


def rmsnorm_bf16(x, gain, tm=512):
    m, d = x.shape
    tm = min(tm, m)
    return pl.pallas_call(
        _rmsnorm_kernel,
        grid=(m // tm,),
        in_specs=[pl.BlockSpec((tm, d), lambda i: (i, 0)), pl.BlockSpec((1, d), lambda i: (0, 0))],
        out_specs=pl.BlockSpec((tm, d), lambda i: (i, 0)),
        out_shape=jax.ShapeDtypeStruct((m, d), BF16),
        compiler_params=_cparams(("parallel",)),
        name="rmsnorm",
    )(x, gain.reshape(1, d))


QK_SUB = 2 * HEAD_DIM


MM_ROWS = 512


def _subtiled_matmul(h_ref, w_ref, epilogue):
    tm = h_ref.shape[0]
    rows_per = min(MM_ROWS, tm)
    units = [(slice(r, r + rows_per), slice(c0, c0 + QK_SUB))
             for c0 in range(0, w_ref.shape[1], QK_SUB) for r in range(0, tm, rows_per)]
    weights = {}

    def matmul(u):
        rs, cs = u
        if cs.start not in weights:
            weights.clear()
            weights[cs.start] = w_ref[:, cs].astype(BF16)
        return jnp.dot(h_ref[rs, :], weights[cs.start], preferred_element_type=F32)

    prev = matmul(units[0])
    for j in range(1, len(units)):
        cur = matmul(units[j])
        epilogue(prev, *units[j - 1])
        prev = cur
    epilogue(prev, *units[-1])


def _stationary_weights(w_ref, wb_scr, rope_order=False):
    @pl.when(pl.program_id(1) == 0)
    def _():
        w = w_ref[...]
        if rope_order:
            tn = w.shape[1]
            quarter = (lax.broadcasted_iota(jnp.int32, w.shape, 1) // (HEAD_DIM // 4)) % 4
            w = jnp.where(quarter == 1, pltpu.roll(w, tn - HEAD_DIM // 4, 1),
                          jnp.where(quarter == 2, pltpu.roll(w, HEAD_DIM // 4, 1), w))
        wb_scr[...] = w.astype(wb_scr.dtype)


def _proj_plain_kernel(h_ref, w_ref, o_ref, wb_scr, *, act):
    def epilogue(acc, rs, cs):
        if act == "silu":
            acc = acc * _sigmoid(acc)
        elif act == "sigmoid":
            acc = _sigmoid(acc)
        o_ref[rs, cs] = acc.astype(o_ref.dtype)

    _stationary_weights(w_ref, wb_scr)
    _subtiled_matmul(h_ref, wb_scr, epilogue)


def _proj_logf_kernel(h_ref, w_ref, lb_ref, o_ref, wb_scr):
    def epilogue(acc, rs, cs):
        lb = lb_ref[:, cs]
        o_ref[rs, cs] = jnp.log(lb + (1.0 - lb) * _sigmoid(acc)) * LOG2_E

    _stationary_weights(w_ref, wb_scr)
    _subtiled_matmul(h_ref, wb_scr, epilogue)


def _proj_qk_kernel(h_ref, w_ref, cg_ref, sg_ref, o_ref, wb_scr):
    ones = jnp.ones((HEAD_DIM, HEAD_DIM), BF16)
    _stationary_weights(w_ref, wb_scr, rope_order=True)

    def epilogue(acc, rs, cs):
        cg, sg = cg_ref[rs, :], sg_ref[rs, :]
        for hd in range(QK_SUB // HEAD_DIM):
            x = acc[:, hd * HEAD_DIM:(hd + 1) * HEAD_DIM]
            ssq = jnp.dot((x * x).astype(BF16), ones, preferred_element_type=F32)
            r = lax.rsqrt(ssq * (1.0 / HEAD_DIM) + EPS)
            y = r * (x * cg + pltpu.roll(x, HEAD_DIM // 2, 1) * sg)
            c0 = cs.start + hd * HEAD_DIM
            o_ref[rs, c0:c0 + HEAD_DIM] = y.astype(o_ref.dtype)

    _subtiled_matmul(h_ref, wb_scr, epilogue)


def _proj_call(kernel, h, w, col0, ncols, out_dtype, extra=(), extra_specs=(), tm=1024, tn=1024):
    m, d = h.shape
    tm = min(tm, m)
    tn = min(tn, ncols)
    assert col0 % tn == 0 and ncols % tn == 0 and m % tm == 0
    jb = col0 // tn
    return pl.pallas_call(
        kernel,
        grid=(ncols // tn, m // tm),
        in_specs=[pl.BlockSpec((tm, d), lambda j, i: (i, 0)),
                  pl.BlockSpec((d, tn), lambda j, i: (0, jb + j))] + list(extra_specs),
        out_specs=pl.BlockSpec((tm, tn), lambda j, i: (i, j)),
        out_shape=jax.ShapeDtypeStruct((m, ncols), out_dtype),
        scratch_shapes=[pltpu.VMEM((d, tn), BF16)],
        compiler_params=_cparams(("parallel", "arbitrary")),
        name="in_proj",
    )(h, w, *extra)


def proj_plain(h, w, col0, ncols, act, out_dtype=BF16):
    return _proj_call(functools.partial(_proj_plain_kernel, act=act), h, w, col0, ncols, out_dtype)


def proj_logf(h, w, col0, ncols, lb, tn=1024):
    return _proj_call(_proj_logf_kernel, h, w, col0, ncols, F32, extra=(lb.reshape(1, ncols),),
                      extra_specs=(pl.BlockSpec((1, tn), lambda j, i: (0, j)),), tn=tn)


def proj_qk(h, w, col0, ncols, gain, rope, seq, scale, tm=1024, tn=1024):
    cos, sin = rope
    gp = gain.astype(F32)[jnp.array(ROPE_PERM)]
    cg = cos * (gp * scale)[None, :]
    sg = sin * (jnp.roll(gp, HEAD_DIM // 2) * scale)[None, :]
    tm = min(tm, seq)
    tn = min(tn, ncols)
    nsb = seq // tm
    tab = pl.BlockSpec((tm, HEAD_DIM), lambda j, i: (i % nsb, 0))
    return _proj_call(_proj_qk_kernel, h, w, col0, ncols, BF16, extra=(cg, sg), extra_specs=(tab, tab),
                      tm=tm, tn=tn)


ROPE_PERM = tuple(list(range(0, 32)) + list(range(64, 96)) + list(range(32, 64)) + list(range(96, 128)))


def rope_tables(seq):
    half = HEAD_DIM // 2
    t = jnp.arange(seq, dtype=jnp.int32)
    row = (t // GRID_W).astype(F32)
    col = (t % GRID_W).astype(F32)
    inv_freq = ROPE_THETA ** (-jnp.arange(0, half, 2, dtype=F32) / half)
    ang = jnp.concatenate([row[:, None] * inv_freq[None, :], col[:, None] * inv_freq[None, :]], axis=-1)
    cos = jnp.concatenate([jnp.cos(ang), jnp.cos(ang)], axis=-1)
    sin = jnp.concatenate([-jnp.sin(ang), jnp.sin(ang)], axis=-1)
    return cos, sin


ATTN_TQ = 128
ATTN_TKC = 8192


def _attn_kernel(q0_ref, qa_ref, qb_ref, k_ref, v_ref, o_ref, s_scr, mrun_scr, mcur_scr, acc_scr, *, tkc):
    t = pl.program_id(2)
    nt = pl.num_programs(2)
    tq = qa_ref.shape[1]
    nch = k_ref.shape[1] // tkc
    nsl = tkc // LANES
    ones = jnp.ones((tkc, LANES), BF16)

    def stack(q_ref):
        return jnp.concatenate([q_ref[0, :, g * HEAD_DIM:(g + 1) * HEAD_DIM] for g in range(GQA_GROUPS)], axis=0)

    def pass1(qs, c, slot):
        off = pl.multiple_of(c * tkc, tkc)
        s = lax.dot_general(qs, k_ref[0, pl.ds(off, tkc), :], (((1,), (1,)), ((), ())),
                            preferred_element_type=F32)
        s_scr[slot, c] = s
        m = s[:, :LANES]
        for i in range(1, nsl):
            m = jnp.maximum(m, s[:, i * LANES:(i + 1) * LANES])
        mrun_scr[...] = jnp.maximum(mrun_scr[...], m)

    def pass2(c, slot):
        off = pl.multiple_of(c * tkc, tkc)
        vext = jnp.concatenate([v_ref[0, pl.ds(off, tkc), :], ones], axis=1)
        m = mcur_scr[...]
        p = jnp.concatenate([jnp.exp2(s_scr[slot, c, :, i * LANES:(i + 1) * LANES] - m) for i in range(nsl)],
                            axis=1)
        acc_scr[...] += jnp.dot(p.astype(BF16), vext, preferred_element_type=F32)

    def begin_pass1():
        mrun_scr[...] = jnp.full(mrun_scr.shape, -jnp.inf, F32)

    def end_pass1():
        mcur_scr[...] = jnp.broadcast_to(jnp.max(mrun_scr[...], axis=1, keepdims=True), mcur_scr.shape)
        acc_scr[...] = jnp.zeros(acc_scr.shape, F32)

    def emit(half):
        o = acc_scr[:, :HEAD_DIM] / acc_scr[:, HEAD_DIM:]
        for g in range(GQA_GROUPS):
            o_ref[0, half * tq:(half + 1) * tq, g * HEAD_DIM:(g + 1) * HEAD_DIM] = (
                o[g * tq:(g + 1) * tq].astype(o_ref.dtype))

    @pl.when(t == 0)
    def _():
        qs0 = stack(q0_ref)
        begin_pass1()

        def body_0(c, carry):
            pass1(qs0, c, 0)
            return carry

        lax.fori_loop(0, nch, body_0, 0)
        end_pass1()

    qs = stack(qa_ref)
    begin_pass1()

    def body_x(c, carry):
        pass1(qs, c, 1)
        pass2(c, 0)
        return carry

    lax.fori_loop(0, nch, body_x, 0)
    emit(0)
    end_pass1()

    @pl.when(t < nt - 1)
    def _():
        qs2 = stack(qb_ref)
        begin_pass1()

        def body_y(c, carry):
            pass1(qs2, c, 0)
            pass2(c, 1)
            return carry

        lax.fori_loop(0, nch, body_y, 0)
        emit(1)
        end_pass1()

    @pl.when(t == nt - 1)
    def _():
        def body_z(c, carry):
            pass2(c, 1)
            return carry

        lax.fori_loop(0, nch, body_z, 0)
        emit(1)


def gqa_attention(q, k, v, tq=ATTN_TQ, tkc=ATTN_TKC):
    b, s, _ = q.shape
    tkc = min(tkc, s)
    nq = s // tq
    assert nq % 2 == 0 and s % tkc == 0
    gw = GQA_GROUPS * HEAD_DIM
    rows = GQA_GROUPS * tq
    qblk = (1, tq, gw)
    kvblk = (1, s, HEAD_DIM)
    return pl.pallas_call(
        functools.partial(_attn_kernel, tkc=tkc),
        grid=(b, N_KV_HEADS, nq // 2),
        in_specs=[pl.BlockSpec(qblk, lambda bi, h, t: (bi, 0, h)),
                  pl.BlockSpec(qblk, lambda bi, h, t: (bi, 2 * t + 1, h)),
                  pl.BlockSpec(qblk, lambda bi, h, t: (bi, jnp.minimum(2 * t + 2, nq - 1), h)),
                  pl.BlockSpec(kvblk, lambda bi, h, t: (bi, 0, h)),
                  pl.BlockSpec(kvblk, lambda bi, h, t: (bi, 0, h))],
        out_specs=pl.BlockSpec((1, 2 * tq, gw), lambda bi, h, t: (bi, t, h)),
        out_shape=jax.ShapeDtypeStruct(q.shape, BF16),
        scratch_shapes=[pltpu.VMEM((2, s // tkc, rows, tkc), F32), pltpu.VMEM((rows, LANES), F32),
                        pltpu.VMEM((rows, LANES), F32), pltpu.VMEM((rows, 2 * HEAD_DIM), F32)],
        compiler_params=_cparams(("parallel", "parallel", "arbitrary")),
        name="gqa_attention",
    )(q, q, q, k, v)


HGRN_CHUNK = 128
HGRN_LAG = 2


def _group_row(x, group, row):
    c, n = x.shape
    xr = x.reshape(c // group, group, n)
    return jnp.broadcast_to(xr[:, row:row + 1, :], xr.shape).reshape(c, n)


def _boundary_row(x, m, rev, odd_block):
    ref_row = m if rev else m - 1
    if 2 * m >= SUBLANES:
        return _group_row(x, 2 * m, ref_row)
    if m == 1:
        c, n = x.shape
        x3 = x.reshape(c // SUBLANES, SUBLANES, n)
        other = pltpu.roll(x3, SUBLANES - 1 if rev else 1, 1).reshape(c, n)
        return jnp.where(odd_block(1), x, other) if rev else jnp.where(odd_block(1), other, x)
    assert 4 * m == SUBLANES
    return jnp.where(odd_block(2 * m), _group_row(x, SUBLANES, 2 * m + ref_row), _group_row(x, SUBLANES, ref_row))


def hgrn_constants():
    import numpy as np
    c = HGRN_CHUNK
    t = np.arange(c)[:, None]
    s = np.arange(c)[None, :]
    pair, tri = [], []
    for rev in (False, True):
        masks = [t == s]
        m = 1
        while m < c:
            same = (t // (2 * m)) == (s // (2 * m))
            t_query = ((t // m) % 2) == (0 if rev else 1)
            s_key = ((s // m) % 2) == (1 if rev else 0)
            masks.append(same & t_query & s_key)
            m *= 2
        pair.append(np.stack(masks))
        tri.append((s >= t) if rev else (s <= t))
    rows = np.stack([np.broadcast_to(((t // n) % 2) == 1, (c, LANES)) for n in (1, 2, 4)])
    return (jnp.asarray(np.stack(pair), jnp.int32), jnp.asarray(rows, jnp.int32),
            jnp.asarray(np.stack(tri), BF16))


def _neg_abs(x):
    return pltpu.bitcast(pltpu.bitcast(x, jnp.uint32) | jnp.uint32(0x80000000), F32)


def _nt(x, y):
    return lax.dot_general(x, y, (((1,), (1,)), ((), ())), preferred_element_type=F32)


def _hgrn_chunk(q_ref, v_ref, lf_ref, o_ref, st_scr, a_scr, p_scr, oi_scr, rows3, col, slot, consts, *, rev):
    pair_ref, rows_ref, tri_ref = consts
    r_prev, r0, r_next = rows3
    d = 1 if rev else 0
    odd_block = lambda n: rows_ref[{1: 0, 2: 1, 4: 2}[n]] != 0
    c = HGRN_CHUNK
    cols = slice(col * HEAD_DIM, (col + 1) * HEAD_DIM)

    pv = jnp.dot(p_scr[slot], v_ref[0, pl.ds(r_prev, c), cols], preferred_element_type=F32)
    a2 = _cumsum_issue(lf_ref[0, pl.ds(r_next, c), cols], tri_ref[d])

    qb = q_ref[0, pl.ds(r0, c), cols]
    vb = v_ref[0, pl.ds(r0, c), cols]
    f = jnp.exp2(lf_ref[0, pl.ds(r0, c), cols])
    k = 1.0 - f
    q = qb.astype(F32)
    sc0 = jnp.sum(q * k, axis=1, keepdims=True)
    k3 = k.reshape(c // SUBLANES, SUBLANES, HEAD_DIM)
    k_other = pltpu.roll(k3, SUBLANES - 1 if rev else 1, 1).reshape(c, HEAD_DIM)
    sc1 = jnp.sum(q * f * k_other, axis=1, keepdims=True)
    a = a_scr[slot]
    edge = 0 if rev else c - 1
    a_end = a[edge:edge + 1, :]

    st = st_scr[col]
    o = jnp.dot((q * jnp.exp2(a)).astype(BF16), st.astype(BF16), preferred_element_type=F32)
    kdec = (k * jnp.exp2(a_end - a)).astype(BF16)
    st_new = lax.dot_general(kdec, vb, (((0,), (0,)), ((), ())), preferred_element_type=F32)
    st_decayed = st * jnp.transpose(jnp.broadcast_to(jnp.exp2(a_end), (HEAD_DIM, HEAD_DIM)))

    nstrip = c // SUBLANES
    strip = lambda x, i: x[i * SUBLANES:(i + 1) * SUBLANES]
    p = [None] * nstrip

    def apply(pending):
        level, scores, rows = pending
        for j, i in enumerate(rows):
            mask = pair_ref[d, level, i * SUBLANES:(i + 1) * SUBLANES, :] != 0
            p[i] = jnp.where(mask, strip(scores, j), 0.0 if p[i] is None else p[i])

    query_first = rev

    def halves(m):
        out = []
        for g in range(c // (2 * m)):
            first = slice(g * 2 * m, g * 2 * m + m)
            second = slice(g * 2 * m + m, (g + 1) * 2 * m)
            out.append((first, second) if query_first else (second, first))
        return out

    def factor(m):
        ref = _boundary_row(a, m, rev, odd_block)
        if m < SUBLANES:
            return jnp.exp2(_neg_abs(a - ref))
        parts = []
        for qs, ks in halves(m):
            dq, dk = a[qs] - ref[qs], ref[ks] - a[ks]
            parts += [dq, dk] if query_first else [dk, dq]
        return jnp.exp2(jnp.concatenate(parts, axis=0))

    every = list(range(nstrip))
    pending = [(0, sc0, every), (1, sc1, every)]
    nlevel = pair_ref.shape[1] - 1
    m = 2
    e = factor(m)
    for level in range(2, nlevel + 1):
        e_next = factor(2 * m) if level < nlevel else None
        if m < SUBLANES:
            qk = jnp.where(odd_block(m), k, q) if query_first else jnp.where(odd_block(m), q, k)
            z = (qk * e).astype(BF16)
            sc = _nt(z, z)
            rows = list(range(nstrip))
        else:
            parts, qparts, rows = [], [], []
            for qs, ks in halves(m):
                zq, zk = q[qs] * e[qs], k[ks] * e[ks]
                parts += [zq, zk] if query_first else [zk, zq]
                qparts.append(zq)
                rows += list(range(qs.start // SUBLANES, qs.stop // SUBLANES))
            z = jnp.concatenate(parts, axis=0).astype(BF16)
            sc = _nt(jnp.concatenate(qparts, axis=0).astype(BF16), z)
        e = e_next
        yield
        if level == 2:
            o_prev = oi_scr[slot]
            oi_scr[slot] = o
            st_scr[col] = st_decayed + st_new
        pending.append((level, sc, rows))
        if len(pending) > HGRN_LAG:
            apply(pending.pop(0))
        m *= 2
    a_scr[slot] = a2[:, :LANES] + a2[:, LANES:]
    o_ref[0, pl.ds(r_prev, c), cols] = (o_prev + pv).astype(o_ref.dtype)
    yield
    for item in pending:
        apply(item)
    p_scr[slot] = jnp.concatenate(p, axis=0).astype(BF16)


def _cumsum_issue(lf, tri):
    hi = lf.astype(BF16)
    lo = (lf - hi.astype(F32)).astype(BF16)
    return jnp.dot(tri, jnp.concatenate([hi, lo], axis=1), preferred_element_type=F32)


def _interleave(chains):
    chains = list(chains)
    while chains:
        for ch in list(chains):
            try:
                next(ch)
            except StopIteration:
                chains.remove(ch)


def _hgrn_kernel(qf_ref, vf_ref, lff_ref, qb_ref, vb_ref, lfb_ref, pair_ref, rows_ref, tri_ref,
                 of_ref, ob_ref, sf_scr, sb_scr, a_scr, p_scr, oi_scr):
    @pl.when(pl.program_id(2) == 0)
    def _():
        sf_scr[...] = jnp.zeros(sf_scr.shape, F32)
        sb_scr[...] = jnp.zeros(sb_scr.shape, F32)

    c = HGRN_CHUNK
    consts = (pair_ref, rows_ref, tri_ref)
    n = qf_ref.shape[1] // c
    heads = qf_ref.shape[2] // HEAD_DIM
    row = lambda idx: pl.multiple_of(idx * c, c)
    dirs = ((qf_ref, vf_ref, lff_ref, of_ref, sf_scr, False), (qb_ref, vb_ref, lfb_ref, ob_ref, sb_scr, True))
    first = lambda rev: n - 1 if rev else 0
    last = lambda rev: 0 if rev else n - 1

    for hd in range(heads):
        cols = slice(hd * HEAD_DIM, (hd + 1) * HEAD_DIM)
        for d, (_, _, lf_ref, _, _, rev) in enumerate(dirs):
            a2 = _cumsum_issue(lf_ref[0, pl.ds(first(rev) * c, c), cols], tri_ref[d])
            a_scr[2 * hd + d] = a2[:, :LANES] + a2[:, LANES:]
    p_scr[...] = jnp.zeros(p_scr.shape, BF16)
    oi_scr[...] = jnp.zeros(oi_scr.shape, F32)

    def body(ci, carry):
        idx = {False: (jnp.maximum(ci - 1, 0), ci, jnp.minimum(ci + 1, n - 1)),
               True: (jnp.minimum(n - ci, n - 1), n - 1 - ci, jnp.maximum(n - 2 - ci, 0))}
        chains = []
        for hd in range(heads):
            for d, (q_ref, v_ref, lf_ref, o_ref, st_scr, rev) in enumerate(dirs):
                rows3 = tuple(row(i) for i in idx[rev])
                chains.append(_hgrn_chunk(q_ref, v_ref, lf_ref, o_ref, st_scr, a_scr, p_scr, oi_scr, rows3, hd,
                                          2 * hd + d, consts, rev=rev))
        _interleave(chains)
        return carry

    lax.fori_loop(0, n, body, 0)

    for hd in range(heads):
        cols = slice(hd * HEAD_DIM, (hd + 1) * HEAD_DIM)
        for d, (_, v_ref, _, o_ref, _, rev) in enumerate(dirs):
            rws = pl.ds(last(rev) * c, c)
            pv = jnp.dot(p_scr[2 * hd + d], v_ref[0, rws, cols], preferred_element_type=F32)
            o_ref[0, rws, cols] = (oi_scr[2 * hd + d] + pv).astype(o_ref.dtype)


HGRN_HEADS_PER_STEP = 4


def hgrn2_bidir(q, v, logf, ts=1024, hps=HGRN_HEADS_PER_STEP):
    b, s, hk = q.shape
    ts = min(ts, s)
    nt = s // ts
    ng = hk // (hps * HEAD_DIM)
    blk = (1, ts, hps * HEAD_DIM)
    fwd = lambda bi, h, i: (bi, i, h)
    bwd = lambda bi, h, i: (bi, nt - 1 - i, h)
    out = jax.ShapeDtypeStruct((b, s, hk), BF16)
    state = pltpu.VMEM((hps, HEAD_DIM, HEAD_DIM), F32)
    consts = hgrn_constants()
    whole = lambda x: pl.BlockSpec(x.shape, lambda bi, h, i: (0,) * x.ndim)
    return pl.pallas_call(
        _hgrn_kernel,
        grid=(b, ng, nt),
        in_specs=[pl.BlockSpec(blk, fwd), pl.BlockSpec(blk, fwd), pl.BlockSpec(blk, fwd),
                  pl.BlockSpec(blk, bwd), pl.BlockSpec(blk, bwd),
                  pl.BlockSpec(blk, lambda bi, h, i: (bi, nt - 1 - i, ng + h))] + [whole(x) for x in consts],
        out_specs=[pl.BlockSpec(blk, fwd), pl.BlockSpec(blk, bwd)],
        out_shape=[out, out],
        scratch_shapes=[state, state, pltpu.VMEM((2 * hps, HGRN_CHUNK, HEAD_DIM), F32),
                        pltpu.VMEM((2 * hps, HGRN_CHUNK, HGRN_CHUNK), BF16),
                        pltpu.VMEM((2 * hps, HGRN_CHUNK, HEAD_DIM), F32)],
        compiler_params=_cparams(("parallel", "parallel", "arbitrary")),
        name="hgrn2",
    )(q, v, logf, q, v, logf, *consts)


def _hgrn_post_kernel(of_ref, ob_ref, gr_ref, gn_ref, o_ref):
    gn = gn_ref[...]
    for hd in range(o_ref.shape[1] // HEAD_DIM):
        sl = slice(hd * HEAD_DIM, (hd + 1) * HEAD_DIM)
        o = _rms(of_ref[:, sl].astype(F32) + ob_ref[:, sl].astype(F32), gn)
        o_ref[:, sl] = (o * gr_ref[:, sl].astype(F32)).astype(o_ref.dtype)


def hgrn_post(o_fw, o_bw, g_silu, g_norm, tm=512):
    m, d = o_fw.shape
    tm = min(tm, m)
    row = pl.BlockSpec((tm, d), lambda i: (i, 0))
    return pl.pallas_call(
        _hgrn_post_kernel,
        grid=(m // tm,),
        in_specs=[row, row, row, pl.BlockSpec((1, HEAD_DIM), lambda i: (0, 0))],
        out_specs=row,
        out_shape=jax.ShapeDtypeStruct((m, d), BF16),
        compiler_params=_cparams(("parallel",)),
        name="hgrn_post",
    )(o_fw, o_bw, g_silu, g_norm.reshape(1, HEAD_DIM))


def _merge_kernel(att_ref, or_ref, woa_ref, woh_ref, sa_ref, sh_ref, o_ref):
    tm = o_ref.shape[0]
    rows_per = min(MM_ROWS, tm)
    woa, woh = woa_ref[...].astype(BF16), woh_ref[...].astype(BF16)
    for r in range(0, tm, rows_per):
        rs = slice(r, r + rows_per)
        ya = jnp.dot(att_ref[rs, :], woa, preferred_element_type=F32)
        yh = jnp.dot(or_ref[rs, :], woh, preferred_element_type=F32)
        o_ref[rs, :] = (sa_ref[rs, :].astype(F32) * ya + sh_ref[rs, :].astype(F32) * yh).astype(o_ref.dtype)


def gated_merge(att, o_r, w_oa, w_oh, gates, tm=1024, tn=512):
    m, d = att.shape
    tm = min(tm, m)
    nj = d // tn
    row = pl.BlockSpec((tm, d), lambda i, j: (i, 0))
    wsp = pl.BlockSpec((d, tn), lambda i, j: (0, j))
    return pl.pallas_call(
        _merge_kernel,
        grid=(m // tm, nj),
        in_specs=[row, row, wsp, wsp,
                  pl.BlockSpec((tm, tn), lambda i, j: (i, j)),
                  pl.BlockSpec((tm, tn), lambda i, j: (i, nj + j))],
        out_specs=pl.BlockSpec((tm, tn), lambda i, j: (i, j)),
        out_shape=jax.ShapeDtypeStruct((m, d), BF16),
        compiler_params=_cparams(("parallel", "arbitrary")),
        name="gated_merge",
    )(att, o_r, w_oa, w_oh, gates, gates)


def _resid_proj_kernel(x_ref, a_ref, w_ref, o_ref):
    tm = o_ref.shape[0]
    rows_per = min(MM_ROWS, tm)
    w = w_ref[...].astype(BF16)
    for r in range(0, tm, rows_per):
        rs = slice(r, r + rows_per)
        o_ref[rs, :] = x_ref[rs, :] + jnp.dot(a_ref[rs, :], w, preferred_element_type=F32)


def resid_proj(x, a, w, tm=1024, tn=512):
    m, d = x.shape
    tm = min(tm, m)
    return pl.pallas_call(
        _resid_proj_kernel,
        grid=(m // tm, d // tn),
        in_specs=[pl.BlockSpec((tm, tn), lambda i, j: (i, j)), pl.BlockSpec((tm, a.shape[1]), lambda i, j: (i, 0)),
                  pl.BlockSpec((a.shape[1], tn), lambda i, j: (0, j))],
        out_specs=pl.BlockSpec((tm, tn), lambda i, j: (i, j)),
        out_shape=jax.ShapeDtypeStruct((m, d), F32),
        compiler_params=_cparams(("parallel", "arbitrary")),
        name="out_proj",
    )(x, a, w)


def _mlp_kernel(x_ref, g_ref, wu_ref, wd_ref, o_ref, h_scr):
    @pl.when(pl.program_id(1) == 0)
    def _():
        x = x_ref[...]
        h_scr[...] = _rms(x, g_ref[...]).astype(h_scr.dtype)
        o_ref[...] = x

    tm = o_ref.shape[0]
    rows_per = min(MM_ROWS, tm)
    wu, wd = wu_ref[...].astype(BF16), wd_ref[...].astype(BF16)
    for r in range(0, tm, rows_per):
        rs = slice(r, r + rows_per)
        u = jnp.maximum(jnp.dot(h_scr[rs, :], wu, preferred_element_type=F32), 0.0)
        o_ref[rs, :] += jnp.dot((u * u).astype(BF16), wd, preferred_element_type=F32)


def mlp_block(x, gain, w_up, w_down, tm=1024, tf=512):
    m, d = x.shape
    ff = w_up.shape[1]
    tm = min(tm, m)
    return pl.pallas_call(
        _mlp_kernel,
        grid=(m // tm, ff // tf),
        in_specs=[pl.BlockSpec((tm, d), lambda i, f: (i, 0)), pl.BlockSpec((1, d), lambda i, f: (0, 0)),
                  pl.BlockSpec((d, tf), lambda i, f: (0, f)), pl.BlockSpec((tf, d), lambda i, f: (f, 0))],
        out_specs=pl.BlockSpec((tm, d), lambda i, f: (i, 0)),
        out_shape=jax.ShapeDtypeStruct((m, d), F32),
        scratch_shapes=[pltpu.VMEM((tm, d), BF16)],
        compiler_params=_cparams(("parallel", "arbitrary")),
        name="mlp",
    )(x, gain.reshape(1, d), w_up, w_down)


PLE_ROWS = 512


def _ple_kernel(x_ref, g_ref, wg_ref, p_ref, wp_ref, gf_ref, o_ref):
    tm = o_ref.shape[0]
    units = [slice(r, r + min(PLE_ROWS, tm)) for r in range(0, tm, min(PLE_ROWS, tm))]

    def issue(rs):
        x = x_ref[rs, :]
        h = _rms(x, g_ref[...]).astype(BF16)
        logit = jnp.dot(h, wg_ref[...], preferred_element_type=F32)
        emb = jnp.dot(p_ref[rs, :].astype(BF16), wp_ref[...], preferred_element_type=F32)
        return x, logit, emb

    def finish(rs, x, logit, emb):
        o_ref[rs, :] = _rms(x + _sigmoid(logit) * emb, gf_ref[...])

    prev = issue(units[0])
    for u in range(1, len(units)):
        cur = issue(units[u])
        finish(units[u - 1], *prev)
        prev = cur
    finish(units[-1], *prev)


def ple_final(x, gain, w_gate, p, w_p, g_final, tm=512):
    m, d = x.shape
    c = p.shape[1]
    tm = min(tm, m)
    const = lambda i: (0, 0)
    return pl.pallas_call(
        _ple_kernel,
        grid=(m // tm,),
        in_specs=[pl.BlockSpec((tm, d), lambda i: (i, 0)), pl.BlockSpec((1, d), const),
                  pl.BlockSpec((d, d), const), pl.BlockSpec((tm, c), lambda i: (i, 0)),
                  pl.BlockSpec((c, d), const), pl.BlockSpec((1, d), const)],
        out_specs=pl.BlockSpec((tm, d), lambda i: (i, 0)),
        out_shape=jax.ShapeDtypeStruct((m, d), F32),
        compiler_params=_cparams(("parallel",)),
        name="ple_final",
    )(x, gain.reshape(1, d), w_gate, p, w_p, g_final.reshape(1, d))


def kernel(x, p, g_mix, w_in, g_q, g_k, w_o_attn, hgrn_lb, g_hgrn, w_o_hgrn, w_out, g_mlp, w_up, w_down,
           g_ple, w_ple_gate, w_ple, g_final):
    b, s, d = x.shape
    m = b * s
    depth = w_in.shape[0]
    attn_q = N_Q_HEADS * HEAD_DIM
    attn_kv = N_KV_HEADS * HEAD_DIM
    hk = HGRN_HEADS * HEAD_DIM
    rope = rope_tables(s)
    lb_all = jnp.cumsum(jax.nn.softmax(hgrn_lb.astype(F32), axis=0), axis=0)

    xf = x.reshape(m, d)
    for i in range(depth):
        w = w_in[i]
        h = rmsnorm_bf16(xf, g_mix[i])
        c0 = 0
        q_a = proj_qk(h, w, c0, attn_q, g_q[i], rope, s, HEAD_DIM ** -0.5 * LOG2_E); c0 += attn_q
        k_a = proj_qk(h, w, c0, attn_kv, g_k[i], rope, s, 1.0); c0 += attn_kv
        v_a = proj_plain(h, w, c0, attn_kv, None); c0 += attn_kv
        q_r = proj_plain(h, w, c0, hk, "silu"); c0 += hk
        logf = proj_logf(h, w, c0, 2 * hk, lb_all[i]); c0 += 2 * hk
        i_r = proj_plain(h, w, c0, hk, None); c0 += hk
        g_r = proj_plain(h, w, c0, hk, "silu"); c0 += hk
        gates = proj_plain(h, w, c0, 2 * d, "sigmoid"); c0 += 2 * d

        att = gqa_attention(q_a.reshape(b, s, attn_q), k_a.reshape(b, s, attn_kv), v_a.reshape(b, s, attn_kv))
        o_fw, o_bw = hgrn2_bidir(q_r.reshape(b, s, hk), i_r.reshape(b, s, hk), logf.reshape(b, s, 2 * hk))
        o_r = hgrn_post(o_fw.reshape(m, hk), o_bw.reshape(m, hk), g_r, g_hgrn[i])
        mixed = gated_merge(att.reshape(m, attn_q), o_r, w_o_attn[i].astype(BF16), w_o_hgrn[i].astype(BF16), gates)
        xf = resid_proj(xf, mixed, w_out[i].astype(BF16))
        xf = mlp_block(xf, g_mlp[i], w_up[i].astype(BF16), w_down[i])
        assert depth == 1
        xf = ple_final(xf, g_ple[i], w_ple_gate[i].astype(BF16), p[i].reshape(m, -1), w_ple[i].astype(BF16),
                       g_final)
    return xf.reshape(b, s, d)
```

```python
import functools

import jax
import jax.numpy as jnp
from jax import lax
from jax.experimental import pallas as pl
from jax.experimental.pallas import tpu as pltpu

F32 = jnp.float32
BF16 = jnp.bfloat16

EPS = 1e-6
LOG2_E = 1.4426950408889634
HEAD_DIM = 128
N_Q_HEADS = 16
N_KV_HEADS = 4
GQA_GROUPS = N_Q_HEADS // N_KV_HEADS
GRID_W = 64
ROPE_THETA = 10000.0
HGRN_HEADS = 16
LANES = 128
SUBLANES = 8
VMEM_LIMIT = 56 * 1024 * 1024


def _cparams(sem):
    return pltpu.CompilerParams(dimension_semantics=sem, vmem_limit_bytes=VMEM_LIMIT)


def _sigmoid(x):
    return 1.0 / (1.0 + jnp.exp(-x))


def _rms(x, gain):
    ms = jnp.mean(x * x, axis=-1, keepdims=True)
    return x * lax.rsqrt(ms + EPS) * gain


def _rmsnorm_kernel(x_ref, g_ref, o_ref):
    o_ref[...] = _rms(x_ref[...], g_ref[...]).astype(o_ref.dtype)


def rmsnorm_bf16(x, gain, tm=512):
    m, d = x.shape
    tm = min(tm, m)
    return pl.pallas_call(
        _rmsnorm_kernel,
        grid=(m // tm,),
        in_specs=[pl.BlockSpec((tm, d), lambda i: (i, 0)), pl.BlockSpec((1, d), lambda i: (0, 0))],
        out_specs=pl.BlockSpec((tm, d), lambda i: (i, 0)),
        out_shape=jax.ShapeDtypeStruct((m, d), BF16),
        compiler_params=_cparams(("parallel",)),
        name="rmsnorm",
    )(x, gain.reshape(1, d))


QK_SUB = 2 * HEAD_DIM


MM_ROWS = 512


def _subtiled_matmul(h_ref, w_ref, epilogue):
    tm = h_ref.shape[0]
    rows_per = min(MM_ROWS, tm)
    units = [(slice(r, r + rows_per), slice(c0, c0 + QK_SUB))
             for c0 in range(0, w_ref.shape[1], QK_SUB) for r in range(0, tm, rows_per)]
    weights = {}

    def matmul(u):
        rs, cs = u
        if cs.start not in weights:
            weights.clear()
            weights[cs.start] = w_ref[:, cs].astype(BF16)
        return jnp.dot(h_ref[rs, :], weights[cs.start], preferred_element_type=F32)

    prev = matmul(units[0])
    for j in range(1, len(units)):
        cur = matmul(units[j])
        epilogue(prev, *units[j - 1])
        prev = cur
    epilogue(prev, *units[-1])


def _stationary_weights(w_ref, wb_scr, rope_order=False):
    @pl.when(pl.program_id(1) == 0)
    def _():
        w = w_ref[...]
        if rope_order:
            tn = w.shape[1]
            quarter = (lax.broadcasted_iota(jnp.int32, w.shape, 1) // (HEAD_DIM // 4)) % 4
            w = jnp.where(quarter == 1, pltpu.roll(w, tn - HEAD_DIM // 4, 1),
                          jnp.where(quarter == 2, pltpu.roll(w, HEAD_DIM // 4, 1), w))
        wb_scr[...] = w.astype(wb_scr.dtype)


def _proj_plain_kernel(h_ref, w_ref, o_ref, wb_scr, *, act):
    def epilogue(acc, rs, cs):
        if act == "silu":
            acc = acc * _sigmoid(acc)
        elif act == "sigmoid":
            acc = _sigmoid(acc)
        o_ref[rs, cs] = acc.astype(o_ref.dtype)

    _stationary_weights(w_ref, wb_scr)
    _subtiled_matmul(h_ref, wb_scr, epilogue)


def _proj_logf_kernel(h_ref, w_ref, lb_ref, o_ref, wb_scr):
    def epilogue(acc, rs, cs):
        lb = lb_ref[:, cs]
        o_ref[rs, cs] = jnp.log(lb + (1.0 - lb) * _sigmoid(acc)) * LOG2_E

    _stationary_weights(w_ref, wb_scr)
    _subtiled_matmul(h_ref, wb_scr, epilogue)


def _proj_qk_kernel(h_ref, w_ref, cg_ref, sg_ref, o_ref, wb_scr):
    ones = jnp.ones((HEAD_DIM, HEAD_DIM), BF16)
    _stationary_weights(w_ref, wb_scr, rope_order=True)

    def epilogue(acc, rs, cs):
        cg, sg = cg_ref[rs, :], sg_ref[rs, :]
        for hd in range(QK_SUB // HEAD_DIM):
            x = acc[:, hd * HEAD_DIM:(hd + 1) * HEAD_DIM]
            ssq = jnp.dot((x * x).astype(BF16), ones, preferred_element_type=F32)
            r = lax.rsqrt(ssq * (1.0 / HEAD_DIM) + EPS)
            y = r * (x * cg + pltpu.roll(x, HEAD_DIM // 2, 1) * sg)
            c0 = cs.start + hd * HEAD_DIM
            o_ref[rs, c0:c0 + HEAD_DIM] = y.astype(o_ref.dtype)

    _subtiled_matmul(h_ref, wb_scr, epilogue)


def _proj_call(kernel, h, w, col0, ncols, out_dtype, extra=(), extra_specs=(), tm=1024, tn=1024):
    m, d = h.shape
    tm = min(tm, m)
    tn = min(tn, ncols)
    assert col0 % tn == 0 and ncols % tn == 0 and m % tm == 0
    jb = col0 // tn
    return pl.pallas_call(
        kernel,
        grid=(ncols // tn, m // tm),
        in_specs=[pl.BlockSpec((tm, d), lambda j, i: (i, 0)),
                  pl.BlockSpec((d, tn), lambda j, i: (0, jb + j))] + list(extra_specs),
        out_specs=pl.BlockSpec((tm, tn), lambda j, i: (i, j)),
        out_shape=jax.ShapeDtypeStruct((m, ncols), out_dtype),
        scratch_shapes=[pltpu.VMEM((d, tn), BF16)],
        compiler_params=_cparams(("parallel", "arbitrary")),
        name="in_proj",
    )(h, w, *extra)


def proj_plain(h, w, col0, ncols, act, out_dtype=BF16):
    return _proj_call(functools.partial(_proj_plain_kernel, act=act), h, w, col0, ncols, out_dtype)


def proj_logf(h, w, col0, ncols, lb, tn=1024):
    return _proj_call(_proj_logf_kernel, h, w, col0, ncols, F32, extra=(lb.reshape(1, ncols),),
                      extra_specs=(pl.BlockSpec((1, tn), lambda j, i: (0, j)),), tn=tn)


def proj_qk(h, w, col0, ncols, gain, rope, seq, scale, tm=1024, tn=1024):
    cos, sin = rope
    gp = gain.astype(F32)[jnp.array(ROPE_PERM)]
    cg = cos * (gp * scale)[None, :]
    sg = sin * (jnp.roll(gp, HEAD_DIM // 2) * scale)[None, :]
    tm = min(tm, seq)
    tn = min(tn, ncols)
    nsb = seq // tm
    tab = pl.BlockSpec((tm, HEAD_DIM), lambda j, i: (i % nsb, 0))
    return _proj_call(_proj_qk_kernel, h, w, col0, ncols, BF16, extra=(cg, sg), extra_specs=(tab, tab),
                      tm=tm, tn=tn)


ROPE_PERM = tuple(list(range(0, 32)) + list(range(64, 96)) + list(range(32, 64)) + list(range(96, 128)))


def rope_tables(seq):
    half = HEAD_DIM // 2
    t = jnp.arange(seq, dtype=jnp.int32)
    row = (t // GRID_W).astype(F32)
    col = (t % GRID_W).astype(F32)
    inv_freq = ROPE_THETA ** (-jnp.arange(0, half, 2, dtype=F32) / half)
    ang = jnp.concatenate([row[:, None] * inv_freq[None, :], col[:, None] * inv_freq[None, :]], axis=-1)
    cos = jnp.concatenate([jnp.cos(ang), jnp.cos(ang)], axis=-1)
    sin = jnp.concatenate([-jnp.sin(ang), jnp.sin(ang)], axis=-1)
    return cos, sin


ATTN_TQ = 128
ATTN_TKC = 8192


def _attn_kernel(q0_ref, qa_ref, qb_ref, k_ref, v_ref, o_ref, s_scr, mrun_scr, mcur_scr, acc_scr, *, tkc):
    t = pl.program_id(2)
    nt = pl.num_programs(2)
    tq = qa_ref.shape[1]
    nch = k_ref.shape[1] // tkc
    nsl = tkc // LANES
    ones = jnp.ones((tkc, LANES), BF16)

    def stack(q_ref):
        return jnp.concatenate([q_ref[0, :, g * HEAD_DIM:(g + 1) * HEAD_DIM] for g in range(GQA_GROUPS)], axis=0)

    def pass1(qs, c, slot):
        off = pl.multiple_of(c * tkc, tkc)
        s = lax.dot_general(qs, k_ref[0, pl.ds(off, tkc), :], (((1,), (1,)), ((), ())),
                            preferred_element_type=F32)
        s_scr[slot, c] = s
        m = s[:, :LANES]
        for i in range(1, nsl):
            m = jnp.maximum(m, s[:, i * LANES:(i + 1) * LANES])
        mrun_scr[...] = jnp.maximum(mrun_scr[...], m)

    def pass2(c, slot):
        off = pl.multiple_of(c * tkc, tkc)
        vext = jnp.concatenate([v_ref[0, pl.ds(off, tkc), :], ones], axis=1)
        m = mcur_scr[...]
        p = jnp.concatenate([jnp.exp2(s_scr[slot, c, :, i * LANES:(i + 1) * LANES] - m) for i in range(nsl)],
                            axis=1)
        acc_scr[...] += jnp.dot(p.astype(BF16), vext, preferred_element_type=F32)

    def begin_pass1():
        mrun_scr[...] = jnp.full(mrun_scr.shape, -jnp.inf, F32)

    def end_pass1():
        mcur_scr[...] = jnp.broadcast_to(jnp.max(mrun_scr[...], axis=1, keepdims=True), mcur_scr.shape)
        acc_scr[...] = jnp.zeros(acc_scr.shape, F32)

    def emit(half):
        o = acc_scr[:, :HEAD_DIM] / acc_scr[:, HEAD_DIM:]
        for g in range(GQA_GROUPS):
            o_ref[0, half * tq:(half + 1) * tq, g * HEAD_DIM:(g + 1) * HEAD_DIM] = (
                o[g * tq:(g + 1) * tq].astype(o_ref.dtype))

    @pl.when(t == 0)
    def _():
        qs0 = stack(q0_ref)
        begin_pass1()

        def body_0(c, carry):
            pass1(qs0, c, 0)
            return carry

        lax.fori_loop(0, nch, body_0, 0)
        end_pass1()

    qs = stack(qa_ref)
    begin_pass1()

    def body_x(c, carry):
        pass1(qs, c, 1)
        pass2(c, 0)
        return carry

    lax.fori_loop(0, nch, body_x, 0)
    emit(0)
    end_pass1()

    @pl.when(t < nt - 1)
    def _():
        qs2 = stack(qb_ref)
        begin_pass1()

        def body_y(c, carry):
            pass1(qs2, c, 0)
            pass2(c, 1)
            return carry

        lax.fori_loop(0, nch, body_y, 0)
        emit(1)
        end_pass1()

    @pl.when(t == nt - 1)
    def _():
        def body_z(c, carry):
            pass2(c, 1)
            return carry

        lax.fori_loop(0, nch, body_z, 0)
        emit(1)


def gqa_attention(q, k, v, tq=ATTN_TQ, tkc=ATTN_TKC):
    b, s, _ = q.shape
    tkc = min(tkc, s)
    nq = s // tq
    assert nq % 2 == 0 and s % tkc == 0
    gw = GQA_GROUPS * HEAD_DIM
    rows = GQA_GROUPS * tq
    qblk = (1, tq, gw)
    kvblk = (1, s, HEAD_DIM)
    return pl.pallas_call(
        functools.partial(_attn_kernel, tkc=tkc),
        grid=(b, N_KV_HEADS, nq // 2),
        in_specs=[pl.BlockSpec(qblk, lambda bi, h, t: (bi, 0, h)),
                  pl.BlockSpec(qblk, lambda bi, h, t: (bi, 2 * t + 1, h)),
                  pl.BlockSpec(qblk, lambda bi, h, t: (bi, jnp.minimum(2 * t + 2, nq - 1), h)),
                  pl.BlockSpec(kvblk, lambda bi, h, t: (bi, 0, h)),
                  pl.BlockSpec(kvblk, lambda bi, h, t: (bi, 0, h))],
        out_specs=pl.BlockSpec((1, 2 * tq, gw), lambda bi, h, t: (bi, t, h)),
        out_shape=jax.ShapeDtypeStruct(q.shape, BF16),
        scratch_shapes=[pltpu.VMEM((2, s // tkc, rows, tkc), F32), pltpu.VMEM((rows, LANES), F32),
                        pltpu.VMEM((rows, LANES), F32), pltpu.VMEM((rows, 2 * HEAD_DIM), F32)],
        compiler_params=_cparams(("parallel", "parallel", "arbitrary")),
        name="gqa_attention",
    )(q, q, q, k, v)


HGRN_CHUNK = 128
HGRN_LAG = 2


def _group_row(x, group, row):
    c, n = x.shape
    xr = x.reshape(c // group, group, n)
    return jnp.broadcast_to(xr[:, row:row + 1, :], xr.shape).reshape(c, n)


def _boundary_row(x, m, rev, odd_block):
    ref_row = m if rev else m - 1
    if 2 * m >= SUBLANES:
        return _group_row(x, 2 * m, ref_row)
    if m == 1:
        c, n = x.shape
        x3 = x.reshape(c // SUBLANES, SUBLANES, n)
        other = pltpu.roll(x3, SUBLANES - 1 if rev else 1, 1).reshape(c, n)
        return jnp.where(odd_block(1), x, other) if rev else jnp.where(odd_block(1), other, x)
    assert 4 * m == SUBLANES
    return jnp.where(odd_block(2 * m), _group_row(x, SUBLANES, 2 * m + ref_row), _group_row(x, SUBLANES, ref_row))


def hgrn_constants():
    import numpy as np
    c = HGRN_CHUNK
    t = np.arange(c)[:, None]
    s = np.arange(c)[None, :]
    pair, tri = [], []
    for rev in (False, True):
        masks = [t == s]
        m = 1
        while m < c:
            same = (t // (2 * m)) == (s // (2 * m))
            t_query = ((t // m) % 2) == (0 if rev else 1)
            s_key = ((s // m) % 2) == (1 if rev else 0)
            masks.append(same & t_query & s_key)
            m *= 2
        pair.append(np.stack(masks))
        tri.append((s >= t) if rev else (s <= t))
    rows = np.stack([np.broadcast_to(((t // n) % 2) == 1, (c, LANES)) for n in (1, 2, 4)])
    return (jnp.asarray(np.stack(pair), jnp.int32), jnp.asarray(rows, jnp.int32),
            jnp.asarray(np.stack(tri), BF16))


def _neg_abs(x):
    return pltpu.bitcast(pltpu.bitcast(x, jnp.uint32) | jnp.uint32(0x80000000), F32)


def _nt(x, y):
    return lax.dot_general(x, y, (((1,), (1,)), ((), ())), preferred_element_type=F32)


def _hgrn_chunk(q_ref, v_ref, lf_ref, o_ref, st_scr, a_scr, p_scr, oi_scr, rows3, col, slot, consts, *, rev):
    pair_ref, rows_ref, tri_ref = consts
    r_prev, r0, r_next = rows3
    d = 1 if rev else 0
    odd_block = lambda n: rows_ref[{1: 0, 2: 1, 4: 2}[n]] != 0
    c = HGRN_CHUNK
    cols = slice(col * HEAD_DIM, (col + 1) * HEAD_DIM)

    pv = jnp.dot(p_scr[slot], v_ref[0, pl.ds(r_prev, c), cols], preferred_element_type=F32)
    a2 = _cumsum_issue(lf_ref[0, pl.ds(r_next, c), cols], tri_ref[d])

    qb = q_ref[0, pl.ds(r0, c), cols]
    vb = v_ref[0, pl.ds(r0, c), cols]
    f = jnp.exp2(lf_ref[0, pl.ds(r0, c), cols])
    k = 1.0 - f
    q = qb.astype(F32)
    sc0 = jnp.sum(q * k, axis=1, keepdims=True)
    k3 = k.reshape(c // SUBLANES, SUBLANES, HEAD_DIM)
    k_other = pltpu.roll(k3, SUBLANES - 1 if rev else 1, 1).reshape(c, HEAD_DIM)
    sc1 = jnp.sum(q * f * k_other, axis=1, keepdims=True)
    a = a_scr[slot]
    edge = 0 if rev else c - 1
    a_end = a[edge:edge + 1, :]

    st = st_scr[col]
    o = jnp.dot((q * jnp.exp2(a)).astype(BF16), st.astype(BF16), preferred_element_type=F32)
    kdec = (k * jnp.exp2(a_end - a)).astype(BF16)
    st_new = lax.dot_general(kdec, vb, (((0,), (0,)), ((), ())), preferred_element_type=F32)
    st_decayed = st * jnp.transpose(jnp.broadcast_to(jnp.exp2(a_end), (HEAD_DIM, HEAD_DIM)))

    nstrip = c // SUBLANES
    strip = lambda x, i: x[i * SUBLANES:(i + 1) * SUBLANES]
    p = [None] * nstrip

    def apply(pending):
        level, scores, rows, shifts = pending
        for j, i in enumerate(rows):
            mask = pair_ref[d, level, i * SUBLANES:(i + 1) * SUBLANES, :] != 0
            s = strip(scores, j)
            if shifts is not None and shifts[j]:
                s = pltpu.roll(s, shifts[j], 1)
            p[i] = jnp.where(mask, s, 0.0 if p[i] is None else p[i])

    query_first = rev

    def halves(m):
        out = []
        for g in range(c // (2 * m)):
            first = slice(g * 2 * m, g * 2 * m + m)
            second = slice(g * 2 * m + m, (g + 1) * 2 * m)
            out.append((first, second) if query_first else (second, first))
        return out

    def factor(m):
        ref = _boundary_row(a, m, rev, odd_block)
        if m < SUBLANES:
            return jnp.exp2(_neg_abs(a - ref))
        parts = []
        for qs, ks in halves(m):
            dq, dk = a[qs] - ref[qs], ref[ks] - a[ks]
            parts += [dq, dk] if query_first else [dk, dq]
        return jnp.exp2(jnp.concatenate(parts, axis=0))

    every = list(range(nstrip))
    pending = [(0, sc0, every, None), (1, sc1, every, None)]
    nlevel = pair_ref.shape[1] - 1
    m = 2
    e = factor(m)
    for level in range(2, nlevel + 1):
        e_next = factor(2 * m) if level < nlevel else None
        if m < SUBLANES:
            qk = jnp.where(odd_block(m), k, q) if query_first else jnp.where(odd_block(m), q, k)
            z = (qk * e).astype(BF16)
            sc = _nt(z, z)
            rows, shifts = every, None
        else:
            qparts, kparts, rows, shifts = [], [], [], []
            for g, (qs, ks) in enumerate(halves(m)):
                qparts.append(q[qs] * e[qs])
                kparts.append(k[ks] * e[ks])
                rows += list(range(qs.start // SUBLANES, qs.stop // SUBLANES))
                shifts += [(ks.start - g * m) % c] * (m // SUBLANES)
            sc = _nt(jnp.concatenate(qparts, axis=0).astype(BF16), jnp.concatenate(kparts, axis=0).astype(BF16))
            sc = jnp.concatenate([sc, jnp.zeros((c // 2, c - c // 2), F32)], axis=1)
        e = e_next
        yield
        if level == 2:
            o_prev = oi_scr[slot]
            oi_scr[slot] = o
            st_scr[col] = st_decayed + st_new
        pending.append((level, sc, rows, shifts))
        if len(pending) > HGRN_LAG:
            apply(pending.pop(0))
        m *= 2
    a_scr[slot] = a2[:, :LANES] + a2[:, LANES:]
    o_ref[0, pl.ds(r_prev, c), cols] = (o_prev + pv).astype(o_ref.dtype)
    yield
    for item in pending:
        apply(item)
    p_scr[slot] = jnp.concatenate(p, axis=0).astype(BF16)


def _cumsum_issue(lf, tri):
    hi = lf.astype(BF16)
    lo = (lf - hi.astype(F32)).astype(BF16)
    return jnp.dot(tri, jnp.concatenate([hi, lo], axis=1), preferred_element_type=F32)


def _interleave(chains):
    chains = list(chains)
    while chains:
        for ch in list(chains):
            try:
                next(ch)
            except StopIteration:
                chains.remove(ch)


def _hgrn_kernel(qf_ref, vf_ref, lff_ref, qb_ref, vb_ref, lfb_ref, pair_ref, rows_ref, tri_ref,
                 of_ref, ob_ref, sf_scr, sb_scr, a_scr, p_scr, oi_scr):
    @pl.when(pl.program_id(2) == 0)
    def _():
        sf_scr[...] = jnp.zeros(sf_scr.shape, F32)
        sb_scr[...] = jnp.zeros(sb_scr.shape, F32)

    c = HGRN_CHUNK
    consts = (pair_ref, rows_ref, tri_ref)
    n = qf_ref.shape[1] // c
    heads = qf_ref.shape[2] // HEAD_DIM
    row = lambda idx: pl.multiple_of(idx * c, c)
    dirs = ((qf_ref, vf_ref, lff_ref, of_ref, sf_scr, False), (qb_ref, vb_ref, lfb_ref, ob_ref, sb_scr, True))
    first = lambda rev: n - 1 if rev else 0
    last = lambda rev: 0 if rev else n - 1

    for hd in range(heads):
        cols = slice(hd * HEAD_DIM, (hd + 1) * HEAD_DIM)
        for d, (_, _, lf_ref, _, _, rev) in enumerate(dirs):
            a2 = _cumsum_issue(lf_ref[0, pl.ds(first(rev) * c, c), cols], tri_ref[d])
            a_scr[2 * hd + d] = a2[:, :LANES] + a2[:, LANES:]
    p_scr[...] = jnp.zeros(p_scr.shape, BF16)
    oi_scr[...] = jnp.zeros(oi_scr.shape, F32)

    def body(ci, carry):
        idx = {False: (jnp.maximum(ci - 1, 0), ci, jnp.minimum(ci + 1, n - 1)),
               True: (jnp.minimum(n - ci, n - 1), n - 1 - ci, jnp.maximum(n - 2 - ci, 0))}
        chains = []
        for hd in range(heads):
            for d, (q_ref, v_ref, lf_ref, o_ref, st_scr, rev) in enumerate(dirs):
                rows3 = tuple(row(i) for i in idx[rev])
                chains.append(_hgrn_chunk(q_ref, v_ref, lf_ref, o_ref, st_scr, a_scr, p_scr, oi_scr, rows3, hd,
                                          2 * hd + d, consts, rev=rev))
        _interleave(chains)
        return carry

    lax.fori_loop(0, n, body, 0)

    for hd in range(heads):
        cols = slice(hd * HEAD_DIM, (hd + 1) * HEAD_DIM)
        for d, (_, v_ref, _, o_ref, _, rev) in enumerate(dirs):
            rws = pl.ds(last(rev) * c, c)
            pv = jnp.dot(p_scr[2 * hd + d], v_ref[0, rws, cols], preferred_element_type=F32)
            o_ref[0, rws, cols] = (oi_scr[2 * hd + d] + pv).astype(o_ref.dtype)


HGRN_HEADS_PER_STEP = 4


def hgrn2_bidir(q, v, logf, ts=1024, hps=HGRN_HEADS_PER_STEP):
    b, s, hk = q.shape
    ts = min(ts, s)
    nt = s // ts
    ng = hk // (hps * HEAD_DIM)
    blk = (1, ts, hps * HEAD_DIM)
    fwd = lambda bi, h, i: (bi, i, h)
    bwd = lambda bi, h, i: (bi, nt - 1 - i, h)
    out = jax.ShapeDtypeStruct((b, s, hk), BF16)
    state = pltpu.VMEM((hps, HEAD_DIM, HEAD_DIM), F32)
    consts = hgrn_constants()
    whole = lambda x: pl.BlockSpec(x.shape, lambda bi, h, i: (0,) * x.ndim)
    return pl.pallas_call(
        _hgrn_kernel,
        grid=(b, ng, nt),
        in_specs=[pl.BlockSpec(blk, fwd), pl.BlockSpec(blk, fwd), pl.BlockSpec(blk, fwd),
                  pl.BlockSpec(blk, bwd), pl.BlockSpec(blk, bwd),
                  pl.BlockSpec(blk, lambda bi, h, i: (bi, nt - 1 - i, ng + h))] + [whole(x) for x in consts],
        out_specs=[pl.BlockSpec(blk, fwd), pl.BlockSpec(blk, bwd)],
        out_shape=[out, out],
        scratch_shapes=[state, state, pltpu.VMEM((2 * hps, HGRN_CHUNK, HEAD_DIM), F32),
                        pltpu.VMEM((2 * hps, HGRN_CHUNK, HGRN_CHUNK), BF16),
                        pltpu.VMEM((2 * hps, HGRN_CHUNK, HEAD_DIM), F32)],
        compiler_params=_cparams(("parallel", "parallel", "arbitrary")),
        name="hgrn2",
    )(q, v, logf, q, v, logf, *consts)


def _hgrn_post_kernel(of_ref, ob_ref, gr_ref, gn_ref, o_ref):
    gn = gn_ref[...]
    for hd in range(o_ref.shape[1] // HEAD_DIM):
        sl = slice(hd * HEAD_DIM, (hd + 1) * HEAD_DIM)
        o = _rms(of_ref[:, sl].astype(F32) + ob_ref[:, sl].astype(F32), gn)
        o_ref[:, sl] = (o * gr_ref[:, sl].astype(F32)).astype(o_ref.dtype)


def hgrn_post(o_fw, o_bw, g_silu, g_norm, tm=512):
    m, d = o_fw.shape
    tm = min(tm, m)
    row = pl.BlockSpec((tm, d), lambda i: (i, 0))
    return pl.pallas_call(
        _hgrn_post_kernel,
        grid=(m // tm,),
        in_specs=[row, row, row, pl.BlockSpec((1, HEAD_DIM), lambda i: (0, 0))],
        out_specs=row,
        out_shape=jax.ShapeDtypeStruct((m, d), BF16),
        compiler_params=_cparams(("parallel",)),
        name="hgrn_post",
    )(o_fw, o_bw, g_silu, g_norm.reshape(1, HEAD_DIM))


def _merge_kernel(att_ref, or_ref, woa_ref, woh_ref, sa_ref, sh_ref, o_ref):
    tm = o_ref.shape[0]
    rows_per = min(MM_ROWS, tm)
    woa, woh = woa_ref[...].astype(BF16), woh_ref[...].astype(BF16)
    for r in range(0, tm, rows_per):
        rs = slice(r, r + rows_per)
        ya = jnp.dot(att_ref[rs, :], woa, preferred_element_type=F32)
        yh = jnp.dot(or_ref[rs, :], woh, preferred_element_type=F32)
        o_ref[rs, :] = (sa_ref[rs, :].astype(F32) * ya + sh_ref[rs, :].astype(F32) * yh).astype(o_ref.dtype)


def gated_merge(att, o_r, w_oa, w_oh, gates, tm=1024, tn=512):
    m, d = att.shape
    tm = min(tm, m)
    nj = d // tn
    row = pl.BlockSpec((tm, d), lambda i, j: (i, 0))
    wsp = pl.BlockSpec((d, tn), lambda i, j: (0, j))
    return pl.pallas_call(
        _merge_kernel,
        grid=(m // tm, nj),
        in_specs=[row, row, wsp, wsp,
                  pl.BlockSpec((tm, tn), lambda i, j: (i, j)),
                  pl.BlockSpec((tm, tn), lambda i, j: (i, nj + j))],
        out_specs=pl.BlockSpec((tm, tn), lambda i, j: (i, j)),
        out_shape=jax.ShapeDtypeStruct((m, d), BF16),
        compiler_params=_cparams(("parallel", "arbitrary")),
        name="gated_merge",
    )(att, o_r, w_oa, w_oh, gates, gates)


def _resid_proj_kernel(x_ref, a_ref, w_ref, o_ref):
    tm = o_ref.shape[0]
    rows_per = min(MM_ROWS, tm)
    w = w_ref[...].astype(BF16)
    for r in range(0, tm, rows_per):
        rs = slice(r, r + rows_per)
        o_ref[rs, :] = x_ref[rs, :] + jnp.dot(a_ref[rs, :], w, preferred_element_type=F32)


def resid_proj(x, a, w, tm=1024, tn=512):
    m, d = x.shape
    tm = min(tm, m)
    return pl.pallas_call(
        _resid_proj_kernel,
        grid=(m // tm, d // tn),
        in_specs=[pl.BlockSpec((tm, tn), lambda i, j: (i, j)), pl.BlockSpec((tm, a.shape[1]), lambda i, j: (i, 0)),
                  pl.BlockSpec((a.shape[1], tn), lambda i, j: (0, j))],
        out_specs=pl.BlockSpec((tm, tn), lambda i, j: (i, j)),
        out_shape=jax.ShapeDtypeStruct((m, d), F32),
        compiler_params=_cparams(("parallel", "arbitrary")),
        name="out_proj",
    )(x, a, w)


def _mlp_kernel(x_ref, g_ref, wu_ref, wd_ref, o_ref, h_scr):
    @pl.when(pl.program_id(1) == 0)
    def _():
        x = x_ref[...]
        h_scr[...] = _rms(x, g_ref[...]).astype(h_scr.dtype)
        o_ref[...] = x

    tm = o_ref.shape[0]
    rows_per = min(MM_ROWS, tm)
    wu, wd = wu_ref[...].astype(BF16), wd_ref[...].astype(BF16)
    for r in range(0, tm, rows_per):
        rs = slice(r, r + rows_per)
        u = jnp.maximum(jnp.dot(h_scr[rs, :], wu, preferred_element_type=F32), 0.0)
        o_ref[rs, :] += jnp.dot((u * u).astype(BF16), wd, preferred_element_type=F32)


def mlp_block(x, gain, w_up, w_down, tm=1024, tf=512):
    m, d = x.shape
    ff = w_up.shape[1]
    tm = min(tm, m)
    return pl.pallas_call(
        _mlp_kernel,
        grid=(m // tm, ff // tf),
        in_specs=[pl.BlockSpec((tm, d), lambda i, f: (i, 0)), pl.BlockSpec((1, d), lambda i, f: (0, 0)),
                  pl.BlockSpec((d, tf), lambda i, f: (0, f)), pl.BlockSpec((tf, d), lambda i, f: (f, 0))],
        out_specs=pl.BlockSpec((tm, d), lambda i, f: (i, 0)),
        out_shape=jax.ShapeDtypeStruct((m, d), F32),
        scratch_shapes=[pltpu.VMEM((tm, d), BF16)],
        compiler_params=_cparams(("parallel", "arbitrary")),
        name="mlp",
    )(x, gain.reshape(1, d), w_up, w_down)


PLE_ROWS = 512


def _ple_kernel(x_ref, g_ref, wg_ref, p_ref, wp_ref, gf_ref, o_ref):
    tm = o_ref.shape[0]
    units = [slice(r, r + min(PLE_ROWS, tm)) for r in range(0, tm, min(PLE_ROWS, tm))]

    def issue(rs):
        x = x_ref[rs, :]
        h = _rms(x, g_ref[...]).astype(BF16)
        logit = jnp.dot(h, wg_ref[...], preferred_element_type=F32)
        emb = jnp.dot(p_ref[rs, :].astype(BF16), wp_ref[...], preferred_element_type=F32)
        return x, logit, emb

    def finish(rs, x, logit, emb):
        o_ref[rs, :] = _rms(x + _sigmoid(logit) * emb, gf_ref[...])

    prev = issue(units[0])
    for u in range(1, len(units)):
        cur = issue(units[u])
        finish(units[u - 1], *prev)
        prev = cur
    finish(units[-1], *prev)


def ple_final(x, gain, w_gate, p, w_p, g_final, tm=512):
    m, d = x.shape
    c = p.shape[1]
    tm = min(tm, m)
    const = lambda i: (0, 0)
    return pl.pallas_call(
        _ple_kernel,
        grid=(m // tm,),
        in_specs=[pl.BlockSpec((tm, d), lambda i: (i, 0)), pl.BlockSpec((1, d), const),
                  pl.BlockSpec((d, d), const), pl.BlockSpec((tm, c), lambda i: (i, 0)),
                  pl.BlockSpec((c, d), const), pl.BlockSpec((1, d), const)],
        out_specs=pl.BlockSpec((tm, d), lambda i: (i, 0)),
        out_shape=jax.ShapeDtypeStruct((m, d), F32),
        compiler_params=_cparams(("parallel",)),
        name="ple_final",
    )(x, gain.reshape(1, d), w_gate, p, w_p, g_final.reshape(1, d))


def kernel(x, p, g_mix, w_in, g_q, g_k, w_o_attn, hgrn_lb, g_hgrn, w_o_hgrn, w_out, g_mlp, w_up, w_down,
           g_ple, w_ple_gate, w_ple, g_final):
    b, s, d = x.shape
    m = b * s
    depth = w_in.shape[0]
    attn_q = N_Q_HEADS * HEAD_DIM
    attn_kv = N_KV_HEADS * HEAD_DIM
    hk = HGRN_HEADS * HEAD_DIM
    rope = rope_tables(s)
    lb_all = jnp.cumsum(jax.nn.softmax(hgrn_lb.astype(F32), axis=0), axis=0)

    xf = x.reshape(m, d)
    for i in range(depth):
        w = w_in[i]
        h = rmsnorm_bf16(xf, g_mix[i])
        c0 = 0
        q_a = proj_qk(h, w, c0, attn_q, g_q[i], rope, s, HEAD_DIM ** -0.5 * LOG2_E); c0 += attn_q
        k_a = proj_qk(h, w, c0, attn_kv, g_k[i], rope, s, 1.0); c0 += attn_kv
        v_a = proj_plain(h, w, c0, attn_kv, None); c0 += attn_kv
        q_r = proj_plain(h, w, c0, hk, "silu"); c0 += hk
        logf = proj_logf(h, w, c0, 2 * hk, lb_all[i]); c0 += 2 * hk
        i_r = proj_plain(h, w, c0, hk, None); c0 += hk
        g_r = proj_plain(h, w, c0, hk, "silu"); c0 += hk
        gates = proj_plain(h, w, c0, 2 * d, "sigmoid"); c0 += 2 * d

        att = gqa_attention(q_a.reshape(b, s, attn_q), k_a.reshape(b, s, attn_kv), v_a.reshape(b, s, attn_kv))
        o_fw, o_bw = hgrn2_bidir(q_r.reshape(b, s, hk), i_r.reshape(b, s, hk), logf.reshape(b, s, 2 * hk))
        o_r = hgrn_post(o_fw.reshape(m, hk), o_bw.reshape(m, hk), g_r, g_hgrn[i])
        mixed = gated_merge(att.reshape(m, attn_q), o_r, w_o_attn[i].astype(BF16), w_o_hgrn[i].astype(BF16), gates)
        xf = resid_proj(xf, mixed, w_out[i].astype(BF16))
        xf = mlp_block(xf, g_mlp[i], w_up[i].astype(BF16), w_down[i])
        assert depth == 1
        xf = ple_final(xf, g_ple[i], w_ple_gate[i].astype(BF16), p[i].reshape(m, -1), w_ple[i].astype(BF16),
                       g_final)
    return xf.reshape(b, s, d)
```

```python
import functools

import jax
import jax.numpy as jnp
from jax import lax
from jax.experimental import pallas as pl
from jax.experimental.pallas import tpu as pltpu

F32 = jnp.float32
BF16 = jnp.bfloat16

EPS = 1e-6
LOG2_E = 1.4426950408889634
HEAD_DIM = 128
N_Q_HEADS = 16
N_KV_HEADS = 4
GQA_GROUPS = N_Q_HEADS // N_KV_HEADS
GRID_W = 64
ROPE_THETA = 10000.0
HGRN_HEADS = 16
LANES = 128
SUBLANES = 8
VMEM_LIMIT = 56 * 1024 * 1024


def _cparams(sem):
    return pltpu.CompilerParams(dimension_semantics=sem, vmem_limit_bytes=VMEM_LIMIT)


def _sigmoid(x):
    return 1.0 / (1.0 + jnp.exp(-x))


def _rms(x, gain):
    ms = jnp.mean(x * x, axis=-1, keepdims=True)
    return x * lax.rsqrt(ms + EPS) * gain


def _rmsnorm_kernel(x_ref, g_ref, o_ref):
    o_ref[...] = _rms(x_ref[...], g_ref[...]).astype(o_ref.dtype)


def rmsnorm_bf16(x, gain, tm=512):
    m, d = x.shape
    tm = min(tm, m)
    return pl.pallas_call(
        _rmsnorm_kernel,
        grid=(m // tm,),
        in_specs=[pl.BlockSpec((tm, d), lambda i: (i, 0)), pl.BlockSpec((1, d), lambda i: (0, 0))],
        out_specs=pl.BlockSpec((tm, d), lambda i: (i, 0)),
        out_shape=jax.ShapeDtypeStruct((m, d), BF16),
        compiler_params=_cparams(("parallel",)),
        name="rmsnorm",
    )(x, gain.reshape(1, d))


QK_SUB = 2 * HEAD_DIM


MM_ROWS = 512


def _subtiled_matmul(h_ref, w_ref, epilogue):
    tm = h_ref.shape[0]
    rows_per = min(MM_ROWS, tm)
    units = [(slice(r, r + rows_per), slice(c0, c0 + QK_SUB))
             for c0 in range(0, w_ref.shape[1], QK_SUB) for r in range(0, tm, rows_per)]
    weights = {}

    def matmul(u):
        rs, cs = u
        if cs.start not in weights:
            weights.clear()
            weights[cs.start] = w_ref[:, cs].astype(BF16)
        return jnp.dot(h_ref[rs, :], weights[cs.start], preferred_element_type=F32)

    prev = matmul(units[0])
    for j in range(1, len(units)):
        cur = matmul(units[j])
        epilogue(prev, *units[j - 1])
        prev = cur
    epilogue(prev, *units[-1])


def _stationary_weights(w_ref, wb_scr, rope_order=False):
    @pl.when(pl.program_id(1) == 0)
    def _():
        w = w_ref[...]
        if rope_order:
            tn = w.shape[1]
            quarter = (lax.broadcasted_iota(jnp.int32, w.shape, 1) // (HEAD_DIM // 4)) % 4
            w = jnp.where(quarter == 1, pltpu.roll(w, tn - HEAD_DIM // 4, 1),
                          jnp.where(quarter == 2, pltpu.roll(w, HEAD_DIM // 4, 1), w))
        wb_scr[...] = w.astype(wb_scr.dtype)


def _proj_plain_kernel(h_ref, w_ref, o_ref, wb_scr, *, act):
    def epilogue(acc, rs, cs):
        if act == "silu":
            acc = acc * _sigmoid(acc)
        elif act == "sigmoid":
            acc = _sigmoid(acc)
        o_ref[rs, cs] = acc.astype(o_ref.dtype)

    _stationary_weights(w_ref, wb_scr)
    _subtiled_matmul(h_ref, wb_scr, epilogue)


def _proj_logf_kernel(h_ref, w_ref, lb_ref, o_ref, wb_scr):
    def epilogue(acc, rs, cs):
        lb = lb_ref[:, cs]
        o_ref[rs, cs] = jnp.log(lb + (1.0 - lb) * _sigmoid(acc)) * LOG2_E

    _stationary_weights(w_ref, wb_scr)
    _subtiled_matmul(h_ref, wb_scr, epilogue)


def _proj_qk_kernel(h_ref, w_ref, cg_ref, sg_ref, o_ref, wb_scr):
    ones = jnp.ones((HEAD_DIM, HEAD_DIM), BF16)
    _stationary_weights(w_ref, wb_scr, rope_order=True)

    def epilogue(acc, rs, cs):
        cg, sg = cg_ref[rs, :], sg_ref[rs, :]
        for hd in range(QK_SUB // HEAD_DIM):
            x = acc[:, hd * HEAD_DIM:(hd + 1) * HEAD_DIM]
            ssq = jnp.dot((x * x).astype(BF16), ones, preferred_element_type=F32)
            r = lax.rsqrt(ssq * (1.0 / HEAD_DIM) + EPS)
            y = r * (x * cg + pltpu.roll(x, HEAD_DIM // 2, 1) * sg)
            c0 = cs.start + hd * HEAD_DIM
            o_ref[rs, c0:c0 + HEAD_DIM] = y.astype(o_ref.dtype)

    _subtiled_matmul(h_ref, wb_scr, epilogue)


def _proj_call(kernel, h, w, col0, ncols, out_dtype, extra=(), extra_specs=(), tm=1024, tn=1024):
    m, d = h.shape
    tm = min(tm, m)
    tn = min(tn, ncols)
    assert col0 % tn == 0 and ncols % tn == 0 and m % tm == 0
    jb = col0 // tn
    return pl.pallas_call(
        kernel,
        grid=(ncols // tn, m // tm),
        in_specs=[pl.BlockSpec((tm, d), lambda j, i: (i, 0)),
                  pl.BlockSpec((d, tn), lambda j, i: (0, jb + j))] + list(extra_specs),
        out_specs=pl.BlockSpec((tm, tn), lambda j, i: (i, j)),
        out_shape=jax.ShapeDtypeStruct((m, ncols), out_dtype),
        scratch_shapes=[pltpu.VMEM((d, tn), BF16)],
        compiler_params=_cparams(("parallel", "arbitrary")),
        name="in_proj",
    )(h, w, *extra)


def proj_plain(h, w, col0, ncols, act, out_dtype=BF16):
    return _proj_call(functools.partial(_proj_plain_kernel, act=act), h, w, col0, ncols, out_dtype)


def proj_logf(h, w, col0, ncols, lb, tn=1024):
    return _proj_call(_proj_logf_kernel, h, w, col0, ncols, F32, extra=(lb.reshape(1, ncols),),
                      extra_specs=(pl.BlockSpec((1, tn), lambda j, i: (0, j)),), tn=tn)


def proj_qk(h, w, col0, ncols, gain, rope, seq, scale, tm=1024, tn=1024):
    cos, sin = rope
    gp = gain.astype(F32)[jnp.array(ROPE_PERM)]
    cg = cos * (gp * scale)[None, :]
    sg = sin * (jnp.roll(gp, HEAD_DIM // 2) * scale)[None, :]
    tm = min(tm, seq)
    tn = min(tn, ncols)
    nsb = seq // tm
    tab = pl.BlockSpec((tm, HEAD_DIM), lambda j, i: (i % nsb, 0))
    return _proj_call(_proj_qk_kernel, h, w, col0, ncols, BF16, extra=(cg, sg), extra_specs=(tab, tab),
                      tm=tm, tn=tn)


ROPE_PERM = tuple(list(range(0, 32)) + list(range(64, 96)) + list(range(32, 64)) + list(range(96, 128)))


def rope_tables(seq):
    half = HEAD_DIM // 2
    t = jnp.arange(seq, dtype=jnp.int32)
    row = (t // GRID_W).astype(F32)
    col = (t % GRID_W).astype(F32)
    inv_freq = ROPE_THETA ** (-jnp.arange(0, half, 2, dtype=F32) / half)
    ang = jnp.concatenate([row[:, None] * inv_freq[None, :], col[:, None] * inv_freq[None, :]], axis=-1)
    cos = jnp.concatenate([jnp.cos(ang), jnp.cos(ang)], axis=-1)
    sin = jnp.concatenate([-jnp.sin(ang), jnp.sin(ang)], axis=-1)
    return cos, sin


ATTN_TQ = 128
ATTN_TKC = 8192


def _attn_kernel(q0_ref, qa_ref, qb_ref, k_ref, v_ref, o_ref, s_scr, mrun_scr, mcur_scr, acc_scr, *, tkc):
    t = pl.program_id(2)
    nt = pl.num_programs(2)
    tq = qa_ref.shape[1]
    nch = k_ref.shape[1] // tkc
    nsl = tkc // LANES
    ones = jnp.ones((tkc, LANES), BF16)

    def stack(q_ref):
        return jnp.concatenate([q_ref[0, :, g * HEAD_DIM:(g + 1) * HEAD_DIM] for g in range(GQA_GROUPS)], axis=0)

    def pass1(qs, c, slot):
        off = pl.multiple_of(c * tkc, tkc)
        s = lax.dot_general(qs, k_ref[0, pl.ds(off, tkc), :], (((1,), (1,)), ((), ())),
                            preferred_element_type=F32)
        s_scr[slot, c] = s
        m = s[:, :LANES]
        for i in range(1, nsl):
            m = jnp.maximum(m, s[:, i * LANES:(i + 1) * LANES])
        mrun_scr[...] = jnp.maximum(mrun_scr[...], m)

    def pass2(c, slot):
        off = pl.multiple_of(c * tkc, tkc)
        vext = jnp.concatenate([v_ref[0, pl.ds(off, tkc), :], ones], axis=1)
        m = mcur_scr[...]
        p = jnp.concatenate([jnp.exp2(s_scr[slot, c, :, i * LANES:(i + 1) * LANES] - m) for i in range(nsl)],
                            axis=1)
        acc_scr[...] += jnp.dot(p.astype(BF16), vext, preferred_element_type=F32)

    def begin_pass1():
        mrun_scr[...] = jnp.full(mrun_scr.shape, -jnp.inf, F32)

    def end_pass1():
        mcur_scr[...] = jnp.broadcast_to(jnp.max(mrun_scr[...], axis=1, keepdims=True), mcur_scr.shape)
        acc_scr[...] = jnp.zeros(acc_scr.shape, F32)

    def emit(half):
        o = acc_scr[:, :HEAD_DIM] / acc_scr[:, HEAD_DIM:]
        for g in range(GQA_GROUPS):
            o_ref[0, half * tq:(half + 1) * tq, g * HEAD_DIM:(g + 1) * HEAD_DIM] = (
                o[g * tq:(g + 1) * tq].astype(o_ref.dtype))

    @pl.when(t == 0)
    def _():
        qs0 = stack(q0_ref)
        begin_pass1()

        def body_0(c, carry):
            pass1(qs0, c, 0)
            return carry

        lax.fori_loop(0, nch, body_0, 0)
        end_pass1()

    qs = stack(qa_ref)
    begin_pass1()

    def body_x(c, carry):
        pass1(qs, c, 1)
        pass2(c, 0)
        return carry

    lax.fori_loop(0, nch, body_x, 0)
    emit(0)
    end_pass1()

    @pl.when(t < nt - 1)
    def _():
        qs2 = stack(qb_ref)
        begin_pass1()

        def body_y(c, carry):
            pass1(qs2, c, 0)
            pass2(c, 1)
            return carry

        lax.fori_loop(0, nch, body_y, 0)
        emit(1)
        end_pass1()

    @pl.when(t == nt - 1)
    def _():
        def body_z(c, carry):
            pass2(c, 1)
            return carry

        lax.fori_loop(0, nch, body_z, 0)
        emit(1)


def gqa_attention(q, k, v, tq=ATTN_TQ, tkc=ATTN_TKC):
    b, s, _ = q.shape
    tkc = min(tkc, s)
    nq = s // tq
    assert nq % 2 == 0 and s % tkc == 0
    gw = GQA_GROUPS * HEAD_DIM
    rows = GQA_GROUPS * tq
    qblk = (1, tq, gw)
    kvblk = (1, s, HEAD_DIM)
    return pl.pallas_call(
        functools.partial(_attn_kernel, tkc=tkc),
        grid=(b, N_KV_HEADS, nq // 2),
        in_specs=[pl.BlockSpec(qblk, lambda bi, h, t: (bi, 0, h)),
                  pl.BlockSpec(qblk, lambda bi, h, t: (bi, 2 * t + 1, h)),
                  pl.BlockSpec(qblk, lambda bi, h, t: (bi, jnp.minimum(2 * t + 2, nq - 1), h)),
                  pl.BlockSpec(kvblk, lambda bi, h, t: (bi, 0, h)),
                  pl.BlockSpec(kvblk, lambda bi, h, t: (bi, 0, h))],
        out_specs=pl.BlockSpec((1, 2 * tq, gw), lambda bi, h, t: (bi, t, h)),
        out_shape=jax.ShapeDtypeStruct(q.shape, BF16),
        scratch_shapes=[pltpu.VMEM((2, s // tkc, rows, tkc), F32), pltpu.VMEM((rows, LANES), F32),
                        pltpu.VMEM((rows, LANES), F32), pltpu.VMEM((rows, 2 * HEAD_DIM), F32)],
        compiler_params=_cparams(("parallel", "parallel", "arbitrary")),
        name="gqa_attention",
    )(q, q, q, k, v)


HGRN_CHUNK = 128
HGRN_LAG = 2


def _group_row(x, group, row):
    c, n = x.shape
    xr = x.reshape(c // group, group, n)
    return jnp.broadcast_to(xr[:, row:row + 1, :], xr.shape).reshape(c, n)


def _boundary_row(x, m, rev, odd_block):
    ref_row = m if rev else m - 1
    if 2 * m >= SUBLANES:
        return _group_row(x, 2 * m, ref_row)
    if m == 1:
        c, n = x.shape
        x3 = x.reshape(c // SUBLANES, SUBLANES, n)
        other = pltpu.roll(x3, SUBLANES - 1 if rev else 1, 1).reshape(c, n)
        return jnp.where(odd_block(1), x, other) if rev else jnp.where(odd_block(1), other, x)
    assert 4 * m == SUBLANES
    return jnp.where(odd_block(2 * m), _group_row(x, SUBLANES, 2 * m + ref_row), _group_row(x, SUBLANES, ref_row))


def hgrn_constants():
    import numpy as np
    c = HGRN_CHUNK
    t = np.arange(c)[:, None]
    s = np.arange(c)[None, :]
    pair, tri = [], []
    for rev in (False, True):
        masks = [t == s]
        m = 1
        while m < c:
            same = (t // (2 * m)) == (s // (2 * m))
            t_query = ((t // m) % 2) == (0 if rev else 1)
            s_key = ((s // m) % 2) == (1 if rev else 0)
            masks.append(same & t_query & s_key)
            m *= 2
        pair.append(np.stack(masks))
        tri.append((s >= t) if rev else (s <= t))
    rows = np.stack([np.broadcast_to(((t // n) % 2) == 1, (c, LANES)) for n in (1, 2, 4)])
    return (jnp.asarray(np.stack(pair), jnp.int32), jnp.asarray(rows, jnp.int32),
            jnp.asarray(np.stack(tri), BF16))


def _neg_abs(x):
    return pltpu.bitcast(pltpu.bitcast(x, jnp.uint32) | jnp.uint32(0x80000000), F32)


def _nt(x, y):
    return lax.dot_general(x, y, (((1,), (1,)), ((), ())), preferred_element_type=F32)


def _hgrn_chunk(q_ref, v_ref, lf_ref, o_ref, st_scr, a_scr, p_scr, oi_scr, rows3, col, slot, consts, *, rev):
    pair_ref, rows_ref, tri_ref = consts
    r_prev, r0, r_next = rows3
    d = 1 if rev else 0
    odd_block = lambda n: rows_ref[{1: 0, 2: 1, 4: 2}[n]] != 0
    c = HGRN_CHUNK
    cols = slice(col * HEAD_DIM, (col + 1) * HEAD_DIM)

    pv = jnp.dot(p_scr[slot], v_ref[0, pl.ds(r_prev, c), cols], preferred_element_type=F32)
    a2 = _cumsum_issue(lf_ref[0, pl.ds(r_next, c), cols], tri_ref[d])

    qb = q_ref[0, pl.ds(r0, c), cols]
    vb = v_ref[0, pl.ds(r0, c), cols]
    f = jnp.exp2(lf_ref[0, pl.ds(r0, c), cols])
    k = 1.0 - f
    q = qb.astype(F32)
    sc0 = jnp.sum(q * k, axis=1, keepdims=True)
    k3 = k.reshape(c // SUBLANES, SUBLANES, HEAD_DIM)
    k_other = pltpu.roll(k3, SUBLANES - 1 if rev else 1, 1).reshape(c, HEAD_DIM)
    sc1 = jnp.sum(q * f * k_other, axis=1, keepdims=True)
    a = a_scr[slot]
    edge = 0 if rev else c - 1
    a_end = a[edge:edge + 1, :]

    st = st_scr[col]
    o = jnp.dot((q * jnp.exp2(a)).astype(BF16), st.astype(BF16), preferred_element_type=F32)
    kdec = (k * jnp.exp2(a_end - a)).astype(BF16)
    st_new = lax.dot_general(kdec, vb, (((0,), (0,)), ((), ())), preferred_element_type=F32)
    st_decayed = st * jnp.transpose(jnp.broadcast_to(jnp.exp2(a_end), (HEAD_DIM, HEAD_DIM)))

    nstrip = c // SUBLANES
    strip = lambda x, i: x[i * SUBLANES:(i + 1) * SUBLANES]
    p = [None] * nstrip

    def apply(pending):
        level, scores, rows, shifts = pending
        for j, i in enumerate(rows):
            mask = pair_ref[d, level, i * SUBLANES:(i + 1) * SUBLANES, :] != 0
            s = strip(scores, j)
            if shifts is not None and shifts[j]:
                s = pltpu.roll(s, shifts[j], 1)
            p[i] = jnp.where(mask, s, 0.0 if p[i] is None else p[i])

    query_first = rev

    def halves(m):
        out = []
        for g in range(c // (2 * m)):
            first = slice(g * 2 * m, g * 2 * m + m)
            second = slice(g * 2 * m + m, (g + 1) * 2 * m)
            out.append((first, second) if query_first else (second, first))
        return out

    def factor(m):
        ref = _boundary_row(a, m, rev, odd_block)
        if m < SUBLANES:
            return jnp.exp2(_neg_abs(a - ref))
        parts = []
        for qs, ks in halves(m):
            dq, dk = a[qs] - ref[qs], ref[ks] - a[ks]
            parts += [dq, dk] if query_first else [dk, dq]
        return jnp.exp2(jnp.concatenate(parts, axis=0))

    every = list(range(nstrip))
    pending = [(0, sc0, every, None), (1, sc1, every, None)]
    nlevel = pair_ref.shape[1] - 1
    m = 2
    e = factor(m)
    for level in range(2, nlevel + 1):
        e_next = factor(2 * m) if level < nlevel else None
        if m < SUBLANES:
            qk = jnp.where(odd_block(m), k, q) if query_first else jnp.where(odd_block(m), q, k)
            z = (qk * e).astype(BF16)
            sc = _nt(z, z)
            rows, shifts = every, None
        else:
            qparts, kparts, rows, shifts = [], [], [], []
            for g, (qs, ks) in enumerate(halves(m)):
                qparts.append(q[qs] * e[qs])
                kparts.append(k[ks] * e[ks])
                rows += list(range(qs.start // SUBLANES, qs.stop // SUBLANES))
                shifts += [(ks.start - g * m) % c] * (m // SUBLANES)
            sc = _nt(jnp.concatenate(qparts, axis=0).astype(BF16), jnp.concatenate(kparts, axis=0).astype(BF16))
            sc = jnp.concatenate([sc, jnp.zeros((c // 2, c - c // 2), F32)], axis=1)
        e = e_next
        yield
        if level == 2:
            o_prev = oi_scr[slot]
            oi_scr[slot] = o
            st_scr[col] = st_decayed + st_new
        pending.append((level, sc, rows, shifts))
        if len(pending) > HGRN_LAG:
            apply(pending.pop(0))
        m *= 2
    a_scr[slot] = a2[:, :LANES] + a2[:, LANES:]
    o_ref[0, pl.ds(r_prev, c), cols] = (o_prev + pv).astype(o_ref.dtype)
    yield
    for item in pending:
        apply(item)
    p_scr[slot] = jnp.concatenate(p, axis=0).astype(BF16)


def _cumsum_issue(lf, tri):
    hi = lf.astype(BF16)
    lo = (lf - hi.astype(F32)).astype(BF16)
    return jnp.dot(tri, jnp.concatenate([hi, lo], axis=1), preferred_element_type=F32)


def _interleave(chains):
    chains = list(chains)
    while chains:
        for ch in list(chains):
            try:
                next(ch)
            except StopIteration:
                chains.remove(ch)


def _hgrn_kernel(qf_ref, vf_ref, lff_ref, qb_ref, vb_ref, lfb_ref, pair_ref, rows_ref, tri_ref,
                 of_ref, ob_ref, sf_scr, sb_scr, a_scr, p_scr, oi_scr):
    @pl.when(pl.program_id(2) == 0)
    def _():
        sf_scr[...] = jnp.zeros(sf_scr.shape, F32)
        sb_scr[...] = jnp.zeros(sb_scr.shape, F32)

    c = HGRN_CHUNK
    consts = (pair_ref, rows_ref, tri_ref)
    n = qf_ref.shape[1] // c
    heads = qf_ref.shape[2] // HEAD_DIM
    row = lambda idx: pl.multiple_of(idx * c, c)
    dirs = ((qf_ref, vf_ref, lff_ref, of_ref, sf_scr, False), (qb_ref, vb_ref, lfb_ref, ob_ref, sb_scr, True))
    first = lambda rev: n - 1 if rev else 0
    last = lambda rev: 0 if rev else n - 1

    for hd in range(heads):
        cols = slice(hd * HEAD_DIM, (hd + 1) * HEAD_DIM)
        for d, (_, _, lf_ref, _, _, rev) in enumerate(dirs):
            a2 = _cumsum_issue(lf_ref[0, pl.ds(first(rev) * c, c), cols], tri_ref[d])
            a_scr[2 * hd + d] = a2[:, :LANES] + a2[:, LANES:]
    p_scr[...] = jnp.zeros(p_scr.shape, BF16)
    oi_scr[...] = jnp.zeros(oi_scr.shape, F32)

    def body(ci, carry):
        idx = {False: (jnp.maximum(ci - 1, 0), ci, jnp.minimum(ci + 1, n - 1)),
               True: (jnp.minimum(n - ci, n - 1), n - 1 - ci, jnp.maximum(n - 2 - ci, 0))}
        chains = []
        for hd in range(heads):
            for d, (q_ref, v_ref, lf_ref, o_ref, st_scr, rev) in enumerate(dirs):
                rows3 = tuple(row(i) for i in idx[rev])
                chains.append(_hgrn_chunk(q_ref, v_ref, lf_ref, o_ref, st_scr, a_scr, p_scr, oi_scr, rows3, hd,
                                          2 * hd + d, consts, rev=rev))
        _interleave(chains)
        return carry

    lax.fori_loop(0, n, body, 0)

    for hd in range(heads):
        cols = slice(hd * HEAD_DIM, (hd + 1) * HEAD_DIM)
        for d, (_, v_ref, _, o_ref, _, rev) in enumerate(dirs):
            rws = pl.ds(last(rev) * c, c)
            pv = jnp.dot(p_scr[2 * hd + d], v_ref[0, rws, cols], preferred_element_type=F32)
            o_ref[0, rws, cols] = (oi_scr[2 * hd + d] + pv).astype(o_ref.dtype)


HGRN_HEADS_PER_STEP = 4


def hgrn2_bidir(q, v, logf, ts=2048, hps=HGRN_HEADS_PER_STEP):
    b, s, hk = q.shape
    ts = min(ts, s)
    nt = s // ts
    ng = hk // (hps * HEAD_DIM)
    blk = (1, ts, hps * HEAD_DIM)
    fwd = lambda bi, h, i: (bi, i, h)
    bwd = lambda bi, h, i: (bi, nt - 1 - i, h)
    out = jax.ShapeDtypeStruct((b, s, hk), BF16)
    state = pltpu.VMEM((hps, HEAD_DIM, HEAD_DIM), F32)
    consts = hgrn_constants()
    whole = lambda x: pl.BlockSpec(x.shape, lambda bi, h, i: (0,) * x.ndim)
    return pl.pallas_call(
        _hgrn_kernel,
        grid=(b, ng, nt),
        in_specs=[pl.BlockSpec(blk, fwd), pl.BlockSpec(blk, fwd), pl.BlockSpec(blk, fwd),
                  pl.BlockSpec(blk, bwd), pl.BlockSpec(blk, bwd),
                  pl.BlockSpec(blk, lambda bi, h, i: (bi, nt - 1 - i, ng + h))] + [whole(x) for x in consts],
        out_specs=[pl.BlockSpec(blk, fwd), pl.BlockSpec(blk, bwd)],
        out_shape=[out, out],
        scratch_shapes=[state, state, pltpu.VMEM((2 * hps, HGRN_CHUNK, HEAD_DIM), F32),
                        pltpu.VMEM((2 * hps, HGRN_CHUNK, HGRN_CHUNK), BF16),
                        pltpu.VMEM((2 * hps, HGRN_CHUNK, HEAD_DIM), F32)],
        compiler_params=_cparams(("parallel", "parallel", "arbitrary")),
        name="hgrn2",
    )(q, v, logf, q, v, logf, *consts)


def _hgrn_post_kernel(of_ref, ob_ref, gr_ref, gn_ref, o_ref):
    gn = gn_ref[...]
    for hd in range(o_ref.shape[1] // HEAD_DIM):
        sl = slice(hd * HEAD_DIM, (hd + 1) * HEAD_DIM)
        o = _rms(of_ref[:, sl].astype(F32) + ob_ref[:, sl].astype(F32), gn)
        o_ref[:, sl] = (o * gr_ref[:, sl].astype(F32)).astype(o_ref.dtype)


def hgrn_post(o_fw, o_bw, g_silu, g_norm, tm=512):
    m, d = o_fw.shape
    tm = min(tm, m)
    row = pl.BlockSpec((tm, d), lambda i: (i, 0))
    return pl.pallas_call(
        _hgrn_post_kernel,
        grid=(m // tm,),
        in_specs=[row, row, row, pl.BlockSpec((1, HEAD_DIM), lambda i: (0, 0))],
        out_specs=row,
        out_shape=jax.ShapeDtypeStruct((m, d), BF16),
        compiler_params=_cparams(("parallel",)),
        name="hgrn_post",
    )(o_fw, o_bw, g_silu, g_norm.reshape(1, HEAD_DIM))


def _merge_kernel(att_ref, or_ref, woa_ref, woh_ref, sa_ref, sh_ref, o_ref):
    tm = o_ref.shape[0]
    rows_per = min(MM_ROWS, tm)
    woa, woh = woa_ref[...].astype(BF16), woh_ref[...].astype(BF16)
    for r in range(0, tm, rows_per):
        rs = slice(r, r + rows_per)
        ya = jnp.dot(att_ref[rs, :], woa, preferred_element_type=F32)
        yh = jnp.dot(or_ref[rs, :], woh, preferred_element_type=F32)
        o_ref[rs, :] = (sa_ref[rs, :].astype(F32) * ya + sh_ref[rs, :].astype(F32) * yh).astype(o_ref.dtype)


def gated_merge(att, o_r, w_oa, w_oh, gates, tm=1024, tn=512):
    m, d = att.shape
    tm = min(tm, m)
    nj = d // tn
    row = pl.BlockSpec((tm, d), lambda i, j: (i, 0))
    wsp = pl.BlockSpec((d, tn), lambda i, j: (0, j))
    return pl.pallas_call(
        _merge_kernel,
        grid=(m // tm, nj),
        in_specs=[row, row, wsp, wsp,
                  pl.BlockSpec((tm, tn), lambda i, j: (i, j)),
                  pl.BlockSpec((tm, tn), lambda i, j: (i, nj + j))],
        out_specs=pl.BlockSpec((tm, tn), lambda i, j: (i, j)),
        out_shape=jax.ShapeDtypeStruct((m, d), BF16),
        compiler_params=_cparams(("parallel", "arbitrary")),
        name="gated_merge",
    )(att, o_r, w_oa, w_oh, gates, gates)


def _resid_proj_kernel(x_ref, a_ref, w_ref, o_ref):
    tm = o_ref.shape[0]
    rows_per = min(MM_ROWS, tm)
    w = w_ref[...].astype(BF16)
    for r in range(0, tm, rows_per):
        rs = slice(r, r + rows_per)
        o_ref[rs, :] = x_ref[rs, :] + jnp.dot(a_ref[rs, :], w, preferred_element_type=F32)


def resid_proj(x, a, w, tm=1024, tn=512):
    m, d = x.shape
    tm = min(tm, m)
    return pl.pallas_call(
        _resid_proj_kernel,
        grid=(m // tm, d // tn),
        in_specs=[pl.BlockSpec((tm, tn), lambda i, j: (i, j)), pl.BlockSpec((tm, a.shape[1]), lambda i, j: (i, 0)),
                  pl.BlockSpec((a.shape[1], tn), lambda i, j: (0, j))],
        out_specs=pl.BlockSpec((tm, tn), lambda i, j: (i, j)),
        out_shape=jax.ShapeDtypeStruct((m, d), F32),
        compiler_params=_cparams(("parallel", "arbitrary")),
        name="out_proj",
    )(x, a, w)


def _mlp_kernel(x_ref, g_ref, wu_ref, wd_ref, o_ref, h_scr):
    @pl.when(pl.program_id(1) == 0)
    def _():
        x = x_ref[...]
        h_scr[...] = _rms(x, g_ref[...]).astype(h_scr.dtype)
        o_ref[...] = x

    tm = o_ref.shape[0]
    rows_per = min(MM_ROWS, tm)
    wu, wd = wu_ref[...].astype(BF16), wd_ref[...].astype(BF16)
    for r in range(0, tm, rows_per):
        rs = slice(r, r + rows_per)
        u = jnp.maximum(jnp.dot(h_scr[rs, :], wu, preferred_element_type=F32), 0.0)
        o_ref[rs, :] += jnp.dot((u * u).astype(BF16), wd, preferred_element_type=F32)


def mlp_block(x, gain, w_up, w_down, tm=1024, tf=512):
    m, d = x.shape
    ff = w_up.shape[1]
    tm = min(tm, m)
    return pl.pallas_call(
        _mlp_kernel,
        grid=(m // tm, ff // tf),
        in_specs=[pl.BlockSpec((tm, d), lambda i, f: (i, 0)), pl.BlockSpec((1, d), lambda i, f: (0, 0)),
                  pl.BlockSpec((d, tf), lambda i, f: (0, f)), pl.BlockSpec((tf, d), lambda i, f: (f, 0))],
        out_specs=pl.BlockSpec((tm, d), lambda i, f: (i, 0)),
        out_shape=jax.ShapeDtypeStruct((m, d), F32),
        scratch_shapes=[pltpu.VMEM((tm, d), BF16)],
        compiler_params=_cparams(("parallel", "arbitrary")),
        name="mlp",
    )(x, gain.reshape(1, d), w_up, w_down)


PLE_ROWS = 512


def _ple_kernel(x_ref, g_ref, wg_ref, p_ref, wp_ref, gf_ref, o_ref):
    tm = o_ref.shape[0]
    units = [slice(r, r + min(PLE_ROWS, tm)) for r in range(0, tm, min(PLE_ROWS, tm))]

    def issue(rs):
        x = x_ref[rs, :]
        h = _rms(x, g_ref[...]).astype(BF16)
        logit = jnp.dot(h, wg_ref[...], preferred_element_type=F32)
        emb = jnp.dot(p_ref[rs, :].astype(BF16), wp_ref[...], preferred_element_type=F32)
        return x, logit, emb

    def finish(rs, x, logit, emb):
        o_ref[rs, :] = _rms(x + _sigmoid(logit) * emb, gf_ref[...])

    prev = issue(units[0])
    for u in range(1, len(units)):
        cur = issue(units[u])
        finish(units[u - 1], *prev)
        prev = cur
    finish(units[-1], *prev)


def ple_final(x, gain, w_gate, p, w_p, g_final, tm=512):
    m, d = x.shape
    c = p.shape[1]
    tm = min(tm, m)
    const = lambda i: (0, 0)
    return pl.pallas_call(
        _ple_kernel,
        grid=(m // tm,),
        in_specs=[pl.BlockSpec((tm, d), lambda i: (i, 0)), pl.BlockSpec((1, d), const),
                  pl.BlockSpec((d, d), const), pl.BlockSpec((tm, c), lambda i: (i, 0)),
                  pl.BlockSpec((c, d), const), pl.BlockSpec((1, d), const)],
        out_specs=pl.BlockSpec((tm, d), lambda i: (i, 0)),
        out_shape=jax.ShapeDtypeStruct((m, d), F32),
        compiler_params=_cparams(("parallel",)),
        name="ple_final",
    )(x, gain.reshape(1, d), w_gate, p, w_p, g_final.reshape(1, d))


def kernel(x, p, g_mix, w_in, g_q, g_k, w_o_attn, hgrn_lb, g_hgrn, w_o_hgrn, w_out, g_mlp, w_up, w_down,
           g_ple, w_ple_gate, w_ple, g_final):
    b, s, d = x.shape
    m = b * s
    depth = w_in.shape[0]
    attn_q = N_Q_HEADS * HEAD_DIM
    attn_kv = N_KV_HEADS * HEAD_DIM
    hk = HGRN_HEADS * HEAD_DIM
    rope = rope_tables(s)
    lb_all = jnp.cumsum(jax.nn.softmax(hgrn_lb.astype(F32), axis=0), axis=0)

    xf = x.reshape(m, d)
    for i in range(depth):
        w = w_in[i]
        h = rmsnorm_bf16(xf, g_mix[i])
        c0 = 0
        q_a = proj_qk(h, w, c0, attn_q, g_q[i], rope, s, HEAD_DIM ** -0.5 * LOG2_E); c0 += attn_q
        k_a = proj_qk(h, w, c0, attn_kv, g_k[i], rope, s, 1.0); c0 += attn_kv
        v_a = proj_plain(h, w, c0, attn_kv, None); c0 += attn_kv
        q_r = proj_plain(h, w, c0, hk, "silu"); c0 += hk
        logf = proj_logf(h, w, c0, 2 * hk, lb_all[i]); c0 += 2 * hk
        i_r = proj_plain(h, w, c0, hk, None); c0 += hk
        g_r = proj_plain(h, w, c0, hk, "silu"); c0 += hk
        gates = proj_plain(h, w, c0, 2 * d, "sigmoid"); c0 += 2 * d

        att = gqa_attention(q_a.reshape(b, s, attn_q), k_a.reshape(b, s, attn_kv), v_a.reshape(b, s, attn_kv))
        o_fw, o_bw = hgrn2_bidir(q_r.reshape(b, s, hk), i_r.reshape(b, s, hk), logf.reshape(b, s, 2 * hk))
        o_r = hgrn_post(o_fw.reshape(m, hk), o_bw.reshape(m, hk), g_r, g_hgrn[i])
        mixed = gated_merge(att.reshape(m, attn_q), o_r, w_o_attn[i].astype(BF16), w_o_hgrn[i].astype(BF16), gates)
        xf = resid_proj(xf, mixed, w_out[i].astype(BF16))
        xf = mlp_block(xf, g_mlp[i], w_up[i].astype(BF16), w_down[i])
        assert depth == 1
        xf = ple_final(xf, g_ple[i], w_ple_gate[i].astype(BF16), p[i].reshape(m, -1), w_ple[i].astype(BF16),
                       g_final)
    return xf.reshape(b, s, d)
```

```python
import functools

import jax
import jax.numpy as jnp
from jax import lax
from jax.experimental import pallas as pl
from jax.experimental.pallas import tpu as pltpu

F32 = jnp.float32
BF16 = jnp.bfloat16

EPS = 1e-6
LOG2_E = 1.4426950408889634
HEAD_DIM = 128
N_Q_HEADS = 16
N_KV_HEADS = 4
GQA_GROUPS = N_Q_HEADS // N_KV_HEADS
GRID_W = 64
ROPE_THETA = 10000.0
HGRN_HEADS = 16
LANES = 128
SUBLANES = 8
VMEM_LIMIT = 56 * 1024 * 1024


def _cparams(sem):
    return pltpu.CompilerParams(dimension_semantics=sem, vmem_limit_bytes=VMEM_LIMIT)


def _sigmoid(x):
    return 1.0 / (1.0 + jnp.exp(-x))


def _rms(x, gain):
    ms = jnp.mean(x * x, axis=-1, keepdims=True)
    return x * lax.rsqrt(ms + EPS) * gain


def _rmsnorm_kernel(x_ref, g_ref, o_ref):
    o_ref[...] = _rms(x_ref[...], g_ref[...]).astype(o_ref.dtype)


def rmsnorm_bf16(x, gain, tm=512):
    m, d = x.shape
    tm = min(tm, m)
    return pl.pallas_call(
        _rmsnorm_kernel,
        grid=(m // tm,),
        in_specs=[pl.BlockSpec((tm, d), lambda i: (i, 0)), pl.BlockSpec((1, d), lambda i: (0, 0))],
        out_specs=pl.BlockSpec((tm, d), lambda i: (i, 0)),
        out_shape=jax.ShapeDtypeStruct((m, d), BF16),
        compiler_params=_cparams(("parallel",)),
        name="rmsnorm",
    )(x, gain.reshape(1, d))


QK_SUB = 2 * HEAD_DIM


MM_ROWS = 512


def _subtiled_matmul(h_ref, w_ref, epilogue):
    tm = h_ref.shape[0]
    rows_per = min(MM_ROWS, tm)
    units = [(slice(r, r + rows_per), slice(c0, c0 + QK_SUB))
             for c0 in range(0, w_ref.shape[1], QK_SUB) for r in range(0, tm, rows_per)]
    weights = {}

    def matmul(u):
        rs, cs = u
        if cs.start not in weights:
            weights.clear()
            weights[cs.start] = w_ref[:, cs].astype(BF16)
        return jnp.dot(h_ref[rs, :], weights[cs.start], preferred_element_type=F32)

    prev = matmul(units[0])
    for j in range(1, len(units)):
        cur = matmul(units[j])
        epilogue(prev, *units[j - 1])
        prev = cur
    epilogue(prev, *units[-1])


def _stationary_weights(w_ref, wb_scr, rope_order=False):
    @pl.when(pl.program_id(1) == 0)
    def _():
        w = w_ref[...]
        if rope_order:
            tn = w.shape[1]
            quarter = (lax.broadcasted_iota(jnp.int32, w.shape, 1) // (HEAD_DIM // 4)) % 4
            w = jnp.where(quarter == 1, pltpu.roll(w, tn - HEAD_DIM // 4, 1),
                          jnp.where(quarter == 2, pltpu.roll(w, HEAD_DIM // 4, 1), w))
        wb_scr[...] = w.astype(wb_scr.dtype)


def _proj_plain_kernel(h_ref, w_ref, o_ref, wb_scr, *, act):
    def epilogue(acc, rs, cs):
        if act == "silu":
            acc = acc * _sigmoid(acc)
        elif act == "sigmoid":
            acc = _sigmoid(acc)
        o_ref[rs, cs] = acc.astype(o_ref.dtype)

    _stationary_weights(w_ref, wb_scr)
    _subtiled_matmul(h_ref, wb_scr, epilogue)


def _proj_logf_kernel(h_ref, w_ref, lb_ref, o_ref, wb_scr):
    def epilogue(acc, rs, cs):
        lb = lb_ref[:, cs]
        o_ref[rs, cs] = jnp.log(lb + (1.0 - lb) * _sigmoid(acc)) * LOG2_E

    _stationary_weights(w_ref, wb_scr)
    _subtiled_matmul(h_ref, wb_scr, epilogue)


def _proj_qk_kernel(h_ref, w_ref, cg_ref, sg_ref, o_ref, wb_scr):
    ones = jnp.ones((HEAD_DIM, HEAD_DIM), BF16)
    _stationary_weights(w_ref, wb_scr, rope_order=True)

    def epilogue(acc, rs, cs):
        cg, sg = cg_ref[rs, :], sg_ref[rs, :]
        for hd in range(QK_SUB // HEAD_DIM):
            x = acc[:, hd * HEAD_DIM:(hd + 1) * HEAD_DIM]
            ssq = jnp.dot((x * x).astype(BF16), ones, preferred_element_type=F32)
            r = lax.rsqrt(ssq * (1.0 / HEAD_DIM) + EPS)
            y = r * (x * cg + pltpu.roll(x, HEAD_DIM // 2, 1) * sg)
            c0 = cs.start + hd * HEAD_DIM
            o_ref[rs, c0:c0 + HEAD_DIM] = y.astype(o_ref.dtype)

    _subtiled_matmul(h_ref, wb_scr, epilogue)


def _proj_call(kernel, h, w, col0, ncols, out_dtype, extra=(), extra_specs=(), tm=1024, tn=1024):
    m, d = h.shape
    tm = min(tm, m)
    tn = min(tn, ncols)
    assert col0 % tn == 0 and ncols % tn == 0 and m % tm == 0
    jb = col0 // tn
    return pl.pallas_call(
        kernel,
        grid=(ncols // tn, m // tm),
        in_specs=[pl.BlockSpec((tm, d), lambda j, i: (i, 0)),
                  pl.BlockSpec((d, tn), lambda j, i: (0, jb + j))] + list(extra_specs),
        out_specs=pl.BlockSpec((tm, tn), lambda j, i: (i, j)),
        out_shape=jax.ShapeDtypeStruct((m, ncols), out_dtype),
        scratch_shapes=[pltpu.VMEM((d, tn), BF16)],
        compiler_params=_cparams(("parallel", "arbitrary")),
        name="in_proj",
    )(h, w, *extra)


def proj_plain(h, w, col0, ncols, act, out_dtype=BF16):
    return _proj_call(functools.partial(_proj_plain_kernel, act=act), h, w, col0, ncols, out_dtype)


def proj_logf(h, w, col0, ncols, lb, tn=1024):
    return _proj_call(_proj_logf_kernel, h, w, col0, ncols, F32, extra=(lb.reshape(1, ncols),),
                      extra_specs=(pl.BlockSpec((1, tn), lambda j, i: (0, j)),), tn=tn)


def proj_qk(h, w, col0, ncols, gain, rope, seq, scale, tm=1024, tn=1024):
    cos, sin = rope
    gp = gain.astype(F32)[jnp.array(ROPE_PERM)]
    cg = cos * (gp * scale)[None, :]
    sg = sin * (jnp.roll(gp, HEAD_DIM // 2) * scale)[None, :]
    tm = min(tm, seq)
    tn = min(tn, ncols)
    nsb = seq // tm
    tab = pl.BlockSpec((tm, HEAD_DIM), lambda j, i: (i % nsb, 0))
    return _proj_call(_proj_qk_kernel, h, w, col0, ncols, BF16, extra=(cg, sg), extra_specs=(tab, tab),
                      tm=tm, tn=tn)


ROPE_PERM = tuple(list(range(0, 32)) + list(range(64, 96)) + list(range(32, 64)) + list(range(96, 128)))


def rope_tables(seq):
    half = HEAD_DIM // 2
    t = jnp.arange(seq, dtype=jnp.int32)
    row = (t // GRID_W).astype(F32)
    col = (t % GRID_W).astype(F32)
    inv_freq = ROPE_THETA ** (-jnp.arange(0, half, 2, dtype=F32) / half)
    ang = jnp.concatenate([row[:, None] * inv_freq[None, :], col[:, None] * inv_freq[None, :]], axis=-1)
    cos = jnp.concatenate([jnp.cos(ang), jnp.cos(ang)], axis=-1)
    sin = jnp.concatenate([-jnp.sin(ang), jnp.sin(ang)], axis=-1)
    return cos, sin


ATTN_TQ = 128
ATTN_TKC = 8192


def _attn_kernel(q0_ref, qa_ref, qb_ref, k_ref, v_ref, o_ref, s_scr, mrun_scr, mcur_scr, acc_scr, *, tkc):
    t = pl.program_id(2)
    nt = pl.num_programs(2)
    tq = qa_ref.shape[1]
    nch = k_ref.shape[1] // tkc
    nsl = tkc // LANES
    ones = jnp.ones((tkc, LANES), BF16)

    def stack(q_ref):
        return jnp.concatenate([q_ref[0, :, g * HEAD_DIM:(g + 1) * HEAD_DIM] for g in range(GQA_GROUPS)], axis=0)

    def pass1(qs, c, slot):
        off = pl.multiple_of(c * tkc, tkc)
        s = lax.dot_general(qs, k_ref[0, pl.ds(off, tkc), :], (((1,), (1,)), ((), ())),
                            preferred_element_type=F32)
        s_scr[slot, c] = s
        m = s[:, :LANES]
        for i in range(1, nsl):
            m = jnp.maximum(m, s[:, i * LANES:(i + 1) * LANES])
        mrun_scr[...] = jnp.maximum(mrun_scr[...], m)

    def pass2(c, slot):
        off = pl.multiple_of(c * tkc, tkc)
        vext = jnp.concatenate([v_ref[0, pl.ds(off, tkc), :], ones], axis=1)
        m = mcur_scr[...]
        p = jnp.concatenate([jnp.exp2(s_scr[slot, c, :, i * LANES:(i + 1) * LANES] - m) for i in range(nsl)],
                            axis=1)
        acc_scr[...] += jnp.dot(p.astype(BF16), vext, preferred_element_type=F32)

    def begin_pass1():
        mrun_scr[...] = jnp.full(mrun_scr.shape, -jnp.inf, F32)

    def end_pass1():
        mcur_scr[...] = jnp.broadcast_to(jnp.max(mrun_scr[...], axis=1, keepdims=True), mcur_scr.shape)
        acc_scr[...] = jnp.zeros(acc_scr.shape, F32)

    def emit(half):
        o = acc_scr[:, :HEAD_DIM] / acc_scr[:, HEAD_DIM:]
        for g in range(GQA_GROUPS):
            o_ref[0, half * tq:(half + 1) * tq, g * HEAD_DIM:(g + 1) * HEAD_DIM] = (
                o[g * tq:(g + 1) * tq].astype(o_ref.dtype))

    @pl.when(t == 0)
    def _():
        qs0 = stack(q0_ref)
        begin_pass1()

        def body_0(c, carry):
            pass1(qs0, c, 0)
            return carry

        lax.fori_loop(0, nch, body_0, 0)
        end_pass1()

    qs = stack(qa_ref)
    begin_pass1()

    def body_x(c, carry):
        pass1(qs, c, 1)
        pass2(c, 0)
        return carry

    lax.fori_loop(0, nch, body_x, 0)
    emit(0)
    end_pass1()

    @pl.when(t < nt - 1)
    def _():
        qs2 = stack(qb_ref)
        begin_pass1()

        def body_y(c, carry):
            pass1(qs2, c, 0)
            pass2(c, 1)
            return carry

        lax.fori_loop(0, nch, body_y, 0)
        emit(1)
        end_pass1()

    @pl.when(t == nt - 1)
    def _():
        def body_z(c, carry):
            pass2(c, 1)
            return carry

        lax.fori_loop(0, nch, body_z, 0)
        emit(1)


def gqa_attention(q, k, v, tq=ATTN_TQ, tkc=ATTN_TKC):
    b, s, _ = q.shape
    tkc = min(tkc, s)
    nq = s // tq
    assert nq % 2 == 0 and s % tkc == 0
    gw = GQA_GROUPS * HEAD_DIM
    rows = GQA_GROUPS * tq
    qblk = (1, tq, gw)
    kvblk = (1, s, HEAD_DIM)
    return pl.pallas_call(
        functools.partial(_attn_kernel, tkc=tkc),
        grid=(b, N_KV_HEADS, nq // 2),
        in_specs=[pl.BlockSpec(qblk, lambda bi, h, t: (bi, 0, h)),
                  pl.BlockSpec(qblk, lambda bi, h, t: (bi, 2 * t + 1, h)),
                  pl.BlockSpec(qblk, lambda bi, h, t: (bi, jnp.minimum(2 * t + 2, nq - 1), h)),
                  pl.BlockSpec(kvblk, lambda bi, h, t: (bi, 0, h)),
                  pl.BlockSpec(kvblk, lambda bi, h, t: (bi, 0, h))],
        out_specs=pl.BlockSpec((1, 2 * tq, gw), lambda bi, h, t: (bi, t, h)),
        out_shape=jax.ShapeDtypeStruct(q.shape, BF16),
        scratch_shapes=[pltpu.VMEM((2, s // tkc, rows, tkc), F32), pltpu.VMEM((rows, LANES), F32),
                        pltpu.VMEM((rows, LANES), F32), pltpu.VMEM((rows, 2 * HEAD_DIM), F32)],
        compiler_params=_cparams(("parallel", "parallel", "arbitrary")),
        name="gqa_attention",
    )(q, q, q, k, v)


HGRN_CHUNK = 128
HGRN_LAG = 2


def _group_row(x, group, row):
    c, n = x.shape
    xr = x.reshape(c // group, group, n)
    return jnp.broadcast_to(xr[:, row:row + 1, :], xr.shape).reshape(c, n)


def _boundary_row(x, m, rev, odd_block):
    ref_row = m if rev else m - 1
    if 2 * m >= SUBLANES:
        return _group_row(x, 2 * m, ref_row)
    assert 4 * m == SUBLANES
    return jnp.where(odd_block(2 * m), _group_row(x, SUBLANES, 2 * m + ref_row), _group_row(x, SUBLANES, ref_row))


def hgrn_constants():
    import numpy as np
    c = HGRN_CHUNK
    t = np.arange(c)[:, None]
    s = np.arange(c)[None, :]
    pair, tri = [], []
    for rev in (False, True):
        masks = [t == s]
        m = 1
        while m < c:
            same = (t // (2 * m)) == (s // (2 * m))
            t_query = ((t // m) % 2) == (0 if rev else 1)
            s_key = ((s // m) % 2) == (1 if rev else 0)
            masks.append(same & t_query & s_key)
            m *= 2
        pair.append(np.stack(masks))
        tri.append((s >= t) if rev else (s <= t))
    rows = np.stack([np.broadcast_to(((t // n) % 2) == 1, (c, LANES)) for n in (1, 2, 4)])
    return (jnp.asarray(np.stack(pair), jnp.int32), jnp.asarray(rows, jnp.int32),
            jnp.asarray(np.stack(tri), BF16))


def _neg_abs(x):
    return pltpu.bitcast(pltpu.bitcast(x, jnp.uint32) | jnp.uint32(0x80000000), F32)


def _nt(x, y):
    return lax.dot_general(x, y, (((1,), (1,)), ((), ())), preferred_element_type=F32)


def _hgrn_chunk(q_ref, v_ref, lf_ref, o_ref, st_scr, a_scr, p_scr, oi_scr, rows3, col, slot, consts, *, rev):
    pair_ref, rows_ref, tri_ref = consts
    r_prev, r0, r_next = rows3
    d = 1 if rev else 0
    odd_block = lambda n: rows_ref[{1: 0, 2: 1, 4: 2}[n]] != 0
    c = HGRN_CHUNK
    cols = slice(col * HEAD_DIM, (col + 1) * HEAD_DIM)

    pv = jnp.dot(p_scr[slot], v_ref[0, pl.ds(r_prev, c), cols], preferred_element_type=F32)
    a2 = _cumsum_issue(lf_ref[0, pl.ds(r_next, c), cols], tri_ref[d])

    qb = q_ref[0, pl.ds(r0, c), cols]
    vb = v_ref[0, pl.ds(r0, c), cols]
    f = jnp.exp2(lf_ref[0, pl.ds(r0, c), cols])
    k = 1.0 - f
    q = qb.astype(F32)
    sc0 = jnp.sum(q * k, axis=1, keepdims=True)
    k3 = k.reshape(c // SUBLANES, SUBLANES, HEAD_DIM)
    k_other = pltpu.roll(k3, SUBLANES - 1 if rev else 1, 1).reshape(c, HEAD_DIM)
    sc1 = jnp.sum(q * f * k_other, axis=1, keepdims=True)
    a = a_scr[slot]
    edge = 0 if rev else c - 1
    a_end = a[edge:edge + 1, :]

    st = st_scr[col]
    o = jnp.dot((q * jnp.exp2(a)).astype(BF16), st.astype(BF16), preferred_element_type=F32)
    kdec = (k * jnp.exp2(a_end - a)).astype(BF16)
    st_new = lax.dot_general(kdec, vb, (((0,), (0,)), ((), ())), preferred_element_type=F32)
    st_decayed = st * jnp.transpose(jnp.broadcast_to(jnp.exp2(a_end), (HEAD_DIM, HEAD_DIM)))

    nstrip = c // SUBLANES
    strip = lambda x, i: x[i * SUBLANES:(i + 1) * SUBLANES]
    p = [None] * nstrip

    def apply(pending):
        level, scores, rows, shifts = pending
        for j, i in enumerate(rows):
            mask = pair_ref[d, level, i * SUBLANES:(i + 1) * SUBLANES, :] != 0
            s = strip(scores, j)
            if shifts is not None and shifts[j]:
                s = pltpu.roll(s, shifts[j], 1)
            p[i] = jnp.where(mask, s, 0.0 if p[i] is None else p[i])

    query_first = rev

    def halves(m):
        out = []
        for g in range(c // (2 * m)):
            first = slice(g * 2 * m, g * 2 * m + m)
            second = slice(g * 2 * m + m, (g + 1) * 2 * m)
            out.append((first, second) if query_first else (second, first))
        return out

    def factor(m):
        ref = _boundary_row(a, m, rev, odd_block)
        if m < SUBLANES:
            return jnp.exp2(_neg_abs(a - ref))
        parts = []
        for qs, ks in halves(m):
            dq, dk = a[qs] - ref[qs], ref[ks] - a[ks]
            parts += [dq, dk] if query_first else [dk, dq]
        return jnp.exp2(jnp.concatenate(parts, axis=0))

    every = list(range(nstrip))
    pending = [(0, sc0, every, None), (1, sc1, every, None)]
    nlevel = pair_ref.shape[1] - 1
    m = 2
    e = factor(m)
    for level in range(2, nlevel + 1):
        e_next = factor(2 * m) if level < nlevel else None
        if m < SUBLANES:
            qk = jnp.where(odd_block(m), k, q) if query_first else jnp.where(odd_block(m), q, k)
            z = (qk * e).astype(BF16)
            sc = _nt(z, z)
            rows, shifts = every, None
        else:
            qparts, kparts, rows, shifts = [], [], [], []
            for g, (qs, ks) in enumerate(halves(m)):
                qparts.append(q[qs] * e[qs])
                kparts.append(k[ks] * e[ks])
                rows += list(range(qs.start // SUBLANES, qs.stop // SUBLANES))
                shifts += [(ks.start - g * m) % c] * (m // SUBLANES)
            sc = _nt(jnp.concatenate(qparts, axis=0).astype(BF16), jnp.concatenate(kparts, axis=0).astype(BF16))
            sc = jnp.concatenate([sc, jnp.zeros((c // 2, c - c // 2), F32)], axis=1)
        e = e_next
        yield
        if level == 2:
            o_prev = oi_scr[slot]
            oi_scr[slot] = o
            st_scr[col] = st_decayed + st_new
        pending.append((level, sc, rows, shifts))
        if len(pending) > HGRN_LAG:
            apply(pending.pop(0))
        m *= 2
    a_scr[slot] = a2[:, :LANES] + a2[:, LANES:]
    o_ref[0, pl.ds(r_prev, c), cols] = (o_prev + pv).astype(o_ref.dtype)
    yield
    for item in pending:
        apply(item)
    p_scr[slot] = jnp.concatenate(p, axis=0).astype(BF16)


def _cumsum_issue(lf, tri):
    hi = lf.astype(BF16)
    lo = (lf - hi.astype(F32)).astype(BF16)
    return jnp.dot(tri, jnp.concatenate([hi, lo], axis=1), preferred_element_type=F32)


def _interleave(chains):
    chains = list(chains)
    while chains:
        for ch in list(chains):
            try:
                next(ch)
            except StopIteration:
                chains.remove(ch)


def _hgrn_kernel(qf_ref, vf_ref, lff_ref, qb_ref, vb_ref, lfb_ref, pair_ref, rows_ref, tri_ref,
                 of_ref, ob_ref, sf_scr, sb_scr, a_scr, p_scr, oi_scr):
    @pl.when(pl.program_id(2) == 0)
    def _():
        sf_scr[...] = jnp.zeros(sf_scr.shape, F32)
        sb_scr[...] = jnp.zeros(sb_scr.shape, F32)

    c = HGRN_CHUNK
    consts = (pair_ref, rows_ref, tri_ref)
    n = qf_ref.shape[1] // c
    heads = qf_ref.shape[2] // HEAD_DIM
    row = lambda idx: pl.multiple_of(idx * c, c)
    dirs = ((qf_ref, vf_ref, lff_ref, of_ref, sf_scr, False), (qb_ref, vb_ref, lfb_ref, ob_ref, sb_scr, True))
    first = lambda rev: n - 1 if rev else 0
    last = lambda rev: 0 if rev else n - 1

    for hd in range(heads):
        cols = slice(hd * HEAD_DIM, (hd + 1) * HEAD_DIM)
        for d, (_, _, lf_ref, _, _, rev) in enumerate(dirs):
            a2 = _cumsum_issue(lf_ref[0, pl.ds(first(rev) * c, c), cols], tri_ref[d])
            a_scr[2 * hd + d] = a2[:, :LANES] + a2[:, LANES:]
    p_scr[...] = jnp.zeros(p_scr.shape, BF16)
    oi_scr[...] = jnp.zeros(oi_scr.shape, F32)

    def body(ci, carry):
        idx = {False: (jnp.maximum(ci - 1, 0), ci, jnp.minimum(ci + 1, n - 1)),
               True: (jnp.minimum(n - ci, n - 1), n - 1 - ci, jnp.maximum(n - 2 - ci, 0))}
        chains = []
        for hd in range(heads):
            for d, (q_ref, v_ref, lf_ref, o_ref, st_scr, rev) in enumerate(dirs):
                rows3 = tuple(row(i) for i in idx[rev])
                chains.append(_hgrn_chunk(q_ref, v_ref, lf_ref, o_ref, st_scr, a_scr, p_scr, oi_scr, rows3, hd,
                                          2 * hd + d, consts, rev=rev))
        _interleave(chains)
        return carry

    lax.fori_loop(0, n, body, 0)

    for hd in range(heads):
        cols = slice(hd * HEAD_DIM, (hd + 1) * HEAD_DIM)
        for d, (_, v_ref, _, o_ref, _, rev) in enumerate(dirs):
            rws = pl.ds(last(rev) * c, c)
            pv = jnp.dot(p_scr[2 * hd + d], v_ref[0, rws, cols], preferred_element_type=F32)
            o_ref[0, rws, cols] = (oi_scr[2 * hd + d] + pv).astype(o_ref.dtype)


HGRN_HEADS_PER_STEP = 4


def hgrn2_bidir(q, v, logf, ts=2048, hps=HGRN_HEADS_PER_STEP):
    b, s, hk = q.shape
    ts = min(ts, s)
    nt = s // ts
    ng = hk // (hps * HEAD_DIM)
    blk = (1, ts, hps * HEAD_DIM)
    fwd = lambda bi, h, i: (bi, i, h)
    bwd = lambda bi, h, i: (bi, nt - 1 - i, h)
    out = jax.ShapeDtypeStruct((b, s, hk), BF16)
    state = pltpu.VMEM((hps, HEAD_DIM, HEAD_DIM), F32)
    consts = hgrn_constants()
    whole = lambda x: pl.BlockSpec(x.shape, lambda bi, h, i: (0,) * x.ndim)
    return pl.pallas_call(
        _hgrn_kernel,
        grid=(b, ng, nt),
        in_specs=[pl.BlockSpec(blk, fwd), pl.BlockSpec(blk, fwd), pl.BlockSpec(blk, fwd),
                  pl.BlockSpec(blk, bwd), pl.BlockSpec(blk, bwd),
                  pl.BlockSpec(blk, lambda bi, h, i: (bi, nt - 1 - i, ng + h))] + [whole(x) for x in consts],
        out_specs=[pl.BlockSpec(blk, fwd), pl.BlockSpec(blk, bwd)],
        out_shape=[out, out],
        scratch_shapes=[state, state, pltpu.VMEM((2 * hps, HGRN_CHUNK, HEAD_DIM), F32),
                        pltpu.VMEM((2 * hps, HGRN_CHUNK, HGRN_CHUNK), BF16),
                        pltpu.VMEM((2 * hps, HGRN_CHUNK, HEAD_DIM), F32)],
        compiler_params=_cparams(("parallel", "parallel", "arbitrary")),
        name="hgrn2",
    )(q, v, logf, q, v, logf, *consts)


def _hgrn_post_kernel(of_ref, ob_ref, gr_ref, gn_ref, o_ref):
    gn = gn_ref[...]
    for hd in range(o_ref.shape[1] // HEAD_DIM):
        sl = slice(hd * HEAD_DIM, (hd + 1) * HEAD_DIM)
        o = _rms(of_ref[:, sl].astype(F32) + ob_ref[:, sl].astype(F32), gn)
        o_ref[:, sl] = (o * gr_ref[:, sl].astype(F32)).astype(o_ref.dtype)


def hgrn_post(o_fw, o_bw, g_silu, g_norm, tm=512):
    m, d = o_fw.shape
    tm = min(tm, m)
    row = pl.BlockSpec((tm, d), lambda i: (i, 0))
    return pl.pallas_call(
        _hgrn_post_kernel,
        grid=(m // tm,),
        in_specs=[row, row, row, pl.BlockSpec((1, HEAD_DIM), lambda i: (0, 0))],
        out_specs=row,
        out_shape=jax.ShapeDtypeStruct((m, d), BF16),
        compiler_params=_cparams(("parallel",)),
        name="hgrn_post",
    )(o_fw, o_bw, g_silu, g_norm.reshape(1, HEAD_DIM))


def _merge_kernel(att_ref, or_ref, woa_ref, woh_ref, sa_ref, sh_ref, o_ref):
    tm = o_ref.shape[0]
    rows_per = min(MM_ROWS, tm)
    woa, woh = woa_ref[...].astype(BF16), woh_ref[...].astype(BF16)
    for r in range(0, tm, rows_per):
        rs = slice(r, r + rows_per)
        ya = jnp.dot(att_ref[rs, :], woa, preferred_element_type=F32)
        yh = jnp.dot(or_ref[rs, :], woh, preferred_element_type=F32)
        o_ref[rs, :] = (sa_ref[rs, :].astype(F32) * ya + sh_ref[rs, :].astype(F32) * yh).astype(o_ref.dtype)


def gated_merge(att, o_r, w_oa, w_oh, gates, tm=1024, tn=512):
    m, d = att.shape
    tm = min(tm, m)
    nj = d // tn
    row = pl.BlockSpec((tm, d), lambda i, j: (i, 0))
    wsp = pl.BlockSpec((d, tn), lambda i, j: (0, j))
    return pl.pallas_call(
        _merge_kernel,
        grid=(m // tm, nj),
        in_specs=[row, row, wsp, wsp,
                  pl.BlockSpec((tm, tn), lambda i, j: (i, j)),
                  pl.BlockSpec((tm, tn), lambda i, j: (i, nj + j))],
        out_specs=pl.BlockSpec((tm, tn), lambda i, j: (i, j)),
        out_shape=jax.ShapeDtypeStruct((m, d), BF16),
        compiler_params=_cparams(("parallel", "arbitrary")),
        name="gated_merge",
    )(att, o_r, w_oa, w_oh, gates, gates)


def _resid_proj_kernel(x_ref, a_ref, w_ref, o_ref):
    tm = o_ref.shape[0]
    rows_per = min(MM_ROWS, tm)
    w = w_ref[...].astype(BF16)
    for r in range(0, tm, rows_per):
        rs = slice(r, r + rows_per)
        o_ref[rs, :] = x_ref[rs, :] + jnp.dot(a_ref[rs, :], w, preferred_element_type=F32)


def resid_proj(x, a, w, tm=1024, tn=512):
    m, d = x.shape
    tm = min(tm, m)
    return pl.pallas_call(
        _resid_proj_kernel,
        grid=(m // tm, d // tn),
        in_specs=[pl.BlockSpec((tm, tn), lambda i, j: (i, j)), pl.BlockSpec((tm, a.shape[1]), lambda i, j: (i, 0)),
                  pl.BlockSpec((a.shape[1], tn), lambda i, j: (0, j))],
        out_specs=pl.BlockSpec((tm, tn), lambda i, j: (i, j)),
        out_shape=jax.ShapeDtypeStruct((m, d), F32),
        compiler_params=_cparams(("parallel", "arbitrary")),
        name="out_proj",
    )(x, a, w)


def _mlp_kernel(x_ref, g_ref, wu_ref, wd_ref, o_ref, h_scr):
    @pl.when(pl.program_id(1) == 0)
    def _():
        x = x_ref[...]
        h_scr[...] = _rms(x, g_ref[...]).astype(h_scr.dtype)
        o_ref[...] = x

    tm = o_ref.shape[0]
    rows_per = min(MM_ROWS, tm)
    wu, wd = wu_ref[...].astype(BF16), wd_ref[...].astype(BF16)
    for r in range(0, tm, rows_per):
        rs = slice(r, r + rows_per)
        u = jnp.maximum(jnp.dot(h_scr[rs, :], wu, preferred_element_type=F32), 0.0)
        o_ref[rs, :] += jnp.dot((u * u).astype(BF16), wd, preferred_element_type=F32)


def mlp_block(x, gain, w_up, w_down, tm=1024, tf=512):
    m, d = x.shape
    ff = w_up.shape[1]
    tm = min(tm, m)
    return pl.pallas_call(
        _mlp_kernel,
        grid=(m // tm, ff // tf),
        in_specs=[pl.BlockSpec((tm, d), lambda i, f: (i, 0)), pl.BlockSpec((1, d), lambda i, f: (0, 0)),
                  pl.BlockSpec((d, tf), lambda i, f: (0, f)), pl.BlockSpec((tf, d), lambda i, f: (f, 0))],
        out_specs=pl.BlockSpec((tm, d), lambda i, f: (i, 0)),
        out_shape=jax.ShapeDtypeStruct((m, d), F32),
        scratch_shapes=[pltpu.VMEM((tm, d), BF16)],
        compiler_params=_cparams(("parallel", "arbitrary")),
        name="mlp",
    )(x, gain.reshape(1, d), w_up, w_down)


def _ple_kernel(x_ref, g_ref, wg_ref, p_ref, wp_ref, gf_ref, o_ref):
    x = x_ref[...]
    h = _rms(x, g_ref[...]).astype(BF16)
    gate = _sigmoid(jnp.dot(h, wg_ref[...], preferred_element_type=F32))
    emb = jnp.dot(p_ref[...].astype(BF16), wp_ref[...], preferred_element_type=F32)
    o_ref[...] = _rms(x + gate * emb, gf_ref[...])


def ple_final(x, gain, w_gate, p, w_p, g_final, tm=512):
    m, d = x.shape
    c = p.shape[1]
    tm = min(tm, m)
    const = lambda i: (0, 0)
    return pl.pallas_call(
        _ple_kernel,
        grid=(m // tm,),
        in_specs=[pl.BlockSpec((tm, d), lambda i: (i, 0)), pl.BlockSpec((1, d), const),
                  pl.BlockSpec((d, d), const), pl.BlockSpec((tm, c), lambda i: (i, 0)),
                  pl.BlockSpec((c, d), const), pl.BlockSpec((1, d), const)],
        out_specs=pl.BlockSpec((tm, d), lambda i: (i, 0)),
        out_shape=jax.ShapeDtypeStruct((m, d), F32),
        compiler_params=_cparams(("parallel",)),
        name="ple_final",
    )(x, gain.reshape(1, d), w_gate, p, w_p, g_final.reshape(1, d))


def kernel(x, p, g_mix, w_in, g_q, g_k, w_o_attn, hgrn_lb, g_hgrn, w_o_hgrn, w_out, g_mlp, w_up, w_down,
           g_ple, w_ple_gate, w_ple, g_final):
    b, s, d = x.shape
    m = b * s
    depth = w_in.shape[0]
    attn_q = N_Q_HEADS * HEAD_DIM
    attn_kv = N_KV_HEADS * HEAD_DIM
    hk = HGRN_HEADS * HEAD_DIM
    rope = rope_tables(s)
    lb_all = jnp.cumsum(jax.nn.softmax(hgrn_lb.astype(F32), axis=0), axis=0)

    xf = x.reshape(m, d)
    for i in range(depth):
        w = w_in[i]
        h = rmsnorm_bf16(xf, g_mix[i])
        c0 = 0
        q_a = proj_qk(h, w, c0, attn_q, g_q[i], rope, s, HEAD_DIM ** -0.5 * LOG2_E); c0 += attn_q
        k_a = proj_qk(h, w, c0, attn_kv, g_k[i], rope, s, 1.0); c0 += attn_kv
        v_a = proj_plain(h, w, c0, attn_kv, None); c0 += attn_kv
        q_r = proj_plain(h, w, c0, hk, "silu"); c0 += hk
        logf = proj_logf(h, w, c0, 2 * hk, lb_all[i]); c0 += 2 * hk
        i_r = proj_plain(h, w, c0, hk, None); c0 += hk
        g_r = proj_plain(h, w, c0, hk, "silu"); c0 += hk
        gates = proj_plain(h, w, c0, 2 * d, "sigmoid"); c0 += 2 * d

        att = gqa_attention(q_a.reshape(b, s, attn_q), k_a.reshape(b, s, attn_kv), v_a.reshape(b, s, attn_kv))
        o_fw, o_bw = hgrn2_bidir(q_r.reshape(b, s, hk), i_r.reshape(b, s, hk), logf.reshape(b, s, 2 * hk))
        o_r = hgrn_post(o_fw.reshape(m, hk), o_bw.reshape(m, hk), g_r, g_hgrn[i])
        mixed = gated_merge(att.reshape(m, attn_q), o_r, w_o_attn[i].astype(BF16), w_o_hgrn[i].astype(BF16), gates)
        xf = resid_proj(xf, mixed, w_out[i].astype(BF16))
        xf = mlp_block(xf, g_mlp[i], w_up[i].astype(BF16), w_down[i])
        assert depth == 1
        xf = ple_final(xf, g_ple[i], w_ple_gate[i].astype(BF16), p[i].reshape(m, -1), w_ple[i].astype(BF16),
                       g_final)
    return xf.reshape(b, s, d)
```

```python
import functools

import jax
import jax.numpy as jnp
from jax import lax
from jax.experimental import pallas as pl
from jax.experimental.pallas import tpu as pltpu

F32 = jnp.float32
BF16 = jnp.bfloat16

EPS = 1e-6
LOG2_E = 1.4426950408889634
HEAD_DIM = 128
N_Q_HEADS = 16
N_KV_HEADS = 4
GQA_GROUPS = N_Q_HEADS // N_KV_HEADS
GRID_W = 64
ROPE_THETA = 10000.0
HGRN_HEADS = 16
LANES = 128
SUBLANES = 8
VMEM_LIMIT = 56 * 1024 * 1024


def _cparams(sem):
    return pltpu.CompilerParams(dimension_semantics=sem, vmem_limit_bytes=VMEM_LIMIT)


def _sigmoid(x):
    return 1.0 / (1.0 + jnp.exp(-x))


def _rms(x, gain):
    ms = jnp.mean(x * x, axis=-1, keepdims=True)
    return x * lax.rsqrt(ms + EPS) * gain


def _rmsnorm_kernel(x_ref, g_ref, o_ref):
    o_ref[...] = _rms(x_ref[...], g_ref[...]).astype(o_ref.dtype)


def rmsnorm_bf16(x, gain, tm=1024):
    m, d = x.shape
    tm = min(tm, m)
    return pl.pallas_call(
        _rmsnorm_kernel,
        grid=(m // tm,),
        in_specs=[pl.BlockSpec((tm, d), lambda i: (i, 0)), pl.BlockSpec((1, d), lambda i: (0, 0))],
        out_specs=pl.BlockSpec((tm, d), lambda i: (i, 0)),
        out_shape=jax.ShapeDtypeStruct((m, d), BF16),
        compiler_params=_cparams(("parallel",)),
        name="rmsnorm",
    )(x, gain.reshape(1, d))


QK_SUB = 2 * HEAD_DIM


MM_ROWS = 512


def _subtiled_matmul(h_ref, w_ref, epilogue):
    tm = h_ref.shape[0]
    rows_per = min(MM_ROWS, tm)
    units = [(slice(r, r + rows_per), slice(c0, c0 + QK_SUB))
             for c0 in range(0, w_ref.shape[1], QK_SUB) for r in range(0, tm, rows_per)]
    weights = {}

    def matmul(u):
        rs, cs = u
        if cs.start not in weights:
            weights.clear()
            weights[cs.start] = w_ref[:, cs].astype(BF16)
        return jnp.dot(h_ref[rs, :], weights[cs.start], preferred_element_type=F32)

    prev = matmul(units[0])
    for j in range(1, len(units)):
        cur = matmul(units[j])
        epilogue(prev, *units[j - 1])
        prev = cur
    epilogue(prev, *units[-1])


def _stationary_weights(w_ref, wb_scr, rope_order=False):
    @pl.when(pl.program_id(1) == 0)
    def _():
        w = w_ref[...]
        if rope_order:
            tn = w.shape[1]
            quarter = (lax.broadcasted_iota(jnp.int32, w.shape, 1) // (HEAD_DIM // 4)) % 4
            w = jnp.where(quarter == 1, pltpu.roll(w, tn - HEAD_DIM // 4, 1),
                          jnp.where(quarter == 2, pltpu.roll(w, HEAD_DIM // 4, 1), w))
        wb_scr[...] = w.astype(wb_scr.dtype)


def _proj_plain_kernel(h_ref, w_ref, o_ref, wb_scr, *, act):
    def epilogue(acc, rs, cs):
        if act == "silu":
            acc = acc * _sigmoid(acc)
        elif act == "sigmoid":
            acc = _sigmoid(acc)
        o_ref[rs, cs] = acc.astype(o_ref.dtype)

    _stationary_weights(w_ref, wb_scr)
    _subtiled_matmul(h_ref, wb_scr, epilogue)


def _proj_logf_kernel(h_ref, w_ref, lb_ref, o_ref, wb_scr):
    def epilogue(acc, rs, cs):
        lb = lb_ref[:, cs]
        o_ref[rs, cs] = jnp.log(lb + (1.0 - lb) * _sigmoid(acc)) * LOG2_E

    _stationary_weights(w_ref, wb_scr)
    _subtiled_matmul(h_ref, wb_scr, epilogue)


def _proj_qk_kernel(h_ref, w_ref, cg_ref, sg_ref, o_ref, wb_scr):
    ones = jnp.ones((HEAD_DIM, HEAD_DIM), BF16)
    _stationary_weights(w_ref, wb_scr, rope_order=True)

    def epilogue(acc, rs, cs):
        cg, sg = cg_ref[rs, :], sg_ref[rs, :]
        for hd in range(QK_SUB // HEAD_DIM):
            x = acc[:, hd * HEAD_DIM:(hd + 1) * HEAD_DIM]
            ssq = jnp.dot((x * x).astype(BF16), ones, preferred_element_type=F32)
            r = lax.rsqrt(ssq * (1.0 / HEAD_DIM) + EPS)
            y = r * (x * cg + pltpu.roll(x, HEAD_DIM // 2, 1) * sg)
            c0 = cs.start + hd * HEAD_DIM
            o_ref[rs, c0:c0 + HEAD_DIM] = y.astype(o_ref.dtype)

    _subtiled_matmul(h_ref, wb_scr, epilogue)


def _proj_call(kernel, h, w, col0, ncols, out_dtype, extra=(), extra_specs=(), tm=1024, tn=1024):
    m, d = h.shape
    tm = min(tm, m)
    tn = min(tn, ncols)
    assert col0 % tn == 0 and ncols % tn == 0 and m % tm == 0
    jb = col0 // tn
    return pl.pallas_call(
        kernel,
        grid=(ncols // tn, m // tm),
        in_specs=[pl.BlockSpec((tm, d), lambda j, i: (i, 0)),
                  pl.BlockSpec((d, tn), lambda j, i: (0, jb + j))] + list(extra_specs),
        out_specs=pl.BlockSpec((tm, tn), lambda j, i: (i, j)),
        out_shape=jax.ShapeDtypeStruct((m, ncols), out_dtype),
        scratch_shapes=[pltpu.VMEM((d, tn), BF16)],
        compiler_params=_cparams(("parallel", "arbitrary")),
        name="in_proj",
    )(h, w, *extra)


def proj_plain(h, w, col0, ncols, act, out_dtype=BF16):
    return _proj_call(functools.partial(_proj_plain_kernel, act=act), h, w, col0, ncols, out_dtype)


def proj_logf(h, w, col0, ncols, lb, tn=1024):
    return _proj_call(_proj_logf_kernel, h, w, col0, ncols, F32, extra=(lb.reshape(1, ncols),),
                      extra_specs=(pl.BlockSpec((1, tn), lambda j, i: (0, j)),), tn=tn)


def proj_qk(h, w, col0, ncols, gain, rope, seq, scale, tm=1024, tn=1024):
    cos, sin = rope
    gp = gain.astype(F32)[jnp.array(ROPE_PERM)]
    cg = cos * (gp * scale)[None, :]
    sg = sin * (jnp.roll(gp, HEAD_DIM // 2) * scale)[None, :]
    tm = min(tm, seq)
    tn = min(tn, ncols)
    nsb = seq // tm
    tab = pl.BlockSpec((tm, HEAD_DIM), lambda j, i: (i % nsb, 0))
    return _proj_call(_proj_qk_kernel, h, w, col0, ncols, BF16, extra=(cg, sg), extra_specs=(tab, tab),
                      tm=tm, tn=tn)


ROPE_PERM = tuple(list(range(0, 32)) + list(range(64, 96)) + list(range(32, 64)) + list(range(96, 128)))


def rope_tables(seq):
    half = HEAD_DIM // 2
    t = jnp.arange(seq, dtype=jnp.int32)
    row = (t // GRID_W).astype(F32)
    col = (t % GRID_W).astype(F32)
    inv_freq = ROPE_THETA ** (-jnp.arange(0, half, 2, dtype=F32) / half)
    ang = jnp.concatenate([row[:, None] * inv_freq[None, :], col[:, None] * inv_freq[None, :]], axis=-1)
    cos = jnp.concatenate([jnp.cos(ang), jnp.cos(ang)], axis=-1)
    sin = jnp.concatenate([-jnp.sin(ang), jnp.sin(ang)], axis=-1)
    return cos, sin


ATTN_TQ = 128
ATTN_TKC = 8192


def _attn_kernel(q0_ref, qa_ref, qb_ref, k_ref, v_ref, o_ref, s_scr, mrun_scr, mcur_scr, acc_scr, *, tkc):
    t = pl.program_id(2)
    nt = pl.num_programs(2)
    tq = qa_ref.shape[1]
    nch = k_ref.shape[1] // tkc
    nsl = tkc // LANES
    ones = jnp.ones((tkc, LANES), BF16)

    def stack(q_ref):
        return jnp.concatenate([q_ref[0, :, g * HEAD_DIM:(g + 1) * HEAD_DIM] for g in range(GQA_GROUPS)], axis=0)

    def pass1(qs, c, slot):
        off = pl.multiple_of(c * tkc, tkc)
        s = lax.dot_general(qs, k_ref[0, pl.ds(off, tkc), :], (((1,), (1,)), ((), ())),
                            preferred_element_type=F32)
        s_scr[slot, c] = s
        m = s[:, :LANES]
        for i in range(1, nsl):
            m = jnp.maximum(m, s[:, i * LANES:(i + 1) * LANES])
        mrun_scr[...] = jnp.maximum(mrun_scr[...], m)

    def pass2(c, slot):
        off = pl.multiple_of(c * tkc, tkc)
        vext = jnp.concatenate([v_ref[0, pl.ds(off, tkc), :], ones], axis=1)
        m = mcur_scr[...]
        p = jnp.concatenate([jnp.exp2(s_scr[slot, c, :, i * LANES:(i + 1) * LANES] - m) for i in range(nsl)],
                            axis=1)
        acc_scr[...] += jnp.dot(p.astype(BF16), vext, preferred_element_type=F32)

    def begin_pass1():
        mrun_scr[...] = jnp.full(mrun_scr.shape, -jnp.inf, F32)

    def end_pass1():
        mcur_scr[...] = jnp.broadcast_to(jnp.max(mrun_scr[...], axis=1, keepdims=True), mcur_scr.shape)
        acc_scr[...] = jnp.zeros(acc_scr.shape, F32)

    def emit(half):
        o = acc_scr[:, :HEAD_DIM] / acc_scr[:, HEAD_DIM:]
        for g in range(GQA_GROUPS):
            o_ref[0, half * tq:(half + 1) * tq, g * HEAD_DIM:(g + 1) * HEAD_DIM] = (
                o[g * tq:(g + 1) * tq].astype(o_ref.dtype))

    @pl.when(t == 0)
    def _():
        qs0 = stack(q0_ref)
        begin_pass1()

        def body_0(c, carry):
            pass1(qs0, c, 0)
            return carry

        lax.fori_loop(0, nch, body_0, 0)
        end_pass1()

    qs = stack(qa_ref)
    begin_pass1()

    def body_x(c, carry):
        pass1(qs, c, 1)
        pass2(c, 0)
        return carry

    lax.fori_loop(0, nch, body_x, 0)
    emit(0)
    end_pass1()

    @pl.when(t < nt - 1)
    def _():
        qs2 = stack(qb_ref)
        begin_pass1()

        def body_y(c, carry):
            pass1(qs2, c, 0)
            pass2(c, 1)
            return carry

        lax.fori_loop(0, nch, body_y, 0)
        emit(1)
        end_pass1()

    @pl.when(t == nt - 1)
    def _():
        def body_z(c, carry):
            pass2(c, 1)
            return carry

        lax.fori_loop(0, nch, body_z, 0)
        emit(1)


def gqa_attention(q, k, v, tq=ATTN_TQ, tkc=ATTN_TKC):
    b, s, _ = q.shape
    tkc = min(tkc, s)
    nq = s // tq
    assert nq % 2 == 0 and s % tkc == 0
    gw = GQA_GROUPS * HEAD_DIM
    rows = GQA_GROUPS * tq
    qblk = (1, tq, gw)
    kvblk = (1, s, HEAD_DIM)
    return pl.pallas_call(
        functools.partial(_attn_kernel, tkc=tkc),
        grid=(b, N_KV_HEADS, nq // 2),
        in_specs=[pl.BlockSpec(qblk, lambda bi, h, t: (bi, 0, h)),
                  pl.BlockSpec(qblk, lambda bi, h, t: (bi, 2 * t + 1, h)),
                  pl.BlockSpec(qblk, lambda bi, h, t: (bi, jnp.minimum(2 * t + 2, nq - 1), h)),
                  pl.BlockSpec(kvblk, lambda bi, h, t: (bi, 0, h)),
                  pl.BlockSpec(kvblk, lambda bi, h, t: (bi, 0, h))],
        out_specs=pl.BlockSpec((1, 2 * tq, gw), lambda bi, h, t: (bi, t, h)),
        out_shape=jax.ShapeDtypeStruct(q.shape, BF16),
        scratch_shapes=[pltpu.VMEM((2, s // tkc, rows, tkc), F32), pltpu.VMEM((rows, LANES), F32),
                        pltpu.VMEM((rows, LANES), F32), pltpu.VMEM((rows, 2 * HEAD_DIM), F32)],
        compiler_params=_cparams(("parallel", "parallel", "arbitrary")),
        name="gqa_attention",
    )(q, q, q, k, v)


HGRN_CHUNK = 128
HGRN_LAG = 2


def _group_row(x, group, row):
    c, n = x.shape
    xr = x.reshape(c // group, group, n)
    return jnp.broadcast_to(xr[:, row:row + 1, :], xr.shape).reshape(c, n)


def _boundary_row(x, m, rev, odd_block):
    ref_row = m if rev else m - 1
    if 2 * m >= SUBLANES:
        return _group_row(x, 2 * m, ref_row)
    assert 4 * m == SUBLANES
    return jnp.where(odd_block(2 * m), _group_row(x, SUBLANES, 2 * m + ref_row), _group_row(x, SUBLANES, ref_row))


def hgrn_constants():
    import numpy as np
    c = HGRN_CHUNK
    t = np.arange(c)[:, None]
    s = np.arange(c)[None, :]
    pair, tri = [], []
    for rev in (False, True):
        masks = [t == s]
        m = 1
        while m < c:
            same = (t // (2 * m)) == (s // (2 * m))
            t_query = ((t // m) % 2) == (0 if rev else 1)
            s_key = ((s // m) % 2) == (1 if rev else 0)
            masks.append(same & t_query & s_key)
            m *= 2
        pair.append(np.stack(masks))
        tri.append((s >= t) if rev else (s <= t))
    rows = np.stack([np.broadcast_to(((t // n) % 2) == 1, (c, LANES)) for n in (1, 2, 4)])
    return (jnp.asarray(np.stack(pair), jnp.int32), jnp.asarray(rows, jnp.int32),
            jnp.asarray(np.stack(tri), BF16))


def _neg_abs(x):
    return pltpu.bitcast(pltpu.bitcast(x, jnp.uint32) | jnp.uint32(0x80000000), F32)


def _nt(x, y):
    return lax.dot_general(x, y, (((1,), (1,)), ((), ())), preferred_element_type=F32)


def _hgrn_chunk(q_ref, v_ref, lf_ref, o_ref, st_scr, a_scr, p_scr, oi_scr, rows3, col, slot, consts, *, rev):
    pair_ref, rows_ref, tri_ref = consts
    r_prev, r0, r_next = rows3
    d = 1 if rev else 0
    odd_block = lambda n: rows_ref[{1: 0, 2: 1, 4: 2}[n]] != 0
    c = HGRN_CHUNK
    cols = slice(col * HEAD_DIM, (col + 1) * HEAD_DIM)

    pv = jnp.dot(p_scr[slot], v_ref[0, pl.ds(r_prev, c), cols], preferred_element_type=F32)
    a2 = _cumsum_issue(lf_ref[0, pl.ds(r_next, c), cols], tri_ref[d])

    qb = q_ref[0, pl.ds(r0, c), cols]
    vb = v_ref[0, pl.ds(r0, c), cols]
    f = jnp.exp2(lf_ref[0, pl.ds(r0, c), cols])
    k = 1.0 - f
    q = qb.astype(F32)
    sc0 = jnp.sum(q * k, axis=1, keepdims=True)
    k3 = k.reshape(c // SUBLANES, SUBLANES, HEAD_DIM)
    k_other = pltpu.roll(k3, SUBLANES - 1 if rev else 1, 1).reshape(c, HEAD_DIM)
    sc1 = jnp.sum(q * f * k_other, axis=1, keepdims=True)
    a = a_scr[slot]
    edge = 0 if rev else c - 1
    a_end = a[edge:edge + 1, :]

    st = st_scr[col]
    o = jnp.dot((q * jnp.exp2(a)).astype(BF16), st.astype(BF16), preferred_element_type=F32)
    kdec = (k * jnp.exp2(a_end - a)).astype(BF16)
    st_new = lax.dot_general(kdec, vb, (((0,), (0,)), ((), ())), preferred_element_type=F32)
    st_decayed = st * jnp.transpose(jnp.broadcast_to(jnp.exp2(a_end), (HEAD_DIM, HEAD_DIM)))

    nstrip = c // SUBLANES
    strip = lambda x, i: x[i * SUBLANES:(i + 1) * SUBLANES]
    p = [None] * nstrip

    def apply(pending):
        level, scores, rows, shifts = pending
        for j, i in enumerate(rows):
            mask = pair_ref[d, level, i * SUBLANES:(i + 1) * SUBLANES, :] != 0
            s = strip(scores, j)
            if shifts is not None and shifts[j]:
                s = pltpu.roll(s, shifts[j], 1)
            p[i] = jnp.where(mask, s, 0.0 if p[i] is None else p[i])

    query_first = rev

    def halves(m):
        out = []
        for g in range(c // (2 * m)):
            first = slice(g * 2 * m, g * 2 * m + m)
            second = slice(g * 2 * m + m, (g + 1) * 2 * m)
            out.append((first, second) if query_first else (second, first))
        return out

    def factor(m):
        ref = _boundary_row(a, m, rev, odd_block)
        if m < SUBLANES:
            return jnp.exp2(_neg_abs(a - ref))
        parts = []
        for qs, ks in halves(m):
            dq, dk = a[qs] - ref[qs], ref[ks] - a[ks]
            parts += [dq, dk] if query_first else [dk, dq]
        return jnp.exp2(jnp.concatenate(parts, axis=0))

    every = list(range(nstrip))
    pending = [(0, sc0, every, None), (1, sc1, every, None)]
    nlevel = pair_ref.shape[1] - 1
    m = 2
    e = factor(m)
    for level in range(2, nlevel + 1):
        e_next = factor(2 * m) if level < nlevel else None
        if m < SUBLANES:
            qk = jnp.where(odd_block(m), k, q) if query_first else jnp.where(odd_block(m), q, k)
            z = (qk * e).astype(BF16)
            sc = _nt(z, z)
            rows, shifts = every, None
        else:
            qparts, kparts, rows, shifts = [], [], [], []
            for g, (qs, ks) in enumerate(halves(m)):
                qparts.append(q[qs] * e[qs])
                kparts.append(k[ks] * e[ks])
                rows += list(range(qs.start // SUBLANES, qs.stop // SUBLANES))
                shifts += [(ks.start - g * m) % c] * (m // SUBLANES)
            sc = _nt(jnp.concatenate(qparts, axis=0).astype(BF16), jnp.concatenate(kparts, axis=0).astype(BF16))
            sc = jnp.concatenate([sc, jnp.zeros((c // 2, c - c // 2), F32)], axis=1)
        e = e_next
        yield
        if level == 2:
            o_prev = oi_scr[slot]
            oi_scr[slot] = o
            st_scr[col] = st_decayed + st_new
        pending.append((level, sc, rows, shifts))
        if len(pending) > HGRN_LAG:
            apply(pending.pop(0))
        m *= 2
    a_scr[slot] = a2[:, :LANES] + a2[:, LANES:]
    o_ref[0, pl.ds(r_prev, c), cols] = (o_prev + pv).astype(o_ref.dtype)
    yield
    for item in pending:
        apply(item)
    p_scr[slot] = jnp.concatenate(p, axis=0).astype(BF16)


def _cumsum_issue(lf, tri):
    hi = lf.astype(BF16)
    lo = (lf - hi.astype(F32)).astype(BF16)
    return jnp.dot(tri, jnp.concatenate([hi, lo], axis=1), preferred_element_type=F32)


def _interleave(chains):
    chains = list(chains)
    while chains:
        for ch in list(chains):
            try:
                next(ch)
            except StopIteration:
                chains.remove(ch)


def _hgrn_kernel(qf_ref, vf_ref, lff_ref, qb_ref, vb_ref, lfb_ref, pair_ref, rows_ref, tri_ref,
                 of_ref, ob_ref, sf_scr, sb_scr, a_scr, p_scr, oi_scr):
    @pl.when(pl.program_id(2) == 0)
    def _():
        sf_scr[...] = jnp.zeros(sf_scr.shape, F32)
        sb_scr[...] = jnp.zeros(sb_scr.shape, F32)

    c = HGRN_CHUNK
    consts = (pair_ref, rows_ref, tri_ref)
    n = qf_ref.shape[1] // c
    heads = qf_ref.shape[2] // HEAD_DIM
    row = lambda idx: pl.multiple_of(idx * c, c)
    dirs = ((qf_ref, vf_ref, lff_ref, of_ref, sf_scr, False), (qb_ref, vb_ref, lfb_ref, ob_ref, sb_scr, True))
    first = lambda rev: n - 1 if rev else 0
    last = lambda rev: 0 if rev else n - 1

    for hd in range(heads):
        cols = slice(hd * HEAD_DIM, (hd + 1) * HEAD_DIM)
        for d, (_, _, lf_ref, _, _, rev) in enumerate(dirs):
            a2 = _cumsum_issue(lf_ref[0, pl.ds(first(rev) * c, c), cols], tri_ref[d])
            a_scr[2 * hd + d] = a2[:, :LANES] + a2[:, LANES:]
    p_scr[...] = jnp.zeros(p_scr.shape, BF16)
    oi_scr[...] = jnp.zeros(oi_scr.shape, F32)

    def body(ci, carry):
        idx = {False: (jnp.maximum(ci - 1, 0), ci, jnp.minimum(ci + 1, n - 1)),
               True: (jnp.minimum(n - ci, n - 1), n - 1 - ci, jnp.maximum(n - 2 - ci, 0))}
        chains = []
        for hd in range(heads):
            for d, (q_ref, v_ref, lf_ref, o_ref, st_scr, rev) in enumerate(dirs):
                rows3 = tuple(row(i) for i in idx[rev])
                chains.append(_hgrn_chunk(q_ref, v_ref, lf_ref, o_ref, st_scr, a_scr, p_scr, oi_scr, rows3, hd,
                                          2 * hd + d, consts, rev=rev))
        _interleave(chains)
        return carry

    lax.fori_loop(0, n, body, 0)

    for hd in range(heads):
        cols = slice(hd * HEAD_DIM, (hd + 1) * HEAD_DIM)
        for d, (_, v_ref, _, o_ref, _, rev) in enumerate(dirs):
            rws = pl.ds(last(rev) * c, c)
            pv = jnp.dot(p_scr[2 * hd + d], v_ref[0, rws, cols], preferred_element_type=F32)
            o_ref[0, rws, cols] = (oi_scr[2 * hd + d] + pv).astype(o_ref.dtype)


HGRN_HEADS_PER_STEP = 4


def hgrn2_bidir(q, v, logf, ts=2048, hps=HGRN_HEADS_PER_STEP):
    b, s, hk = q.shape
    ts = min(ts, s)
    nt = s // ts
    ng = hk // (hps * HEAD_DIM)
    blk = (1, ts, hps * HEAD_DIM)
    fwd = lambda bi, h, i: (bi, i, h)
    bwd = lambda bi, h, i: (bi, nt - 1 - i, h)
    out = jax.ShapeDtypeStruct((b, s, hk), BF16)
    state = pltpu.VMEM((hps, HEAD_DIM, HEAD_DIM), F32)
    consts = hgrn_constants()
    whole = lambda x: pl.BlockSpec(x.shape, lambda bi, h, i: (0,) * x.ndim)
    return pl.pallas_call(
        _hgrn_kernel,
        grid=(b, ng, nt),
        in_specs=[pl.BlockSpec(blk, fwd), pl.BlockSpec(blk, fwd), pl.BlockSpec(blk, fwd),
                  pl.BlockSpec(blk, bwd), pl.BlockSpec(blk, bwd),
                  pl.BlockSpec(blk, lambda bi, h, i: (bi, nt - 1 - i, ng + h))] + [whole(x) for x in consts],
        out_specs=[pl.BlockSpec(blk, fwd), pl.BlockSpec(blk, bwd)],
        out_shape=[out, out],
        scratch_shapes=[state, state, pltpu.VMEM((2 * hps, HGRN_CHUNK, HEAD_DIM), F32),
                        pltpu.VMEM((2 * hps, HGRN_CHUNK, HGRN_CHUNK), BF16),
                        pltpu.VMEM((2 * hps, HGRN_CHUNK, HEAD_DIM), F32)],
        compiler_params=_cparams(("parallel", "parallel", "arbitrary")),
        name="hgrn2",
    )(q, v, logf, q, v, logf, *consts)


def _hgrn_post_kernel(of_ref, ob_ref, gr_ref, gn_ref, o_ref):
    gn = gn_ref[...]
    for hd in range(o_ref.shape[1] // HEAD_DIM):
        sl = slice(hd * HEAD_DIM, (hd + 1) * HEAD_DIM)
        o = _rms(of_ref[:, sl].astype(F32) + ob_ref[:, sl].astype(F32), gn)
        o_ref[:, sl] = (o * gr_ref[:, sl].astype(F32)).astype(o_ref.dtype)


def hgrn_post(o_fw, o_bw, g_silu, g_norm, tm=1024):
    m, d = o_fw.shape
    tm = min(tm, m)
    row = pl.BlockSpec((tm, d), lambda i: (i, 0))
    return pl.pallas_call(
        _hgrn_post_kernel,
        grid=(m // tm,),
        in_specs=[row, row, row, pl.BlockSpec((1, HEAD_DIM), lambda i: (0, 0))],
        out_specs=row,
        out_shape=jax.ShapeDtypeStruct((m, d), BF16),
        compiler_params=_cparams(("parallel",)),
        name="hgrn_post",
    )(o_fw, o_bw, g_silu, g_norm.reshape(1, HEAD_DIM))


def _merge_kernel(att_ref, or_ref, woa_ref, woh_ref, sa_ref, sh_ref, o_ref):
    tm = o_ref.shape[0]
    rows_per = min(MM_ROWS, tm)
    woa, woh = woa_ref[...].astype(BF16), woh_ref[...].astype(BF16)
    for r in range(0, tm, rows_per):
        rs = slice(r, r + rows_per)
        ya = jnp.dot(att_ref[rs, :], woa, preferred_element_type=F32)
        yh = jnp.dot(or_ref[rs, :], woh, preferred_element_type=F32)
        o_ref[rs, :] = (sa_ref[rs, :].astype(F32) * ya + sh_ref[rs, :].astype(F32) * yh).astype(o_ref.dtype)


def gated_merge(att, o_r, w_oa, w_oh, gates, tm=1024, tn=1024):
    m, d = att.shape
    tm = min(tm, m)
    nj = d // tn
    row = pl.BlockSpec((tm, d), lambda i, j: (i, 0))
    wsp = pl.BlockSpec((d, tn), lambda i, j: (0, j))
    return pl.pallas_call(
        _merge_kernel,
        grid=(m // tm, nj),
        in_specs=[row, row, wsp, wsp,
                  pl.BlockSpec((tm, tn), lambda i, j: (i, j)),
                  pl.BlockSpec((tm, tn), lambda i, j: (i, nj + j))],
        out_specs=pl.BlockSpec((tm, tn), lambda i, j: (i, j)),
        out_shape=jax.ShapeDtypeStruct((m, d), BF16),
        compiler_params=_cparams(("parallel", "arbitrary")),
        name="gated_merge",
    )(att, o_r, w_oa, w_oh, gates, gates)


def _resid_proj_kernel(x_ref, a_ref, w_ref, o_ref):
    tm = o_ref.shape[0]
    rows_per = min(MM_ROWS, tm)
    w = w_ref[...].astype(BF16)
    for r in range(0, tm, rows_per):
        rs = slice(r, r + rows_per)
        o_ref[rs, :] = x_ref[rs, :] + jnp.dot(a_ref[rs, :], w, preferred_element_type=F32)


def resid_proj(x, a, w, tm=1024, tn=1024):
    m, d = x.shape
    tm = min(tm, m)
    return pl.pallas_call(
        _resid_proj_kernel,
        grid=(m // tm, d // tn),
        in_specs=[pl.BlockSpec((tm, tn), lambda i, j: (i, j)), pl.BlockSpec((tm, a.shape[1]), lambda i, j: (i, 0)),
                  pl.BlockSpec((a.shape[1], tn), lambda i, j: (0, j))],
        out_specs=pl.BlockSpec((tm, tn), lambda i, j: (i, j)),
        out_shape=jax.ShapeDtypeStruct((m, d), F32),
        compiler_params=_cparams(("parallel", "arbitrary")),
        name="out_proj",
    )(x, a, w)


def _mlp_kernel(x_ref, g_ref, wu_ref, wd_ref, o_ref, h_scr):
    @pl.when(pl.program_id(1) == 0)
    def _():
        x = x_ref[...]
        h_scr[...] = _rms(x, g_ref[...]).astype(h_scr.dtype)
        o_ref[...] = x

    tm = o_ref.shape[0]
    rows_per = min(MM_ROWS, tm)
    wu, wd = wu_ref[...].astype(BF16), wd_ref[...].astype(BF16)
    for r in range(0, tm, rows_per):
        rs = slice(r, r + rows_per)
        u = jnp.maximum(jnp.dot(h_scr[rs, :], wu, preferred_element_type=F32), 0.0)
        o_ref[rs, :] += jnp.dot((u * u).astype(BF16), wd, preferred_element_type=F32)


def mlp_block(x, gain, w_up, w_down, tm=1024, tf=512):
    m, d = x.shape
    ff = w_up.shape[1]
    tm = min(tm, m)
    return pl.pallas_call(
        _mlp_kernel,
        grid=(m // tm, ff // tf),
        in_specs=[pl.BlockSpec((tm, d), lambda i, f: (i, 0)), pl.BlockSpec((1, d), lambda i, f: (0, 0)),
                  pl.BlockSpec((d, tf), lambda i, f: (0, f)), pl.BlockSpec((tf, d), lambda i, f: (f, 0))],
        out_specs=pl.BlockSpec((tm, d), lambda i, f: (i, 0)),
        out_shape=jax.ShapeDtypeStruct((m, d), F32),
        scratch_shapes=[pltpu.VMEM((tm, d), BF16)],
        compiler_params=_cparams(("parallel", "arbitrary")),
        name="mlp",
    )(x, gain.reshape(1, d), w_up, w_down)


def _ple_kernel(x_ref, g_ref, wg_ref, p_ref, wp_ref, gf_ref, o_ref):
    x = x_ref[...]
    h = _rms(x, g_ref[...]).astype(BF16)
    gate = _sigmoid(jnp.dot(h, wg_ref[...], preferred_element_type=F32))
    emb = jnp.dot(p_ref[...].astype(BF16), wp_ref[...], preferred_element_type=F32)
    o_ref[...] = _rms(x + gate * emb, gf_ref[...])


def ple_final(x, gain, w_gate, p, w_p, g_final, tm=512):
    m, d = x.shape
    c = p.shape[1]
    tm = min(tm, m)
    const = lambda i: (0, 0)
    return pl.pallas_call(
        _ple_kernel,
        grid=(m // tm,),
        in_specs=[pl.BlockSpec((tm, d), lambda i: (i, 0)), pl.BlockSpec((1, d), const),
                  pl.BlockSpec((d, d), const), pl.BlockSpec((tm, c), lambda i: (i, 0)),
                  pl.BlockSpec((c, d), const), pl.BlockSpec((1, d), const)],
        out_specs=pl.BlockSpec((tm, d), lambda i: (i, 0)),
        out_shape=jax.ShapeDtypeStruct((m, d), F32),
        compiler_params=_cparams(("parallel",)),
        name="ple_final",
    )(x, gain.reshape(1, d), w_gate, p, w_p, g_final.reshape(1, d))


def kernel(x, p, g_mix, w_in, g_q, g_k, w_o_attn, hgrn_lb, g_hgrn, w_o_hgrn, w_out, g_mlp, w_up, w_down,
           g_ple, w_ple_gate, w_ple, g_final):
    b, s, d = x.shape
    m = b * s
    depth = w_in.shape[0]
    attn_q = N_Q_HEADS * HEAD_DIM
    attn_kv = N_KV_HEADS * HEAD_DIM
    hk = HGRN_HEADS * HEAD_DIM
    rope = rope_tables(s)
    lb_all = jnp.cumsum(jax.nn.softmax(hgrn_lb.astype(F32), axis=0), axis=0)

    xf = x.reshape(m, d)
    for i in range(depth):
        w = w_in[i]
        h = rmsnorm_bf16(xf, g_mix[i])
        c0 = 0
        q_a = proj_qk(h, w, c0, attn_q, g_q[i], rope, s, HEAD_DIM ** -0.5 * LOG2_E); c0 += attn_q
        k_a = proj_qk(h, w, c0, attn_kv, g_k[i], rope, s, 1.0); c0 += attn_kv
        v_a = proj_plain(h, w, c0, attn_kv, None); c0 += attn_kv
        q_r = proj_plain(h, w, c0, hk, "silu"); c0 += hk
        logf = proj_logf(h, w, c0, 2 * hk, lb_all[i]); c0 += 2 * hk
        i_r = proj_plain(h, w, c0, hk, None); c0 += hk
        g_r = proj_plain(h, w, c0, hk, "silu"); c0 += hk
        gates = proj_plain(h, w, c0, 2 * d, "sigmoid"); c0 += 2 * d

        att = gqa_attention(q_a.reshape(b, s, attn_q), k_a.reshape(b, s, attn_kv), v_a.reshape(b, s, attn_kv))
        o_fw, o_bw = hgrn2_bidir(q_r.reshape(b, s, hk), i_r.reshape(b, s, hk), logf.reshape(b, s, 2 * hk))
        o_r = hgrn_post(o_fw.reshape(m, hk), o_bw.reshape(m, hk), g_r, g_hgrn[i])
        mixed = gated_merge(att.reshape(m, attn_q), o_r, w_o_attn[i].astype(BF16), w_o_hgrn[i].astype(BF16), gates)
        xf = resid_proj(xf, mixed, w_out[i].astype(BF16))
        xf = mlp_block(xf, g_mlp[i], w_up[i].astype(BF16), w_down[i])
        assert depth == 1
        xf = ple_final(xf, g_ple[i], w_ple_gate[i].astype(BF16), p[i].reshape(m, -1), w_ple[i].astype(BF16),
                       g_final)
    return xf.reshape(b, s, d)
```

```python
import functools

import jax
import jax.numpy as jnp
from jax import lax
from jax.experimental import pallas as pl
from jax.experimental.pallas import tpu as pltpu

F32 = jnp.float32
BF16 = jnp.bfloat16

EPS = 1e-6
LOG2_E = 1.4426950408889634
HEAD_DIM = 128
N_Q_HEADS = 16
N_KV_HEADS = 4
GQA_GROUPS = N_Q_HEADS // N_KV_HEADS
GRID_W = 64
ROPE_THETA = 10000.0
HGRN_HEADS = 16
LANES = 128
SUBLANES = 8
VMEM_LIMIT = 56 * 1024 * 1024


def _cparams(sem):
    return pltpu.CompilerParams(dimension_semantics=sem, vmem_limit_bytes=VMEM_LIMIT)


def _sigmoid(x):
    return 1.0 / (1.0 + jnp.exp(-x))


def _rms(x, gain):
    ms = jnp.mean(x * x, axis=-1, keepdims=True)
    return x * lax.rsqrt(ms + EPS) * gain


def _rmsnorm_kernel(x_ref, g_ref, o_ref):
    o_ref[...] = _rms(x_ref[...], g_ref[...]).astype(o_ref.dtype)


def rmsnorm_bf16(x, gain, tm=1024):
    m, d = x.shape
    tm = min(tm, m)
    return pl.pallas_call(
        _rmsnorm_kernel,
        grid=(m // tm,),
        in_specs=[pl.BlockSpec((tm, d), lambda i: (i, 0)), pl.BlockSpec((1, d), lambda i: (0, 0))],
        out_specs=pl.BlockSpec((tm, d), lambda i: (i, 0)),
        out_shape=jax.ShapeDtypeStruct((m, d), BF16),
        compiler_params=_cparams(("parallel",)),
        name="rmsnorm",
    )(x, gain.reshape(1, d))


QK_SUB = 2 * HEAD_DIM


MM_ROWS = 512
PROJ_ROWS = 256


def _subtiled_matmul(h_ref, w_ref, epilogue):
    tm = h_ref.shape[0]
    rows_per = min(PROJ_ROWS, tm)
    units = [(slice(r, r + rows_per), slice(c0, c0 + QK_SUB))
             for c0 in range(0, w_ref.shape[1], QK_SUB) for r in range(0, tm, rows_per)]
    weights = {}

    def matmul(u):
        rs, cs = u
        if cs.start not in weights:
            weights.clear()
            weights[cs.start] = w_ref[:, cs].astype(BF16)
        return jnp.dot(h_ref[rs, :], weights[cs.start], preferred_element_type=F32)

    prev = matmul(units[0])
    for j in range(1, len(units)):
        cur = matmul(units[j])
        epilogue(prev, *units[j - 1])
        prev = cur
    epilogue(prev, *units[-1])


def _stationary_weights(w_ref, wb_scr, rope_order=False):
    @pl.when(pl.program_id(1) == 0)
    def _():
        w = w_ref[...]
        if rope_order:
            tn = w.shape[1]
            quarter = (lax.broadcasted_iota(jnp.int32, w.shape, 1) // (HEAD_DIM // 4)) % 4
            w = jnp.where(quarter == 1, pltpu.roll(w, tn - HEAD_DIM // 4, 1),
                          jnp.where(quarter == 2, pltpu.roll(w, HEAD_DIM // 4, 1), w))
        wb_scr[...] = w.astype(wb_scr.dtype)


def _proj_plain_kernel(h_ref, w_ref, o_ref, wb_scr, *, act):
    def epilogue(acc, rs, cs):
        if act == "silu":
            acc = acc * _sigmoid(acc)
        elif act == "sigmoid":
            acc = _sigmoid(acc)
        o_ref[rs, cs] = acc.astype(o_ref.dtype)

    _stationary_weights(w_ref, wb_scr)
    _subtiled_matmul(h_ref, wb_scr, epilogue)


def _proj_logf_kernel(h_ref, w_ref, lb_ref, o_ref, wb_scr):
    def epilogue(acc, rs, cs):
        lb = lb_ref[:, cs]
        o_ref[rs, cs] = jnp.log(lb + (1.0 - lb) * _sigmoid(acc)) * LOG2_E

    _stationary_weights(w_ref, wb_scr)
    _subtiled_matmul(h_ref, wb_scr, epilogue)


def _proj_qk_kernel(h_ref, w_ref, cg_ref, sg_ref, o_ref, wb_scr):
    ones = jnp.ones((HEAD_DIM, HEAD_DIM), BF16)
    _stationary_weights(w_ref, wb_scr, rope_order=True)

    def epilogue(acc, rs, cs):
        cg, sg = cg_ref[rs, :], sg_ref[rs, :]
        for hd in range(QK_SUB // HEAD_DIM):
            x = acc[:, hd * HEAD_DIM:(hd + 1) * HEAD_DIM]
            ssq = jnp.dot((x * x).astype(BF16), ones, preferred_element_type=F32)
            r = lax.rsqrt(ssq * (1.0 / HEAD_DIM) + EPS)
            y = r * (x * cg + pltpu.roll(x, HEAD_DIM // 2, 1) * sg)
            c0 = cs.start + hd * HEAD_DIM
            o_ref[rs, c0:c0 + HEAD_DIM] = y.astype(o_ref.dtype)

    _subtiled_matmul(h_ref, wb_scr, epilogue)


def _proj_call(kernel, h, w, col0, ncols, out_dtype, extra=(), extra_specs=(), tm=1024, tn=1024):
    m, d = h.shape
    tm = min(tm, m)
    tn = min(tn, ncols)
    assert col0 % tn == 0 and ncols % tn == 0 and m % tm == 0
    jb = col0 // tn
    return pl.pallas_call(
        kernel,
        grid=(ncols // tn, m // tm),
        in_specs=[pl.BlockSpec((tm, d), lambda j, i: (i, 0)),
                  pl.BlockSpec((d, tn), lambda j, i: (0, jb + j))] + list(extra_specs),
        out_specs=pl.BlockSpec((tm, tn), lambda j, i: (i, j)),
        out_shape=jax.ShapeDtypeStruct((m, ncols), out_dtype),
        scratch_shapes=[pltpu.VMEM((d, tn), BF16)],
        compiler_params=_cparams(("parallel", "arbitrary")),
        name="in_proj",
    )(h, w, *extra)


def proj_plain(h, w, col0, ncols, act, out_dtype=BF16):
    return _proj_call(functools.partial(_proj_plain_kernel, act=act), h, w, col0, ncols, out_dtype)


def proj_logf(h, w, col0, ncols, lb, tn=1024):
    return _proj_call(_proj_logf_kernel, h, w, col0, ncols, F32, extra=(lb.reshape(1, ncols),),
                      extra_specs=(pl.BlockSpec((1, tn), lambda j, i: (0, j)),), tn=tn)


def proj_qk(h, w, col0, ncols, gain, rope, seq, scale, tm=1024, tn=1024):
    cos, sin = rope
    gp = gain.astype(F32)[jnp.array(ROPE_PERM)]
    cg = cos * (gp * scale)[None, :]
    sg = sin * (jnp.roll(gp, HEAD_DIM // 2) * scale)[None, :]
    tm = min(tm, seq)
    tn = min(tn, ncols)
    nsb = seq // tm
    tab = pl.BlockSpec((tm, HEAD_DIM), lambda j, i: (i % nsb, 0))
    return _proj_call(_proj_qk_kernel, h, w, col0, ncols, BF16, extra=(cg, sg), extra_specs=(tab, tab),
                      tm=tm, tn=tn)


ROPE_PERM = tuple(list(range(0, 32)) + list(range(64, 96)) + list(range(32, 64)) + list(range(96, 128)))


def rope_tables(seq):
    half = HEAD_DIM // 2
    t = jnp.arange(seq, dtype=jnp.int32)
    row = (t // GRID_W).astype(F32)
    col = (t % GRID_W).astype(F32)
    inv_freq = ROPE_THETA ** (-jnp.arange(0, half, 2, dtype=F32) / half)
    ang = jnp.concatenate([row[:, None] * inv_freq[None, :], col[:, None] * inv_freq[None, :]], axis=-1)
    cos = jnp.concatenate([jnp.cos(ang), jnp.cos(ang)], axis=-1)
    sin = jnp.concatenate([-jnp.sin(ang), jnp.sin(ang)], axis=-1)
    return cos, sin


ATTN_TQ = 128
ATTN_TKC = 8192


def _attn_kernel(q0_ref, qa_ref, qb_ref, k_ref, v_ref, o_ref, s_scr, mrun_scr, mcur_scr, acc_scr, *, tkc):
    t = pl.program_id(2)
    nt = pl.num_programs(2)
    tq = qa_ref.shape[1]
    nch = k_ref.shape[1] // tkc
    nsl = tkc // LANES
    ones = jnp.ones((tkc, LANES), BF16)

    def stack(q_ref):
        return jnp.concatenate([q_ref[0, :, g * HEAD_DIM:(g + 1) * HEAD_DIM] for g in range(GQA_GROUPS)], axis=0)

    def pass1(qs, c, slot):
        off = pl.multiple_of(c * tkc, tkc)
        s = lax.dot_general(qs, k_ref[0, pl.ds(off, tkc), :], (((1,), (1,)), ((), ())),
                            preferred_element_type=F32)
        s_scr[slot, c] = s
        m = s[:, :LANES]
        for i in range(1, nsl):
            m = jnp.maximum(m, s[:, i * LANES:(i + 1) * LANES])
        mrun_scr[...] = jnp.maximum(mrun_scr[...], m)

    def pass2(c, slot):
        off = pl.multiple_of(c * tkc, tkc)
        vext = jnp.concatenate([v_ref[0, pl.ds(off, tkc), :], ones], axis=1)
        m = mcur_scr[...]
        p = jnp.concatenate([jnp.exp2(s_scr[slot, c, :, i * LANES:(i + 1) * LANES] - m) for i in range(nsl)],
                            axis=1)
        acc_scr[...] += jnp.dot(p.astype(BF16), vext, preferred_element_type=F32)

    def begin_pass1():
        mrun_scr[...] = jnp.full(mrun_scr.shape, -jnp.inf, F32)

    def end_pass1():
        mcur_scr[...] = jnp.broadcast_to(jnp.max(mrun_scr[...], axis=1, keepdims=True), mcur_scr.shape)
        acc_scr[...] = jnp.zeros(acc_scr.shape, F32)

    def emit(half):
        o = acc_scr[:, :HEAD_DIM] / acc_scr[:, HEAD_DIM:]
        for g in range(GQA_GROUPS):
            o_ref[0, half * tq:(half + 1) * tq, g * HEAD_DIM:(g + 1) * HEAD_DIM] = (
                o[g * tq:(g + 1) * tq].astype(o_ref.dtype))

    @pl.when(t == 0)
    def _():
        qs0 = stack(q0_ref)
        begin_pass1()

        def body_0(c, carry):
            pass1(qs0, c, 0)
            return carry

        lax.fori_loop(0, nch, body_0, 0)
        end_pass1()

    qs = stack(qa_ref)
    begin_pass1()

    def body_x(c, carry):
        pass1(qs, c, 1)
        pass2(c, 0)
        return carry

    lax.fori_loop(0, nch, body_x, 0)
    emit(0)
    end_pass1()

    @pl.when(t < nt - 1)
    def _():
        qs2 = stack(qb_ref)
        begin_pass1()

        def body_y(c, carry):
            pass1(qs2, c, 0)
            pass2(c, 1)
            return carry

        lax.fori_loop(0, nch, body_y, 0)
        emit(1)
        end_pass1()

    @pl.when(t == nt - 1)
    def _():
        def body_z(c, carry):
            pass2(c, 1)
            return carry

        lax.fori_loop(0, nch, body_z, 0)
        emit(1)


def gqa_attention(q, k, v, tq=ATTN_TQ, tkc=ATTN_TKC):
    b, s, _ = q.shape
    tkc = min(tkc, s)
    nq = s // tq
    assert nq % 2 == 0 and s % tkc == 0
    gw = GQA_GROUPS * HEAD_DIM
    rows = GQA_GROUPS * tq
    qblk = (1, tq, gw)
    kvblk = (1, s, HEAD_DIM)
    return pl.pallas_call(
        functools.partial(_attn_kernel, tkc=tkc),
        grid=(b, N_KV_HEADS, nq // 2),
        in_specs=[pl.BlockSpec(qblk, lambda bi, h, t: (bi, 0, h)),
                  pl.BlockSpec(qblk, lambda bi, h, t: (bi, 2 * t + 1, h)),
                  pl.BlockSpec(qblk, lambda bi, h, t: (bi, jnp.minimum(2 * t + 2, nq - 1), h)),
                  pl.BlockSpec(kvblk, lambda bi, h, t: (bi, 0, h)),
                  pl.BlockSpec(kvblk, lambda bi, h, t: (bi, 0, h))],
        out_specs=pl.BlockSpec((1, 2 * tq, gw), lambda bi, h, t: (bi, t, h)),
        out_shape=jax.ShapeDtypeStruct(q.shape, BF16),
        scratch_shapes=[pltpu.VMEM((2, s // tkc, rows, tkc), F32), pltpu.VMEM((rows, LANES), F32),
                        pltpu.VMEM((rows, LANES), F32), pltpu.VMEM((rows, 2 * HEAD_DIM), F32)],
        compiler_params=_cparams(("parallel", "parallel", "arbitrary")),
        name="gqa_attention",
    )(q, q, q, k, v)


HGRN_CHUNK = 128
HGRN_LAG = 2


def _group_row(x, group, row):
    c, n = x.shape
    xr = x.reshape(c // group, group, n)
    return jnp.broadcast_to(xr[:, row:row + 1, :], xr.shape).reshape(c, n)


def _boundary_row(x, m, rev, odd_block):
    ref_row = m if rev else m - 1
    if 2 * m >= SUBLANES:
        return _group_row(x, 2 * m, ref_row)
    assert 4 * m == SUBLANES
    return jnp.where(odd_block(2 * m), _group_row(x, SUBLANES, 2 * m + ref_row), _group_row(x, SUBLANES, ref_row))


def hgrn_constants():
    import numpy as np
    c = HGRN_CHUNK
    t = np.arange(c)[:, None]
    s = np.arange(c)[None, :]
    pair, tri = [], []
    for rev in (False, True):
        masks = [t == s]
        m = 1
        while m < c:
            same = (t // (2 * m)) == (s // (2 * m))
            t_query = ((t // m) % 2) == (0 if rev else 1)
            s_key = ((s // m) % 2) == (1 if rev else 0)
            masks.append(same & t_query & s_key)
            m *= 2
        pair.append(np.stack(masks))
        tri.append((s >= t) if rev else (s <= t))
    rows = np.stack([np.broadcast_to(((t // n) % 2) == 1, (c, LANES)) for n in (1, 2, 4)])
    return (jnp.asarray(np.stack(pair), jnp.int32), jnp.asarray(rows, jnp.int32),
            jnp.asarray(np.stack(tri), BF16))


def _neg_abs(x):
    return pltpu.bitcast(pltpu.bitcast(x, jnp.uint32) | jnp.uint32(0x80000000), F32)


def _nt(x, y):
    return lax.dot_general(x, y, (((1,), (1,)), ((), ())), preferred_element_type=F32)


def _hgrn_chunk(q_ref, v_ref, lf_ref, o_ref, st_scr, a_scr, p_scr, oi_scr, rows3, col, slot, consts, *, rev):
    pair_ref, rows_ref, tri_ref = consts
    r_prev, r0, r_next = rows3
    d = 1 if rev else 0
    odd_block = lambda n: rows_ref[{1: 0, 2: 1, 4: 2}[n]] != 0
    c = HGRN_CHUNK
    cols = slice(col * HEAD_DIM, (col + 1) * HEAD_DIM)

    pv = jnp.dot(p_scr[slot], v_ref[0, pl.ds(r_prev, c), cols], preferred_element_type=F32)
    a2 = _cumsum_issue(lf_ref[0, pl.ds(r_next, c), cols], tri_ref[d])

    qb = q_ref[0, pl.ds(r0, c), cols]
    vb = v_ref[0, pl.ds(r0, c), cols]
    f = jnp.exp2(lf_ref[0, pl.ds(r0, c), cols])
    k = 1.0 - f
    q = qb.astype(F32)
    sc0 = jnp.sum(q * k, axis=1, keepdims=True)
    k3 = k.reshape(c // SUBLANES, SUBLANES, HEAD_DIM)
    k_other = pltpu.roll(k3, SUBLANES - 1 if rev else 1, 1).reshape(c, HEAD_DIM)
    sc1 = jnp.sum(q * f * k_other, axis=1, keepdims=True)
    a = a_scr[slot]
    edge = 0 if rev else c - 1
    a_end = a[edge:edge + 1, :]

    st = st_scr[col]
    o = jnp.dot((q * jnp.exp2(a)).astype(BF16), st.astype(BF16), preferred_element_type=F32)
    kdec = (k * jnp.exp2(a_end - a)).astype(BF16)
    st_new = lax.dot_general(kdec, vb, (((0,), (0,)), ((), ())), preferred_element_type=F32)
    st_decayed = st * jnp.transpose(jnp.broadcast_to(jnp.exp2(a_end), (HEAD_DIM, HEAD_DIM)))

    nstrip = c // SUBLANES
    strip = lambda x, i: x[i * SUBLANES:(i + 1) * SUBLANES]
    p = [None] * nstrip

    def apply(pending):
        level, scores, rows, shifts = pending
        for j, i in enumerate(rows):
            mask = pair_ref[d, level, i * SUBLANES:(i + 1) * SUBLANES, :] != 0
            s = strip(scores, j)
            if shifts is not None and shifts[j]:
                s = pltpu.roll(s, shifts[j], 1)
            p[i] = jnp.where(mask, s, 0.0 if p[i] is None else p[i])

    query_first = rev

    def halves(m):
        out = []
        for g in range(c // (2 * m)):
            first = slice(g * 2 * m, g * 2 * m + m)
            second = slice(g * 2 * m + m, (g + 1) * 2 * m)
            out.append((first, second) if query_first else (second, first))
        return out

    def factor(m):
        ref = _boundary_row(a, m, rev, odd_block)
        if m < SUBLANES:
            return jnp.exp2(_neg_abs(a - ref))
        parts = []
        for qs, ks in halves(m):
            dq, dk = a[qs] - ref[qs], ref[ks] - a[ks]
            parts += [dq, dk] if query_first else [dk, dq]
        return jnp.exp2(jnp.concatenate(parts, axis=0))

    every = list(range(nstrip))
    pending = [(0, sc0, every, None), (1, sc1, every, None)]
    nlevel = pair_ref.shape[1] - 1
    m = 2
    e = factor(m)
    for level in range(2, nlevel + 1):
        e_next = factor(2 * m) if level < nlevel else None
        if m < SUBLANES:
            qk = jnp.where(odd_block(m), k, q) if query_first else jnp.where(odd_block(m), q, k)
            z = (qk * e).astype(BF16)
            sc = _nt(z, z)
            rows, shifts = every, None
        else:
            qparts, kparts, rows, shifts = [], [], [], []
            for g, (qs, ks) in enumerate(halves(m)):
                qparts.append(q[qs] * e[qs])
                kparts.append(k[ks] * e[ks])
                rows += list(range(qs.start // SUBLANES, qs.stop // SUBLANES))
                shifts += [(ks.start - g * m) % c] * (m // SUBLANES)
            sc = _nt(jnp.concatenate(qparts, axis=0).astype(BF16), jnp.concatenate(kparts, axis=0).astype(BF16))
            sc = jnp.concatenate([sc, jnp.zeros((c // 2, c - c // 2), F32)], axis=1)
        e = e_next
        yield
        if level == 2:
            o_prev = oi_scr[slot]
            oi_scr[slot] = o
            st_scr[col] = st_decayed + st_new
        pending.append((level, sc, rows, shifts))
        if len(pending) > HGRN_LAG:
            apply(pending.pop(0))
        m *= 2
    a_scr[slot] = a2[:, :LANES] + a2[:, LANES:]
    o_ref[0, pl.ds(r_prev, c), cols] = (o_prev + pv).astype(o_ref.dtype)
    yield
    for item in pending:
        apply(item)
    p_scr[slot] = jnp.concatenate(p, axis=0).astype(BF16)


def _cumsum_issue(lf, tri):
    hi = lf.astype(BF16)
    lo = (lf - hi.astype(F32)).astype(BF16)
    return jnp.dot(tri, jnp.concatenate([hi, lo], axis=1), preferred_element_type=F32)


def _interleave(chains):
    chains = list(chains)
    while chains:
        for ch in list(chains):
            try:
                next(ch)
            except StopIteration:
                chains.remove(ch)


def _hgrn_kernel(qf_ref, vf_ref, lff_ref, qb_ref, vb_ref, lfb_ref, pair_ref, rows_ref, tri_ref,
                 of_ref, ob_ref, sf_scr, sb_scr, a_scr, p_scr, oi_scr):
    @pl.when(pl.program_id(2) == 0)
    def _():
        sf_scr[...] = jnp.zeros(sf_scr.shape, F32)
        sb_scr[...] = jnp.zeros(sb_scr.shape, F32)

    c = HGRN_CHUNK
    consts = (pair_ref, rows_ref, tri_ref)
    n = qf_ref.shape[1] // c
    heads = qf_ref.shape[2] // HEAD_DIM
    row = lambda idx: pl.multiple_of(idx * c, c)
    dirs = ((qf_ref, vf_ref, lff_ref, of_ref, sf_scr, False), (qb_ref, vb_ref, lfb_ref, ob_ref, sb_scr, True))
    first = lambda rev: n - 1 if rev else 0
    last = lambda rev: 0 if rev else n - 1

    for hd in range(heads):
        cols = slice(hd * HEAD_DIM, (hd + 1) * HEAD_DIM)
        for d, (_, _, lf_ref, _, _, rev) in enumerate(dirs):
            a2 = _cumsum_issue(lf_ref[0, pl.ds(first(rev) * c, c), cols], tri_ref[d])
            a_scr[2 * hd + d] = a2[:, :LANES] + a2[:, LANES:]
    p_scr[...] = jnp.zeros(p_scr.shape, BF16)
    oi_scr[...] = jnp.zeros(oi_scr.shape, F32)

    def body(ci, carry):
        idx = {False: (jnp.maximum(ci - 1, 0), ci, jnp.minimum(ci + 1, n - 1)),
               True: (jnp.minimum(n - ci, n - 1), n - 1 - ci, jnp.maximum(n - 2 - ci, 0))}
        chains = []
        for hd in range(heads):
            for d, (q_ref, v_ref, lf_ref, o_ref, st_scr, rev) in enumerate(dirs):
                rows3 = tuple(row(i) for i in idx[rev])
                chains.append(_hgrn_chunk(q_ref, v_ref, lf_ref, o_ref, st_scr, a_scr, p_scr, oi_scr, rows3, hd,
                                          2 * hd + d, consts, rev=rev))
        _interleave(chains)
        return carry

    lax.fori_loop(0, n, body, 0)

    for hd in range(heads):
        cols = slice(hd * HEAD_DIM, (hd + 1) * HEAD_DIM)
        for d, (_, v_ref, _, o_ref, _, rev) in enumerate(dirs):
            rws = pl.ds(last(rev) * c, c)
            pv = jnp.dot(p_scr[2 * hd + d], v_ref[0, rws, cols], preferred_element_type=F32)
            o_ref[0, rws, cols] = (oi_scr[2 * hd + d] + pv).astype(o_ref.dtype)


HGRN_HEADS_PER_STEP = 4


def hgrn2_bidir(q, v, logf, ts=2048, hps=HGRN_HEADS_PER_STEP):
    b, s, hk = q.shape
    ts = min(ts, s)
    nt = s // ts
    ng = hk // (hps * HEAD_DIM)
    blk = (1, ts, hps * HEAD_DIM)
    fwd = lambda bi, h, i: (bi, i, h)
    bwd = lambda bi, h, i: (bi, nt - 1 - i, h)
    out = jax.ShapeDtypeStruct((b, s, hk), BF16)
    state = pltpu.VMEM((hps, HEAD_DIM, HEAD_DIM), F32)
    consts = hgrn_constants()
    whole = lambda x: pl.BlockSpec(x.shape, lambda bi, h, i: (0,) * x.ndim)
    return pl.pallas_call(
        _hgrn_kernel,
        grid=(b, ng, nt),
        in_specs=[pl.BlockSpec(blk, fwd), pl.BlockSpec(blk, fwd), pl.BlockSpec(blk, fwd),
                  pl.BlockSpec(blk, bwd), pl.BlockSpec(blk, bwd),
                  pl.BlockSpec(blk, lambda bi, h, i: (bi, nt - 1 - i, ng + h))] + [whole(x) for x in consts],
        out_specs=[pl.BlockSpec(blk, fwd), pl.BlockSpec(blk, bwd)],
        out_shape=[out, out],
        scratch_shapes=[state, state, pltpu.VMEM((2 * hps, HGRN_CHUNK, HEAD_DIM), F32),
                        pltpu.VMEM((2 * hps, HGRN_CHUNK, HGRN_CHUNK), BF16),
                        pltpu.VMEM((2 * hps, HGRN_CHUNK, HEAD_DIM), F32)],
        compiler_params=_cparams(("parallel", "parallel", "arbitrary")),
        name="hgrn2",
    )(q, v, logf, q, v, logf, *consts)


def _hgrn_post_kernel(of_ref, ob_ref, gr_ref, gn_ref, o_ref):
    gn = gn_ref[...]
    for hd in range(o_ref.shape[1] // HEAD_DIM):
        sl = slice(hd * HEAD_DIM, (hd + 1) * HEAD_DIM)
        o = _rms(of_ref[:, sl].astype(F32) + ob_ref[:, sl].astype(F32), gn)
        o_ref[:, sl] = (o * gr_ref[:, sl].astype(F32)).astype(o_ref.dtype)


def hgrn_post(o_fw, o_bw, g_silu, g_norm, tm=1024):
    m, d = o_fw.shape
    tm = min(tm, m)
    row = pl.BlockSpec((tm, d), lambda i: (i, 0))
    return pl.pallas_call(
        _hgrn_post_kernel,
        grid=(m // tm,),
        in_specs=[row, row, row, pl.BlockSpec((1, HEAD_DIM), lambda i: (0, 0))],
        out_specs=row,
        out_shape=jax.ShapeDtypeStruct((m, d), BF16),
        compiler_params=_cparams(("parallel",)),
        name="hgrn_post",
    )(o_fw, o_bw, g_silu, g_norm.reshape(1, HEAD_DIM))


def _merge_kernel(att_ref, or_ref, woa_ref, woh_ref, sa_ref, sh_ref, o_ref):
    tm = o_ref.shape[0]
    rows_per = min(MM_ROWS, tm)
    woa, woh = woa_ref[...].astype(BF16), woh_ref[...].astype(BF16)
    for r in range(0, tm, rows_per):
        rs = slice(r, r + rows_per)
        ya = jnp.dot(att_ref[rs, :], woa, preferred_element_type=F32)
        yh = jnp.dot(or_ref[rs, :], woh, preferred_element_type=F32)
        o_ref[rs, :] = (sa_ref[rs, :].astype(F32) * ya + sh_ref[rs, :].astype(F32) * yh).astype(o_ref.dtype)


def gated_merge(att, o_r, w_oa, w_oh, gates, tm=1024, tn=1024):
    m, d = att.shape
    tm = min(tm, m)
    nj = d // tn
    row = pl.BlockSpec((tm, d), lambda i, j: (i, 0))
    wsp = pl.BlockSpec((d, tn), lambda i, j: (0, j))
    return pl.pallas_call(
        _merge_kernel,
        grid=(m // tm, nj),
        in_specs=[row, row, wsp, wsp,
                  pl.BlockSpec((tm, tn), lambda i, j: (i, j)),
                  pl.BlockSpec((tm, tn), lambda i, j: (i, nj + j))],
        out_specs=pl.BlockSpec((tm, tn), lambda i, j: (i, j)),
        out_shape=jax.ShapeDtypeStruct((m, d), BF16),
        compiler_params=_cparams(("parallel", "arbitrary")),
        name="gated_merge",
    )(att, o_r, w_oa, w_oh, gates, gates)


def _resid_proj_kernel(x_ref, a_ref, w_ref, o_ref):
    tm = o_ref.shape[0]
    rows_per = min(MM_ROWS, tm)
    w = w_ref[...].astype(BF16)
    for r in range(0, tm, rows_per):
        rs = slice(r, r + rows_per)
        o_ref[rs, :] = x_ref[rs, :] + jnp.dot(a_ref[rs, :], w, preferred_element_type=F32)


def resid_proj(x, a, w, tm=1024, tn=1024):
    m, d = x.shape
    tm = min(tm, m)
    return pl.pallas_call(
        _resid_proj_kernel,
        grid=(m // tm, d // tn),
        in_specs=[pl.BlockSpec((tm, tn), lambda i, j: (i, j)), pl.BlockSpec((tm, a.shape[1]), lambda i, j: (i, 0)),
                  pl.BlockSpec((a.shape[1], tn), lambda i, j: (0, j))],
        out_specs=pl.BlockSpec((tm, tn), lambda i, j: (i, j)),
        out_shape=jax.ShapeDtypeStruct((m, d), F32),
        compiler_params=_cparams(("parallel", "arbitrary")),
        name="out_proj",
    )(x, a, w)


def _mlp_kernel(x_ref, g_ref, wu_ref, wd_ref, o_ref, h_scr):
    @pl.when(pl.program_id(1) == 0)
    def _():
        x = x_ref[...]
        h_scr[...] = _rms(x, g_ref[...]).astype(h_scr.dtype)
        o_ref[...] = x

    tm = o_ref.shape[0]
    rows_per = min(MM_ROWS, tm)
    wu, wd = wu_ref[...].astype(BF16), wd_ref[...].astype(BF16)
    for r in range(0, tm, rows_per):
        rs = slice(r, r + rows_per)
        u = jnp.maximum(jnp.dot(h_scr[rs, :], wu, preferred_element_type=F32), 0.0)
        o_ref[rs, :] += jnp.dot((u * u).astype(BF16), wd, preferred_element_type=F32)


def mlp_block(x, gain, w_up, w_down, tm=1024, tf=512):
    m, d = x.shape
    ff = w_up.shape[1]
    tm = min(tm, m)
    return pl.pallas_call(
        _mlp_kernel,
        grid=(m // tm, ff // tf),
        in_specs=[pl.BlockSpec((tm, d), lambda i, f: (i, 0)), pl.BlockSpec((1, d), lambda i, f: (0, 0)),
                  pl.BlockSpec((d, tf), lambda i, f: (0, f)), pl.BlockSpec((tf, d), lambda i, f: (f, 0))],
        out_specs=pl.BlockSpec((tm, d), lambda i, f: (i, 0)),
        out_shape=jax.ShapeDtypeStruct((m, d), F32),
        scratch_shapes=[pltpu.VMEM((tm, d), BF16)],
        compiler_params=_cparams(("parallel", "arbitrary")),
        name="mlp",
    )(x, gain.reshape(1, d), w_up, w_down)


def _ple_kernel(x_ref, g_ref, wg_ref, p_ref, wp_ref, gf_ref, o_ref):
    x = x_ref[...]
    h = _rms(x, g_ref[...]).astype(BF16)
    gate = _sigmoid(jnp.dot(h, wg_ref[...], preferred_element_type=F32))
    emb = jnp.dot(p_ref[...].astype(BF16), wp_ref[...], preferred_element_type=F32)
    o_ref[...] = _rms(x + gate * emb, gf_ref[...])


def ple_final(x, gain, w_gate, p, w_p, g_final, tm=512):
    m, d = x.shape
    c = p.shape[1]
    tm = min(tm, m)
    const = lambda i: (0, 0)
    return pl.pallas_call(
        _ple_kernel,
        grid=(m // tm,),
        in_specs=[pl.BlockSpec((tm, d), lambda i: (i, 0)), pl.BlockSpec((1, d), const),
                  pl.BlockSpec((d, d), const), pl.BlockSpec((tm, c), lambda i: (i, 0)),
                  pl.BlockSpec((c, d), const), pl.BlockSpec((1, d), const)],
        out_specs=pl.BlockSpec((tm, d), lambda i: (i, 0)),
        out_shape=jax.ShapeDtypeStruct((m, d), F32),
        compiler_params=_cparams(("parallel",)),
        name="ple_final",
    )(x, gain.reshape(1, d), w_gate, p, w_p, g_final.reshape(1, d))


def kernel(x, p, g_mix, w_in, g_q, g_k, w_o_attn, hgrn_lb, g_hgrn, w_o_hgrn, w_out, g_mlp, w_up, w_down,
           g_ple, w_ple_gate, w_ple, g_final):
    b, s, d = x.shape
    m = b * s
    depth = w_in.shape[0]
    attn_q = N_Q_HEADS * HEAD_DIM
    attn_kv = N_KV_HEADS * HEAD_DIM
    hk = HGRN_HEADS * HEAD_DIM
    rope = rope_tables(s)
    lb_all = jnp.cumsum(jax.nn.softmax(hgrn_lb.astype(F32), axis=0), axis=0)

    xf = x.reshape(m, d)
    for i in range(depth):
        w = w_in[i]
        h = rmsnorm_bf16(xf, g_mix[i])
        c0 = 0
        q_a = proj_qk(h, w, c0, attn_q, g_q[i], rope, s, HEAD_DIM ** -0.5 * LOG2_E); c0 += attn_q
        k_a = proj_qk(h, w, c0, attn_kv, g_k[i], rope, s, 1.0); c0 += attn_kv
        v_a = proj_plain(h, w, c0, attn_kv, None); c0 += attn_kv
        q_r = proj_plain(h, w, c0, hk, "silu"); c0 += hk
        logf = proj_logf(h, w, c0, 2 * hk, lb_all[i]); c0 += 2 * hk
        i_r = proj_plain(h, w, c0, hk, None); c0 += hk
        g_r = proj_plain(h, w, c0, hk, "silu"); c0 += hk
        gates = proj_plain(h, w, c0, 2 * d, "sigmoid"); c0 += 2 * d

        att = gqa_attention(q_a.reshape(b, s, attn_q), k_a.reshape(b, s, attn_kv), v_a.reshape(b, s, attn_kv))
        o_fw, o_bw = hgrn2_bidir(q_r.reshape(b, s, hk), i_r.reshape(b, s, hk), logf.reshape(b, s, 2 * hk))
        o_r = hgrn_post(o_fw.reshape(m, hk), o_bw.reshape(m, hk), g_r, g_hgrn[i])
        mixed = gated_merge(att.reshape(m, attn_q), o_r, w_o_attn[i].astype(BF16), w_o_hgrn[i].astype(BF16), gates)
        xf = resid_proj(xf, mixed, w_out[i].astype(BF16))
        xf = mlp_block(xf, g_mlp[i], w_up[i].astype(BF16), w_down[i])
        assert depth == 1
        xf = ple_final(xf, g_ple[i], w_ple_gate[i].astype(BF16), p[i].reshape(m, -1), w_ple[i].astype(BF16),
                       g_final)
    return xf.reshape(b, s, d)
```

```python
import functools

import jax
import jax.numpy as jnp
from jax import lax
from jax.experimental import pallas as pl
from jax.experimental.pallas import tpu as pltpu

F32 = jnp.float32
BF16 = jnp.bfloat16

EPS = 1e-6
LOG2_E = 1.4426950408889634
HEAD_DIM = 128
N_Q_HEADS = 16
N_KV_HEADS = 4
GQA_GROUPS = N_Q_HEADS // N_KV_HEADS
GRID_W = 64
ROPE_THETA = 10000.0
HGRN_HEADS = 16
LANES = 128
SUBLANES = 8
VMEM_LIMIT = 56 * 1024 * 1024


def _cparams(sem):
    return pltpu.CompilerParams(dimension_semantics=sem, vmem_limit_bytes=VMEM_LIMIT)


def _sigmoid(x):
    return 1.0 / (1.0 + jnp.exp(-x))


def _rms(x, gain):
    ms = jnp.mean(x * x, axis=-1, keepdims=True)
    return x * lax.rsqrt(ms + EPS) * gain


def _rmsnorm_kernel(x_ref, g_ref, o_ref):
    o_ref[...] = _rms(x_ref[...], g_ref[...]).astype(o_ref.dtype)


def rmsnorm_bf16(x, gain, tm=1024):
    m, d = x.shape
    tm = min(tm, m)
    return pl.pallas_call(
        _rmsnorm_kernel,
        grid=(m // tm,),
        in_specs=[pl.BlockSpec((tm, d), lambda i: (i, 0)), pl.BlockSpec((1, d), lambda i: (0, 0))],
        out_specs=pl.BlockSpec((tm, d), lambda i: (i, 0)),
        out_shape=jax.ShapeDtypeStruct((m, d), BF16),
        compiler_params=_cparams(("parallel",)),
        name="rmsnorm",
    )(x, gain.reshape(1, d))


QK_SUB = 2 * HEAD_DIM


MM_ROWS = 512
PROJ_ROWS = 256


def _subtiled_matmul(h_ref, w_ref, epilogue, rows=PROJ_ROWS):
    tm = h_ref.shape[0]
    rows_per = min(rows, tm)
    units = [(slice(r, r + rows_per), slice(c0, c0 + QK_SUB))
             for c0 in range(0, w_ref.shape[1], QK_SUB) for r in range(0, tm, rows_per)]
    weights = {}

    def matmul(u):
        rs, cs = u
        if cs.start not in weights:
            weights.clear()
            weights[cs.start] = w_ref[:, cs].astype(BF16)
        return jnp.dot(h_ref[rs, :], weights[cs.start], preferred_element_type=F32)

    prev = matmul(units[0])
    for j in range(1, len(units)):
        cur = matmul(units[j])
        epilogue(prev, *units[j - 1])
        prev = cur
    epilogue(prev, *units[-1])


def _stationary_weights(w_ref, wb_scr, rope_order=False):
    @pl.when(pl.program_id(1) == 0)
    def _():
        w = w_ref[...]
        if rope_order:
            tn = w.shape[1]
            quarter = (lax.broadcasted_iota(jnp.int32, w.shape, 1) // (HEAD_DIM // 4)) % 4
            w = jnp.where(quarter == 1, pltpu.roll(w, tn - HEAD_DIM // 4, 1),
                          jnp.where(quarter == 2, pltpu.roll(w, HEAD_DIM // 4, 1), w))
        wb_scr[...] = w.astype(wb_scr.dtype)


def _proj_plain_kernel(h_ref, w_ref, o_ref, wb_scr, *, act):
    def epilogue(acc, rs, cs):
        if act == "silu":
            acc = acc * _sigmoid(acc)
        elif act == "sigmoid":
            acc = _sigmoid(acc)
        o_ref[rs, cs] = acc.astype(o_ref.dtype)

    _stationary_weights(w_ref, wb_scr)
    _subtiled_matmul(h_ref, wb_scr, epilogue)


def _proj_logf_kernel(h_ref, w_ref, lb_ref, o_ref, wb_scr):
    def epilogue(acc, rs, cs):
        lb = lb_ref[:, cs]
        o_ref[rs, cs] = jnp.log(lb + (1.0 - lb) * _sigmoid(acc)) * LOG2_E

    _stationary_weights(w_ref, wb_scr)
    _subtiled_matmul(h_ref, wb_scr, epilogue)


def _proj_qk_kernel(h_ref, w_ref, cg_ref, sg_ref, o_ref, wb_scr):
    ones = jnp.ones((HEAD_DIM, HEAD_DIM), BF16)
    _stationary_weights(w_ref, wb_scr, rope_order=True)

    def epilogue(acc, rs, cs):
        cg, sg = cg_ref[rs, :], sg_ref[rs, :]
        for hd in range(QK_SUB // HEAD_DIM):
            x = acc[:, hd * HEAD_DIM:(hd + 1) * HEAD_DIM]
            ssq = jnp.dot((x * x).astype(BF16), ones, preferred_element_type=F32)
            r = lax.rsqrt(ssq * (1.0 / HEAD_DIM) + EPS)
            y = r * (x * cg + pltpu.roll(x, HEAD_DIM // 2, 1) * sg)
            c0 = cs.start + hd * HEAD_DIM
            o_ref[rs, c0:c0 + HEAD_DIM] = y.astype(o_ref.dtype)

    _subtiled_matmul(h_ref, wb_scr, epilogue, rows=MM_ROWS)


def _proj_call(kernel, h, w, col0, ncols, out_dtype, extra=(), extra_specs=(), tm=1024, tn=1024):
    m, d = h.shape
    tm = min(tm, m)
    tn = min(tn, ncols)
    assert col0 % tn == 0 and ncols % tn == 0 and m % tm == 0
    jb = col0 // tn
    return pl.pallas_call(
        kernel,
        grid=(ncols // tn, m // tm),
        in_specs=[pl.BlockSpec((tm, d), lambda j, i: (i, 0)),
                  pl.BlockSpec((d, tn), lambda j, i: (0, jb + j))] + list(extra_specs),
        out_specs=pl.BlockSpec((tm, tn), lambda j, i: (i, j)),
        out_shape=jax.ShapeDtypeStruct((m, ncols), out_dtype),
        scratch_shapes=[pltpu.VMEM((d, tn), BF16)],
        compiler_params=_cparams(("parallel", "arbitrary")),
        name="in_proj",
    )(h, w, *extra)


def proj_plain(h, w, col0, ncols, act, out_dtype=BF16):
    return _proj_call(functools.partial(_proj_plain_kernel, act=act), h, w, col0, ncols, out_dtype)


def proj_logf(h, w, col0, ncols, lb, tn=1024):
    return _proj_call(_proj_logf_kernel, h, w, col0, ncols, F32, extra=(lb.reshape(1, ncols),),
                      extra_specs=(pl.BlockSpec((1, tn), lambda j, i: (0, j)),), tn=tn)


def proj_qk(h, w, col0, ncols, gain, rope, seq, scale, tm=1024, tn=1024):
    cos, sin = rope
    gp = gain.astype(F32)[jnp.array(ROPE_PERM)]
    cg = cos * (gp * scale)[None, :]
    sg = sin * (jnp.roll(gp, HEAD_DIM // 2) * scale)[None, :]
    tm = min(tm, seq)
    tn = min(tn, ncols)
    nsb = seq // tm
    tab = pl.BlockSpec((tm, HEAD_DIM), lambda j, i: (i % nsb, 0))
    return _proj_call(_proj_qk_kernel, h, w, col0, ncols, BF16, extra=(cg, sg), extra_specs=(tab, tab),
                      tm=tm, tn=tn)


ROPE_PERM = tuple(list(range(0, 32)) + list(range(64, 96)) + list(range(32, 64)) + list(range(96, 128)))


def rope_tables(seq):
    half = HEAD_DIM // 2
    t = jnp.arange(seq, dtype=jnp.int32)
    row = (t // GRID_W).astype(F32)
    col = (t % GRID_W).astype(F32)
    inv_freq = ROPE_THETA ** (-jnp.arange(0, half, 2, dtype=F32) / half)
    ang = jnp.concatenate([row[:, None] * inv_freq[None, :], col[:, None] * inv_freq[None, :]], axis=-1)
    cos = jnp.concatenate([jnp.cos(ang), jnp.cos(ang)], axis=-1)
    sin = jnp.concatenate([-jnp.sin(ang), jnp.sin(ang)], axis=-1)
    return cos, sin


ATTN_TQ = 128
ATTN_TKC = 8192


def _attn_kernel(q0_ref, qa_ref, qb_ref, k_ref, v_ref, o_ref, s_scr, mrun_scr, mcur_scr, acc_scr, *, tkc):
    t = pl.program_id(2)
    nt = pl.num_programs(2)
    tq = qa_ref.shape[1]
    nch = k_ref.shape[1] // tkc
    nsl = tkc // LANES
    ones = jnp.ones((tkc, LANES), BF16)

    def stack(q_ref):
        return jnp.concatenate([q_ref[0, :, g * HEAD_DIM:(g + 1) * HEAD_DIM] for g in range(GQA_GROUPS)], axis=0)

    def pass1(qs, c, slot):
        off = pl.multiple_of(c * tkc, tkc)
        s = lax.dot_general(qs, k_ref[0, pl.ds(off, tkc), :], (((1,), (1,)), ((), ())),
                            preferred_element_type=F32)
        s_scr[slot, c] = s
        m = s[:, :LANES]
        for i in range(1, nsl):
            m = jnp.maximum(m, s[:, i * LANES:(i + 1) * LANES])
        mrun_scr[...] = jnp.maximum(mrun_scr[...], m)

    def pass2(c, slot):
        off = pl.multiple_of(c * tkc, tkc)
        vext = jnp.concatenate([v_ref[0, pl.ds(off, tkc), :], ones], axis=1)
        m = mcur_scr[...]
        p = jnp.concatenate([jnp.exp2(s_scr[slot, c, :, i * LANES:(i + 1) * LANES] - m) for i in range(nsl)],
                            axis=1)
        acc_scr[...] += jnp.dot(p.astype(BF16), vext, preferred_element_type=F32)

    def begin_pass1():
        mrun_scr[...] = jnp.full(mrun_scr.shape, -jnp.inf, F32)

    def end_pass1():
        mcur_scr[...] = jnp.broadcast_to(jnp.max(mrun_scr[...], axis=1, keepdims=True), mcur_scr.shape)
        acc_scr[...] = jnp.zeros(acc_scr.shape, F32)

    def emit(half):
        o = acc_scr[:, :HEAD_DIM] / acc_scr[:, HEAD_DIM:]
        for g in range(GQA_GROUPS):
            o_ref[0, half * tq:(half + 1) * tq, g * HEAD_DIM:(g + 1) * HEAD_DIM] = (
                o[g * tq:(g + 1) * tq].astype(o_ref.dtype))

    @pl.when(t == 0)
    def _():
        qs0 = stack(q0_ref)
        begin_pass1()

        def body_0(c, carry):
            pass1(qs0, c, 0)
            return carry

        lax.fori_loop(0, nch, body_0, 0)
        end_pass1()

    qs = stack(qa_ref)
    begin_pass1()

    def body_x(c, carry):
        pass1(qs, c, 1)
        pass2(c, 0)
        return carry

    lax.fori_loop(0, nch, body_x, 0)
    emit(0)
    end_pass1()

    @pl.when(t < nt - 1)
    def _():
        qs2 = stack(qb_ref)
        begin_pass1()

        def body_y(c, carry):
            pass1(qs2, c, 0)
            pass2(c, 1)
            return carry

        lax.fori_loop(0, nch, body_y, 0)
        emit(1)
        end_pass1()

    @pl.when(t == nt - 1)
    def _():
        def body_z(c, carry):
            pass2(c, 1)
            return carry

        lax.fori_loop(0, nch, body_z, 0)
        emit(1)


def gqa_attention(q, k, v, tq=ATTN_TQ, tkc=ATTN_TKC):
    b, s, _ = q.shape
    tkc = min(tkc, s)
    nq = s // tq
    assert nq % 2 == 0 and s % tkc == 0
    gw = GQA_GROUPS * HEAD_DIM
    rows = GQA_GROUPS * tq
    qblk = (1, tq, gw)
    kvblk = (1, s, HEAD_DIM)
    return pl.pallas_call(
        functools.partial(_attn_kernel, tkc=tkc),
        grid=(b, N_KV_HEADS, nq // 2),
        in_specs=[pl.BlockSpec(qblk, lambda bi, h, t: (bi, 0, h)),
                  pl.BlockSpec(qblk, lambda bi, h, t: (bi, 2 * t + 1, h)),
                  pl.BlockSpec(qblk, lambda bi, h, t: (bi, jnp.minimum(2 * t + 2, nq - 1), h)),
                  pl.BlockSpec(kvblk, lambda bi, h, t: (bi, 0, h)),
                  pl.BlockSpec(kvblk, lambda bi, h, t: (bi, 0, h))],
        out_specs=pl.BlockSpec((1, 2 * tq, gw), lambda bi, h, t: (bi, t, h)),
        out_shape=jax.ShapeDtypeStruct(q.shape, BF16),
        scratch_shapes=[pltpu.VMEM((2, s // tkc, rows, tkc), F32), pltpu.VMEM((rows, LANES), F32),
                        pltpu.VMEM((rows, LANES), F32), pltpu.VMEM((rows, 2 * HEAD_DIM), F32)],
        compiler_params=_cparams(("parallel", "parallel", "arbitrary")),
        name="gqa_attention",
    )(q, q, q, k, v)


HGRN_CHUNK = 128
HGRN_LAG = 2


def _group_row(x, group, row):
    c, n = x.shape
    xr = x.reshape(c // group, group, n)
    return jnp.broadcast_to(xr[:, row:row + 1, :], xr.shape).reshape(c, n)


def _boundary_row(x, m, rev, odd_block):
    ref_row = m if rev else m - 1
    if 2 * m >= SUBLANES:
        return _group_row(x, 2 * m, ref_row)
    assert 4 * m == SUBLANES
    return jnp.where(odd_block(2 * m), _group_row(x, SUBLANES, 2 * m + ref_row), _group_row(x, SUBLANES, ref_row))


def hgrn_constants():
    import numpy as np
    c = HGRN_CHUNK
    t = np.arange(c)[:, None]
    s = np.arange(c)[None, :]
    pair, tri = [], []
    for rev in (False, True):
        masks = [t == s]
        m = 1
        while m < c:
            same = (t // (2 * m)) == (s // (2 * m))
            t_query = ((t // m) % 2) == (0 if rev else 1)
            s_key = ((s // m) % 2) == (1 if rev else 0)
            masks.append(same & t_query & s_key)
            m *= 2
        pair.append(np.stack(masks))
        tri.append((s >= t) if rev else (s <= t))
    rows = np.stack([np.broadcast_to(((t // n) % 2) == 1, (c, LANES)) for n in (1, 2, 4)])
    return (jnp.asarray(np.stack(pair), jnp.int32), jnp.asarray(rows, jnp.int32),
            jnp.asarray(np.stack(tri), BF16))


def _neg_abs(x):
    return pltpu.bitcast(pltpu.bitcast(x, jnp.uint32) | jnp.uint32(0x80000000), F32)


def _nt(x, y):
    return lax.dot_general(x, y, (((1,), (1,)), ((), ())), preferred_element_type=F32)


def _hgrn_chunk(q_ref, v_ref, lf_ref, o_ref, st_scr, a_scr, p_scr, oi_scr, rows3, col, slot, consts, *, rev):
    pair_ref, rows_ref, tri_ref = consts
    r_prev, r0, r_next = rows3
    d = 1 if rev else 0
    odd_block = lambda n: rows_ref[{1: 0, 2: 1, 4: 2}[n]] != 0
    c = HGRN_CHUNK
    cols = slice(col * HEAD_DIM, (col + 1) * HEAD_DIM)

    pv = jnp.dot(p_scr[slot], v_ref[0, pl.ds(r_prev, c), cols], preferred_element_type=F32)
    a2 = _cumsum_issue(lf_ref[0, pl.ds(r_next, c), cols], tri_ref[d])

    qb = q_ref[0, pl.ds(r0, c), cols]
    vb = v_ref[0, pl.ds(r0, c), cols]
    f = jnp.exp2(lf_ref[0, pl.ds(r0, c), cols])
    k = 1.0 - f
    q = qb.astype(F32)
    sc0 = jnp.sum(q * k, axis=1, keepdims=True)
    k3 = k.reshape(c // SUBLANES, SUBLANES, HEAD_DIM)
    k_other = pltpu.roll(k3, SUBLANES - 1 if rev else 1, 1).reshape(c, HEAD_DIM)
    sc1 = jnp.sum(q * f * k_other, axis=1, keepdims=True)
    a = a_scr[slot]
    edge = 0 if rev else c - 1
    a_end = a[edge:edge + 1, :]

    st = st_scr[col]
    o = jnp.dot((q * jnp.exp2(a)).astype(BF16), st.astype(BF16), preferred_element_type=F32)
    kdec = (k * jnp.exp2(a_end - a)).astype(BF16)
    st_new = lax.dot_general(kdec, vb, (((0,), (0,)), ((), ())), preferred_element_type=F32)
    st_decayed = st * jnp.transpose(jnp.broadcast_to(jnp.exp2(a_end), (HEAD_DIM, HEAD_DIM)))

    nstrip = c // SUBLANES
    strip = lambda x, i: x[i * SUBLANES:(i + 1) * SUBLANES]
    p = [None] * nstrip

    def apply(pending):
        level, scores, rows, shifts = pending
        for j, i in enumerate(rows):
            mask = pair_ref[d, level, i * SUBLANES:(i + 1) * SUBLANES, :] != 0
            s = strip(scores, j)
            if shifts is not None and shifts[j]:
                s = pltpu.roll(s, shifts[j], 1)
            p[i] = jnp.where(mask, s, 0.0 if p[i] is None else p[i])

    query_first = rev

    def halves(m):
        out = []
        for g in range(c // (2 * m)):
            first = slice(g * 2 * m, g * 2 * m + m)
            second = slice(g * 2 * m + m, (g + 1) * 2 * m)
            out.append((first, second) if query_first else (second, first))
        return out

    def factor(m):
        ref = _boundary_row(a, m, rev, odd_block)
        if m < SUBLANES:
            return jnp.exp2(_neg_abs(a - ref))
        parts = []
        for qs, ks in halves(m):
            dq, dk = a[qs] - ref[qs], ref[ks] - a[ks]
            parts += [dq, dk] if query_first else [dk, dq]
        return jnp.exp2(jnp.concatenate(parts, axis=0))

    every = list(range(nstrip))
    pending = [(0, sc0, every, None), (1, sc1, every, None)]
    nlevel = pair_ref.shape[1] - 1
    m = 2
    e = factor(m)
    for level in range(2, nlevel + 1):
        e_next = factor(2 * m) if level < nlevel else None
        if m < SUBLANES:
            qk = jnp.where(odd_block(m), k, q) if query_first else jnp.where(odd_block(m), q, k)
            z = (qk * e).astype(BF16)
            sc = _nt(z, z)
            rows, shifts = every, None
        else:
            qparts, kparts, rows, shifts = [], [], [], []
            for g, (qs, ks) in enumerate(halves(m)):
                qparts.append(q[qs] * e[qs])
                kparts.append(k[ks] * e[ks])
                rows += list(range(qs.start // SUBLANES, qs.stop // SUBLANES))
                shifts += [(ks.start - g * m) % c] * (m // SUBLANES)
            sc = _nt(jnp.concatenate(qparts, axis=0).astype(BF16), jnp.concatenate(kparts, axis=0).astype(BF16))
            sc = jnp.concatenate([sc, jnp.zeros((c // 2, c - c // 2), F32)], axis=1)
        e = e_next
        yield
        if level == 2:
            o_prev = oi_scr[slot]
            oi_scr[slot] = o
            st_scr[col] = st_decayed + st_new
        pending.append((level, sc, rows, shifts))
        if len(pending) > HGRN_LAG:
            apply(pending.pop(0))
        m *= 2
    a_scr[slot] = a2[:, :LANES] + a2[:, LANES:]
    o_ref[0, pl.ds(r_prev, c), cols] = (o_prev + pv).astype(o_ref.dtype)
    yield
    for item in pending:
        apply(item)
    p_scr[slot] = jnp.concatenate(p, axis=0).astype(BF16)


def _cumsum_issue(lf, tri):
    hi = lf.astype(BF16)
    lo = (lf - hi.astype(F32)).astype(BF16)
    return jnp.dot(tri, jnp.concatenate([hi, lo], axis=1), preferred_element_type=F32)


def _interleave(chains):
    chains = list(chains)
    while chains:
        for ch in list(chains):
            try:
                next(ch)
            except StopIteration:
                chains.remove(ch)


def _hgrn_kernel(qf_ref, vf_ref, lff_ref, qb_ref, vb_ref, lfb_ref, pair_ref, rows_ref, tri_ref,
                 of_ref, ob_ref, sf_scr, sb_scr, a_scr, p_scr, oi_scr):
    @pl.when(pl.program_id(2) == 0)
    def _():
        sf_scr[...] = jnp.zeros(sf_scr.shape, F32)
        sb_scr[...] = jnp.zeros(sb_scr.shape, F32)

    c = HGRN_CHUNK
    consts = (pair_ref, rows_ref, tri_ref)
    n = qf_ref.shape[1] // c
    heads = qf_ref.shape[2] // HEAD_DIM
    row = lambda idx: pl.multiple_of(idx * c, c)
    dirs = ((qf_ref, vf_ref, lff_ref, of_ref, sf_scr, False), (qb_ref, vb_ref, lfb_ref, ob_ref, sb_scr, True))
    first = lambda rev: n - 1 if rev else 0
    last = lambda rev: 0 if rev else n - 1

    for hd in range(heads):
        cols = slice(hd * HEAD_DIM, (hd + 1) * HEAD_DIM)
        for d, (_, _, lf_ref, _, _, rev) in enumerate(dirs):
            a2 = _cumsum_issue(lf_ref[0, pl.ds(first(rev) * c, c), cols], tri_ref[d])
            a_scr[2 * hd + d] = a2[:, :LANES] + a2[:, LANES:]
    p_scr[...] = jnp.zeros(p_scr.shape, BF16)
    oi_scr[...] = jnp.zeros(oi_scr.shape, F32)

    def body(ci, carry):
        idx = {False: (jnp.maximum(ci - 1, 0), ci, jnp.minimum(ci + 1, n - 1)),
               True: (jnp.minimum(n - ci, n - 1), n - 1 - ci, jnp.maximum(n - 2 - ci, 0))}
        chains = []
        for hd in range(heads):
            for d, (q_ref, v_ref, lf_ref, o_ref, st_scr, rev) in enumerate(dirs):
                rows3 = tuple(row(i) for i in idx[rev])
                chains.append(_hgrn_chunk(q_ref, v_ref, lf_ref, o_ref, st_scr, a_scr, p_scr, oi_scr, rows3, hd,
                                          2 * hd + d, consts, rev=rev))
        _interleave(chains)
        return carry

    lax.fori_loop(0, n, body, 0)

    for hd in range(heads):
        cols = slice(hd * HEAD_DIM, (hd + 1) * HEAD_DIM)
        for d, (_, v_ref, _, o_ref, _, rev) in enumerate(dirs):
            rws = pl.ds(last(rev) * c, c)
            pv = jnp.dot(p_scr[2 * hd + d], v_ref[0, rws, cols], preferred_element_type=F32)
            o_ref[0, rws, cols] = (oi_scr[2 * hd + d] + pv).astype(o_ref.dtype)


HGRN_HEADS_PER_STEP = 4


def hgrn2_bidir(q, v, logf, ts=2048, hps=HGRN_HEADS_PER_STEP):
    b, s, hk = q.shape
    ts = min(ts, s)
    nt = s // ts
    ng = hk // (hps * HEAD_DIM)
    blk = (1, ts, hps * HEAD_DIM)
    fwd = lambda bi, h, i: (bi, i, h)
    bwd = lambda bi, h, i: (bi, nt - 1 - i, h)
    out = jax.ShapeDtypeStruct((b, s, hk), BF16)
    state = pltpu.VMEM((hps, HEAD_DIM, HEAD_DIM), F32)
    consts = hgrn_constants()
    whole = lambda x: pl.BlockSpec(x.shape, lambda bi, h, i: (0,) * x.ndim)
    return pl.pallas_call(
        _hgrn_kernel,
        grid=(b, ng, nt),
        in_specs=[pl.BlockSpec(blk, fwd), pl.BlockSpec(blk, fwd), pl.BlockSpec(blk, fwd),
                  pl.BlockSpec(blk, bwd), pl.BlockSpec(blk, bwd),
                  pl.BlockSpec(blk, lambda bi, h, i: (bi, nt - 1 - i, ng + h))] + [whole(x) for x in consts],
        out_specs=[pl.BlockSpec(blk, fwd), pl.BlockSpec(blk, bwd)],
        out_shape=[out, out],
        scratch_shapes=[state, state, pltpu.VMEM((2 * hps, HGRN_CHUNK, HEAD_DIM), F32),
                        pltpu.VMEM((2 * hps, HGRN_CHUNK, HGRN_CHUNK), BF16),
                        pltpu.VMEM((2 * hps, HGRN_CHUNK, HEAD_DIM), F32)],
        compiler_params=_cparams(("parallel", "parallel", "arbitrary")),
        name="hgrn2",
    )(q, v, logf, q, v, logf, *consts)


def _hgrn_post_kernel(of_ref, ob_ref, gr_ref, gn_ref, o_ref):
    gn = gn_ref[...]
    for hd in range(o_ref.shape[1] // HEAD_DIM):
        sl = slice(hd * HEAD_DIM, (hd + 1) * HEAD_DIM)
        o = _rms(of_ref[:, sl].astype(F32) + ob_ref[:, sl].astype(F32), gn)
        o_ref[:, sl] = (o * gr_ref[:, sl].astype(F32)).astype(o_ref.dtype)


def hgrn_post(o_fw, o_bw, g_silu, g_norm, tm=1024):
    m, d = o_fw.shape
    tm = min(tm, m)
    row = pl.BlockSpec((tm, d), lambda i: (i, 0))
    return pl.pallas_call(
        _hgrn_post_kernel,
        grid=(m // tm,),
        in_specs=[row, row, row, pl.BlockSpec((1, HEAD_DIM), lambda i: (0, 0))],
        out_specs=row,
        out_shape=jax.ShapeDtypeStruct((m, d), BF16),
        compiler_params=_cparams(("parallel",)),
        name="hgrn_post",
    )(o_fw, o_bw, g_silu, g_norm.reshape(1, HEAD_DIM))


def _merge_kernel(att_ref, or_ref, woa_ref, woh_ref, sa_ref, sh_ref, o_ref):
    tm = o_ref.shape[0]
    rows_per = min(MM_ROWS, tm)
    woa, woh = woa_ref[...].astype(BF16), woh_ref[...].astype(BF16)
    for r in range(0, tm, rows_per):
        rs = slice(r, r + rows_per)
        ya = jnp.dot(att_ref[rs, :], woa, preferred_element_type=F32)
        yh = jnp.dot(or_ref[rs, :], woh, preferred_element_type=F32)
        o_ref[rs, :] = (sa_ref[rs, :].astype(F32) * ya + sh_ref[rs, :].astype(F32) * yh).astype(o_ref.dtype)


def gated_merge(att, o_r, w_oa, w_oh, gates, tm=1024, tn=1024):
    m, d = att.shape
    tm = min(tm, m)
    nj = d // tn
    row = pl.BlockSpec((tm, d), lambda i, j: (i, 0))
    wsp = pl.BlockSpec((d, tn), lambda i, j: (0, j))
    return pl.pallas_call(
        _merge_kernel,
        grid=(m // tm, nj),
        in_specs=[row, row, wsp, wsp,
                  pl.BlockSpec((tm, tn), lambda i, j: (i, j)),
                  pl.BlockSpec((tm, tn), lambda i, j: (i, nj + j))],
        out_specs=pl.BlockSpec((tm, tn), lambda i, j: (i, j)),
        out_shape=jax.ShapeDtypeStruct((m, d), BF16),
        compiler_params=_cparams(("parallel", "arbitrary")),
        name="gated_merge",
    )(att, o_r, w_oa, w_oh, gates, gates)


def _resid_proj_kernel(x_ref, a_ref, w_ref, o_ref):
    tm = o_ref.shape[0]
    rows_per = min(MM_ROWS, tm)
    w = w_ref[...].astype(BF16)
    for r in range(0, tm, rows_per):
        rs = slice(r, r + rows_per)
        o_ref[rs, :] = x_ref[rs, :] + jnp.dot(a_ref[rs, :], w, preferred_element_type=F32)


def resid_proj(x, a, w, tm=1024, tn=1024):
    m, d = x.shape
    tm = min(tm, m)
    return pl.pallas_call(
        _resid_proj_kernel,
        grid=(m // tm, d // tn),
        in_specs=[pl.BlockSpec((tm, tn), lambda i, j: (i, j)), pl.BlockSpec((tm, a.shape[1]), lambda i, j: (i, 0)),
                  pl.BlockSpec((a.shape[1], tn), lambda i, j: (0, j))],
        out_specs=pl.BlockSpec((tm, tn), lambda i, j: (i, j)),
        out_shape=jax.ShapeDtypeStruct((m, d), F32),
        compiler_params=_cparams(("parallel", "arbitrary")),
        name="out_proj",
    )(x, a, w)


def _mlp_kernel(x_ref, g_ref, wu_ref, wd_ref, o_ref, h_scr):
    @pl.when(pl.program_id(1) == 0)
    def _():
        x = x_ref[...]
        h_scr[...] = _rms(x, g_ref[...]).astype(h_scr.dtype)
        o_ref[...] = x

    tm = o_ref.shape[0]
    rows_per = min(MM_ROWS, tm)
    wu, wd = wu_ref[...].astype(BF16), wd_ref[...].astype(BF16)
    for r in range(0, tm, rows_per):
        rs = slice(r, r + rows_per)
        u = jnp.maximum(jnp.dot(h_scr[rs, :], wu, preferred_element_type=F32), 0.0)
        o_ref[rs, :] += jnp.dot((u * u).astype(BF16), wd, preferred_element_type=F32)


def mlp_block(x, gain, w_up, w_down, tm=1024, tf=512):
    m, d = x.shape
    ff = w_up.shape[1]
    tm = min(tm, m)
    return pl.pallas_call(
        _mlp_kernel,
        grid=(m // tm, ff // tf),
        in_specs=[pl.BlockSpec((tm, d), lambda i, f: (i, 0)), pl.BlockSpec((1, d), lambda i, f: (0, 0)),
                  pl.BlockSpec((d, tf), lambda i, f: (0, f)), pl.BlockSpec((tf, d), lambda i, f: (f, 0))],
        out_specs=pl.BlockSpec((tm, d), lambda i, f: (i, 0)),
        out_shape=jax.ShapeDtypeStruct((m, d), F32),
        scratch_shapes=[pltpu.VMEM((tm, d), BF16)],
        compiler_params=_cparams(("parallel", "arbitrary")),
        name="mlp",
    )(x, gain.reshape(1, d), w_up, w_down)


def _ple_kernel(x_ref, g_ref, wg_ref, p_ref, wp_ref, gf_ref, o_ref):
    x = x_ref[...]
    h = _rms(x, g_ref[...]).astype(BF16)
    gate = _sigmoid(jnp.dot(h, wg_ref[...], preferred_element_type=F32))
    emb = jnp.dot(p_ref[...].astype(BF16), wp_ref[...], preferred_element_type=F32)
    o_ref[...] = _rms(x + gate * emb, gf_ref[...])


def ple_final(x, gain, w_gate, p, w_p, g_final, tm=512):
    m, d = x.shape
    c = p.shape[1]
    tm = min(tm, m)
    const = lambda i: (0, 0)
    return pl.pallas_call(
        _ple_kernel,
        grid=(m // tm,),
        in_specs=[pl.BlockSpec((tm, d), lambda i: (i, 0)), pl.BlockSpec((1, d), const),
                  pl.BlockSpec((d, d), const), pl.BlockSpec((tm, c), lambda i: (i, 0)),
                  pl.BlockSpec((c, d), const), pl.BlockSpec((1, d), const)],
        out_specs=pl.BlockSpec((tm, d), lambda i: (i, 0)),
        out_shape=jax.ShapeDtypeStruct((m, d), F32),
        compiler_params=_cparams(("parallel",)),
        name="ple_final",
    )(x, gain.reshape(1, d), w_gate, p, w_p, g_final.reshape(1, d))


def kernel(x, p, g_mix, w_in, g_q, g_k, w_o_attn, hgrn_lb, g_hgrn, w_o_hgrn, w_out, g_mlp, w_up, w_down,
           g_ple, w_ple_gate, w_ple, g_final):
    b, s, d = x.shape
    m = b * s
    depth = w_in.shape[0]
    attn_q = N_Q_HEADS * HEAD_DIM
    attn_kv = N_KV_HEADS * HEAD_DIM
    hk = HGRN_HEADS * HEAD_DIM
    rope = rope_tables(s)
    lb_all = jnp.cumsum(jax.nn.softmax(hgrn_lb.astype(F32), axis=0), axis=0)

    xf = x.reshape(m, d)
    for i in range(depth):
        w = w_in[i]
        h = rmsnorm_bf16(xf, g_mix[i])
        c0 = 0
        q_a = proj_qk(h, w, c0, attn_q, g_q[i], rope, s, HEAD_DIM ** -0.5 * LOG2_E); c0 += attn_q
        k_a = proj_qk(h, w, c0, attn_kv, g_k[i], rope, s, 1.0); c0 += attn_kv
        v_a = proj_plain(h, w, c0, attn_kv, None); c0 += attn_kv
        q_r = proj_plain(h, w, c0, hk, "silu"); c0 += hk
        logf = proj_logf(h, w, c0, 2 * hk, lb_all[i]); c0 += 2 * hk
        i_r = proj_plain(h, w, c0, hk, None); c0 += hk
        g_r = proj_plain(h, w, c0, hk, "silu"); c0 += hk
        gates = proj_plain(h, w, c0, 2 * d, "sigmoid"); c0 += 2 * d

        att = gqa_attention(q_a.reshape(b, s, attn_q), k_a.reshape(b, s, attn_kv), v_a.reshape(b, s, attn_kv))
        o_fw, o_bw = hgrn2_bidir(q_r.reshape(b, s, hk), i_r.reshape(b, s, hk), logf.reshape(b, s, 2 * hk))
        o_r = hgrn_post(o_fw.reshape(m, hk), o_bw.reshape(m, hk), g_r, g_hgrn[i])
        mixed = gated_merge(att.reshape(m, attn_q), o_r, w_o_attn[i].astype(BF16), w_o_hgrn[i].astype(BF16), gates)
        xf = resid_proj(xf, mixed, w_out[i].astype(BF16))
        xf = mlp_block(xf, g_mlp[i], w_up[i].astype(BF16), w_down[i])
        assert depth == 1
        xf = ple_final(xf, g_ple[i], w_ple_gate[i].astype(BF16), p[i].reshape(m, -1), w_ple[i].astype(BF16),
                       g_final)
    return xf.reshape(b, s, d)
```
